```python
import math
import jax, jax.numpy as jnp
from jax import lax
import numpy as np

D_MODEL = 1024
BATCH = 8
SEQ = 2048
DEPTH = 2
DEC_BATCH = 32
DEC_SEQ = 1
PAST_LEN = 8192
PAGE_SIZE = 128

HEAD_DIM = 64
MOBA_HEADS = 8
NSA_HEADS = 8
NSA_KV_HEADS = 2
NSA_GROUP = NSA_HEADS // NSA_KV_HEADS
ATTN_HEADS = MOBA_HEADS + NSA_HEADS
MOBA_BLOCK = 256
MOBA_TOPK = 3
CMP_BLOCK = 32
CMP_HIDDEN = 2 * HEAD_DIM
SEL_BLOCK = 64
SEL_TOPK = 16
WINDOW = 512
WIN_QBLOCK = 128
Q_CHUNK = 16
NUM_BUCKETS = 32
MAX_DISTANCE = 128
S5_GROUP_CH = 16
S5_GROUPS = D_MODEL // S5_GROUP_CH
S5_STATE = 64
DT_MIN = 0.001
DT_MAX = 0.1
D_FF = 4 * D_MODEL
N_ATTN_LAYERS = (DEPTH + 1) // 2
N_SSM_LAYERS = DEPTH // 2
MOBA_W = MOBA_HEADS * HEAD_DIM
NSA_Q_W = NSA_HEADS * HEAD_DIM
NSA_KV_W = NSA_KV_HEADS * HEAD_DIM
IN_COLS = 3 * MOBA_W + NSA_Q_W + 6 * NSA_KV_W + 3 * NSA_HEADS
SCALE = HEAD_DIM ** -0.5
EPS = 1e-6

kernel_name = "moba_nsa_s5_hybrid_step"


def rms_norm(x, g):
    xf = x.astype(jnp.float32)
    y = xf * lax.rsqrt(jnp.mean(xf * xf, axis=-1, keepdims=True) + EPS)
    return (y * g.astype(jnp.float32)).astype(x.dtype)


def adaln(c, w, b):
    m = (jax.nn.silu(c) @ w + b)[:, None, :]
    return jnp.split(m, 3, axis=-1)


def modulate(x, g, shift, scale):
    return rms_norm(x, g) * (1 + scale) + shift


def rel_bucket(dist):
    n = jnp.maximum(dist, 0)
    max_exact = NUM_BUCKETS // 2
    nf = jnp.maximum(n, 1).astype(jnp.float32)
    large = max_exact + (jnp.log(nf / max_exact) / math.log(MAX_DISTANCE / max_exact)
                         * (NUM_BUCKETS - max_exact)).astype(jnp.int32)
    large = jnp.minimum(large, NUM_BUCKETS - 1)
    return jnp.where(n < max_exact, n, large)


def masked_softmax(logits, mask):
    lf = jnp.where(mask, logits.astype(jnp.float32), -jnp.inf)
    m = jnp.max(lf, axis=-1, keepdims=True)
    m = jnp.where(jnp.isfinite(m), m, 0.0)
    p = jnp.exp(lf - m)
    return p / jnp.maximum(jnp.sum(p, axis=-1, keepdims=True), 1e-30)


def map_query_chunks(fn, q_pos, *xs):
    t = q_pos.shape[0]
    c = min(Q_CHUNK, t)
    n = -(-t // c)
    pad = n * c - t
    pos = jnp.pad(q_pos, (0, pad), mode="edge").reshape(n, c)

    def split(x):
        x = jnp.pad(x, [(0, 0), (0, pad)] + [(0, 0)] * (x.ndim - 2))
        return jnp.moveaxis(x.reshape((x.shape[0], n, c) + x.shape[2:]), 1, 0)

    out = lax.map(lambda a: fn(*a), (pos,) + tuple(split(x) for x in xs))
    out = jnp.moveaxis(out, 0, 1)
    return out.reshape((out.shape[0], n * c) + out.shape[3:])[:, :t]


def attn_project(h, w_in, qk_g):
    b, t = h.shape[:2]
    z = h @ w_in
    sizes = [MOBA_W, MOBA_W, MOBA_W, NSA_Q_W] + [NSA_KV_W] * 6
    mq, mk, mv, nq, kc, vc, ks, vs, kw, vw, gl = jnp.split(z, np.cumsum(sizes).tolist(), axis=-1)
    hd = lambda a, nh: a.reshape(b, t, nh, HEAD_DIM)
    mq = rms_norm(hd(mq, MOBA_HEADS), qk_g[0])
    mk = rms_norm(hd(mk, MOBA_HEADS), qk_g[1])
    mv = hd(mv, MOBA_HEADS)
    nq = rms_norm(hd(nq, NSA_HEADS), qk_g[2])
    kc, vc = hd(kc, NSA_KV_HEADS), hd(vc, NSA_KV_HEADS)
    ks, vs = rms_norm(hd(ks, NSA_KV_HEADS), qk_g[4]), hd(vs, NSA_KV_HEADS)
    kw, vw = rms_norm(hd(kw, NSA_KV_HEADS), qk_g[5]), hd(vw, NSA_KV_HEADS)
    gates = jax.nn.sigmoid(gl.astype(jnp.float32)).astype(h.dtype).reshape(b, t, NSA_HEADS, 3)
    return mq, mk, mv, nq, kc, vc, ks, vs, kw, vw, gates


def moba_attention(q, k, v, q_pos, bias_h):
    b, L = k.shape[:2]
    nb = -(-L // MOBA_BLOCK)
    pad = ((0, 0), (0, nb * MOBA_BLOCK - L), (0, 0), (0, 0))
    kb = jnp.pad(k, pad).reshape(b, nb, MOBA_BLOCK, MOBA_HEADS, HEAD_DIM)
    vb = jnp.pad(v, pad).reshape(b, nb, MOBA_BLOCK, MOBA_HEADS, HEAD_DIM)
    k_mean = jnp.mean(kb.astype(jnp.float32), axis=2).astype(k.dtype)
    n_top = min(MOBA_TOPK, nb)
    bi = jnp.arange(b)[:, None, None, None]
    hi = jnp.arange(MOBA_HEADS)[None, :, None, None]
    offs = jnp.arange(MOBA_BLOCK, dtype=jnp.int32)

    def chunk(pc, qc):
        own = pc // MOBA_BLOCK
        gate = jnp.einsum("bqhd,bnhd->bhqn", qc, k_mean).astype(jnp.float32)
        past = jnp.arange(nb)[None, :] < own[:, None]
        gate = jnp.where(past, gate, -jnp.inf)
        top_v, top_i = lax.top_k(gate, n_top)
        own_b = jnp.broadcast_to(own[None, None, :, None], top_i.shape[:3] + (1,)).astype(top_i.dtype)
        blocks = jnp.concatenate([top_i, own_b], axis=-1)
        blk_ok = jnp.concatenate([jnp.isfinite(top_v), jnp.ones(own_b.shape, bool)], axis=-1)
        kg = kb[bi, blocks, :, hi]
        vg = vb[bi, blocks, :, hi]
        dist = pc[None, None, :, None, None] - (blocks[..., None] * MOBA_BLOCK + offs)
        mask = blk_ok[..., None] & (dist >= 0)
        bias = bias_h[hi[..., None], rel_bucket(dist)]
        logits = jnp.einsum("bqhd,bhqjsd->bhqjs", qc, kg).astype(jnp.float32) * SCALE + bias
        shp = logits.shape[:3] + (-1,)
        p = masked_softmax(logits.reshape(shp), mask.reshape(shp))
        return jnp.einsum("bhqm,bhqmd->bqhd", p.astype(v.dtype), vg.reshape(shp + (HEAD_DIM,)))

    return map_query_chunks(chunk, q_pos, q)


def nsa_compress(x, pos_emb, w1, w2):
    b, L = x.shape[:2]
    nc = L // CMP_BLOCK
    blk = x[:, :nc * CMP_BLOCK].reshape(b, nc, CMP_BLOCK, NSA_KV_HEADS, HEAD_DIM) + pos_emb[:, None, :]
    flat = blk.transpose(0, 1, 3, 2, 4).reshape(b, nc, NSA_KV_HEADS, CMP_BLOCK * HEAD_DIM)
    return jax.nn.gelu(flat @ w1) @ w2


def nsa_cmp_sel(q, kc, vc, ks, vs, q_pos, cmp_pos, cmp_w1, cmp_w2, g_kc, bias_h):
    b, tq = q.shape[:2]
    L = ks.shape[1]
    qg = q.reshape(b, tq, NSA_KV_HEADS, NSA_GROUP, HEAD_DIM)
    k_cmp = rms_norm(nsa_compress(kc, cmp_pos[0], cmp_w1[0], cmp_w2[0]), g_kc)
    v_cmp = nsa_compress(vc, cmp_pos[1], cmp_w1[1], cmp_w2[1])
    nc = k_cmp.shape[1]
    dist_c = q_pos[:, None] - ((jnp.arange(nc, dtype=jnp.int32) + 1) * CMP_BLOCK - 1)[None, :]
    bias_c = bias_h[:, rel_bucket(dist_c)].reshape(NSA_KV_HEADS, NSA_GROUP, tq, nc)
    logits_c = jnp.einsum("bqkgd,bnkd->bkgqn", qg, k_cmp).astype(jnp.float32) * SCALE + bias_c
    p_cmp = masked_softmax(logits_c, dist_c >= 0)
    o_cmp = jnp.einsum("bkgqn,bnkd->bqkgd", p_cmp.astype(vc.dtype), v_cmp).reshape(b, tq, NSA_HEADS, HEAD_DIM)
    ns = -(-L // SEL_BLOCK)
    ratio = SEL_BLOCK // CMP_BLOCK
    imp = jnp.pad(jnp.sum(p_cmp, axis=2), ((0, 0), (0, 0), (0, 0), (0, ns * ratio - nc)))
    imp = imp.reshape(b, NSA_KV_HEADS, tq, ns, ratio).sum(-1).transpose(0, 2, 1, 3)
    pad = ((0, 0), (0, ns * SEL_BLOCK - L), (0, 0), (0, 0))
    ksb = jnp.pad(ks, pad).reshape(b, ns, SEL_BLOCK, NSA_KV_HEADS, HEAD_DIM)
    vsb = jnp.pad(vs, pad).reshape(b, ns, SEL_BLOCK, NSA_KV_HEADS, HEAD_DIM)
    n_top = min(SEL_TOPK, ns)
    bi = jnp.arange(b)[:, None, None, None]
    ki = jnp.arange(NSA_KV_HEADS)[None, :, None, None]
    bias_g = bias_h.reshape(NSA_KV_HEADS, NSA_GROUP, NUM_BUCKETS)
    ki6 = jnp.arange(NSA_KV_HEADS)[None, :, None, None, None, None]
    gi6 = jnp.arange(NSA_GROUP)[None, None, :, None, None, None]
    offs = jnp.arange(SEL_BLOCK, dtype=jnp.int32)

    def chunk(pc, qc, ic):
        own = pc // SEL_BLOCK
        past = jnp.arange(ns)[None, :] < own[:, None]
        score = jnp.where(past[None, :, None, :], ic, -jnp.inf).transpose(0, 2, 1, 3)
        top_v, top_i = lax.top_k(score, n_top)
        own_b = jnp.broadcast_to(own[None, None, :, None], top_i.shape[:3] + (1,)).astype(top_i.dtype)
        blocks = jnp.concatenate([top_i, own_b], axis=-1)
        blk_ok = jnp.concatenate([jnp.isfinite(top_v), jnp.ones(own_b.shape, bool)], axis=-1)
        kg = ksb[bi, blocks, :, ki]
        vg = vsb[bi, blocks, :, ki]
        dist = pc[None, None, :, None, None] - (blocks[..., None] * SEL_BLOCK + offs)
        mask = (blk_ok[..., None] & (dist >= 0))[:, :, None]
        bias = bias_g[ki6, gi6, rel_bucket(dist)[:, :, None]]
        logits = jnp.einsum("bqkgd,bkqjsd->bkgqjs", qc, kg).astype(jnp.float32) * SCALE + bias
        p = masked_softmax(logits.reshape(logits.shape[:4] + (-1,)), mask.reshape(mask.shape[:4] + (-1,)))
        return jnp.einsum("bkgqm,bkqmd->bqkgd", p.astype(vs.dtype), vg.reshape(vg.shape[:3] + (-1, HEAD_DIM)))

    o_sel = map_query_chunks(chunk, q_pos, qg, imp).reshape(b, tq, NSA_HEADS, HEAD_DIM)
    return o_cmp, o_sel


def window_attend(q, k, v, q_pos, k_pos, bias_h):
    n, tq, tk = q.shape[1], q.shape[2], k.shape[2]
    dist = q_pos[:, :, None] - k_pos[:, None, :]
    mask = (dist >= 0) & (dist <= WINDOW) & (k_pos[:, None, :] >= 0)
    bias = bias_h[:, rel_bucket(dist)].reshape(NSA_KV_HEADS, NSA_GROUP, n, tq, tk)
    logits = jnp.einsum("bnqkgd,bntkd->bkgnqt", q, k).astype(jnp.float32) * SCALE + bias
    p = masked_softmax(logits, mask)
    return jnp.einsum("bkgnqt,bntkd->bnqkgd", p.astype(v.dtype), v)


def window_prompt(q, k, v, bias_h):
    b, s = q.shape[:2]
    nqb = s // WIN_QBLOCK
    idx = jnp.arange(nqb, dtype=jnp.int32)[:, None] * WIN_QBLOCK + jnp.arange(WINDOW + WIN_QBLOCK, dtype=jnp.int32)[None, :]
    pad = ((0, 0), (WINDOW, 0), (0, 0), (0, 0))
    kb = jnp.pad(k, pad)[:, idx]
    vb = jnp.pad(v, pad)[:, idx]
    qb = q.reshape(b, nqb, WIN_QBLOCK, NSA_KV_HEADS, NSA_GROUP, HEAD_DIM)
    q_pos = jnp.arange(s, dtype=jnp.int32).reshape(nqb, WIN_QBLOCK)
    out = window_attend(qb, kb, vb, q_pos, idx - WINDOW, bias_h)
    return out.reshape(b, s, NSA_HEADS, HEAD_DIM)


def attn_core(mq, nq, gates, mk, mv, kc, vc, ks, vs, o_win, q_pos, cmp_pos, cmp_w1, cmp_w2, g_kc, bias_moba, bias_nsa):
    o_moba = moba_attention(mq, mk, mv, q_pos, bias_moba)
    o_cmp, o_sel = nsa_cmp_sel(nq, kc, vc, ks, vs, q_pos, cmp_pos, cmp_w1, cmp_w2, g_kc, bias_nsa)
    o_nsa = gates[..., 0:1] * o_cmp + gates[..., 1:2] * o_sel + gates[..., 2:3] * o_win
    b, t = mq.shape[:2]
    return jnp.concatenate([o_moba, o_nsa], axis=2).reshape(b, t, ATTN_HEADS * HEAD_DIM)


def _complex_affine_combine(e1, e2):
    a1r, a1i, b1r, b1i = e1
    a2r, a2i, b2r, b2i = e2
    return (a2r * a1r - a2i * a1i, a2r * a1i + a2i * a1r,
            a2r * b1r - a2i * b1i + b2r, a2r * b1i + a2i * b1r + b2i)


def s5_mixer(u, h0, a_re, a_im, log_dt, b_re, b_im, c_re, c_im, d_skip, w_glu):
    f32 = jnp.float32
    bsz, t, d = u.shape
    a_re, a_im = a_re.astype(f32), a_im.astype(f32)
    dt = jnp.exp(log_dt.astype(f32))[:, None]
    decay = jnp.exp(dt * a_re)
    ab_re, ab_im = decay * jnp.cos(dt * a_im), decay * jnp.sin(dt * a_im)
    den = a_re * a_re + a_im * a_im
    f_re = ((ab_re - 1) * a_re + ab_im * a_im) / den
    f_im = (ab_im * a_re - (ab_re - 1) * a_im) / den
    br, bim = b_re.astype(f32), b_im.astype(f32)
    bb_re = f_re[..., None] * br - f_im[..., None] * bim
    bb_im = f_re[..., None] * bim + f_im[..., None] * br
    uf = u.astype(f32)
    ug = uf.reshape(bsz, t, S5_GROUPS, S5_GROUP_CH)
    bu_re = jnp.einsum("btgc,gnc->tbgn", ug, bb_re)
    bu_im = jnp.einsum("btgc,gnc->tbgn", ug, bb_im)
    h_re, h_im = h0[:, 0].astype(f32), h0[:, 1].astype(f32)
    bu_re = bu_re.at[0].add(ab_re * h_re - ab_im * h_im)
    bu_im = bu_im.at[0].add(ab_re * h_im + ab_im * h_re)
    a_seq_re = jnp.broadcast_to(ab_re, (t, 1) + ab_re.shape)
    a_seq_im = jnp.broadcast_to(ab_im, (t, 1) + ab_im.shape)
    _, _, x_re, x_im = lax.associative_scan(_complex_affine_combine, (a_seq_re, a_seq_im, bu_re, bu_im), axis=0)
    y = (jnp.einsum("tbgn,gcn->btgc", x_re, c_re.astype(f32))
         - jnp.einsum("tbgn,gcn->btgc", x_im, c_im.astype(f32)))
    y = y.reshape(bsz, t, d) + d_skip.astype(f32) * uf
    z = jax.nn.gelu(y) @ w_glu.astype(f32)
    out = z[..., :d] * jax.nn.sigmoid(z[..., d:])
    new_state = jnp.stack([x_re[-1], x_im[-1]], axis=1)
    return out.astype(u.dtype), new_state.astype(h0.dtype)


def mlp_sublayer(x, c, g, ada_w, ada_b, w1, w2):
    shift, scale, gate = adaln(c, ada_w, ada_b)
    h = modulate(x, g, shift, scale)
    return x + gate * (jnp.square(jax.nn.relu(h @ w1)) @ w2)


def setup_inputs(seed: int = 0) -> dict:
    key = jax.random.key(seed)
    keys = list(jax.random.split(key, 48))

    def normal(shape, scale=1.0):
        r = jax.random.normal(keys.pop(), shape, jnp.float32)
        return r if scale == 1.0 else scale * r

    def gain(shape):
        return 1.0 + normal(shape, 0.02)

    d = D_MODEL
    na, nsl = N_ATTN_LAYERS, N_SSM_LAYERS
    n_pages = PAST_LEN // PAGE_SIZE
    n_pool = (DEC_BATCH * n_pages * 5) // 4
    w_eff = min(WINDOW, PAST_LEN)
    page_table = jax.random.permutation(keys.pop(), n_pool)[: DEC_BATCH * n_pages].reshape(DEC_BATCH, n_pages).astype(jnp.int32)
    a_im0 = math.pi * jnp.arange(S5_STATE, dtype=jnp.float32)
    return {
        "x_prompt": normal((BATCH, SEQ, d)),
        "x_sample": normal((DEC_BATCH, DEC_SEQ, d)),
        "cache_moba_kv": normal((na, n_pool, PAGE_SIZE, 2, MOBA_HEADS, HEAD_DIM)),
        "cache_nsa_kv": normal((na, n_pool, PAGE_SIZE, 4, NSA_KV_HEADS, HEAD_DIM)),
        "state_nsa_win": normal((na, DEC_BATCH, w_eff, 2, NSA_KV_HEADS, HEAD_DIM)),
        "state_s5": normal((nsl, DEC_BATCH, 2, S5_GROUPS, S5_STATE), 0.5),
        "page_table": page_table,
        "c_prompt": normal((BATCH, d)),
        "c_sample": normal((DEC_BATCH, d)),
        "rel_bias": normal((NUM_BUCKETS, ATTN_HEADS), 0.2),
        "attn_norm_g": gain((na, d)),
        "attn_ada_w": normal((na, d, 3 * d), 0.5 * d ** -0.5),
        "attn_ada_b": normal((na, 3 * d), 0.02),
        "attn_w_in": normal((na, d, IN_COLS), d ** -0.5),
        "attn_qk_g": gain((na, 6, HEAD_DIM)),
        "nsa_cmp_pos": normal((na, 2, CMP_BLOCK, HEAD_DIM), 0.1),
        "nsa_cmp_w1": normal((na, 2, CMP_BLOCK * HEAD_DIM, CMP_HIDDEN), (CMP_BLOCK * HEAD_DIM) ** -0.5),
        "nsa_cmp_w2": normal((na, 2, CMP_HIDDEN, HEAD_DIM), CMP_HIDDEN ** -0.5),
        "attn_w_out": normal((na, ATTN_HEADS * HEAD_DIM, d), (ATTN_HEADS * HEAD_DIM) ** -0.5),
        "ssm_norm_g": gain((nsl, d)),
        "ssm_ada_w": normal((nsl, d, 3 * d), 0.5 * d ** -0.5),
        "ssm_ada_b": normal((nsl, 3 * d), 0.02),
        "s5_a_re": -0.5 + normal((nsl, S5_GROUPS, S5_STATE), 0.01),
        "s5_a_im": a_im0 + normal((nsl, S5_GROUPS, S5_STATE), 0.01),
        "s5_log_dt": jax.random.uniform(keys.pop(), (nsl, S5_GROUPS), jnp.float32, math.log(DT_MIN), math.log(DT_MAX)),
        "s5_b_re": normal((nsl, S5_GROUPS, S5_STATE, S5_GROUP_CH), (2 * S5_GROUP_CH) ** -0.5),
        "s5_b_im": normal((nsl, S5_GROUPS, S5_STATE, S5_GROUP_CH), (2 * S5_GROUP_CH) ** -0.5),
        "s5_c_re": normal((nsl, S5_GROUPS, S5_GROUP_CH, S5_STATE), S5_STATE ** -0.5),
        "s5_c_im": normal((nsl, S5_GROUPS, S5_GROUP_CH, S5_STATE), S5_STATE ** -0.5),
        "s5_d": normal((nsl, d)),
        "s5_w_glu": normal((nsl, d, 2 * d), d ** -0.5),
        "mlp_norm_g": gain((DEPTH, d)),
        "mlp_ada_w": normal((DEPTH, d, 3 * d), 0.5 * d ** -0.5),
        "mlp_ada_b": normal((DEPTH, 3 * d), 0.02),
        "mlp_w1": normal((DEPTH, d, D_FF), d ** -0.5),
        "mlp_w2": normal((DEPTH, D_FF, d), D_FF ** -0.5),
    }


def reference(x_prompt, x_sample, cache_moba_kv, cache_nsa_kv, state_nsa_win, state_s5, page_table,
              c_prompt, c_sample, rel_bias, attn_norm_g, attn_ada_w, attn_ada_b, attn_w_in, attn_qk_g,
              nsa_cmp_pos, nsa_cmp_w1, nsa_cmp_w2, attn_w_out, ssm_norm_g, ssm_ada_w, ssm_ada_b,
              s5_a_re, s5_a_im, s5_log_dt, s5_b_re, s5_b_im, s5_c_re, s5_c_im, s5_d, s5_w_glu,
              mlp_norm_g, mlp_ada_w, mlp_ada_b, mlp_w1, mlp_w2):
    bias_t = rel_bias.T
    bias_moba, bias_nsa = bias_t[:MOBA_HEADS], bias_t[MOBA_HEADS:]
    bp, seq = x_prompt.shape[:2]
    bs, dec_seq = x_sample.shape[:2]
    pos_p = jnp.arange(seq, dtype=jnp.int32)
    pos_s = PAST_LEN + jnp.arange(dec_seq, dtype=jnp.int32)
    xp, xs = x_prompt, x_sample
    moba_p, moba_s, nsa_p, nsa_s, win_p, win_s, s5_p, s5_s = [], [], [], [], [], [], [], []
    cat = lambda a, b: jnp.concatenate([a, b], axis=1)
    for layer in range(DEPTH):
        if layer % 2 == 0:
            ia = layer // 2
            cmp_args = (nsa_cmp_pos[ia], nsa_cmp_w1[ia], nsa_cmp_w2[ia], attn_qk_g[ia, 3], bias_moba, bias_nsa)
            shift, scale, gate = adaln(c_prompt, attn_ada_w[ia], attn_ada_b[ia])
            h = modulate(xp, attn_norm_g[ia], shift, scale)
            mq, mk, mv, nq, kc, vc, ks, vs, kw, vw, gt = attn_project(h, attn_w_in[ia], attn_qk_g[ia])
            o_win = window_prompt(nq, kw, vw, bias_nsa)
            o = attn_core(mq, nq, gt, mk, mv, kc, vc, ks, vs, o_win, pos_p, *cmp_args)
            xp = xp + gate * (o @ attn_w_out[ia])
            moba_p.append(jnp.stack([mk, mv], axis=2).reshape(bp, seq // PAGE_SIZE, PAGE_SIZE, 2, MOBA_HEADS, HEAD_DIM))
            nsa_p.append(jnp.stack([kc, vc, ks, vs], axis=2).reshape(bp, seq // PAGE_SIZE, PAGE_SIZE, 4, NSA_KV_HEADS, HEAD_DIM))
            win_p.append(jnp.stack([kw, vw], axis=2)[:, seq - min(WINDOW, seq):])
            shift, scale, gate = adaln(c_sample, attn_ada_w[ia], attn_ada_b[ia])
            h = modulate(xs, attn_norm_g[ia], shift, scale)
            mq, mk, mv, nq, kc, vc, ks, vs, kw, vw, gt = attn_project(h, attn_w_in[ia], attn_qk_g[ia])
            past_m = cache_moba_kv[ia, page_table].reshape(bs, -1, 2, MOBA_HEADS, HEAD_DIM)
            past_n = cache_nsa_kv[ia, page_table].reshape(bs, -1, 4, NSA_KV_HEADS, HEAD_DIM)
            win = state_nsa_win[ia]
            w_eff = win.shape[1]
            kw_all, vw_all = cat(win[:, :, 0], kw), cat(win[:, :, 1], vw)
            kpos_w = (PAST_LEN - w_eff + jnp.arange(w_eff + dec_seq, dtype=jnp.int32))[None]
            o_win = window_attend(nq.reshape(bs, 1, dec_seq, NSA_KV_HEADS, NSA_GROUP, HEAD_DIM),
                                  kw_all[:, None], vw_all[:, None], pos_s[None], kpos_w,
                                  bias_nsa).reshape(bs, dec_seq, NSA_HEADS, HEAD_DIM)
            o = attn_core(mq, nq, gt, cat(past_m[:, :, 0], mk), cat(past_m[:, :, 1], mv),
                          cat(past_n[:, :, 0], kc), cat(past_n[:, :, 1], vc),
                          cat(past_n[:, :, 2], ks), cat(past_n[:, :, 3], vs), o_win, pos_s, *cmp_args)
            xs = xs + gate * (o @ attn_w_out[ia])
            moba_s.append(jnp.stack([mk, mv], axis=2))
            nsa_s.append(jnp.stack([kc, vc, ks, vs], axis=2))
            win_s.append(jnp.stack([kw_all, vw_all], axis=2)[:, dec_seq:])
        else:
            isx = layer // 2
            s5_args = (s5_a_re[isx], s5_a_im[isx], s5_log_dt[isx], s5_b_re[isx], s5_b_im[isx],
                       s5_c_re[isx], s5_c_im[isx], s5_d[isx], s5_w_glu[isx])
            shift, scale, gate = adaln(c_prompt, ssm_ada_w[isx], ssm_ada_b[isx])
            h = modulate(xp, ssm_norm_g[isx], shift, scale)
            y, st = s5_mixer(h, jnp.zeros((bp, 2, S5_GROUPS, S5_STATE), h.dtype), *s5_args)
            xp = xp + gate * y
            s5_p.append(st)
            shift, scale, gate = adaln(c_sample, ssm_ada_w[isx], ssm_ada_b[isx])
            h = modulate(xs, ssm_norm_g[isx], shift, scale)
            y, st = s5_mixer(h, state_s5[isx], *s5_args)
            xs = xs + gate * y
            s5_s.append(st)
        xp = mlp_sublayer(xp, c_prompt, mlp_norm_g[layer], mlp_ada_w[layer], mlp_ada_b[layer], mlp_w1[layer], mlp_w2[layer])
        xs = mlp_sublayer(xs, c_sample, mlp_norm_g[layer], mlp_ada_w[layer], mlp_ada_b[layer], mlp_w1[layer], mlp_w2[layer])
    return (xp, xs, jnp.stack(moba_p), jnp.stack(moba_s), jnp.stack(nsa_p), jnp.stack(nsa_s),
            jnp.stack(win_p), jnp.stack(win_s), jnp.stack(s5_p), jnp.stack(s5_s))
```

```python
import functools
import math

import jax
import jax.numpy as jnp
from jax import lax
from jax.experimental import pallas as pl
from jax.experimental.pallas import tpu as pltpu

F32 = jnp.float32
BF16 = jnp.bfloat16
HIGHEST = lax.Precision.HIGHEST

D = 1024
HEAD_DIM = 64
MOBA_HEADS = 8
NSA_HEADS = 8
NSA_GROUP = 4
MOBA_BLOCK = 256
MOBA_TOPK = 3
CMP_BLOCK = 32
CMP_HIDDEN = 128
SEL_BLOCK = 64
SEL_TOPK = 16
WINDOW = 512
NUM_BUCKETS = 32
MAX_DISTANCE = 128
PAGE = 128
PAST_LEN = 8192
D_FF = 4 * D
S5_GROUPS = 64
S5_STATE = 64
S5_GROUP_CH = 16
IN_COLS = 3 * 512 + 512 + 6 * 128 + 3 * NSA_HEADS
IN_COLS_PAD = 23 * 128
EPS = 1e-6
SCALE = HEAD_DIM ** -0.5
LANES = 128
TQ = 256
NEG = -1e30
M_INIT = -1e15
VMEM_LIMIT = 56 * 1024 * 1024

_NT = (((1,), (1,)), ((), ()))


def _cparams(sem):
    return pltpu.CompilerParams(dimension_semantics=sem, vmem_limit_bytes=VMEM_LIMIT)


def _dot(a, b, **kw):
    return jnp.dot(a, b, preferred_element_type=F32, **kw)


def _dot_nt(a, b, **kw):
    return lax.dot_general(a, b, _NT, preferred_element_type=F32, **kw)


def _modulate(x, g, shift, scale):
    ms = jnp.mean(x * x, axis=-1, keepdims=True)
    return x * lax.rsqrt(ms + EPS) * g * (1.0 + scale) + shift


def _group_mean_sq(z, avg):
    sq = z * z
    hi = sq.astype(BF16)
    lo = (sq - hi.astype(F32)).astype(BF16)
    return _dot(hi, avg) + _dot(lo, avg)


def _col(x, lane, idx):
    return jnp.sum(jnp.where(lane == idx, x, 0.0), axis=1, keepdims=True)


def _adaln_kernel(c_ref, w_ref, b_ref, o_ref):
    c = c_ref[...]
    s = c * jax.nn.sigmoid(c)
    o_ref[0] = _dot(s, w_ref[0], precision=HIGHEST) + b_ref[0]


def _adaln(c_all, w, b):
    nl, n = w.shape[0], c_all.shape[0]
    return pl.pallas_call(
        _adaln_kernel,
        grid=(nl, 3),
        in_specs=[pl.BlockSpec((n, D), lambda l, j: (0, 0)),
                  pl.BlockSpec((1, D, D), lambda l, j: (l, 0, j)),
                  pl.BlockSpec((1, 1, D), lambda l, j: (l, 0, j))],
        out_specs=pl.BlockSpec((1, n, D), lambda l, j: (l, 0, j)),
        out_shape=jax.ShapeDtypeStruct((nl, n, 3 * D), F32),
        compiler_params=_cparams(("arbitrary", "arbitrary")),
        name="adaln",
    )(c_all, w, b.reshape(nl, 1, 3 * D))


def _bias_kernel(rb_ref, d_ref, o_ref):
    h = pl.program_id(0)
    dist = d_ref[...]
    n = jnp.maximum(dist, 0)
    max_exact = NUM_BUCKETS // 2
    nf = jnp.maximum(n, 1).astype(F32)
    large = max_exact + (jnp.log(nf / max_exact) / math.log(MAX_DISTANCE / max_exact)
                         * (NUM_BUCKETS - max_exact)).astype(jnp.int32)
    large = jnp.minimum(large, NUM_BUCKETS - 1)
    bucket = jnp.where(n < max_exact, n, large)
    acc = jnp.zeros(dist.shape, F32)
    for k in range(NUM_BUCKETS):
        acc = jnp.where(bucket == k, rb_ref[k, h], acc)
    o_ref[0] = jnp.where(dist < 0, NEG, acc)


def _bias_table(rel_bias, dist):
    r, c = dist.shape
    nh = rel_bias.shape[1]
    return pl.pallas_call(
        _bias_kernel,
        grid=(nh,),
        in_specs=[pl.BlockSpec(memory_space=pltpu.SMEM),
                  pl.BlockSpec((r, c), lambda h: (0, 0))],
        out_specs=pl.BlockSpec((1, r, c), lambda h: (h, 0, 0)),
        out_shape=jax.ShapeDtypeStruct((nh, r, c), F32),
        compiler_params=_cparams(("arbitrary",)),
        name="bias_table",
    )(rel_bias, dist)


def _proj_kernel(x_ref, mod_ref, g_ref, w_ref, qkg_ref, avg_ref,
                 mq_ref, mkv_ref, nq_ref, nkv_ref, wkv_ref, gt_ref):
    x = x_ref[0]
    mod = mod_ref[0]
    h = _modulate(x, g_ref[...], mod[:, 0:D], mod[:, D:2 * D])
    z = _dot(h.astype(BF16), w_ref[...])
    avg = avg_ref[...]

    def normed(lo, gi):
        zs = z[:, lo:lo + LANES]
        return zs * lax.rsqrt(_group_mean_sq(zs, avg) + EPS) * qkg_ref[gi:gi + 1, :]

    for t in range(4):
        mq_ref[0, :, t * LANES:(t + 1) * LANES] = normed(t * LANES, 0) * SCALE
        mkv_ref[0, :, t * LANES:(t + 1) * LANES] = normed(512 + t * LANES, 1)
        nq_ref[0, :, t * LANES:(t + 1) * LANES] = normed(1536 + t * LANES, 2) * SCALE
    mkv_ref[0, :, 512:1024] = z[:, 1024:1536]
    nkv_ref[0, :, 0:256] = z[:, 2048:2304]
    nkv_ref[0, :, 256:384] = normed(2304, 4)
    nkv_ref[0, :, 384:512] = z[:, 2432:2560]
    wkv_ref[0, :, 0:128] = normed(2560, 5)
    wkv_ref[0, :, 128:256] = z[:, 2688:2816]
    gt_ref[0] = jax.nn.sigmoid(z[:, 2816:2944])


def _attn_proj(x, mod, g, w_pad, qkg_t, avg, tm):
    b, t, _ = x.shape
    tmod = mod.shape[1]
    row = lambda width: pl.BlockSpec((1, tm, width), lambda i, j: (i, j, 0))
    shp = lambda width: jax.ShapeDtypeStruct((b, t, width), F32)
    return pl.pallas_call(
        _proj_kernel,
        grid=(b, t // tm),
        in_specs=[row(D),
                  pl.BlockSpec((1, tmod, 3 * D), lambda i, j: (i, 0, 0)),
                  pl.BlockSpec((1, D), lambda i, j: (0, 0)),
                  pl.BlockSpec((D, IN_COLS_PAD), lambda i, j: (0, 0)),
                  pl.BlockSpec((8, LANES), lambda i, j: (0, 0)),
                  pl.BlockSpec((LANES, LANES), lambda i, j: (0, 0))],
        out_specs=[row(512), row(1024), row(512), row(512), row(256), row(128)],
        out_shape=[shp(512), shp(1024), shp(512), shp(512), shp(256), shp(128)],
        compiler_params=_cparams(("arbitrary", "arbitrary")),
        name="attn_proj",
    )(x, mod, g, w_pad, qkg_t, avg)


def _moba_prompt_kernel(q_ref, k_ref, v_ref, t_ref, o_ref, km_scr):
    s_len = q_ref.shape[1]
    nblk = s_len // MOBA_BLOCK
    lane = lax.broadcasted_iota(jnp.int32, (TQ, LANES), 1)
    km_scr[...] = jnp.zeros(km_scr.shape, F32)
    for n in range(nblk):
        km_scr[n:n + 1, :] = jnp.mean(k_ref[0, n * MOBA_BLOCK:(n + 1) * MOBA_BLOCK, :], axis=0, keepdims=True)
    kmean = km_scr[...]

    def qtile(i, _):
        r0 = pl.multiple_of(i * TQ, TQ)
        q2 = q_ref[0, pl.ds(r0, TQ), :]
        outs = []
        for e in range(2):
            qe = jnp.where(lane // HEAD_DIM == e, q2, 0.0)
            gate = _dot_nt(qe, kmean, precision=HIGHEST)
            gm = jnp.where(lane < i, gate, -jnp.inf)
            rank = jnp.zeros((TQ, LANES), F32)
            for m in range(nblk):
                col = gm[:, m:m + 1]
                beats = (col > gm) | ((col == gm) & (m < lane))
                rank = rank + jnp.where(beats, 1.0, 0.0)
            sel = ((rank < MOBA_TOPK) & (lane < i)) | (lane == i)
            colbias = jnp.where(sel, 0.0, NEG)
            qb = qe.astype(BF16)

            def kstep(n, carry):
                m_, l_, acc = carry
                c0 = pl.multiple_of(n * TQ, TQ)
                kt = k_ref[0, pl.ds(c0, TQ), :].astype(BF16)
                vt = v_ref[0, pl.ds(c0, TQ), :].astype(BF16)
                s = _dot_nt(qb, kt) + t_ref[e, jnp.minimum(i - n, 2)] + _col(colbias, lane, n)
                mn = jnp.maximum(m_, jnp.max(s, axis=1, keepdims=True))
                alpha = jnp.exp(m_ - mn)
                p = jnp.exp(s - mn)
                l_ = alpha * l_ + jnp.sum(p, axis=1, keepdims=True)
                acc = alpha * acc + _dot(p.astype(BF16), vt)
                return mn, l_, acc

            init = (jnp.full((TQ, 1), M_INIT, F32), jnp.zeros((TQ, 1), F32), jnp.zeros((TQ, LANES), F32))
            _, l_, acc = lax.fori_loop(0, i + 1, kstep, init)
            outs.append(acc / l_)
        o_ref[0, pl.ds(r0, TQ), :] = jnp.where(lane < HEAD_DIM, outs[0], outs[1])
        return 0

    lax.fori_loop(0, s_len // TQ, qtile, 0)


def _moba_prompt(mq, mkv, tb):
    b, s, _ = mq.shape
    npair = MOBA_HEADS // 2
    return pl.pallas_call(
        _moba_prompt_kernel,
        grid=(b, npair),
        in_specs=[pl.BlockSpec((1, s, LANES), lambda i, p: (i, 0, p)),
                  pl.BlockSpec((1, s, LANES), lambda i, p: (i, 0, p)),
                  pl.BlockSpec((1, s, LANES), lambda i, p: (i, 0, npair + p)),
                  pl.BlockSpec((2, 4, TQ, TQ), lambda i, p: (p, 0, 0, 0))],
        out_specs=pl.BlockSpec((1, s, LANES), lambda i, p: (i, 0, p)),
        out_shape=jax.ShapeDtypeStruct((b, s, 512), F32),
        scratch_shapes=[pltpu.VMEM((LANES, LANES), F32)],
        compiler_params=_cparams(("arbitrary", "arbitrary")),
        name="moba_prompt",
    )(mq, mkv, mkv, tb)


def _compress_tokens(load_k, load_v, pos_ref, w1_ref, w2_ref):
    hk = hv = None
    for r in range(CMP_BLOCK):
        dk = _dot((load_k(r) + pos_ref[r:r + 1, 0:LANES]).astype(BF16), w1_ref[0, r])
        dv = _dot((load_v(r) + pos_ref[r:r + 1, LANES:2 * LANES]).astype(BF16), w1_ref[1, r])
        hk = dk if hk is None else hk + dk
        hv = dv if hv is None else hv + dv
    ck = _dot(jax.nn.gelu(hk).astype(BF16), w2_ref[0])
    cv = _dot(jax.nn.gelu(hv).astype(BF16), w2_ref[1])
    return ck, cv


def _cmp_prompt_kernel(xk_ref, xv_ref, pos_ref, w1_ref, w2_ref, gkc_ref, avg_ref, kc_ref, vc_ref):
    nblk = xk_ref.shape[1] // CMP_BLOCK
    ck, cv = _compress_tokens(lambda r: xk_ref[0, pl.ds(r, nblk, stride=CMP_BLOCK), :],
                              lambda r: xv_ref[0, pl.ds(r, nblk, stride=CMP_BLOCK), :], pos_ref, w1_ref, w2_ref)
    ck = ck * lax.rsqrt(_group_mean_sq(ck, avg_ref[...]) + EPS) * gkc_ref[...]
    kc_ref[0] = jnp.zeros((LANES, LANES), F32)
    vc_ref[0] = jnp.zeros((LANES, LANES), F32)
    kc_ref[0, 0:nblk, :] = ck
    vc_ref[0, 0:nblk, :] = cv


def _cmp_prompt(nkv, pos, w1bd, w2bd, gkc, avg):
    b, s, _ = nkv.shape
    const = lambda shape: pl.BlockSpec(shape, lambda i: (0,) * len(shape))
    return pl.pallas_call(
        _cmp_prompt_kernel,
        grid=(b,),
        in_specs=[pl.BlockSpec((1, s, LANES), lambda i: (i, 0, 0)), pl.BlockSpec((1, s, LANES), lambda i: (i, 0, 1)),
                  const((CMP_BLOCK, 256)), const((2, CMP_BLOCK, LANES, 256)), const((2, 256, LANES)),
                  const((1, LANES)), const((LANES, LANES))],
        out_specs=[pl.BlockSpec((1, LANES, LANES), lambda i: (i, 0, 0))] * 2,
        out_shape=[jax.ShapeDtypeStruct((b, LANES, LANES), F32)] * 2,
        compiler_params=_cparams(("arbitrary",)),
        name="nsa_compress_prompt",
    )(nkv, nkv, pos, w1bd, w2bd, gkc, avg)


def _nsa_prompt_kernel(q_ref, ks_ref, vs_ref, kw_ref, vw_ref, kc_ref, vc_ref, g_ref, t_ref, tc_ref,
                       o_ref, am_scr):
    s_len = q_ref.shape[1]
    k = pl.program_id(1)
    lane = lax.broadcasted_iota(jnp.int32, (TQ, LANES), 1)
    rowi = lax.broadcasted_iota(jnp.int32, (TQ, LANES), 0)
    kvmask = (lane // HEAD_DIM) == k
    kc = kc_ref[0].astype(BF16)
    vc = vc_ref[0].astype(BF16)
    e_r = lax.broadcasted_iota(jnp.int32, (LANES, TQ), 0)
    e_c = lax.broadcasted_iota(jnp.int32, (LANES, TQ), 1)
    nsel = s_len // SEL_BLOCK

    def online(s, vt, carry):
        m_, l_, acc = carry
        mn = jnp.maximum(m_, jnp.max(s, axis=1, keepdims=True))
        alpha = jnp.exp(m_ - mn)
        p = jnp.exp(s - mn)
        l_ = alpha * l_ + jnp.sum(p, axis=1, keepdims=True)
        acc = alpha * acc + _dot(p.astype(BF16), vt)
        return mn, l_, acc

    def qtile(i, _):
        r0 = pl.multiple_of(i * TQ, TQ)
        qs = []
        for h in range(NSA_GROUP):
            q2 = q_ref[0, pl.ds(r0, TQ), (h // 2) * LANES:(h // 2 + 1) * LANES]
            qa = jnp.where(k == (h % 2), q2, pltpu.roll(q2, HEAD_DIM, 1))
            qs.append(jnp.where(kvmask, qa, 0.0).astype(BF16))

        imp = jnp.zeros((TQ, LANES), F32)
        o_cmp = []
        for h in range(NSA_GROUP):
            s = _dot_nt(qs[h], kc) + tc_ref[h, pl.ds(r0, TQ), :]
            m = jnp.maximum(jnp.max(s, axis=1, keepdims=True), M_INIT)
            p = jnp.exp(s - m)
            p = p / jnp.maximum(jnp.sum(p, axis=1, keepdims=True), 1e-30)
            imp = imp + p
            o_cmp.append(_dot(p.astype(BF16), vc))

        imp2 = imp + pltpu.roll(imp, LANES - 1, 1)
        own2 = ((r0 + rowi) // SEL_BLOCK) * 2
        sc = jnp.where(lane < own2, imp2, -jnp.inf)
        rank = jnp.zeros((TQ, LANES), F32)
        for m in range(nsel):
            col = sc[:, 2 * m:2 * m + 1]
            beats = (col > sc) | ((col == sc) & (2 * m < lane))
            rank = rank + jnp.where(beats, 1.0, 0.0)
        sel = ((rank < SEL_TOPK) & (lane < own2)) | (lane == own2)
        selb = jnp.where(sel, 1.0, 0.0).astype(BF16)

        def mk_mask(n, _):
            expand = jnp.where(e_r == 2 * (n * (TQ // SEL_BLOCK) + e_c // SEL_BLOCK), 1.0, 0.0).astype(BF16)
            am_scr[n] = (_dot(selb, expand) - 1.0) * (-NEG)
            return 0

        lax.fori_loop(0, i + 1, mk_mask, 0)

        g = g_ref[0, pl.ds(r0, TQ), :]
        init = (jnp.full((TQ, 1), M_INIT, F32), jnp.zeros((TQ, 1), F32), jnp.zeros((TQ, LANES), F32))
        res = []
        for h in range(NSA_GROUP):
            def sel_step(n, carry):
                c0 = pl.multiple_of(n * TQ, TQ)
                kt = ks_ref[0, pl.ds(c0, TQ), :].astype(BF16)
                vt = vs_ref[0, pl.ds(c0, TQ), :].astype(BF16)
                s = _dot_nt(qs[h], kt) + t_ref[h, jnp.minimum(i - n, 2)] + am_scr[n]
                return online(s, vt, carry)

            _, l_, acc = lax.fori_loop(0, i + 1, sel_step, init)
            o_sel = acc / l_

            def win_step(n, carry):
                c0 = pl.multiple_of(n * TQ, TQ)
                kt = kw_ref[0, pl.ds(c0, TQ), :].astype(BF16)
                vt = vw_ref[0, pl.ds(c0, TQ), :].astype(BF16)
                d = i - n
                s = _dot_nt(qs[h], kt) + t_ref[h, jnp.where(d == 2, 3, d)]
                return online(s, vt, carry)

            _, l_, acc = lax.fori_loop(jnp.maximum(i - WINDOW // TQ, 0), i + 1, win_step, init)
            o_win = acc / l_

            hg = (k * NSA_GROUP + h) * 3
            o = _col(g, lane, hg) * o_cmp[h] + _col(g, lane, hg + 1) * o_sel + _col(g, lane, hg + 2) * o_win
            res.append(jnp.where(k == (h % 2), o, pltpu.roll(o, HEAD_DIM, 1)))
        for t in range(2):
            o_ref[0, pl.ds(r0, TQ), t * LANES:(t + 1) * LANES] = jnp.where(lane < HEAD_DIM, res[2 * t], res[2 * t + 1])
        return 0

    lax.fori_loop(0, s_len // TQ, qtile, 0)


def _nsa_prompt(nq, nkv, wkv, kcmp, vcmp, gates, tb, tc):
    b, s, _ = nq.shape
    col = lambda arr_cols, cb: pl.BlockSpec((1, s, LANES), lambda i, k: (i, 0, cb))
    return pl.pallas_call(
        _nsa_prompt_kernel,
        grid=(b, 2),
        in_specs=[pl.BlockSpec((1, s, 256), lambda i, k: (i, 0, k)),
                  col(512, 2), col(512, 3), col(256, 0), col(256, 1),
                  pl.BlockSpec((1, LANES, LANES), lambda i, k: (i, 0, 0)),
                  pl.BlockSpec((1, LANES, LANES), lambda i, k: (i, 0, 0)),
                  pl.BlockSpec((1, s, LANES), lambda i, k: (i, 0, 0)),
                  pl.BlockSpec((NSA_GROUP, 4, TQ, TQ), lambda i, k: (k, 0, 0, 0)),
                  pl.BlockSpec((NSA_GROUP, s, LANES), lambda i, k: (k, 0, 0))],
        out_specs=pl.BlockSpec((1, s, 256), lambda i, k: (i, 0, k)),
        out_shape=jax.ShapeDtypeStruct((b, s, 512), F32),
        scratch_shapes=[pltpu.VMEM((s // TQ, TQ, TQ), F32)],
        compiler_params=_cparams(("arbitrary", "arbitrary")),
        name="nsa_prompt",
    )(nq, nkv, nkv, wkv, wkv, kcmp, vcmp, gates, tb, tc)


PAGES_PER_STEP = 8


def _rank_lt(score, lane, ncand, topk):
    rank = jnp.zeros(score.shape, F32)
    for m in range(ncand):
        col = score[:, m:m + 1]
        beats = (col > score) | ((col == score) & (m < lane))
        rank = rank + jnp.where(beats, 1.0, 0.0)
    return rank < topk


def _merge_blocks(sel, m_all, l_all, acc_scr, nblk, s_self, v_self):
    mx = jnp.maximum(jnp.max(jnp.where(sel, m_all, NEG), axis=1, keepdims=True), s_self)
    w = jnp.exp(jnp.where(sel, m_all - mx, NEG))
    w_self = jnp.exp(s_self - mx)
    den = jnp.sum(w * l_all, axis=1, keepdims=True) + w_self
    num = w_self * v_self
    for j in range(nblk):
        num = num + w[:, j:j + 1] * acc_scr[j]
    return num / den


def _moba_sample_kernel(pt_ref, *refs):
    pages = refs[:PAGES_PER_STEP]
    qm_ref, kn_ref, vn_ref, tsb_ref, misc_ref, o_ref, km_scr, m_scr, l_scr, acc_scr = refs[PAGES_PER_STEP:]
    s = pl.program_id(1)
    nstep = pl.num_programs(1)
    nblk = PAST_LEN // MOBA_BLOCK
    width = MOBA_HEADS * HEAD_DIM
    qm = qm_ref[0]
    qb = qm.astype(BF16)
    lane = lax.broadcasted_iota(jnp.int32, (MOBA_HEADS, LANES), 1)
    rowk = lax.broadcasted_iota(jnp.int32, km_scr.shape, 0)

    @pl.when(s == 0)
    def _():
        km_scr[...] = jnp.zeros(km_scr.shape, F32)
        m_scr[...] = jnp.zeros(m_scr.shape, F32)
        l_scr[...] = jnp.zeros(l_scr.shape, F32)

    for j in range(PAGES_PER_STEP // 2):
        blk = s * (PAGES_PER_STEP // 2) + j
        kb = jnp.concatenate([pages[2 * j][0, :, 0:width], pages[2 * j + 1][0, :, 0:width]], axis=0)
        vb = jnp.concatenate([pages[2 * j][0, :, width:2 * width], pages[2 * j + 1][0, :, width:2 * width]], axis=0)
        km_scr[...] = jnp.where(rowk == blk, jnp.mean(kb, axis=0, keepdims=True), km_scr[...])
        sc = _dot_nt(qb, kb.astype(BF16)) + jnp.where(blk == nblk - 1, tsb_ref[...], misc_ref[:, 1:2])
        mj = jnp.max(sc, axis=1, keepdims=True)
        p = jnp.exp(sc - mj)
        m_scr[...] = jnp.where(lane == blk, mj, m_scr[...])
        l_scr[...] = jnp.where(lane == blk, jnp.sum(p, axis=1, keepdims=True), l_scr[...])
        acc_scr[blk] = _dot(p.astype(BF16), vb.astype(BF16))

    @pl.when(s == nstep - 1)
    def _():
        gate = _dot_nt(qm, km_scr[...], precision=HIGHEST)
        gm = jnp.where(lane < nblk, gate, -jnp.inf)
        sel = _rank_lt(gm, lane, nblk, MOBA_TOPK) & (lane < nblk)
        s_self = jnp.sum(qm * kn_ref[0], axis=1, keepdims=True) + misc_ref[:, 0:1]
        o = _merge_blocks(sel, m_scr[...], l_scr[...], acc_scr, nblk, s_self, vn_ref[0])
        hrow = lax.broadcasted_iota(jnp.int32, (MOBA_HEADS, width), 0)
        hlane = lax.broadcasted_iota(jnp.int32, (MOBA_HEADS, width), 1)
        o_ref[0] = jnp.sum(jnp.where(hlane // HEAD_DIM == hrow, o, 0.0), axis=0, keepdims=True)


def _moba_sample(page_table, cache, qmat, knew, vnew, tsb, misc):
    nb, npages = page_table.shape
    nstep = npages // PAGES_PER_STEP
    width = MOBA_HEADS * HEAD_DIM
    nblk = PAST_LEN // MOBA_BLOCK

    def page_spec(j):
        return pl.BlockSpec((1, PAGE, 2 * width), lambda b, s, pt: (pt[b, s * PAGES_PER_STEP + j], 0, 0))

    per_b = lambda shape: pl.BlockSpec((1,) + shape, lambda b, s, pt: (b, 0, 0))
    const = lambda shape: pl.BlockSpec(shape, lambda b, s, pt: (0,) * len(shape))
    grid_spec = pltpu.PrefetchScalarGridSpec(
        num_scalar_prefetch=1,
        grid=(nb, nstep),
        in_specs=[page_spec(j) for j in range(PAGES_PER_STEP)]
        + [per_b((MOBA_HEADS, width)), per_b((1, width)), per_b((1, width)),
           const((MOBA_HEADS, MOBA_BLOCK)), const((MOBA_HEADS, LANES))],
        out_specs=per_b((1, width)),
        scratch_shapes=[pltpu.VMEM((LANES, width), F32), pltpu.VMEM((MOBA_HEADS, LANES), F32),
                        pltpu.VMEM((MOBA_HEADS, LANES), F32), pltpu.VMEM((nblk, MOBA_HEADS, width), F32)],
    )
    return pl.pallas_call(
        _moba_sample_kernel,
        grid_spec=grid_spec,
        out_shape=jax.ShapeDtypeStruct((nb, 1, width), F32),
        compiler_params=_cparams(("arbitrary", "arbitrary")),
        name="moba_sample",
    )(page_table, *([cache] * PAGES_PER_STEP), qmat, knew, vnew, tsb, misc)


def _nsa_sample_kernel(pt_ref, *refs):
    pages = refs[:PAGES_PER_STEP]
    (qm_ref, ksn_ref, vsn_ref, kwn_ref, vwn_ref, win_ref, g_ref, tsn_ref, misc_ref, tcs_ref, tws_ref,
     pos_ref, w1_ref, w2_ref, gkc_ref, avg_ref, o_ref, xk_scr, xv_scr, m_scr, l_scr, acc_scr) = refs[PAGES_PER_STEP:]
    s = pl.program_id(1)
    nstep = pl.num_programs(1)
    nsel = PAST_LEN // SEL_BLOCK
    ncmp = PAST_LEN // CMP_BLOCK
    npage = PAST_LEN // PAGE
    qm = qm_ref[0]
    qb = qm.astype(BF16)
    lane = lax.broadcasted_iota(jnp.int32, (NSA_HEADS, LANES), 1)
    row = lax.broadcasted_iota(jnp.int32, (NSA_HEADS, LANES), 0)
    lo = lane < HEAD_DIM

    @pl.when(s == 0)
    def _():
        m_scr[...] = jnp.zeros(m_scr.shape, F32)
        l_scr[...] = jnp.zeros(l_scr.shape, F32)

    for j in range(PAGES_PER_STEP):
        pg = s * PAGES_PER_STEP + j
        r0 = pl.multiple_of(pg * PAGE, PAGE)
        xk_scr[pl.ds(r0, PAGE), :] = pages[j][0, :, 0:LANES]
        xv_scr[pl.ds(r0, PAGE), :] = pages[j][0, :, LANES:2 * LANES]
        ks = pages[j][0, :, 256:384].astype(BF16)
        vs = pages[j][0, :, 384:512].astype(BF16)
        sc = _dot_nt(qb, ks) + jnp.where(pg == npage - 1, tsn_ref[...], misc_ref[:, 1:2])
        m0 = jnp.max(jnp.where(lo, sc, NEG), axis=1, keepdims=True)
        m1 = jnp.max(jnp.where(lo, NEG, sc), axis=1, keepdims=True)
        p = jnp.exp(sc - jnp.where(lo, m0, m1))
        l0 = jnp.sum(jnp.where(lo, p, 0.0), axis=1, keepdims=True)
        l1 = jnp.sum(jnp.where(lo, 0.0, p), axis=1, keepdims=True)
        b0 = 2 * pg
        m_scr[...] = jnp.where(lane == b0, m0, jnp.where(lane == b0 + 1, m1, m_scr[...]))
        l_scr[...] = jnp.where(lane == b0, l0, jnp.where(lane == b0 + 1, l1, l_scr[...]))
        acc_scr[b0] = _dot(jnp.where(lo, p, 0.0).astype(BF16), vs)
        acc_scr[b0 + 1] = _dot(jnp.where(lo, 0.0, p).astype(BF16), vs)

    @pl.when(s == nstep - 1)
    def _():
        ck, cv = _compress_tokens(lambda r: xk_scr[pl.ds(r, ncmp, stride=CMP_BLOCK), :],
                                  lambda r: xv_scr[pl.ds(r, ncmp, stride=CMP_BLOCK), :], pos_ref, w1_ref, w2_ref)
        ck = ck * lax.rsqrt(_group_mean_sq(ck, avg_ref[...]) + EPS) * gkc_ref[...]
        sc = _dot_nt(qb, ck.astype(BF16)) + tcs_ref[...]
        m = jnp.maximum(jnp.max(sc, axis=1, keepdims=True), M_INIT)
        pc = jnp.exp(sc - m)
        pc = pc / jnp.maximum(jnp.sum(pc, axis=1, keepdims=True), 1e-30)
        o_cmp = _dot(pc.astype(BF16), cv.astype(BF16))
        g0 = pc[0:1] + pc[1:2] + pc[2:3] + pc[3:4]
        g1 = pc[4:5] + pc[5:6] + pc[6:7] + pc[7:8]
        rowc = lax.broadcasted_iota(jnp.int32, (NSA_HEADS, ncmp), 0)
        imp = jnp.where(rowc < NSA_GROUP, g0, g1)
        pr = lax.broadcasted_iota(jnp.int32, (ncmp, LANES), 0)
        pc_ = lax.broadcasted_iota(jnp.int32, (ncmp, LANES), 1)
        pair = jnp.where(pr // (SEL_BLOCK // CMP_BLOCK) == pc_, 1.0, 0.0)
        impb = _dot(imp, pair, precision=HIGHEST)
        own = PAST_LEN // SEL_BLOCK
        sel = _rank_lt(jnp.where(lane < own, impb, -jnp.inf), lane, nsel, SEL_TOPK) & (lane < own)
        s_self = jnp.sum(qm * ksn_ref[0], axis=1, keepdims=True) + misc_ref[:, 0:1]
        o_sel = _merge_blocks(sel, m_scr[...], l_scr[...], acc_scr, nsel, s_self, vsn_ref[0])
        kw = win_ref[0, :, 0:LANES].astype(BF16)
        vw = win_ref[0, :, LANES:2 * LANES].astype(BF16)
        sw = _dot_nt(qb, kw) + tws_ref[...]
        sw_self = jnp.sum(qm * kwn_ref[0], axis=1, keepdims=True) + misc_ref[:, 0:1]
        mw = jnp.maximum(jnp.max(sw, axis=1, keepdims=True), sw_self)
        pw = jnp.exp(sw - mw)
        pw_self = jnp.exp(sw_self - mw)
        o_win = (_dot(pw.astype(BF16), vw) + pw_self * vwn_ref[0]) / (jnp.sum(pw, axis=1, keepdims=True) + pw_self)
        gt = jnp.broadcast_to(g_ref[0], (NSA_HEADS, LANES))
        o8 = (_col(gt, lane, 3 * row) * o_cmp + _col(gt, lane, 3 * row + 1) * o_sel
              + _col(gt, lane, 3 * row + 2) * o_win)
        lane1 = lax.broadcasted_iota(jnp.int32, (1, LANES), 1)
        tiles = []
        for t in range(NSA_HEADS // 2):
            ha, hb = 2 * t, 2 * t + 1
            ra = o8[ha:ha + 1, :]
            rb = o8[hb:hb + 1, :]
            if ha // NSA_GROUP == 1:
                ra = pltpu.roll(ra, HEAD_DIM, 1)
            if hb // NSA_GROUP == 0:
                rb = pltpu.roll(rb, HEAD_DIM, 1)
            tiles.append(jnp.where(lane1 < HEAD_DIM, ra, rb))
        o_ref[0] = jnp.concatenate(tiles, axis=1)


def _nsa_sample(page_table, cache, qmat, ksn, vsn, kwn, vwn, win, gates, tsn, misc, tcs, tws,
                pos, w1bd, w2bd, gkc, avg):
    nb, npages = page_table.shape
    nstep = npages // PAGES_PER_STEP
    nsel = PAST_LEN // SEL_BLOCK

    def page_spec(j):
        return pl.BlockSpec((1, PAGE, 512), lambda b, s, pt: (pt[b, s * PAGES_PER_STEP + j], 0, 0))

    per_b = lambda shape: pl.BlockSpec((1,) + shape, lambda b, s, pt: (b, 0, 0))
    const = lambda shape: pl.BlockSpec(shape, lambda b, s, pt: (0,) * len(shape))
    grid_spec = pltpu.PrefetchScalarGridSpec(
        num_scalar_prefetch=1,
        grid=(nb, nstep),
        in_specs=[page_spec(j) for j in range(PAGES_PER_STEP)]
        + [per_b((NSA_HEADS, LANES)), per_b((1, LANES)), per_b((1, LANES)), per_b((1, LANES)), per_b((1, LANES)),
           per_b((WINDOW, 256)), per_b((1, LANES)),
           const((NSA_HEADS, LANES)), const((NSA_HEADS, LANES)), const((NSA_HEADS, PAST_LEN // CMP_BLOCK)),
           const((NSA_HEADS, WINDOW)),
           const((CMP_BLOCK, 256)), const((2, CMP_BLOCK, LANES, 256)), const((2, 256, LANES)),
           const((1, LANES)), const((LANES, LANES))],
        out_specs=per_b((1, 512)),
        scratch_shapes=[pltpu.VMEM((PAST_LEN, LANES), F32), pltpu.VMEM((PAST_LEN, LANES), F32),
                        pltpu.VMEM((NSA_HEADS, LANES), F32),
                        pltpu.VMEM((NSA_HEADS, LANES), F32), pltpu.VMEM((nsel, NSA_HEADS, LANES), F32)],
    )
    return pl.pallas_call(
        _nsa_sample_kernel,
        grid_spec=grid_spec,
        out_shape=jax.ShapeDtypeStruct((nb, 1, 512), F32),
        compiler_params=_cparams(("arbitrary", "arbitrary")),
        name="nsa_sample",
    )(page_table, *([cache] * PAGES_PER_STEP), qmat, ksn, vsn, kwn, vwn, win, gates, tsn, misc, tcs, tws,
      pos, w1bd, w2bd, gkc, avg)


def _outproj_kernel(x_ref, mod_ref, om_ref, on_ref, w_ref, o_ref):
    y = _dot(om_ref[0].astype(BF16), w_ref[0:512, :]) + _dot(on_ref[0].astype(BF16), w_ref[512:1024, :])
    o_ref[0] = x_ref[0] + mod_ref[0][:, 2 * D:3 * D] * y


def _outproj(x, mod, o_m, o_n, w, tm):
    b, t, _ = x.shape
    tmod = mod.shape[1]
    row = lambda width: pl.BlockSpec((1, tm, width), lambda i, j: (i, j, 0))
    return pl.pallas_call(
        _outproj_kernel,
        grid=(b, t // tm),
        in_specs=[row(D), pl.BlockSpec((1, tmod, 3 * D), lambda i, j: (i, 0, 0)), row(512), row(512),
                  pl.BlockSpec((D, D), lambda i, j: (0, 0))],
        out_specs=row(D),
        out_shape=jax.ShapeDtypeStruct((b, t, D), F32),
        compiler_params=_cparams(("arbitrary", "arbitrary")),
        name="attn_outproj",
    )(x, mod, o_m, o_n, w)


def _mlp_kernel(x_ref, mod_ref, g_ref, w1_ref, w2_ref, o_ref, h_scr, acc_scr):
    kf = pl.program_id(2)

    @pl.when(kf == 0)
    def _():
        mod = mod_ref[0]
        h_scr[...] = _modulate(x_ref[0], g_ref[...], mod[:, 0:D], mod[:, D:2 * D]).astype(BF16)
        acc_scr[...] = jnp.zeros(acc_scr.shape, F32)

    a = jnp.square(jnp.maximum(_dot(h_scr[...], w1_ref[...]), 0.0))
    acc_scr[...] += _dot(a.astype(BF16), w2_ref[...])

    @pl.when(kf == pl.num_programs(2) - 1)
    def _():
        o_ref[0] = x_ref[0] + mod_ref[0][:, 2 * D:3 * D] * acc_scr[...]


def _mlp(x, mod, g, w1, w2, tm, tf):
    b, t, _ = x.shape
    tmod = mod.shape[1]
    return pl.pallas_call(
        _mlp_kernel,
        grid=(b, t // tm, D_FF // tf),
        in_specs=[pl.BlockSpec((1, tm, D), lambda i, j, kf: (i, j, 0)),
                  pl.BlockSpec((1, tmod, 3 * D), lambda i, j, kf: (i, 0, 0)),
                  pl.BlockSpec((1, D), lambda i, j, kf: (0, 0)),
                  pl.BlockSpec((D, tf), lambda i, j, kf: (0, kf)),
                  pl.BlockSpec((tf, D), lambda i, j, kf: (kf, 0))],
        out_specs=pl.BlockSpec((1, tm, D), lambda i, j, kf: (i, j, 0)),
        out_shape=jax.ShapeDtypeStruct((b, t, D), F32),
        scratch_shapes=[pltpu.VMEM((tm, D), BF16), pltpu.VMEM((tm, D), F32)],
        compiler_params=_cparams(("arbitrary", "arbitrary", "arbitrary")),
        name="mlp",
    )(x, mod, g, w1, w2)


def _s5_disc_kernel(are_ref, aim_ref, ldt_ref, bre_ref, bim_ref, abre_ref, abim_ref, bbre_ref, bbim_ref):
    a_re, a_im = are_ref[...], aim_ref[...]
    dt = jnp.exp(ldt_ref[...])
    decay = jnp.exp(dt * a_re)
    ab_re, ab_im = decay * jnp.cos(dt * a_im), decay * jnp.sin(dt * a_im)
    den = a_re * a_re + a_im * a_im
    f_re = ((ab_re - 1) * a_re + ab_im * a_im) / den
    f_im = (ab_im * a_re - (ab_re - 1) * a_im) / den
    br, bi = bre_ref[...], bim_ref[...]
    abre_ref[...] = ab_re
    abim_ref[...] = ab_im
    bbre_ref[...] = f_re * br - f_im * bi
    bbim_ref[...] = f_re * bi + f_im * br


def _s5_discretize(a_re, a_im, log_dt, b_re, b_im):
    rep = lambda a: jnp.repeat(a, S5_GROUP_CH, axis=1)
    shp = jax.ShapeDtypeStruct((S5_GROUPS, S5_STATE * S5_GROUP_CH), F32)
    ldt = jnp.broadcast_to(log_dt[:, None], (S5_GROUPS, S5_STATE * S5_GROUP_CH))
    flat = lambda a: a.reshape(S5_GROUPS, S5_STATE * S5_GROUP_CH)
    ab_re, ab_im, bb_re, bb_im = pl.pallas_call(
        _s5_disc_kernel, out_shape=[shp] * 4, name="s5_discretize",
    )(rep(a_re), rep(a_im), ldt, flat(b_re), flat(b_im))
    unrep = lambda a: a[:, ::S5_GROUP_CH]
    unflat = lambda a: a.reshape(S5_GROUPS, S5_STATE, S5_GROUP_CH)
    return unrep(ab_re), unrep(ab_im), unflat(bb_re), unflat(bb_im)


def _modulate_tm_kernel(x_ref, mod_ref, g_ref, o_ref):
    mod = mod_ref[0]
    o_ref[...] = _modulate(x_ref[0], g_ref[...], mod[:, 0:D], mod[:, D:2 * D])


def _modulate_time_major(x, mod, g, tl):
    b, t, _ = x.shape
    tmod = mod.shape[1]
    return pl.pallas_call(
        _modulate_tm_kernel,
        grid=(b, t // tl),
        in_specs=[pl.BlockSpec((1, tl, D), lambda i, j: (i, j, 0)),
                  pl.BlockSpec((1, tmod, 3 * D), lambda i, j: (i, 0, 0)),
                  pl.BlockSpec((1, D), lambda i, j: (0, 0))],
        out_specs=pl.BlockSpec((tl, D), lambda i, j: (j, i)),
        out_shape=jax.ShapeDtypeStruct((t, b * D), F32),
        compiler_params=_cparams(("arbitrary", "arbitrary")),
        name="s5_modulate",
    )(x, mod, g)


S5_CB = 256
S5_NS = S5_CB // S5_GROUP_CH * S5_STATE


def _s5_scan_kernel(h_ref, wb_ref, wc_ref, ar_ref, ai_ref, d_ref, h0_ref, y_ref, so_ref, xs_scr, st_scr, *, tl, r):
    i = pl.program_id(1)

    @pl.when(i == 0)
    def _():
        st_scr[...] = h0_ref[...]

    u = h_ref[...]
    xs_scr[...] = _dot(u.astype(BF16), wb_ref[0])
    ar = jnp.broadcast_to(ar_ref[0], (r, S5_NS))
    ai = jnp.broadcast_to(ai_ref[0], (r, S5_NS))

    def step(t, carry):
        xr, xi = carry
        r0 = pl.multiple_of(t * r, r)
        nr = ar * xr - ai * xi + xs_scr[pl.ds(r0, r), 0:S5_NS]
        ni = ar * xi + ai * xr + xs_scr[pl.ds(r0, r), S5_NS:2 * S5_NS]
        xs_scr[pl.ds(r0, r), 0:S5_NS] = nr
        xs_scr[pl.ds(r0, r), S5_NS:2 * S5_NS] = ni
        return nr, ni

    xr, xi = lax.fori_loop(0, tl, step, (st_scr[0], st_scr[1]))
    st_scr[0] = xr
    st_scr[1] = xi
    y_ref[...] = _dot(xs_scr[...].astype(BF16), wc_ref[0]) + d_ref[...] * u

    @pl.when(i == pl.num_programs(1) - 1)
    def _():
        so_ref[...] = st_scr[...]


def _s5_scan(h_tm, wb, wc, ar, ai, d_skip, h0, r, tl):
    rows = h_tm.shape[0]
    nj = D // S5_CB
    return pl.pallas_call(
        functools.partial(_s5_scan_kernel, tl=tl, r=r),
        grid=(nj, rows // (tl * r)),
        in_specs=[pl.BlockSpec((tl * r, S5_CB), lambda j, i: (i, j)),
                  pl.BlockSpec((1, S5_CB, 2 * S5_NS), lambda j, i: (j, 0, 0)),
                  pl.BlockSpec((1, 2 * S5_NS, S5_CB), lambda j, i: (j, 0, 0)),
                  pl.BlockSpec((1, 1, S5_NS), lambda j, i: (j, 0, 0)),
                  pl.BlockSpec((1, 1, S5_NS), lambda j, i: (j, 0, 0)),
                  pl.BlockSpec((1, S5_CB), lambda j, i: (0, j)),
                  pl.BlockSpec((2, r, S5_NS), lambda j, i: (0, 0, j))],
        out_specs=[pl.BlockSpec((tl * r, S5_CB), lambda j, i: (i, j)),
                   pl.BlockSpec((2, r, S5_NS), lambda j, i: (0, 0, j))],
        out_shape=[jax.ShapeDtypeStruct((rows, D), F32), jax.ShapeDtypeStruct((2, r, S5_GROUPS * S5_STATE), F32)],
        scratch_shapes=[pltpu.VMEM((tl * r, 2 * S5_NS), F32), pltpu.VMEM((2, r, S5_NS), F32)],
        compiler_params=_cparams(("arbitrary", "arbitrary")),
        name="s5_scan",
    )(h_tm, wb, wc, ar, ai, d_skip, h0)


def _glu_kernel(y_ref, x_ref, mod_ref, w_ref, o_ref):
    z = _dot(jax.nn.gelu(y_ref[...]).astype(BF16), w_ref[...])
    o_ref[0] = x_ref[0] + mod_ref[0][:, 2 * D:3 * D] * (z[:, 0:D] * jax.nn.sigmoid(z[:, D:2 * D]))


def _glu_residual(y_tm, x, mod, w, tl):
    b, t, _ = x.shape
    tmod = mod.shape[1]
    return pl.pallas_call(
        _glu_kernel,
        grid=(b, t // tl),
        in_specs=[pl.BlockSpec((tl, D), lambda i, j: (j, i)),
                  pl.BlockSpec((1, tl, D), lambda i, j: (i, j, 0)),
                  pl.BlockSpec((1, tmod, 3 * D), lambda i, j: (i, 0, 0)),
                  pl.BlockSpec((D, 2 * D), lambda i, j: (0, 0))],
        out_specs=pl.BlockSpec((1, tl, D), lambda i, j: (i, j, 0)),
        out_shape=jax.ShapeDtypeStruct((b, t, D), F32),
        compiler_params=_cparams(("arbitrary", "arbitrary")),
        name="s5_glu",
    )(y_tm, x, mod, w)


def _s5_block_weights(bb_re, bb_im, c_re, c_im):
    nj, ng = D // S5_CB, S5_CB // S5_GROUP_CH
    eye = jnp.eye(ng, dtype=F32)

    def wb_part(bb):
        t = bb.reshape(nj, ng, S5_STATE, S5_GROUP_CH).transpose(0, 1, 3, 2)
        return jnp.einsum("jgcn,gh->jgchn", t, eye).reshape(nj, S5_CB, S5_NS)

    def wc_part(c):
        t = c.reshape(nj, ng, S5_GROUP_CH, S5_STATE).transpose(0, 1, 3, 2)
        return jnp.einsum("jgnc,gh->jgnhc", t, eye).reshape(nj, S5_NS, S5_CB)

    wb = jnp.concatenate([wb_part(bb_re), wb_part(bb_im)], axis=2).astype(BF16)
    wc = jnp.concatenate([wc_part(c_re), -wc_part(c_im)], axis=1).astype(BF16)
    return wb, wc


def _s5_layer(x, mod, g, wb, wc, ar, ai, d_skip, h0, w_glu, r, tl):
    b, t, _ = x.shape
    h_tm = _modulate_time_major(x, mod, g, tl).reshape(t * b, D)
    h0_t = h0.reshape(b, 2, S5_GROUPS * S5_STATE).transpose(1, 0, 2)
    y, st = _s5_scan(h_tm, wb, wc, ar, ai, d_skip, h0_t, r, tl)
    x_new = _glu_residual(y.reshape(t, b * D), x, mod, w_glu, tl)
    return x_new, st.transpose(1, 0, 2).reshape(b, 2, S5_GROUPS, S5_STATE)


def _dist_tiles():
    r = jnp.arange(TQ, dtype=jnp.int32)[:, None]
    c = jnp.arange(TQ, dtype=jnp.int32)[None, :]
    d0 = r - c
    edge = 2 * TQ + r - c
    return jnp.concatenate([d0, TQ + d0, 2 * TQ + d0, jnp.where(edge <= WINDOW, edge, -1)], axis=0)


def _dist_cmp(seq):
    q = jnp.arange(seq, dtype=jnp.int32)[:, None]
    n = jnp.arange(LANES, dtype=jnp.int32)[None, :]
    return jnp.where(n < seq // CMP_BLOCK, q - ((n + 1) * CMP_BLOCK - 1), -1)


def _dist_sample():
    ar = lambda n: jnp.arange(n, dtype=jnp.int32)
    misc = jnp.zeros((LANES,), jnp.int32).at[1].set(MAX_DISTANCE * 4)
    moba = MOBA_BLOCK - ar(MOBA_BLOCK)
    sel = PAGE - ar(PAGE)
    cmp_ = PAST_LEN - ((ar(PAST_LEN // CMP_BLOCK) + 1) * CMP_BLOCK - 1)
    win = WINDOW - ar(WINDOW)
    return jnp.concatenate([misc, moba, sel, cmp_, win])[None, :]


def _block_diag2(w):
    z = jnp.zeros_like(w)
    return jnp.concatenate([jnp.concatenate([w, z], axis=-1), jnp.concatenate([z, w], axis=-1)], axis=-2)


def kernel(x_prompt, x_sample, cache_moba_kv, cache_nsa_kv, state_nsa_win, state_s5, page_table, c_prompt, c_sample, rel_bias, attn_norm_g, attn_ada_w, attn_ada_b, attn_w_in, attn_qk_g, nsa_cmp_pos, nsa_cmp_w1, nsa_cmp_w2, attn_w_out, ssm_norm_g, ssm_ada_w, ssm_ada_b, s5_a_re, s5_a_im, s5_log_dt, s5_b_re, s5_b_im, s5_c_re, s5_c_im, s5_d, s5_w_glu, mlp_norm_g, mlp_ada_w, mlp_ada_b, mlp_w1, mlp_w2):
    bp, seq, _ = x_prompt.shape
    bs = x_sample.shape[0]
    assert seq % TQ == 0 and x_sample.shape[1] == 1
    n_pool = cache_moba_kv.shape[1]

    c_all = jnp.concatenate([c_prompt, c_sample], axis=0)
    split_mod = lambda m: (m[:bp, None, :], m[None, bp:, :])
    mod_attn = _adaln(c_all, attn_ada_w, attn_ada_b)
    mod_ssm = _adaln(c_all, ssm_ada_w, ssm_ada_b)
    mod_mlp = _adaln(c_all, mlp_ada_w, mlp_ada_b)

    xp = x_prompt
    xs = x_sample.reshape(1, bs, D)

    tb = _bias_table(rel_bias, _dist_tiles()).reshape(2 * MOBA_HEADS, 4, TQ, TQ)
    tc = _bias_table(rel_bias, _dist_cmp(seq))[MOBA_HEADS:]
    ts = _bias_table(rel_bias, _dist_sample())[:, 0, :]
    o = LANES
    misc_m, misc_n = ts[:MOBA_HEADS, 0:o], ts[MOBA_HEADS:, 0:o]
    tsb = ts[:MOBA_HEADS, o:o + MOBA_BLOCK]
    o += MOBA_BLOCK
    tsn = ts[MOBA_HEADS:, o:o + PAGE]
    o += PAGE
    tcs = ts[MOBA_HEADS:, o:o + PAST_LEN // CMP_BLOCK]
    o += PAST_LEN // CMP_BLOCK
    tws = ts[MOBA_HEADS:, o:o + WINDOW]

    w_in = jnp.pad(attn_w_in[0], ((0, 0), (0, IN_COLS_PAD - IN_COLS))).astype(BF16)
    qkg_t = jnp.pad(jnp.tile(attn_qk_g[0], (1, 2)), ((0, 2), (0, 0)))
    lr = jnp.arange(LANES)
    avg = jnp.where(lr[:, None] // HEAD_DIM == lr[None, :] // HEAD_DIM, 1.0 / HEAD_DIM, 0.0).astype(BF16)
    g_attn = attn_norm_g[0][None, :]
    w_out = attn_w_out[0].astype(BF16)
    pos = jnp.concatenate([nsa_cmp_pos[0, 0], nsa_cmp_pos[0, 0], nsa_cmp_pos[0, 1], nsa_cmp_pos[0, 1]], axis=1)
    w1bd = _block_diag2(nsa_cmp_w1[0].reshape(2, CMP_BLOCK, HEAD_DIM, CMP_HIDDEN)).astype(BF16)
    w2bd = _block_diag2(nsa_cmp_w2[0]).astype(BF16)
    gkc = qkg_t[3:4]

    mp_attn, ms_attn = split_mod(mod_attn[0])
    mq, mkv, nq, nkv, wkv, gates = _attn_proj(xp, mp_attn, g_attn, w_in, qkg_t, avg, 512)
    o_moba = _moba_prompt(mq, mkv, tb[:MOBA_HEADS])
    kcmp, vcmp = _cmp_prompt(nkv, pos, w1bd, w2bd, gkc, avg)
    o_nsa = _nsa_prompt(nq, nkv, wkv, kcmp, vcmp, gates, tb[MOBA_HEADS:], tc)
    xp = _outproj(xp, mp_attn, o_moba, o_nsa, w_out, 512)
    npg = seq // PAGE
    moba_p = mkv.reshape(1, bp, npg, PAGE, 2, MOBA_HEADS, HEAD_DIM)
    nsa_p = nkv.reshape(1, bp, npg, PAGE, 4, 2, HEAD_DIM)
    win_p = wkv[:, seq - min(WINDOW, seq):].reshape(1, bp, min(WINDOW, seq), 2, 2, HEAD_DIM)
    mq_s, mkv_s, nq_s, nkv_s, wkv_s, gates_s = _attn_proj(xs, ms_attn, g_attn, w_in, qkg_t, avg, bs)
    lw = jnp.arange(512)
    qmat_m = jnp.where(lw[None, None, :] // HEAD_DIM == jnp.arange(MOBA_HEADS)[None, :, None], mq_s[0][:, None, :], 0.0)
    col3 = lambda a, lo, hi: a[0][:, None, lo:hi]
    o_moba_s = _moba_sample(page_table, cache_moba_kv.reshape(n_pool, PAGE, 1024), qmat_m,
                            col3(mkv_s, 0, 512), col3(mkv_s, 512, 1024), tsb, misc_m)
    nq4 = nq_s[0].reshape(bs, NSA_HEADS, HEAD_DIM)
    kvh = jnp.arange(NSA_HEADS) // NSA_GROUP
    qmat_n = jnp.concatenate([jnp.where(kvh[None, :, None] == 0, nq4, 0.0),
                              jnp.where(kvh[None, :, None] == 1, nq4, 0.0)], axis=2)
    win_in = state_nsa_win[0].reshape(bs, WINDOW, 256)
    o_nsa_s = _nsa_sample(page_table, cache_nsa_kv.reshape(n_pool, PAGE, 512), qmat_n,
                          col3(nkv_s, 256, 384), col3(nkv_s, 384, 512), col3(wkv_s, 0, 128), col3(wkv_s, 128, 256),
                          win_in, gates_s.reshape(bs, 1, LANES), tsn, misc_n, tcs, tws, pos, w1bd, w2bd, gkc, avg)
    xs = _outproj(xs, ms_attn, o_moba_s.reshape(1, bs, 512), o_nsa_s.reshape(1, bs, 512), w_out, bs)
    moba_s = mkv_s.reshape(1, bs, 1, 2, MOBA_HEADS, HEAD_DIM)
    nsa_s = nkv_s.reshape(1, bs, 1, 4, 2, HEAD_DIM)
    win_s = jnp.concatenate([win_in[:, 1:], wkv_s[0][:, None, :]], axis=1).reshape(1, bs, WINDOW, 2, 2, HEAD_DIM)

    w1_0, w2_0 = mlp_w1[0].astype(BF16), mlp_w2[0].astype(BF16)
    mp_mlp, ms_mlp = split_mod(mod_mlp[0])
    g_mlp0 = mlp_norm_g[0][None, :]
    xp = _mlp(xp, mp_mlp, g_mlp0, w1_0, w2_0, 512, 1024)
    xs = _mlp(xs, ms_mlp, g_mlp0, w1_0, w2_0, bs, 1024)

    ab_re, ab_im, bb_re, bb_im = _s5_discretize(s5_a_re[0], s5_a_im[0], s5_log_dt[0], s5_b_re[0], s5_b_im[0])
    wb, wc = _s5_block_weights(bb_re, bb_im, s5_c_re[0], s5_c_im[0])
    nj = D // S5_CB
    ar = ab_re.reshape(nj, 1, S5_NS)
    ai = ab_im.reshape(nj, 1, S5_NS)
    g_ssm = ssm_norm_g[0][None, :]
    d_skip = s5_d[0][None, :]
    w_glu = s5_w_glu[0].astype(BF16)
    mp_ssm, ms_ssm = split_mod(mod_ssm[0])
    xp, st_p = _s5_layer(xp, mp_ssm, g_ssm, wb, wc, ar, ai, d_skip,
                         jnp.zeros((bp, 2, S5_GROUPS, S5_STATE), F32), w_glu, bp, 128)
    xs_t = xs.reshape(bs, 1, D)
    ms_ssm_t = ms_ssm.reshape(bs, 1, 3 * D)
    h_s = _modulate_time_major(xs, ms_ssm, g_ssm, bs)
    y_s, st_s = _s5_scan(h_s, wb, wc, ar, ai, d_skip,
                         state_s5[0].reshape(bs, 2, S5_GROUPS * S5_STATE).transpose(1, 0, 2), bs, 1)
    xs = _glu_residual(y_s, xs, ms_ssm, w_glu, bs)
    st_s = st_s.transpose(1, 0, 2).reshape(bs, 2, S5_GROUPS, S5_STATE)
    del xs_t, ms_ssm_t

    w1_1, w2_1 = mlp_w1[1].astype(BF16), mlp_w2[1].astype(BF16)
    mp_mlp, ms_mlp = split_mod(mod_mlp[1])
    g_mlp1 = mlp_norm_g[1][None, :]
    xp = _mlp(xp, mp_mlp, g_mlp1, w1_1, w2_1, 512, 1024)
    xs = _mlp(xs, ms_mlp, g_mlp1, w1_1, w2_1, bs, 1024)

    return (xp, xs.reshape(bs, 1, D), moba_p, moba_s, nsa_p, nsa_s, win_p, win_s, st_p[None], st_s[None])
```

```python
import functools
import math

import jax
import jax.numpy as jnp
from jax import lax
from jax.experimental import pallas as pl
from jax.experimental.pallas import tpu as pltpu

F32 = jnp.float32
BF16 = jnp.bfloat16
HIGHEST = lax.Precision.HIGHEST

D = 1024
HEAD_DIM = 64
MOBA_HEADS = 8
NSA_HEADS = 8
NSA_GROUP = 4
MOBA_BLOCK = 256
MOBA_TOPK = 3
CMP_BLOCK = 32
CMP_HIDDEN = 128
SEL_BLOCK = 64
SEL_TOPK = 16
WINDOW = 512
NUM_BUCKETS = 32
MAX_DISTANCE = 128
PAGE = 128
PAST_LEN = 8192
D_FF = 4 * D
S5_GROUPS = 64
S5_STATE = 64
S5_GROUP_CH = 16
IN_COLS = 3 * 512 + 512 + 6 * 128 + 3 * NSA_HEADS
IN_COLS_PAD = 23 * 128
EPS = 1e-6
SCALE = HEAD_DIM ** -0.5
LANES = 128
TQ = 256
N_BIAS_TILES = 5
NEG = -1e30
M_INIT = -1e15
VMEM_LIMIT = 56 * 1024 * 1024

_NT = (((1,), (1,)), ((), ()))


def _cparams(sem):
    return pltpu.CompilerParams(dimension_semantics=sem, vmem_limit_bytes=VMEM_LIMIT)


def _dot(a, b, **kw):
    return jnp.dot(a, b, preferred_element_type=F32, **kw)


def _dot_nt(a, b, **kw):
    return lax.dot_general(a, b, _NT, preferred_element_type=F32, **kw)


def _modulate(x, g, shift, scale):
    ms = jnp.mean(x * x, axis=-1, keepdims=True)
    return x * lax.rsqrt(ms + EPS) * g * (1.0 + scale) + shift


def _group_mean_sq(z, avg):
    sq = z * z
    hi = sq.astype(BF16)
    lo = (sq - hi.astype(F32)).astype(BF16)
    return _dot(hi, avg) + _dot(lo, avg)


def _col(x, lane, idx):
    return jnp.sum(jnp.where(lane == idx, x, 0.0), axis=1, keepdims=True)


def _adaln_kernel(c_ref, w_ref, b_ref, o_ref):
    c = c_ref[...]
    s = c * jax.nn.sigmoid(c)
    o_ref[0] = _dot(s, w_ref[0], precision=HIGHEST) + b_ref[0]


def _adaln(c_all, w, b):
    nl, n = w.shape[0], c_all.shape[0]
    return pl.pallas_call(
        _adaln_kernel,
        grid=(nl, 3),
        in_specs=[pl.BlockSpec((n, D), lambda l, j: (0, 0)),
                  pl.BlockSpec((1, D, D), lambda l, j: (l, 0, j)),
                  pl.BlockSpec((1, 1, D), lambda l, j: (l, 0, j))],
        out_specs=pl.BlockSpec((1, n, D), lambda l, j: (l, 0, j)),
        out_shape=jax.ShapeDtypeStruct((nl, n, 3 * D), F32),
        compiler_params=_cparams(("arbitrary", "arbitrary")),
        name="adaln",
    )(c_all, w, b.reshape(nl, 1, 3 * D))


def _bias_kernel(rb_ref, d_ref, o_ref):
    h = pl.program_id(0)
    dist = d_ref[...]
    n = jnp.maximum(dist, 0)
    max_exact = NUM_BUCKETS // 2
    nf = jnp.maximum(n, 1).astype(F32)
    large = max_exact + (jnp.log(nf / max_exact) / math.log(MAX_DISTANCE / max_exact)
                         * (NUM_BUCKETS - max_exact)).astype(jnp.int32)
    large = jnp.minimum(large, NUM_BUCKETS - 1)
    bucket = jnp.where(n < max_exact, n, large)
    acc = jnp.zeros(dist.shape, F32)
    for k in range(NUM_BUCKETS):
        acc = jnp.where(bucket == k, rb_ref[k, h], acc)
    o_ref[0] = jnp.where(dist < 0, NEG, acc)


def _bias_table(rel_bias, dist):
    r, c = dist.shape
    nh = rel_bias.shape[1]
    return pl.pallas_call(
        _bias_kernel,
        grid=(nh,),
        in_specs=[pl.BlockSpec(memory_space=pltpu.SMEM),
                  pl.BlockSpec((r, c), lambda h: (0, 0))],
        out_specs=pl.BlockSpec((1, r, c), lambda h: (h, 0, 0)),
        out_shape=jax.ShapeDtypeStruct((nh, r, c), F32),
        compiler_params=_cparams(("arbitrary",)),
        name="bias_table",
    )(rel_bias, dist)


def _proj_kernel(x_ref, mod_ref, g_ref, w_ref, qkg_ref, avg_ref,
                 mq_ref, mkv_ref, nq_ref, nkv_ref, wkv_ref, gt_ref):
    x = x_ref[0]
    mod = mod_ref[0]
    h = _modulate(x, g_ref[...], mod[:, 0:D], mod[:, D:2 * D])
    z = _dot(h.astype(BF16), w_ref[...])
    avg = avg_ref[...]

    def normed(lo, gi):
        zs = z[:, lo:lo + LANES]
        return zs * lax.rsqrt(_group_mean_sq(zs, avg) + EPS) * qkg_ref[gi:gi + 1, :]

    for t in range(4):
        mq_ref[0, :, t * LANES:(t + 1) * LANES] = normed(t * LANES, 0) * SCALE
        mkv_ref[0, :, t * LANES:(t + 1) * LANES] = normed(512 + t * LANES, 1)
        nq_ref[0, :, t * LANES:(t + 1) * LANES] = normed(1536 + t * LANES, 2) * SCALE
    mkv_ref[0, :, 512:1024] = z[:, 1024:1536]
    nkv_ref[0, :, 0:256] = z[:, 2048:2304]
    nkv_ref[0, :, 256:384] = normed(2304, 4)
    nkv_ref[0, :, 384:512] = z[:, 2432:2560]
    wkv_ref[0, :, 0:128] = normed(2560, 5)
    wkv_ref[0, :, 128:256] = z[:, 2688:2816]
    gt_ref[0] = jax.nn.sigmoid(z[:, 2816:2944])


def _attn_proj(x, mod, g, w_pad, qkg_t, avg, tm):
    b, t, _ = x.shape
    tmod = mod.shape[1]
    row = lambda width: pl.BlockSpec((1, tm, width), lambda i, j: (i, j, 0))
    shp = lambda width: jax.ShapeDtypeStruct((b, t, width), F32)
    return pl.pallas_call(
        _proj_kernel,
        grid=(b, t // tm),
        in_specs=[row(D),
                  pl.BlockSpec((1, tmod, 3 * D), lambda i, j: (i, 0, 0)),
                  pl.BlockSpec((1, D), lambda i, j: (0, 0)),
                  pl.BlockSpec((D, IN_COLS_PAD), lambda i, j: (0, 0)),
                  pl.BlockSpec((8, LANES), lambda i, j: (0, 0)),
                  pl.BlockSpec((LANES, LANES), lambda i, j: (0, 0))],
        out_specs=[row(512), row(1024), row(512), row(512), row(256), row(128)],
        out_shape=[shp(512), shp(1024), shp(512), shp(512), shp(256), shp(128)],
        compiler_params=_cparams(("arbitrary", "arbitrary")),
        name="attn_proj",
    )(x, mod, g, w_pad, qkg_t, avg)


def _rank_rows(score, rowi, ncand, step=1):
    rank = jnp.zeros(score.shape, F32)
    for m in range(0, ncand * step, step):
        rm = score[m:m + 1, :]
        rank = rank + jnp.where(rm > score, 1.0, 0.0) + jnp.where((rm == score) & (m < rowi), 1.0, 0.0)
    return rank


def _columns_from_rows(x_t):
    pad = jnp.zeros((LANES - x_t.shape[0], x_t.shape[1]), F32)
    return jnp.concatenate([x_t, pad], axis=0).T


def _softmax_pv(pieces, v_all):
    m = pieces[0]
    for s in pieces[1:]:
        m = jnp.maximum(m, s)
    m = jnp.maximum(jnp.max(m, axis=1, keepdims=True), M_INIT)
    ps = [jnp.exp(s - m) for s in pieces]
    tot = ps[0]
    for p in ps[1:]:
        tot = tot + p
    l = jnp.sum(tot, axis=1, keepdims=True)
    p_all = jnp.concatenate([p.astype(BF16) for p in ps], axis=1) if len(ps) > 1 else ps[0].astype(BF16)
    return _dot(p_all, v_all) / jnp.maximum(l, 1e-30)


def _moba_prompt_kernel(q_ref, k_ref, v_ref, t_ref, o_ref, km_scr, kb_scr, vb_scr):
    s_len = q_ref.shape[1]
    nblk = s_len // MOBA_BLOCK
    nq = s_len // TQ
    lane = lax.broadcasted_iota(jnp.int32, (TQ, LANES), 1)
    rowb = lax.broadcasted_iota(jnp.int32, (nblk, TQ), 0)
    km_scr[...] = jnp.zeros(km_scr.shape, F32)
    for n in range(nblk):
        km_scr[n:n + 1, :] = jnp.mean(k_ref[0, n * MOBA_BLOCK:(n + 1) * MOBA_BLOCK, :], axis=0, keepdims=True)
    kmean = km_scr[...]
    kb_scr[...] = k_ref[0].astype(BF16)
    vb_scr[...] = v_ref[0].astype(BF16)

    def qtile(i, _):
        r0 = pl.multiple_of(i * TQ, TQ)
        q2 = q_ref[0, pl.ds(r0, TQ), :]
        qbs, cbs = [], []
        for e in range(2):
            qe = jnp.where(lane // HEAD_DIM == e, q2, 0.0)
            gate_t = _dot_nt(kmean, qe, precision=HIGHEST)[0:nblk]
            gm = jnp.where(rowb < i, gate_t, -jnp.inf)
            sel = ((_rank_rows(gm, rowb, nblk) < MOBA_TOPK) & (rowb < i)) | (rowb == i)
            cbs.append(_columns_from_rows(jnp.where(sel, 0.0, NEG)))
            qbs.append(qe.astype(BF16))

        for c in range(1, nq // 2 + 1):
            @pl.when(i // 2 + 1 == c)
            def _():
                kall = kb_scr[0:2 * c * TQ, :]
                vall = vb_scr[0:2 * c * TQ, :]
                outs = []
                for e in range(2):
                    s = _dot_nt(qbs[e], kall)
                    pieces = [s[:, n * TQ:(n + 1) * TQ] + t_ref[e, jnp.clip(i - n, 0, 2)] + cbs[e][:, n:n + 1]
                              for n in range(2 * c)]
                    outs.append(_softmax_pv(pieces, vall))
                o_ref[0, pl.ds(r0, TQ), :] = jnp.where(lane < HEAD_DIM, outs[0], outs[1])
        return 0

    lax.fori_loop(0, nq, qtile, 0)


def _moba_prompt(mq, mkv, tb):
    b, s, _ = mq.shape
    npair = MOBA_HEADS // 2
    return pl.pallas_call(
        _moba_prompt_kernel,
        grid=(b, npair),
        in_specs=[pl.BlockSpec((1, s, LANES), lambda i, p: (i, 0, p)),
                  pl.BlockSpec((1, s, LANES), lambda i, p: (i, 0, p)),
                  pl.BlockSpec((1, s, LANES), lambda i, p: (i, 0, npair + p)),
                  pl.BlockSpec((2, N_BIAS_TILES, TQ, TQ), lambda i, p: (p, 0, 0, 0))],
        out_specs=pl.BlockSpec((1, s, LANES), lambda i, p: (i, 0, p)),
        out_shape=jax.ShapeDtypeStruct((b, s, 512), F32),
        scratch_shapes=[pltpu.VMEM((LANES, LANES), F32), pltpu.VMEM((s, LANES), BF16), pltpu.VMEM((s, LANES), BF16)],
        compiler_params=_cparams(("arbitrary", "arbitrary")),
        name="moba_prompt",
    )(mq, mkv, mkv, tb)


def _compress_tokens(load_k, load_v, pos_ref, w1_ref, w2_ref):
    hk = hv = None
    for r in range(0, CMP_BLOCK, 2):
        xk = [(load_k(r + t) + pos_ref[r + t:r + t + 1, 0:LANES]).astype(BF16) for t in range(2)]
        xv = [(load_v(r + t) + pos_ref[r + t:r + t + 1, LANES:2 * LANES]).astype(BF16) for t in range(2)]
        dk = _dot(jnp.concatenate(xk, axis=1), w1_ref[0, r // 2])
        dv = _dot(jnp.concatenate(xv, axis=1), w1_ref[1, r // 2])
        hk = dk if hk is None else hk + dk
        hv = dv if hv is None else hv + dv
    ck = _dot(jax.nn.gelu(hk).astype(BF16), w2_ref[0])
    cv = _dot(jax.nn.gelu(hv).astype(BF16), w2_ref[1])
    return ck, cv


def _cmp_prompt_kernel(xk_ref, xv_ref, pos_ref, w1_ref, w2_ref, gkc_ref, avg_ref, kc_ref, vc_ref):
    nblk = xk_ref.shape[1] // CMP_BLOCK
    ck, cv = _compress_tokens(lambda r: xk_ref[0, pl.ds(r, nblk, stride=CMP_BLOCK), :],
                              lambda r: xv_ref[0, pl.ds(r, nblk, stride=CMP_BLOCK), :], pos_ref, w1_ref, w2_ref)
    ck = ck * lax.rsqrt(_group_mean_sq(ck, avg_ref[...]) + EPS) * gkc_ref[...]
    kc_ref[0] = jnp.zeros((LANES, LANES), F32)
    vc_ref[0] = jnp.zeros((LANES, LANES), F32)
    kc_ref[0, 0:nblk, :] = ck
    vc_ref[0, 0:nblk, :] = cv


def _cmp_prompt(nkv, pos, w1bd, w2bd, gkc, avg):
    b, s, _ = nkv.shape
    const = lambda shape: pl.BlockSpec(shape, lambda i: (0,) * len(shape))
    return pl.pallas_call(
        _cmp_prompt_kernel,
        grid=(b,),
        in_specs=[pl.BlockSpec((1, s, LANES), lambda i: (i, 0, 0)), pl.BlockSpec((1, s, LANES), lambda i: (i, 0, 1)),
                  const((CMP_BLOCK, 256)), const((2, CMP_BLOCK // 2, 256, 256)), const((2, 256, LANES)),
                  const((1, LANES)), const((LANES, LANES))],
        out_specs=[pl.BlockSpec((1, LANES, LANES), lambda i: (i, 0, 0))] * 2,
        out_shape=[jax.ShapeDtypeStruct((b, LANES, LANES), F32)] * 2,
        compiler_params=_cparams(("arbitrary",)),
        name="nsa_compress_prompt",
    )(nkv, nkv, pos, w1bd, w2bd, gkc, avg)


def _nsa_prompt_kernel(q_ref, ks_ref, vs_ref, kw_ref, vw_ref, kc_ref, vc_ref, g_ref, t_ref, tc_ref,
                       o_ref, ksb_scr, vsb_scr, kwb_scr, vwb_scr):
    s_len = q_ref.shape[1]
    k = pl.program_id(1)
    lane = lax.broadcasted_iota(jnp.int32, (TQ, LANES), 1)
    kvmask = (lane // HEAD_DIM) == k
    kc = kc_ref[0].astype(BF16)
    vc = vc_ref[0].astype(BF16)
    nsel = s_len // SEL_BLOCK
    nq = s_len // TQ
    ncmp = s_len // CMP_BLOCK
    rowb = lax.broadcasted_iota(jnp.int32, (nsel, TQ), 0)
    qpos = lax.broadcasted_iota(jnp.int32, (nsel, TQ), 1)
    pair_r = lax.broadcasted_iota(jnp.int32, (nsel, LANES), 0)
    pair_c = lax.broadcasted_iota(jnp.int32, (nsel, LANES), 1)
    pair_t = jnp.where((pair_c // (SEL_BLOCK // CMP_BLOCK) == pair_r) & (pair_c < ncmp), 1.0, 0.0)
    ksb_scr[...] = ks_ref[0].astype(BF16)
    vsb_scr[...] = vs_ref[0].astype(BF16)
    kwb_scr[...] = kw_ref[0].astype(BF16)
    vwb_scr[...] = vw_ref[0].astype(BF16)

    def qtile(i, _):
        r0 = pl.multiple_of(i * TQ, TQ)
        qs = []
        for h in range(NSA_GROUP):
            q2 = q_ref[0, pl.ds(r0, TQ), (h // 2) * LANES:(h // 2 + 1) * LANES]
            qa = jnp.where(k == (h % 2), q2, pltpu.roll(q2, HEAD_DIM, 1))
            qs.append(jnp.where(kvmask, qa, 0.0).astype(BF16))

        imp = jnp.zeros((TQ, LANES), F32)
        o_cmp = []
        for h in range(NSA_GROUP):
            s = _dot_nt(qs[h], kc) + tc_ref[h, pl.ds(r0, TQ), :]
            m = jnp.maximum(jnp.max(s, axis=1, keepdims=True), M_INIT)
            p = jnp.exp(s - m)
            p = p / jnp.maximum(jnp.sum(p, axis=1, keepdims=True), 1e-30)
            imp = imp + p
            o_cmp.append(_dot(p.astype(BF16), vc))

        imp_t = _dot_nt(pair_t, imp, precision=HIGHEST)
        own = (r0 + qpos) // SEL_BLOCK
        sc = jnp.where(rowb < own, imp_t, -jnp.inf)
        sel = ((_rank_rows(sc, rowb, nsel) < SEL_TOPK) & (rowb < own)) | (rowb == own)
        selb = _columns_from_rows(jnp.where(sel, 1.0, 0.0)).astype(BF16)

        wk, wv, wt = [], [], []
        for j, tidx in enumerate((3, 1, 0)):
            n = i - 2 + j
            c0 = pl.multiple_of(jnp.maximum(n, 0) * TQ, TQ)
            wk.append(kwb_scr[pl.ds(c0, TQ), :])
            wv.append(vwb_scr[pl.ds(c0, TQ), :])
            wt.append(jnp.where(n < 0, N_BIAS_TILES - 1, tidx))
        kw_all = jnp.concatenate(wk, axis=0)
        vw_all = jnp.concatenate(wv, axis=0)
        g = g_ref[0, pl.ds(r0, TQ), :]
        o_win = []
        for h in range(NSA_GROUP):
            s = _dot_nt(qs[h], kw_all)
            o_win.append(_softmax_pv([s[:, j * TQ:(j + 1) * TQ] + t_ref[h, wt[j]] for j in range(3)], vw_all))

        for c in range(1, nq // 2 + 1):
            @pl.when(i // 2 + 1 == c)
            def _():
                nkeys = 2 * c * TQ
                e_r = lax.broadcasted_iota(jnp.int32, (LANES, nkeys), 0)
                e_c = lax.broadcasted_iota(jnp.int32, (LANES, nkeys), 1)
                expand = jnp.where(e_r == e_c // SEL_BLOCK, 1.0, 0.0).astype(BF16)
                addm = (_dot(selb, expand) - 1.0) * (-NEG)
                kall = ksb_scr[0:nkeys, :]
                vall = vsb_scr[0:nkeys, :]
                res = []
                for h in range(NSA_GROUP):
                    s = _dot_nt(qs[h], kall) + addm
                    pieces = [s[:, n * TQ:(n + 1) * TQ] + t_ref[h, jnp.clip(i - n, 0, 2)] for n in range(2 * c)]
                    o_sel = _softmax_pv(pieces, vall)
                    hg = (k * NSA_GROUP + h) * 3
                    o = (_col(g, lane, hg) * o_cmp[h] + _col(g, lane, hg + 1) * o_sel
                         + _col(g, lane, hg + 2) * o_win[h])
                    res.append(jnp.where(k == (h % 2), o, pltpu.roll(o, HEAD_DIM, 1)))
                for t in range(2):
                    o_ref[0, pl.ds(r0, TQ), t * LANES:(t + 1) * LANES] = jnp.where(
                        lane < HEAD_DIM, res[2 * t], res[2 * t + 1])
        return 0

    lax.fori_loop(0, nq, qtile, 0)


def _nsa_prompt(nq, nkv, wkv, kcmp, vcmp, gates, tb, tc):
    b, s, _ = nq.shape
    col = lambda arr_cols, cb: pl.BlockSpec((1, s, LANES), lambda i, k: (i, 0, cb))
    return pl.pallas_call(
        _nsa_prompt_kernel,
        grid=(b, 2),
        in_specs=[pl.BlockSpec((1, s, 256), lambda i, k: (i, 0, k)),
                  col(512, 2), col(512, 3), col(256, 0), col(256, 1),
                  pl.BlockSpec((1, LANES, LANES), lambda i, k: (i, 0, 0)),
                  pl.BlockSpec((1, LANES, LANES), lambda i, k: (i, 0, 0)),
                  pl.BlockSpec((1, s, LANES), lambda i, k: (i, 0, 0)),
                  pl.BlockSpec((NSA_GROUP, N_BIAS_TILES, TQ, TQ), lambda i, k: (k, 0, 0, 0)),
                  pl.BlockSpec((NSA_GROUP, s, LANES), lambda i, k: (k, 0, 0))],
        out_specs=pl.BlockSpec((1, s, 256), lambda i, k: (i, 0, k)),
        out_shape=jax.ShapeDtypeStruct((b, s, 512), F32),
        scratch_shapes=[pltpu.VMEM((s, LANES), BF16)] * 4,
        compiler_params=_cparams(("arbitrary", "arbitrary")),
        name="nsa_prompt",
    )(nq, nkv, nkv, wkv, wkv, kcmp, vcmp, gates, tb, tc)


PAGES_PER_STEP = 8
CMP_PITCH = 40


def _rank_lt(score, lane, ncand, topk):
    rank = jnp.zeros(score.shape, F32)
    for m in range(ncand):
        col = score[:, m:m + 1]
        beats = (col > score) | ((col == score) & (m < lane))
        rank = rank + jnp.where(beats, 1.0, 0.0)
    return rank < topk


def _merge_blocks(sel, m_all, l_all, acc_scr, nblk, s_self, v_self):
    mx = jnp.maximum(jnp.max(jnp.where(sel, m_all, NEG), axis=1, keepdims=True), s_self)
    w = jnp.exp(jnp.where(sel, m_all - mx, NEG))
    w_self = jnp.exp(s_self - mx)
    den = jnp.sum(w * l_all, axis=1, keepdims=True) + w_self
    num = w_self * v_self
    for j in range(nblk):
        num = num + w[:, j:j + 1] * acc_scr[j]
    return num / den


def _moba_sample_kernel(pt_ref, *refs):
    pages = refs[:PAGES_PER_STEP]
    q_ref, kn_ref, vn_ref, tsb_ref, misc_ref, o_ref, g_scr, m_scr, l_scr, acc_scr = refs[PAGES_PER_STEP:]
    s = pl.program_id(1)
    nstep = pl.num_programs(1)
    nblk = PAST_LEN // MOBA_BLOCK
    flat = MOBA_BLOCK * MOBA_HEADS
    q8 = q_ref[0]
    qb = q8.astype(BF16)
    lane = lax.broadcasted_iota(jnp.int32, (MOBA_HEADS, LANES), 1)
    own_head = (lax.broadcasted_iota(jnp.int32, (MOBA_HEADS, flat), 1) % MOBA_HEADS
                == lax.broadcasted_iota(jnp.int32, (MOBA_HEADS, flat), 0))

    @pl.when(s == 0)
    def _():
        g_scr[...] = jnp.zeros(g_scr.shape, F32)
        m_scr[...] = jnp.zeros(m_scr.shape, F32)
        l_scr[...] = jnp.zeros(l_scr.shape, F32)

    for j in range(PAGES_PER_STEP // 2):
        blk = s * (PAGES_PER_STEP // 2) + j
        ka, kb_ = pages[2 * j][:, 0], pages[2 * j + 1][:, 0]
        va, vb_ = pages[2 * j][:, 1], pages[2 * j + 1][:, 1]
        kmean = (jnp.sum(ka, axis=0) + jnp.sum(kb_, axis=0)) * (1.0 / MOBA_BLOCK)
        gate = jnp.sum(q8 * kmean, axis=1, keepdims=True)
        k2 = jnp.concatenate([ka.reshape(flat // 2, HEAD_DIM), kb_.reshape(flat // 2, HEAD_DIM)], axis=0).astype(BF16)
        v2 = jnp.concatenate([va.reshape(flat // 2, HEAD_DIM), vb_.reshape(flat // 2, HEAD_DIM)], axis=0).astype(BF16)
        sc = _dot_nt(qb, k2) + jnp.where(blk == nblk - 1, tsb_ref[...], misc_ref[:, 1:2])
        sc = jnp.where(own_head, sc, NEG)
        mj = jnp.max(sc, axis=1, keepdims=True)
        p = jnp.exp(sc - mj)
        g_scr[...] = jnp.where(lane == blk, gate, g_scr[...])
        m_scr[...] = jnp.where(lane == blk, mj, m_scr[...])
        l_scr[...] = jnp.where(lane == blk, jnp.sum(p, axis=1, keepdims=True), l_scr[...])
        acc_scr[blk] = _dot(p.astype(BF16), v2)

    @pl.when(s == nstep - 1)
    def _():
        gm = jnp.where(lane < nblk, g_scr[...], -jnp.inf)
        sel = _rank_lt(gm, lane, nblk, MOBA_TOPK) & (lane < nblk)
        s_self = jnp.sum(q8 * kn_ref[0], axis=1, keepdims=True) + misc_ref[:, 0:1]
        o_ref[0] = _merge_blocks(sel, m_scr[...], l_scr[...], acc_scr, nblk, s_self, vn_ref[0])


def _moba_sample(page_table, cache, q8, knew, vnew, tsb, misc):
    nb, npages = page_table.shape
    nstep = npages // PAGES_PER_STEP
    nblk = PAST_LEN // MOBA_BLOCK

    def page_spec(j):
        return pl.BlockSpec((PAGE, 2, MOBA_HEADS, HEAD_DIM), lambda b, s, pt: (pt[b, s * PAGES_PER_STEP + j], 0, 0, 0))

    per_b = lambda shape: pl.BlockSpec((1,) + shape, lambda b, s, pt: (b, 0, 0))
    const = lambda shape: pl.BlockSpec(shape, lambda b, s, pt: (0,) * len(shape))
    hd = (MOBA_HEADS, HEAD_DIM)
    grid_spec = pltpu.PrefetchScalarGridSpec(
        num_scalar_prefetch=1,
        grid=(nb, nstep),
        in_specs=[page_spec(j) for j in range(PAGES_PER_STEP)]
        + [per_b(hd), per_b(hd), per_b(hd), const((MOBA_HEADS, MOBA_BLOCK * MOBA_HEADS)), const((MOBA_HEADS, LANES))],
        out_specs=per_b(hd),
        scratch_shapes=[pltpu.VMEM((MOBA_HEADS, LANES), F32)] * 3 + [pltpu.VMEM((nblk,) + hd, F32)],
    )
    return pl.pallas_call(
        _moba_sample_kernel,
        grid_spec=grid_spec,
        out_shape=jax.ShapeDtypeStruct((nb,) + hd, F32),
        compiler_params=_cparams(("arbitrary", "arbitrary")),
        name="moba_sample",
    )(page_table, *([cache] * PAGES_PER_STEP), q8, knew, vnew, tsb, misc)


def _nsa_sample_kernel(pt_ref, *refs):
    pages = refs[:PAGES_PER_STEP]
    (qm_ref, ksn_ref, vsn_ref, kwn_ref, vwn_ref, win_ref, g_ref, tsn_ref, misc_ref, tcs_ref, tws_ref,
     pos_ref, w1_ref, w2_ref, gkc_ref, avg_ref, o_ref, xk_scr, xv_scr, m_scr, l_scr, acc_scr) = refs[PAGES_PER_STEP:]
    s = pl.program_id(1)
    nstep = pl.num_programs(1)
    nsel = PAST_LEN // SEL_BLOCK
    ncmp = PAST_LEN // CMP_BLOCK
    npage = PAST_LEN // PAGE
    qm = qm_ref[0]
    qb = qm.astype(BF16)
    lane = lax.broadcasted_iota(jnp.int32, (NSA_HEADS, LANES), 1)
    row = lax.broadcasted_iota(jnp.int32, (NSA_HEADS, LANES), 0)
    lo = lane < HEAD_DIM

    @pl.when(s == 0)
    def _():
        m_scr[...] = jnp.zeros(m_scr.shape, F32)
        l_scr[...] = jnp.zeros(l_scr.shape, F32)

    for j in range(PAGES_PER_STEP):
        pg = s * PAGES_PER_STEP + j
        for b4 in range(PAGE // CMP_BLOCK):
            r0 = pl.multiple_of((pg * (PAGE // CMP_BLOCK) + b4) * CMP_PITCH, 8)
            xk_scr[pl.ds(r0, CMP_BLOCK), :] = pages[j][0, b4 * CMP_BLOCK:(b4 + 1) * CMP_BLOCK, 0:LANES]
            xv_scr[pl.ds(r0, CMP_BLOCK), :] = pages[j][0, b4 * CMP_BLOCK:(b4 + 1) * CMP_BLOCK, LANES:2 * LANES]
        ks = pages[j][0, :, 256:384].astype(BF16)
        vs = pages[j][0, :, 384:512].astype(BF16)
        sc = _dot_nt(qb, ks) + jnp.where(pg == npage - 1, tsn_ref[...], misc_ref[:, 1:2])
        m0 = jnp.max(jnp.where(lo, sc, NEG), axis=1, keepdims=True)
        m1 = jnp.max(jnp.where(lo, NEG, sc), axis=1, keepdims=True)
        p = jnp.exp(sc - jnp.where(lo, m0, m1))
        l0 = jnp.sum(jnp.where(lo, p, 0.0), axis=1, keepdims=True)
        l1 = jnp.sum(jnp.where(lo, 0.0, p), axis=1, keepdims=True)
        b0 = 2 * pg
        m_scr[...] = jnp.where(lane == b0, m0, jnp.where(lane == b0 + 1, m1, m_scr[...]))
        l_scr[...] = jnp.where(lane == b0, l0, jnp.where(lane == b0 + 1, l1, l_scr[...]))
        acc_scr[b0] = _dot(jnp.where(lo, p, 0.0).astype(BF16), vs)
        acc_scr[b0 + 1] = _dot(jnp.where(lo, 0.0, p).astype(BF16), vs)

    @pl.when(s == nstep - 1)
    def _():
        ck, cv = _compress_tokens(lambda r: xk_scr[pl.ds(r, ncmp, stride=CMP_PITCH), :],
                                  lambda r: xv_scr[pl.ds(r, ncmp, stride=CMP_PITCH), :], pos_ref, w1_ref, w2_ref)
        ck = ck * lax.rsqrt(_group_mean_sq(ck, avg_ref[...]) + EPS) * gkc_ref[...]
        sc = _dot_nt(qb, ck.astype(BF16)) + tcs_ref[...]
        m = jnp.maximum(jnp.max(sc, axis=1, keepdims=True), M_INIT)
        pc = jnp.exp(sc - m)
        pc = pc / jnp.maximum(jnp.sum(pc, axis=1, keepdims=True), 1e-30)
        o_cmp = _dot(pc.astype(BF16), cv.astype(BF16))
        g0 = pc[0:1] + pc[1:2] + pc[2:3] + pc[3:4]
        g1 = pc[4:5] + pc[5:6] + pc[6:7] + pc[7:8]
        rowc = lax.broadcasted_iota(jnp.int32, (NSA_HEADS, ncmp), 0)
        imp = jnp.where(rowc < NSA_GROUP, g0, g1)
        pr = lax.broadcasted_iota(jnp.int32, (ncmp, LANES), 0)
        pc_ = lax.broadcasted_iota(jnp.int32, (ncmp, LANES), 1)
        pair = jnp.where(pr // (SEL_BLOCK // CMP_BLOCK) == pc_, 1.0, 0.0)
        impb = _dot(imp, pair, precision=HIGHEST)
        own = PAST_LEN // SEL_BLOCK
        sel = _rank_lt(jnp.where(lane < own, impb, -jnp.inf), lane, nsel, SEL_TOPK) & (lane < own)
        s_self = jnp.sum(qm * ksn_ref[0], axis=1, keepdims=True) + misc_ref[:, 0:1]
        o_sel = _merge_blocks(sel, m_scr[...], l_scr[...], acc_scr, nsel, s_self, vsn_ref[0])
        kw = win_ref[0, :, 0:LANES].astype(BF16)
        vw = win_ref[0, :, LANES:2 * LANES].astype(BF16)
        sw = _dot_nt(qb, kw) + tws_ref[...]
        sw_self = jnp.sum(qm * kwn_ref[0], axis=1, keepdims=True) + misc_ref[:, 0:1]
        mw = jnp.maximum(jnp.max(sw, axis=1, keepdims=True), sw_self)
        pw = jnp.exp(sw - mw)
        pw_self = jnp.exp(sw_self - mw)
        o_win = (_dot(pw.astype(BF16), vw) + pw_self * vwn_ref[0]) / (jnp.sum(pw, axis=1, keepdims=True) + pw_self)
        gt = jnp.broadcast_to(g_ref[0], (NSA_HEADS, LANES))
        o8 = (_col(gt, lane, 3 * row) * o_cmp + _col(gt, lane, 3 * row + 1) * o_sel
              + _col(gt, lane, 3 * row + 2) * o_win)
        lane1 = lax.broadcasted_iota(jnp.int32, (1, LANES), 1)
        tiles = []
        for t in range(NSA_HEADS // 2):
            ha, hb = 2 * t, 2 * t + 1
            ra = o8[ha:ha + 1, :]
            rb = o8[hb:hb + 1, :]
            if ha // NSA_GROUP == 1:
                ra = pltpu.roll(ra, HEAD_DIM, 1)
            if hb // NSA_GROUP == 0:
                rb = pltpu.roll(rb, HEAD_DIM, 1)
            tiles.append(jnp.where(lane1 < HEAD_DIM, ra, rb))
        o_ref[0] = jnp.concatenate(tiles, axis=1)


def _nsa_sample(page_table, cache, qmat, ksn, vsn, kwn, vwn, win, gates, tsn, misc, tcs, tws,
                pos, w1bd, w2bd, gkc, avg):
    nb, npages = page_table.shape
    nstep = npages // PAGES_PER_STEP
    nsel = PAST_LEN // SEL_BLOCK

    def page_spec(j):
        return pl.BlockSpec((1, PAGE, 512), lambda b, s, pt: (pt[b, s * PAGES_PER_STEP + j], 0, 0))

    per_b = lambda shape: pl.BlockSpec((1,) + shape, lambda b, s, pt: (b, 0, 0))
    const = lambda shape: pl.BlockSpec(shape, lambda b, s, pt: (0,) * len(shape))
    cmp_rows = PAST_LEN // CMP_BLOCK * CMP_PITCH
    grid_spec = pltpu.PrefetchScalarGridSpec(
        num_scalar_prefetch=1,
        grid=(nb, nstep),
        in_specs=[page_spec(j) for j in range(PAGES_PER_STEP)]
        + [per_b((NSA_HEADS, LANES)), per_b((1, LANES)), per_b((1, LANES)), per_b((1, LANES)), per_b((1, LANES)),
           per_b((WINDOW, 256)), per_b((1, LANES)),
           const((NSA_HEADS, PAGE)), const((NSA_HEADS, LANES)), const((NSA_HEADS, PAST_LEN // CMP_BLOCK)),
           const((NSA_HEADS, WINDOW)),
           const((CMP_BLOCK, 256)), const((2, CMP_BLOCK // 2, 256, 256)), const((2, 256, LANES)),
           const((1, LANES)), const((LANES, LANES))],
        out_specs=per_b((1, 512)),
        scratch_shapes=[pltpu.VMEM((cmp_rows, LANES), F32), pltpu.VMEM((cmp_rows, LANES), F32),
                        pltpu.VMEM((NSA_HEADS, LANES), F32),
                        pltpu.VMEM((NSA_HEADS, LANES), F32), pltpu.VMEM((nsel, NSA_HEADS, LANES), F32)],
    )
    return pl.pallas_call(
        _nsa_sample_kernel,
        grid_spec=grid_spec,
        out_shape=jax.ShapeDtypeStruct((nb, 1, 512), F32),
        compiler_params=_cparams(("arbitrary", "arbitrary")),
        name="nsa_sample",
    )(page_table, *([cache] * PAGES_PER_STEP), qmat, ksn, vsn, kwn, vwn, win, gates, tsn, misc, tcs, tws,
      pos, w1bd, w2bd, gkc, avg)


def _outproj_kernel(x_ref, mod_ref, om_ref, on_ref, w_ref, o_ref):
    y = _dot(om_ref[0].astype(BF16), w_ref[0:512, :]) + _dot(on_ref[0].astype(BF16), w_ref[512:1024, :])
    o_ref[0] = x_ref[0] + mod_ref[0][:, 2 * D:3 * D] * y


def _outproj(x, mod, o_m, o_n, w, tm):
    b, t, _ = x.shape
    tmod = mod.shape[1]
    row = lambda width: pl.BlockSpec((1, tm, width), lambda i, j: (i, j, 0))
    return pl.pallas_call(
        _outproj_kernel,
        grid=(b, t // tm),
        in_specs=[row(D), pl.BlockSpec((1, tmod, 3 * D), lambda i, j: (i, 0, 0)), row(512), row(512),
                  pl.BlockSpec((D, D), lambda i, j: (0, 0))],
        out_specs=row(D),
        out_shape=jax.ShapeDtypeStruct((b, t, D), F32),
        compiler_params=_cparams(("arbitrary", "arbitrary")),
        name="attn_outproj",
    )(x, mod, o_m, o_n, w)


def _mlp_kernel(x_ref, mod_ref, g_ref, w1_ref, w2_ref, o_ref, h_scr, acc_scr):
    kf = pl.program_id(2)

    @pl.when(kf == 0)
    def _():
        mod = mod_ref[0]
        h_scr[...] = _modulate(x_ref[0], g_ref[...], mod[:, 0:D], mod[:, D:2 * D]).astype(BF16)
        acc_scr[...] = jnp.zeros(acc_scr.shape, F32)

    a = jnp.square(jnp.maximum(_dot(h_scr[...], w1_ref[...]), 0.0))
    acc_scr[...] += _dot(a.astype(BF16), w2_ref[...])

    @pl.when(kf == pl.num_programs(2) - 1)
    def _():
        o_ref[0] = x_ref[0] + mod_ref[0][:, 2 * D:3 * D] * acc_scr[...]


def _mlp(x, mod, g, w1, w2, tm, tf):
    b, t, _ = x.shape
    tmod = mod.shape[1]
    return pl.pallas_call(
        _mlp_kernel,
        grid=(b, t // tm, D_FF // tf),
        in_specs=[pl.BlockSpec((1, tm, D), lambda i, j, kf: (i, j, 0)),
                  pl.BlockSpec((1, tmod, 3 * D), lambda i, j, kf: (i, 0, 0)),
                  pl.BlockSpec((1, D), lambda i, j, kf: (0, 0)),
                  pl.BlockSpec((D, tf), lambda i, j, kf: (0, kf)),
                  pl.BlockSpec((tf, D), lambda i, j, kf: (kf, 0))],
        out_specs=pl.BlockSpec((1, tm, D), lambda i, j, kf: (i, j, 0)),
        out_shape=jax.ShapeDtypeStruct((b, t, D), F32),
        scratch_shapes=[pltpu.VMEM((tm, D), BF16), pltpu.VMEM((tm, D), F32)],
        compiler_params=_cparams(("arbitrary", "arbitrary", "arbitrary")),
        name="mlp",
    )(x, mod, g, w1, w2)


def _s5_disc_kernel(are_ref, aim_ref, ldt_ref, bre_ref, bim_ref, abre_ref, abim_ref, bbre_ref, bbim_ref):
    a_re, a_im = are_ref[...], aim_ref[...]
    dt = jnp.exp(ldt_ref[...])
    decay = jnp.exp(dt * a_re)
    ab_re, ab_im = decay * jnp.cos(dt * a_im), decay * jnp.sin(dt * a_im)
    den = a_re * a_re + a_im * a_im
    f_re = ((ab_re - 1) * a_re + ab_im * a_im) / den
    f_im = (ab_im * a_re - (ab_re - 1) * a_im) / den
    br, bi = bre_ref[...], bim_ref[...]
    abre_ref[...] = ab_re
    abim_ref[...] = ab_im
    bbre_ref[...] = f_re * br - f_im * bi
    bbim_ref[...] = f_re * bi + f_im * br


def _s5_discretize(a_re, a_im, log_dt, b_re, b_im):
    rep = lambda a: jnp.repeat(a, S5_GROUP_CH, axis=1)
    shp = jax.ShapeDtypeStruct((S5_GROUPS, S5_STATE * S5_GROUP_CH), F32)
    ldt = jnp.broadcast_to(log_dt[:, None], (S5_GROUPS, S5_STATE * S5_GROUP_CH))
    flat = lambda a: a.reshape(S5_GROUPS, S5_STATE * S5_GROUP_CH)
    ab_re, ab_im, bb_re, bb_im = pl.pallas_call(
        _s5_disc_kernel, out_shape=[shp] * 4, name="s5_discretize",
    )(rep(a_re), rep(a_im), ldt, flat(b_re), flat(b_im))
    unrep = lambda a: a[:, ::S5_GROUP_CH]
    unflat = lambda a: a.reshape(S5_GROUPS, S5_STATE, S5_GROUP_CH)
    return unrep(ab_re), unrep(ab_im), unflat(bb_re), unflat(bb_im)


def _modulate_tm_kernel(x_ref, mod_ref, g_ref, o_ref):
    mod = mod_ref[0]
    o_ref[...] = _modulate(x_ref[0], g_ref[...], mod[:, 0:D], mod[:, D:2 * D])


def _modulate_time_major(x, mod, g, tl):
    b, t, _ = x.shape
    tmod = mod.shape[1]
    return pl.pallas_call(
        _modulate_tm_kernel,
        grid=(b, t // tl),
        in_specs=[pl.BlockSpec((1, tl, D), lambda i, j: (i, j, 0)),
                  pl.BlockSpec((1, tmod, 3 * D), lambda i, j: (i, 0, 0)),
                  pl.BlockSpec((1, D), lambda i, j: (0, 0))],
        out_specs=pl.BlockSpec((tl, D), lambda i, j: (j, i)),
        out_shape=jax.ShapeDtypeStruct((t, b * D), F32),
        compiler_params=_cparams(("arbitrary", "arbitrary")),
        name="s5_modulate",
    )(x, mod, g)


S5_CB = 256
S5_NS = S5_CB // S5_GROUP_CH * S5_STATE


def _s5_scan_kernel(h_ref, wb_ref, wc_ref, ar_ref, ai_ref, d_ref, h0_ref, y_ref, so_ref, xs_scr, st_scr, *, tl, r):
    i = pl.program_id(1)

    @pl.when(i == 0)
    def _():
        st_scr[...] = h0_ref[...]

    u = h_ref[...]
    xs_scr[...] = _dot(u.astype(BF16), wb_ref[0])
    ar = jnp.broadcast_to(ar_ref[0], (r, S5_NS))
    ai = jnp.broadcast_to(ai_ref[0], (r, S5_NS))

    def step(t, carry):
        xr, xi = carry
        r0 = pl.multiple_of(t * r, r)
        nr = ar * xr - ai * xi + xs_scr[pl.ds(r0, r), 0:S5_NS]
        ni = ar * xi + ai * xr + xs_scr[pl.ds(r0, r), S5_NS:2 * S5_NS]
        xs_scr[pl.ds(r0, r), 0:S5_NS] = nr
        xs_scr[pl.ds(r0, r), S5_NS:2 * S5_NS] = ni
        return nr, ni

    xr, xi = lax.fori_loop(0, tl, step, (st_scr[0], st_scr[1]))
    st_scr[0] = xr
    st_scr[1] = xi
    y_ref[...] = _dot(xs_scr[...].astype(BF16), wc_ref[0]) + d_ref[...] * u

    @pl.when(i == pl.num_programs(1) - 1)
    def _():
        so_ref[...] = st_scr[...]


def _s5_scan(h_tm, wb, wc, ar, ai, d_skip, h0, r, tl):
    rows = h_tm.shape[0]
    nj = D // S5_CB
    return pl.pallas_call(
        functools.partial(_s5_scan_kernel, tl=tl, r=r),
        grid=(nj, rows // (tl * r)),
        in_specs=[pl.BlockSpec((tl * r, S5_CB), lambda j, i: (i, j)),
                  pl.BlockSpec((1, S5_CB, 2 * S5_NS), lambda j, i: (j, 0, 0)),
                  pl.BlockSpec((1, 2 * S5_NS, S5_CB), lambda j, i: (j, 0, 0)),
                  pl.BlockSpec((1, 1, S5_NS), lambda j, i: (j, 0, 0)),
                  pl.BlockSpec((1, 1, S5_NS), lambda j, i: (j, 0, 0)),
                  pl.BlockSpec((1, S5_CB), lambda j, i: (0, j)),
                  pl.BlockSpec((2, r, S5_NS), lambda j, i: (0, 0, j))],
        out_specs=[pl.BlockSpec((tl * r, S5_CB), lambda j, i: (i, j)),
                   pl.BlockSpec((2, r, S5_NS), lambda j, i: (0, 0, j))],
        out_shape=[jax.ShapeDtypeStruct((rows, D), F32), jax.ShapeDtypeStruct((2, r, S5_GROUPS * S5_STATE), F32)],
        scratch_shapes=[pltpu.VMEM((tl * r, 2 * S5_NS), F32), pltpu.VMEM((2, r, S5_NS), F32)],
        compiler_params=_cparams(("arbitrary", "arbitrary")),
        name="s5_scan",
    )(h_tm, wb, wc, ar, ai, d_skip, h0)


def _glu_kernel(y_ref, x_ref, mod_ref, w_ref, o_ref):
    z = _dot(jax.nn.gelu(y_ref[...]).astype(BF16), w_ref[...])
    o_ref[0] = x_ref[0] + mod_ref[0][:, 2 * D:3 * D] * (z[:, 0:D] * jax.nn.sigmoid(z[:, D:2 * D]))


def _glu_residual(y_tm, x, mod, w, tl):
    b, t, _ = x.shape
    tmod = mod.shape[1]
    return pl.pallas_call(
        _glu_kernel,
        grid=(b, t // tl),
        in_specs=[pl.BlockSpec((tl, D), lambda i, j: (j, i)),
                  pl.BlockSpec((1, tl, D), lambda i, j: (i, j, 0)),
                  pl.BlockSpec((1, tmod, 3 * D), lambda i, j: (i, 0, 0)),
                  pl.BlockSpec((D, 2 * D), lambda i, j: (0, 0))],
        out_specs=pl.BlockSpec((1, tl, D), lambda i, j: (i, j, 0)),
        out_shape=jax.ShapeDtypeStruct((b, t, D), F32),
        compiler_params=_cparams(("arbitrary", "arbitrary")),
        name="s5_glu",
    )(y_tm, x, mod, w)


def _s5_block_weights(bb_re, bb_im, c_re, c_im):
    nj, ng = D // S5_CB, S5_CB // S5_GROUP_CH
    eye = jnp.eye(ng, dtype=F32)

    def wb_part(bb):
        t = bb.reshape(nj, ng, S5_STATE, S5_GROUP_CH).transpose(0, 1, 3, 2)
        return jnp.einsum("jgcn,gh->jgchn", t, eye).reshape(nj, S5_CB, S5_NS)

    def wc_part(c):
        t = c.reshape(nj, ng, S5_GROUP_CH, S5_STATE).transpose(0, 1, 3, 2)
        return jnp.einsum("jgnc,gh->jgnhc", t, eye).reshape(nj, S5_NS, S5_CB)

    wb = jnp.concatenate([wb_part(bb_re), wb_part(bb_im)], axis=2).astype(BF16)
    wc = jnp.concatenate([wc_part(c_re), -wc_part(c_im)], axis=1).astype(BF16)
    return wb, wc


def _s5_layer(x, mod, g, wb, wc, ar, ai, d_skip, h0, w_glu, r, tl):
    b, t, _ = x.shape
    h_tm = _modulate_time_major(x, mod, g, tl).reshape(t * b, D)
    h0_t = h0.reshape(b, 2, S5_GROUPS * S5_STATE).transpose(1, 0, 2)
    y, st = _s5_scan(h_tm, wb, wc, ar, ai, d_skip, h0_t, r, tl)
    x_new = _glu_residual(y.reshape(t, b * D), x, mod, w_glu, tl)
    return x_new, st.transpose(1, 0, 2).reshape(b, 2, S5_GROUPS, S5_STATE)


def _dist_tiles():
    r = jnp.arange(TQ, dtype=jnp.int32)[:, None]
    c = jnp.arange(TQ, dtype=jnp.int32)[None, :]
    d0 = r - c
    edge = 2 * TQ + r - c
    return jnp.concatenate([d0, TQ + d0, 2 * TQ + d0, jnp.where(edge <= WINDOW, edge, -1),
                            jnp.full((TQ, TQ), -1, jnp.int32)], axis=0)


def _dist_cmp(seq):
    q = jnp.arange(seq, dtype=jnp.int32)[:, None]
    n = jnp.arange(LANES, dtype=jnp.int32)[None, :]
    return jnp.where(n < seq // CMP_BLOCK, q - ((n + 1) * CMP_BLOCK - 1), -1)


_SAMPLE_TABLE_SIZES = (LANES, MOBA_BLOCK * MOBA_HEADS, PAGE, PAST_LEN // CMP_BLOCK, WINDOW)


def _dist_sample():
    ar = lambda n: jnp.arange(n, dtype=jnp.int32)
    misc = jnp.zeros((LANES,), jnp.int32).at[1].set(MAX_DISTANCE * 4)
    moba = MOBA_BLOCK - ar(MOBA_BLOCK * MOBA_HEADS) // MOBA_HEADS
    sel = PAGE - ar(PAGE)
    cmp_ = PAST_LEN - ((ar(PAST_LEN // CMP_BLOCK) + 1) * CMP_BLOCK - 1)
    win = WINDOW - ar(WINDOW)
    return jnp.concatenate([misc, moba, sel, cmp_, win])[None, :]


def _block_diag2(w):
    z = jnp.zeros_like(w)
    return jnp.concatenate([jnp.concatenate([w, z], axis=-1), jnp.concatenate([z, w], axis=-1)], axis=-2)


def kernel(x_prompt, x_sample, cache_moba_kv, cache_nsa_kv, state_nsa_win, state_s5, page_table, c_prompt, c_sample, rel_bias, attn_norm_g, attn_ada_w, attn_ada_b, attn_w_in, attn_qk_g, nsa_cmp_pos, nsa_cmp_w1, nsa_cmp_w2, attn_w_out, ssm_norm_g, ssm_ada_w, ssm_ada_b, s5_a_re, s5_a_im, s5_log_dt, s5_b_re, s5_b_im, s5_c_re, s5_c_im, s5_d, s5_w_glu, mlp_norm_g, mlp_ada_w, mlp_ada_b, mlp_w1, mlp_w2):
    bp, seq, _ = x_prompt.shape
    bs = x_sample.shape[0]
    assert seq % TQ == 0 and x_sample.shape[1] == 1
    n_pool = cache_moba_kv.shape[1]

    c_all = jnp.concatenate([c_prompt, c_sample], axis=0)
    split_mod = lambda m: (m[:bp, None, :], m[None, bp:, :])
    mod_attn = _adaln(c_all, attn_ada_w, attn_ada_b)
    mod_ssm = _adaln(c_all, ssm_ada_w, ssm_ada_b)
    mod_mlp = _adaln(c_all, mlp_ada_w, mlp_ada_b)

    xp = x_prompt
    xs = x_sample.reshape(1, bs, D)

    tb = _bias_table(rel_bias, _dist_tiles()).reshape(2 * MOBA_HEADS, N_BIAS_TILES, TQ, TQ)
    tc = _bias_table(rel_bias, _dist_cmp(seq))[MOBA_HEADS:]
    ts = _bias_table(rel_bias, _dist_sample())[:, 0, :]
    offs = [0]
    for size in _SAMPLE_TABLE_SIZES:
        offs.append(offs[-1] + size)
    part = lambda heads, t: ts[heads, offs[t]:offs[t + 1]]
    hm, hn = slice(0, MOBA_HEADS), slice(MOBA_HEADS, 2 * MOBA_HEADS)
    misc_m, misc_n, tsb, tsn, tcs, tws = part(hm, 0), part(hn, 0), part(hm, 1), part(hn, 2), part(hn, 3), part(hn, 4)

    w_in = jnp.pad(attn_w_in[0], ((0, 0), (0, IN_COLS_PAD - IN_COLS))).astype(BF16)
    qkg_t = jnp.pad(jnp.tile(attn_qk_g[0], (1, 2)), ((0, 2), (0, 0)))
    lr = jnp.arange(LANES)
    avg = jnp.where(lr[:, None] // HEAD_DIM == lr[None, :] // HEAD_DIM, 1.0 / HEAD_DIM, 0.0).astype(BF16)
    g_attn = attn_norm_g[0][None, :]
    w_out = attn_w_out[0].astype(BF16)
    pos = jnp.concatenate([nsa_cmp_pos[0, 0], nsa_cmp_pos[0, 0], nsa_cmp_pos[0, 1], nsa_cmp_pos[0, 1]], axis=1)
    w1bd = _block_diag2(nsa_cmp_w1[0].reshape(2, CMP_BLOCK, HEAD_DIM, CMP_HIDDEN)).astype(BF16)
    w1bd = w1bd.reshape(2, CMP_BLOCK // 2, 256, 256)
    w2bd = _block_diag2(nsa_cmp_w2[0]).astype(BF16)
    gkc = qkg_t[3:4]

    mp_attn, ms_attn = split_mod(mod_attn[0])
    mq, mkv, nq, nkv, wkv, gates = _attn_proj(xp, mp_attn, g_attn, w_in, qkg_t, avg, 512)
    o_moba = _moba_prompt(mq, mkv, tb[:MOBA_HEADS])
    kcmp, vcmp = _cmp_prompt(nkv, pos, w1bd, w2bd, gkc, avg)
    o_nsa = _nsa_prompt(nq, nkv, wkv, kcmp, vcmp, gates, tb[MOBA_HEADS:], tc)
    xp = _outproj(xp, mp_attn, o_moba, o_nsa, w_out, 512)
    npg = seq // PAGE
    moba_p = mkv.reshape(1, bp, npg, PAGE, 2, MOBA_HEADS, HEAD_DIM)
    nsa_p = nkv.reshape(1, bp, npg, PAGE, 4, 2, HEAD_DIM)
    win_p = wkv[:, seq - min(WINDOW, seq):].reshape(1, bp, min(WINDOW, seq), 2, 2, HEAD_DIM)
    mq_s, mkv_s, nq_s, nkv_s, wkv_s, gates_s = _attn_proj(xs, ms_attn, g_attn, w_in, qkg_t, avg, bs)
    heads = lambda a, lo: a[0][:, lo:lo + 512].reshape(bs, MOBA_HEADS, HEAD_DIM)
    o_moba_s = _moba_sample(page_table, cache_moba_kv.reshape(n_pool * PAGE, 2, MOBA_HEADS, HEAD_DIM),
                            heads(mq_s, 0), heads(mkv_s, 0), heads(mkv_s, 512), tsb, misc_m)
    nq4 = nq_s[0].reshape(bs, NSA_HEADS, HEAD_DIM)
    kvh = jnp.arange(NSA_HEADS) // NSA_GROUP
    qmat_n = jnp.concatenate([jnp.where(kvh[None, :, None] == 0, nq4, 0.0),
                              jnp.where(kvh[None, :, None] == 1, nq4, 0.0)], axis=2)
    col3 = lambda a, lo: a[0][:, None, lo:lo + LANES]
    o_nsa_s = _nsa_sample(page_table, cache_nsa_kv.reshape(n_pool, PAGE, 512), qmat_n,
                          col3(nkv_s, 256), col3(nkv_s, 384), col3(wkv_s, 0), col3(wkv_s, 128),
                          state_nsa_win[0].reshape(bs, WINDOW, 256), gates_s.reshape(bs, 1, LANES),
                          tsn, misc_n, tcs, tws, pos, w1bd, w2bd, gkc, avg)
    xs = _outproj(xs, ms_attn, o_moba_s.reshape(1, bs, 512), o_nsa_s.reshape(1, bs, 512), w_out, bs)
    moba_s = mkv_s.reshape(1, bs, 1, 2, MOBA_HEADS, HEAD_DIM)
    nsa_s = nkv_s.reshape(1, bs, 1, 4, 2, HEAD_DIM)
    win_s = jnp.concatenate([state_nsa_win[0][:, 1:], wkv_s[0].reshape(bs, 1, 2, 2, HEAD_DIM)], axis=1)[None]

    w1_0, w2_0 = mlp_w1[0].astype(BF16), mlp_w2[0].astype(BF16)
    mp_mlp, ms_mlp = split_mod(mod_mlp[0])
    g_mlp0 = mlp_norm_g[0][None, :]
    xp = _mlp(xp, mp_mlp, g_mlp0, w1_0, w2_0, 512, 1024)
    xs = _mlp(xs, ms_mlp, g_mlp0, w1_0, w2_0, bs, 1024)

    ab_re, ab_im, bb_re, bb_im = _s5_discretize(s5_a_re[0], s5_a_im[0], s5_log_dt[0], s5_b_re[0], s5_b_im[0])
    wb, wc = _s5_block_weights(bb_re, bb_im, s5_c_re[0], s5_c_im[0])
    nj = D // S5_CB
    ar = ab_re.reshape(nj, 1, S5_NS)
    ai = ab_im.reshape(nj, 1, S5_NS)
    g_ssm = ssm_norm_g[0][None, :]
    d_skip = s5_d[0][None, :]
    w_glu = s5_w_glu[0].astype(BF16)
    mp_ssm, ms_ssm = split_mod(mod_ssm[0])
    xp, st_p = _s5_layer(xp, mp_ssm, g_ssm, wb, wc, ar, ai, d_skip,
                         jnp.zeros((bp, 2, S5_GROUPS, S5_STATE), F32), w_glu, bp, 128)
    h_s = _modulate_time_major(xs, ms_ssm, g_ssm, bs)
    y_s, st_s = _s5_scan(h_s, wb, wc, ar, ai, d_skip,
                         state_s5[0].reshape(bs, 2, S5_GROUPS * S5_STATE).transpose(1, 0, 2), bs, 1)
    xs = _glu_residual(y_s, xs, ms_ssm, w_glu, bs)
    st_s = st_s.transpose(1, 0, 2).reshape(bs, 2, S5_GROUPS, S5_STATE)

    w1_1, w2_1 = mlp_w1[1].astype(BF16), mlp_w2[1].astype(BF16)
    mp_mlp, ms_mlp = split_mod(mod_mlp[1])
    g_mlp1 = mlp_norm_g[1][None, :]
    xp = _mlp(xp, mp_mlp, g_mlp1, w1_1, w2_1, 512, 1024)
    xs = _mlp(xs, ms_mlp, g_mlp1, w1_1, w2_1, bs, 1024)

    return (xp, xs.reshape(bs, 1, D), moba_p, moba_s, nsa_p, nsa_s, win_p, win_s, st_p[None], st_s[None])
```

```python
import functools
import math

import jax
import jax.numpy as jnp
from jax import lax
from jax.experimental import pallas as pl
from jax.experimental.pallas import tpu as pltpu

F32 = jnp.float32
BF16 = jnp.bfloat16
HIGHEST = lax.Precision.HIGHEST

D = 1024
HEAD_DIM = 64
MOBA_HEADS = 8
NSA_HEADS = 8
NSA_GROUP = 4
MOBA_BLOCK = 256
MOBA_TOPK = 3
CMP_BLOCK = 32
CMP_HIDDEN = 128
SEL_BLOCK = 64
SEL_TOPK = 16
WINDOW = 512
NUM_BUCKETS = 32
MAX_DISTANCE = 128
PAGE = 128
PAST_LEN = 8192
D_FF = 4 * D
S5_GROUPS = 64
S5_STATE = 64
S5_GROUP_CH = 16
IN_COLS = 3 * 512 + 512 + 6 * 128 + 3 * NSA_HEADS
IN_COLS_PAD = 23 * 128
EPS = 1e-6
SCALE = HEAD_DIM ** -0.5
LANES = 128
TQ = 256
N_BIAS_TILES = 5
NEG = -1e30
M_INIT = -1e15
VMEM_LIMIT = 56 * 1024 * 1024

_NT = (((1,), (1,)), ((), ()))


def _cparams(sem):
    return pltpu.CompilerParams(dimension_semantics=sem, vmem_limit_bytes=VMEM_LIMIT)


def _dot(a, b, **kw):
    return jnp.dot(a, b, preferred_element_type=F32, **kw)


def _dot_nt(a, b, **kw):
    return lax.dot_general(a, b, _NT, preferred_element_type=F32, **kw)


def _modulate(x, g, shift, scale):
    ms = jnp.mean(x * x, axis=-1, keepdims=True)
    return x * lax.rsqrt(ms + EPS) * g * (1.0 + scale) + shift


def _group_mean_sq(z, avg):
    sq = z * z
    hi = sq.astype(BF16)
    lo = (sq - hi.astype(F32)).astype(BF16)
    return _dot(hi, avg) + _dot(lo, avg)


def _col(x, lane, idx):
    return jnp.sum(jnp.where(lane == idx, x, 0.0), axis=1, keepdims=True)


def _adaln_kernel(c_ref, w_ref, b_ref, o_ref):
    c = c_ref[...]
    s = c * jax.nn.sigmoid(c)
    o_ref[0] = _dot(s, w_ref[0], precision=HIGHEST) + b_ref[0]


def _adaln(c_all, w, b):
    nl, n = w.shape[0], c_all.shape[0]
    return pl.pallas_call(
        _adaln_kernel,
        grid=(nl, 3),
        in_specs=[pl.BlockSpec((n, D), lambda l, j: (0, 0)),
                  pl.BlockSpec((1, D, D), lambda l, j: (l, 0, j)),
                  pl.BlockSpec((1, 1, D), lambda l, j: (l, 0, j))],
        out_specs=pl.BlockSpec((1, n, D), lambda l, j: (l, 0, j)),
        out_shape=jax.ShapeDtypeStruct((nl, n, 3 * D), F32),
        compiler_params=_cparams(("arbitrary", "arbitrary")),
        name="adaln",
    )(c_all, w, b.reshape(nl, 1, 3 * D))


def _bias_kernel(rb_ref, d_ref, o_ref):
    h = pl.program_id(0)
    dist = d_ref[...]
    n = jnp.maximum(dist, 0)
    max_exact = NUM_BUCKETS // 2
    nf = jnp.maximum(n, 1).astype(F32)
    large = max_exact + (jnp.log(nf / max_exact) / math.log(MAX_DISTANCE / max_exact)
                         * (NUM_BUCKETS - max_exact)).astype(jnp.int32)
    large = jnp.minimum(large, NUM_BUCKETS - 1)
    bucket = jnp.where(n < max_exact, n, large)
    acc = jnp.zeros(dist.shape, F32)
    for k in range(NUM_BUCKETS):
        acc = jnp.where(bucket == k, rb_ref[k, h], acc)
    o_ref[0] = jnp.where(dist < 0, NEG, acc)


def _bias_table(rel_bias, dist):
    r, c = dist.shape
    nh = rel_bias.shape[1]
    return pl.pallas_call(
        _bias_kernel,
        grid=(nh,),
        in_specs=[pl.BlockSpec(memory_space=pltpu.SMEM),
                  pl.BlockSpec((r, c), lambda h: (0, 0))],
        out_specs=pl.BlockSpec((1, r, c), lambda h: (h, 0, 0)),
        out_shape=jax.ShapeDtypeStruct((nh, r, c), F32),
        compiler_params=_cparams(("arbitrary",)),
        name="bias_table",
    )(rel_bias, dist)


def _proj_kernel(x_ref, mod_ref, g_ref, w_ref, qkg_ref, avg_ref,
                 mq_ref, mkv_ref, nq_ref, nkv_ref, wkv_ref, gt_ref, *page_major_refs):
    x = x_ref[0]
    mod = mod_ref[0]
    h = _modulate(x, g_ref[...], mod[:, 0:D], mod[:, D:2 * D])
    z = _dot(h.astype(BF16), w_ref[...])
    avg = avg_ref[...]

    def normed(lo, gi):
        zs = z[:, lo:lo + LANES]
        return zs * lax.rsqrt(_group_mean_sq(zs, avg) + EPS) * qkg_ref[gi:gi + 1, :]

    for t in range(4):
        mq_ref[0, :, t * LANES:(t + 1) * LANES] = normed(t * LANES, 0) * SCALE
        mkv_ref[0, :, t * LANES:(t + 1) * LANES] = normed(512 + t * LANES, 1)
        nq_ref[0, :, t * LANES:(t + 1) * LANES] = normed(1536 + t * LANES, 2) * SCALE
    mkv_ref[0, :, 512:1024] = z[:, 1024:1536]
    nkv_ref[0, :, 0:256] = z[:, 2048:2304]
    nkv_ref[0, :, 256:384] = normed(2304, 4)
    nkv_ref[0, :, 384:512] = z[:, 2432:2560]
    wkv_ref[0, :, 0:128] = normed(2560, 5)
    wkv_ref[0, :, 128:256] = z[:, 2688:2816]
    gt_ref[0] = jax.nn.sigmoid(z[:, 2816:2944])
    if page_major_refs:
        mkv_t_ref, nkv_t_ref = page_major_refs
        for p in range(x.shape[0] // PAGE):
            mkv_t_ref[0, p] = mkv_ref[0, p * PAGE:(p + 1) * PAGE, :].T
            nkv_t_ref[0, p] = nkv_ref[0, p * PAGE:(p + 1) * PAGE, :].T


def _attn_proj(x, mod, g, w_pad, qkg_t, avg, tm, page_major=False):
    b, t, _ = x.shape
    tmod = mod.shape[1]
    row = lambda width: pl.BlockSpec((1, tm, width), lambda i, j: (i, j, 0))
    shp = lambda width: jax.ShapeDtypeStruct((b, t, width), F32)
    out_specs = [row(512), row(1024), row(512), row(512), row(256), row(128)]
    out_shape = [shp(512), shp(1024), shp(512), shp(512), shp(256), shp(128)]
    if page_major:
        for width in (1024, 512):
            out_specs.append(pl.BlockSpec((1, tm // PAGE, width, PAGE), lambda i, j: (i, j, 0, 0)))
            out_shape.append(jax.ShapeDtypeStruct((b, t // PAGE, width, PAGE), F32))
    return pl.pallas_call(
        _proj_kernel,
        grid=(b, t // tm),
        in_specs=[row(D),
                  pl.BlockSpec((1, tmod, 3 * D), lambda i, j: (i, 0, 0)),
                  pl.BlockSpec((1, D), lambda i, j: (0, 0)),
                  pl.BlockSpec((D, IN_COLS_PAD), lambda i, j: (0, 0)),
                  pl.BlockSpec((8, LANES), lambda i, j: (0, 0)),
                  pl.BlockSpec((LANES, LANES), lambda i, j: (0, 0))],
        out_specs=out_specs,
        out_shape=out_shape,
        compiler_params=_cparams(("arbitrary", "arbitrary")),
        name="attn_proj",
    )(x, mod, g, w_pad, qkg_t, avg)


def _rank_rows(score, rowi, ncand, step=1):
    rank = jnp.zeros(score.shape, F32)
    for m in range(0, ncand * step, step):
        rm = score[m:m + 1, :]
        rank = rank + jnp.where(rm > score, 1.0, 0.0) + jnp.where((rm == score) & (m < rowi), 1.0, 0.0)
    return rank


def _columns_from_rows(x_t):
    pad = jnp.zeros((LANES - x_t.shape[0], x_t.shape[1]), F32)
    return jnp.concatenate([x_t, pad], axis=0).T


def _softmax_pv(pieces, v_all):
    m = pieces[0]
    for s in pieces[1:]:
        m = jnp.maximum(m, s)
    m = jnp.maximum(jnp.max(m, axis=1, keepdims=True), M_INIT)
    ps = [jnp.exp(s - m) for s in pieces]
    tot = ps[0]
    for p in ps[1:]:
        tot = tot + p
    l = jnp.sum(tot, axis=1, keepdims=True)
    p_all = jnp.concatenate([p.astype(BF16) for p in ps], axis=1) if len(ps) > 1 else ps[0].astype(BF16)
    return _dot(p_all, v_all) / jnp.maximum(l, 1e-30)


def _moba_prompt_kernel(q_ref, k_ref, v_ref, t_ref, o_ref, km_scr, kb_scr, vb_scr):
    s_len = q_ref.shape[1]
    nblk = s_len // MOBA_BLOCK
    nq = s_len // TQ
    lane = lax.broadcasted_iota(jnp.int32, (TQ, LANES), 1)
    rowb = lax.broadcasted_iota(jnp.int32, (nblk, TQ), 0)
    km_scr[...] = jnp.zeros(km_scr.shape, F32)
    for n in range(nblk):
        km_scr[n:n + 1, :] = jnp.mean(k_ref[0, n * MOBA_BLOCK:(n + 1) * MOBA_BLOCK, :], axis=0, keepdims=True)
    kmean = km_scr[...]
    kb_scr[...] = k_ref[0].astype(BF16)
    vb_scr[...] = v_ref[0].astype(BF16)

    def qtile(i, _):
        r0 = pl.multiple_of(i * TQ, TQ)
        q2 = q_ref[0, pl.ds(r0, TQ), :]
        qbs, cbs = [], []
        for e in range(2):
            qe = jnp.where(lane // HEAD_DIM == e, q2, 0.0)
            gate_t = _dot_nt(kmean, qe, precision=HIGHEST)[0:nblk]
            gm = jnp.where(rowb < i, gate_t, -jnp.inf)
            sel = ((_rank_rows(gm, rowb, nblk) < MOBA_TOPK) & (rowb < i)) | (rowb == i)
            cbs.append(_columns_from_rows(jnp.where(sel, 0.0, NEG)))
            qbs.append(qe.astype(BF16))

        for c in range(1, nq // 2 + 1):
            @pl.when(i // 2 + 1 == c)
            def _():
                kall = kb_scr[0:2 * c * TQ, :]
                vall = vb_scr[0:2 * c * TQ, :]
                outs = []
                for e in range(2):
                    s = _dot_nt(qbs[e], kall)
                    pieces = [s[:, n * TQ:(n + 1) * TQ] + t_ref[e, jnp.clip(i - n, 0, 2)] + cbs[e][:, n:n + 1]
                              for n in range(2 * c)]
                    outs.append(_softmax_pv(pieces, vall))
                o_ref[0, pl.ds(r0, TQ), :] = jnp.where(lane < HEAD_DIM, outs[0], outs[1])
        return 0

    lax.fori_loop(0, nq, qtile, 0)


def _moba_prompt(mq, mkv, tb):
    b, s, _ = mq.shape
    npair = MOBA_HEADS // 2
    return pl.pallas_call(
        _moba_prompt_kernel,
        grid=(b, npair),
        in_specs=[pl.BlockSpec((1, s, LANES), lambda i, p: (i, 0, p)),
                  pl.BlockSpec((1, s, LANES), lambda i, p: (i, 0, p)),
                  pl.BlockSpec((1, s, LANES), lambda i, p: (i, 0, npair + p)),
                  pl.BlockSpec((2, N_BIAS_TILES, TQ, TQ), lambda i, p: (p, 0, 0, 0))],
        out_specs=pl.BlockSpec((1, s, LANES), lambda i, p: (i, 0, p)),
        out_shape=jax.ShapeDtypeStruct((b, s, 512), F32),
        scratch_shapes=[pltpu.VMEM((LANES, LANES), F32), pltpu.VMEM((s, LANES), BF16), pltpu.VMEM((s, LANES), BF16)],
        compiler_params=_cparams(("arbitrary", "arbitrary")),
        name="moba_prompt",
    )(mq, mkv, mkv, tb)


def _compress_tokens(load_k, load_v, pos_ref, w1_ref, w2_ref):
    hk = hv = None
    for r in range(0, CMP_BLOCK, 2):
        xk = [(load_k(r + t) + pos_ref[r + t:r + t + 1, 0:LANES]).astype(BF16) for t in range(2)]
        xv = [(load_v(r + t) + pos_ref[r + t:r + t + 1, LANES:2 * LANES]).astype(BF16) for t in range(2)]
        dk = _dot(jnp.concatenate(xk, axis=1), w1_ref[0, r // 2])
        dv = _dot(jnp.concatenate(xv, axis=1), w1_ref[1, r // 2])
        hk = dk if hk is None else hk + dk
        hv = dv if hv is None else hv + dv
    ck = _dot(jax.nn.gelu(hk).astype(BF16), w2_ref[0])
    cv = _dot(jax.nn.gelu(hv).astype(BF16), w2_ref[1])
    return ck, cv


def _cmp_prompt_kernel(xk_ref, xv_ref, pos_ref, w1_ref, w2_ref, gkc_ref, avg_ref, kc_ref, vc_ref):
    nblk = xk_ref.shape[1] // CMP_BLOCK
    ck, cv = _compress_tokens(lambda r: xk_ref[0, pl.ds(r, nblk, stride=CMP_BLOCK), :],
                              lambda r: xv_ref[0, pl.ds(r, nblk, stride=CMP_BLOCK), :], pos_ref, w1_ref, w2_ref)
    ck = ck * lax.rsqrt(_group_mean_sq(ck, avg_ref[...]) + EPS) * gkc_ref[...]
    kc_ref[0] = jnp.zeros((LANES, LANES), F32)
    vc_ref[0] = jnp.zeros((LANES, LANES), F32)
    kc_ref[0, 0:nblk, :] = ck
    vc_ref[0, 0:nblk, :] = cv


def _cmp_prompt(nkv, pos, w1bd, w2bd, gkc, avg):
    b, s, _ = nkv.shape
    const = lambda shape: pl.BlockSpec(shape, lambda i: (0,) * len(shape))
    return pl.pallas_call(
        _cmp_prompt_kernel,
        grid=(b,),
        in_specs=[pl.BlockSpec((1, s, LANES), lambda i: (i, 0, 0)), pl.BlockSpec((1, s, LANES), lambda i: (i, 0, 1)),
                  const((CMP_BLOCK, 256)), const((2, CMP_BLOCK // 2, 256, 256)), const((2, 256, LANES)),
                  const((1, LANES)), const((LANES, LANES))],
        out_specs=[pl.BlockSpec((1, LANES, LANES), lambda i: (i, 0, 0))] * 2,
        out_shape=[jax.ShapeDtypeStruct((b, LANES, LANES), F32)] * 2,
        compiler_params=_cparams(("arbitrary",)),
        name="nsa_compress_prompt",
    )(nkv, nkv, pos, w1bd, w2bd, gkc, avg)


def _nsa_prompt_kernel(q_ref, ks_ref, vs_ref, kw_ref, vw_ref, kc_ref, vc_ref, g_ref, t_ref, tc_ref,
                       o_ref, ksb_scr, vsb_scr, kwb_scr, vwb_scr):
    s_len = q_ref.shape[1]
    k = pl.program_id(1)
    lane = lax.broadcasted_iota(jnp.int32, (TQ, LANES), 1)
    kvmask = (lane // HEAD_DIM) == k
    kc = kc_ref[0].astype(BF16)
    vc = vc_ref[0].astype(BF16)
    nsel = s_len // SEL_BLOCK
    nq = s_len // TQ
    ncmp = s_len // CMP_BLOCK
    rowb = lax.broadcasted_iota(jnp.int32, (nsel, TQ), 0)
    qpos = lax.broadcasted_iota(jnp.int32, (nsel, TQ), 1)
    pair_r = lax.broadcasted_iota(jnp.int32, (nsel, LANES), 0)
    pair_c = lax.broadcasted_iota(jnp.int32, (nsel, LANES), 1)
    pair_t = jnp.where((pair_c // (SEL_BLOCK // CMP_BLOCK) == pair_r) & (pair_c < ncmp), 1.0, 0.0)
    ksb_scr[...] = ks_ref[0].astype(BF16)
    vsb_scr[...] = vs_ref[0].astype(BF16)
    kwb_scr[...] = kw_ref[0].astype(BF16)
    vwb_scr[...] = vw_ref[0].astype(BF16)

    def qtile(i, _):
        r0 = pl.multiple_of(i * TQ, TQ)
        qs = []
        for h in range(NSA_GROUP):
            q2 = q_ref[0, pl.ds(r0, TQ), (h // 2) * LANES:(h // 2 + 1) * LANES]
            qa = jnp.where(k == (h % 2), q2, pltpu.roll(q2, HEAD_DIM, 1))
            qs.append(jnp.where(kvmask, qa, 0.0).astype(BF16))

        imp = jnp.zeros((TQ, LANES), F32)
        o_cmp = []
        for h in range(NSA_GROUP):
            s = _dot_nt(qs[h], kc) + tc_ref[h, pl.ds(r0, TQ), :]
            m = jnp.maximum(jnp.max(s, axis=1, keepdims=True), M_INIT)
            p = jnp.exp(s - m)
            p = p / jnp.maximum(jnp.sum(p, axis=1, keepdims=True), 1e-30)
            imp = imp + p
            o_cmp.append(_dot(p.astype(BF16), vc))

        imp_t = _dot_nt(pair_t, imp, precision=HIGHEST)
        own = (r0 + qpos) // SEL_BLOCK
        sc = jnp.where(rowb < own, imp_t, -jnp.inf)
        sel = ((_rank_rows(sc, rowb, nsel) < SEL_TOPK) & (rowb < own)) | (rowb == own)
        selb = _columns_from_rows(jnp.where(sel, 1.0, 0.0)).astype(BF16)

        wk, wv, wt = [], [], []
        for j, tidx in enumerate((3, 1, 0)):
            n = i - 2 + j
            c0 = pl.multiple_of(jnp.maximum(n, 0) * TQ, TQ)
            wk.append(kwb_scr[pl.ds(c0, TQ), :])
            wv.append(vwb_scr[pl.ds(c0, TQ), :])
            wt.append(jnp.where(n < 0, N_BIAS_TILES - 1, tidx))
        kw_all = jnp.concatenate(wk, axis=0)
        vw_all = jnp.concatenate(wv, axis=0)
        g = g_ref[0, pl.ds(r0, TQ), :]
        o_win = []
        for h in range(NSA_GROUP):
            s = _dot_nt(qs[h], kw_all)
            o_win.append(_softmax_pv([s[:, j * TQ:(j + 1) * TQ] + t_ref[h, wt[j]] for j in range(3)], vw_all))

        for c in range(1, nq // 2 + 1):
            @pl.when(i // 2 + 1 == c)
            def _():
                nkeys = 2 * c * TQ
                e_r = lax.broadcasted_iota(jnp.int32, (LANES, nkeys), 0)
                e_c = lax.broadcasted_iota(jnp.int32, (LANES, nkeys), 1)
                expand = jnp.where(e_r == e_c // SEL_BLOCK, 1.0, 0.0).astype(BF16)
                addm = (_dot(selb, expand) - 1.0) * (-NEG)
                kall = ksb_scr[0:nkeys, :]
                vall = vsb_scr[0:nkeys, :]
                res = []
                for h in range(NSA_GROUP):
                    s = _dot_nt(qs[h], kall) + addm
                    pieces = [s[:, n * TQ:(n + 1) * TQ] + t_ref[h, jnp.clip(i - n, 0, 2)] for n in range(2 * c)]
                    o_sel = _softmax_pv(pieces, vall)
                    hg = (k * NSA_GROUP + h) * 3
                    o = (_col(g, lane, hg) * o_cmp[h] + _col(g, lane, hg + 1) * o_sel
                         + _col(g, lane, hg + 2) * o_win[h])
                    res.append(jnp.where(k == (h % 2), o, pltpu.roll(o, HEAD_DIM, 1)))
                for t in range(2):
                    o_ref[0, pl.ds(r0, TQ), t * LANES:(t + 1) * LANES] = jnp.where(
                        lane < HEAD_DIM, res[2 * t], res[2 * t + 1])
        return 0

    lax.fori_loop(0, nq, qtile, 0)


def _nsa_prompt(nq, nkv, wkv, kcmp, vcmp, gates, tb, tc):
    b, s, _ = nq.shape
    col = lambda arr_cols, cb: pl.BlockSpec((1, s, LANES), lambda i, k: (i, 0, cb))
    return pl.pallas_call(
        _nsa_prompt_kernel,
        grid=(b, 2),
        in_specs=[pl.BlockSpec((1, s, 256), lambda i, k: (i, 0, k)),
                  col(512, 2), col(512, 3), col(256, 0), col(256, 1),
                  pl.BlockSpec((1, LANES, LANES), lambda i, k: (i, 0, 0)),
                  pl.BlockSpec((1, LANES, LANES), lambda i, k: (i, 0, 0)),
                  pl.BlockSpec((1, s, LANES), lambda i, k: (i, 0, 0)),
                  pl.BlockSpec((NSA_GROUP, N_BIAS_TILES, TQ, TQ), lambda i, k: (k, 0, 0, 0)),
                  pl.BlockSpec((NSA_GROUP, s, LANES), lambda i, k: (k, 0, 0))],
        out_specs=pl.BlockSpec((1, s, 256), lambda i, k: (i, 0, k)),
        out_shape=jax.ShapeDtypeStruct((b, s, 512), F32),
        scratch_shapes=[pltpu.VMEM((s, LANES), BF16)] * 4,
        compiler_params=_cparams(("arbitrary", "arbitrary")),
        name="nsa_prompt",
    )(nq, nkv, nkv, wkv, wkv, kcmp, vcmp, gates, tb, tc)


PAGES_PER_STEP = 8
CMP_PITCH = 40


def _rank_lt(score, lane, ncand, topk):
    rank = jnp.zeros(score.shape, F32)
    for m in range(ncand):
        col = score[:, m:m + 1]
        beats = (col > score) | ((col == score) & (m < lane))
        rank = rank + jnp.where(beats, 1.0, 0.0)
    return rank < topk


def _merge_blocks(sel, m_all, l_all, acc_scr, nblk, s_self, v_self):
    mx = jnp.maximum(jnp.max(jnp.where(sel, m_all, NEG), axis=1, keepdims=True), s_self)
    w = jnp.exp(jnp.where(sel, m_all - mx, NEG))
    w_self = jnp.exp(s_self - mx)
    den = jnp.sum(w * l_all, axis=1, keepdims=True) + w_self
    num = w_self * v_self
    for j in range(nblk):
        num = num + w[:, j:j + 1] * acc_scr[j]
    return num / den


def _moba_sample_kernel(pt_ref, *refs):
    pages = refs[:PAGES_PER_STEP]
    qm_ref, kn_ref, vn_ref, tsb_ref, misc_ref, o_ref, g_scr, m_scr, l_scr, acc_scr = refs[PAGES_PER_STEP:]
    s = pl.program_id(1)
    nstep = pl.num_programs(1)
    nblk = PAST_LEN // MOBA_BLOCK
    width = MOBA_HEADS * HEAD_DIM
    qm = qm_ref[0]
    qb = qm.astype(BF16)
    lane = lax.broadcasted_iota(jnp.int32, (MOBA_HEADS, LANES), 1)

    @pl.when(s == 0)
    def _():
        g_scr[...] = jnp.zeros(g_scr.shape, F32)
        m_scr[...] = jnp.zeros(m_scr.shape, F32)
        l_scr[...] = jnp.zeros(l_scr.shape, F32)

    for j in range(PAGES_PER_STEP // 2):
        blk = s * (PAGES_PER_STEP // 2) + j
        pa, pb = pages[2 * j], pages[2 * j + 1]
        raw = jnp.concatenate([_dot(qb, pa[0, 0].astype(BF16)), _dot(qb, pb[0, 0].astype(BF16))], axis=1)
        gate = jnp.sum(raw, axis=1, keepdims=True)
        sc = raw + jnp.where(blk == nblk - 1, tsb_ref[...], misc_ref[:, 1:2])
        mj = jnp.max(sc, axis=1, keepdims=True)
        p = jnp.exp(sc - mj)
        g_scr[...] = jnp.where(lane == blk, gate, g_scr[...])
        m_scr[...] = jnp.where(lane == blk, mj, m_scr[...])
        l_scr[...] = jnp.where(lane == blk, jnp.sum(p, axis=1, keepdims=True), l_scr[...])
        acc_scr[blk] = (_dot_nt(p[:, 0:PAGE].astype(BF16), pa[0, 1].astype(BF16))
                        + _dot_nt(p[:, PAGE:2 * PAGE].astype(BF16), pb[0, 1].astype(BF16)))

    @pl.when(s == nstep - 1)
    def _():
        gm = jnp.where(lane < nblk, g_scr[...], -jnp.inf)
        sel = _rank_lt(gm, lane, nblk, MOBA_TOPK) & (lane < nblk)
        s_self = jnp.sum(qm * kn_ref[0], axis=1, keepdims=True) + misc_ref[:, 0:1]
        o = _merge_blocks(sel, m_scr[...], l_scr[...], acc_scr, nblk, s_self, vn_ref[0])
        hrow = lax.broadcasted_iota(jnp.int32, (MOBA_HEADS, width), 0)
        hlane = lax.broadcasted_iota(jnp.int32, (MOBA_HEADS, width), 1)
        o_ref[0] = jnp.sum(jnp.where(hlane // HEAD_DIM == hrow, o, 0.0), axis=0, keepdims=True)


def _moba_sample(page_table, cache_t, qmat, knew, vnew, tsb, misc):
    nb, npages = page_table.shape
    nstep = npages // PAGES_PER_STEP
    width = MOBA_HEADS * HEAD_DIM
    nblk = PAST_LEN // MOBA_BLOCK

    def page_spec(j):
        return pl.BlockSpec((1, 2, width, PAGE), lambda b, s, pt: (pt[b, s * PAGES_PER_STEP + j], 0, 0, 0))

    per_b = lambda shape: pl.BlockSpec((1,) + shape, lambda b, s, pt: (b, 0, 0))
    const = lambda shape: pl.BlockSpec(shape, lambda b, s, pt: (0,) * len(shape))
    grid_spec = pltpu.PrefetchScalarGridSpec(
        num_scalar_prefetch=1,
        grid=(nb, nstep),
        in_specs=[page_spec(j) for j in range(PAGES_PER_STEP)]
        + [per_b((MOBA_HEADS, width)), per_b((1, width)), per_b((1, width)),
           const((MOBA_HEADS, MOBA_BLOCK)), const((MOBA_HEADS, LANES))],
        out_specs=per_b((1, width)),
        scratch_shapes=[pltpu.VMEM((MOBA_HEADS, LANES), F32)] * 3 + [pltpu.VMEM((nblk, MOBA_HEADS, width), F32)],
    )
    return pl.pallas_call(
        _moba_sample_kernel,
        grid_spec=grid_spec,
        out_shape=jax.ShapeDtypeStruct((nb, 1, width), F32),
        compiler_params=_cparams(("arbitrary", "arbitrary")),
        name="moba_sample",
    )(page_table, *([cache_t] * PAGES_PER_STEP), qmat, knew, vnew, tsb, misc)


def _nsa_sample_kernel(pt_ref, *refs):
    pages = refs[:PAGES_PER_STEP]
    (qm_ref, ksn_ref, vsn_ref, kwn_ref, vwn_ref, win_ref, g_ref, tsn_ref, misc_ref, tcs_ref, tws_ref,
     pos_ref, w1_ref, w2_ref, gkc_ref, avg_ref, o_ref, xk_scr, xv_scr, m_scr, l_scr, acc_scr) = refs[PAGES_PER_STEP:]
    s = pl.program_id(1)
    nstep = pl.num_programs(1)
    nsel = PAST_LEN // SEL_BLOCK
    ncmp = PAST_LEN // CMP_BLOCK
    npage = PAST_LEN // PAGE
    qm = qm_ref[0]
    qb = qm.astype(BF16)
    lane = lax.broadcasted_iota(jnp.int32, (NSA_HEADS, LANES), 1)
    row = lax.broadcasted_iota(jnp.int32, (NSA_HEADS, LANES), 0)
    lo = lane < HEAD_DIM

    @pl.when(s == 0)
    def _():
        m_scr[...] = jnp.zeros(m_scr.shape, F32)
        l_scr[...] = jnp.zeros(l_scr.shape, F32)

    for j in range(PAGES_PER_STEP):
        pg = s * PAGES_PER_STEP + j
        kc = pages[j][0, 0:LANES, :].T
        vc = pages[j][0, LANES:2 * LANES, :].T
        for b4 in range(PAGE // CMP_BLOCK):
            r0 = pl.multiple_of((pg * (PAGE // CMP_BLOCK) + b4) * CMP_PITCH, 8)
            xk_scr[pl.ds(r0, CMP_BLOCK), :] = kc[b4 * CMP_BLOCK:(b4 + 1) * CMP_BLOCK, :]
            xv_scr[pl.ds(r0, CMP_BLOCK), :] = vc[b4 * CMP_BLOCK:(b4 + 1) * CMP_BLOCK, :]
        ks_t = pages[j][0, 256:384, :].astype(BF16)
        vs_t = pages[j][0, 384:512, :].astype(BF16)
        sc = _dot(qb, ks_t) + jnp.where(pg == npage - 1, tsn_ref[...], misc_ref[:, 1:2])
        m0 = jnp.max(jnp.where(lo, sc, NEG), axis=1, keepdims=True)
        m1 = jnp.max(jnp.where(lo, NEG, sc), axis=1, keepdims=True)
        p = jnp.exp(sc - jnp.where(lo, m0, m1))
        l0 = jnp.sum(jnp.where(lo, p, 0.0), axis=1, keepdims=True)
        l1 = jnp.sum(jnp.where(lo, 0.0, p), axis=1, keepdims=True)
        b0 = 2 * pg
        m_scr[...] = jnp.where(lane == b0, m0, jnp.where(lane == b0 + 1, m1, m_scr[...]))
        l_scr[...] = jnp.where(lane == b0, l0, jnp.where(lane == b0 + 1, l1, l_scr[...]))
        acc_scr[b0] = _dot_nt(jnp.where(lo, p, 0.0).astype(BF16), vs_t)
        acc_scr[b0 + 1] = _dot_nt(jnp.where(lo, 0.0, p).astype(BF16), vs_t)

    @pl.when(s == nstep - 1)
    def _():
        ck, cv = _compress_tokens(lambda r: xk_scr[pl.ds(r, ncmp, stride=CMP_PITCH), :],
                                  lambda r: xv_scr[pl.ds(r, ncmp, stride=CMP_PITCH), :], pos_ref, w1_ref, w2_ref)
        ck = ck * lax.rsqrt(_group_mean_sq(ck, avg_ref[...]) + EPS) * gkc_ref[...]
        sc = _dot_nt(qb, ck.astype(BF16)) + tcs_ref[...]
        m = jnp.maximum(jnp.max(sc, axis=1, keepdims=True), M_INIT)
        pc = jnp.exp(sc - m)
        pc = pc / jnp.maximum(jnp.sum(pc, axis=1, keepdims=True), 1e-30)
        o_cmp = _dot(pc.astype(BF16), cv.astype(BF16))
        g0 = pc[0:1] + pc[1:2] + pc[2:3] + pc[3:4]
        g1 = pc[4:5] + pc[5:6] + pc[6:7] + pc[7:8]
        rowc = lax.broadcasted_iota(jnp.int32, (NSA_HEADS, ncmp), 0)
        imp = jnp.where(rowc < NSA_GROUP, g0, g1)
        pr = lax.broadcasted_iota(jnp.int32, (ncmp, LANES), 0)
        pc_ = lax.broadcasted_iota(jnp.int32, (ncmp, LANES), 1)
        pair = jnp.where(pr // (SEL_BLOCK // CMP_BLOCK) == pc_, 1.0, 0.0)
        impb = _dot(imp, pair, precision=HIGHEST)
        own = PAST_LEN // SEL_BLOCK
        sel = _rank_lt(jnp.where(lane < own, impb, -jnp.inf), lane, nsel, SEL_TOPK) & (lane < own)
        s_self = jnp.sum(qm * ksn_ref[0], axis=1, keepdims=True) + misc_ref[:, 0:1]
        o_sel = _merge_blocks(sel, m_scr[...], l_scr[...], acc_scr, nsel, s_self, vsn_ref[0])
        kw_t = win_ref[0, 0:LANES, :].astype(BF16)
        vw_t = win_ref[0, LANES:2 * LANES, :].astype(BF16)
        sw = _dot(qb, kw_t) + tws_ref[...]
        sw_self = jnp.sum(qm * kwn_ref[0], axis=1, keepdims=True) + misc_ref[:, 0:1]
        mw = jnp.maximum(jnp.max(sw, axis=1, keepdims=True), sw_self)
        pw = jnp.exp(sw - mw)
        pw_self = jnp.exp(sw_self - mw)
        o_win = ((_dot_nt(pw.astype(BF16), vw_t) + pw_self * vwn_ref[0])
                 / (jnp.sum(pw, axis=1, keepdims=True) + pw_self))
        gt = jnp.broadcast_to(g_ref[0], (NSA_HEADS, LANES))
        o8 = (_col(gt, lane, 3 * row) * o_cmp + _col(gt, lane, 3 * row + 1) * o_sel
              + _col(gt, lane, 3 * row + 2) * o_win)
        lane1 = lax.broadcasted_iota(jnp.int32, (1, LANES), 1)
        tiles = []
        for t in range(NSA_HEADS // 2):
            ha, hb = 2 * t, 2 * t + 1
            ra = o8[ha:ha + 1, :]
            rb = o8[hb:hb + 1, :]
            if ha // NSA_GROUP == 1:
                ra = pltpu.roll(ra, HEAD_DIM, 1)
            if hb // NSA_GROUP == 0:
                rb = pltpu.roll(rb, HEAD_DIM, 1)
            tiles.append(jnp.where(lane1 < HEAD_DIM, ra, rb))
        o_ref[0] = jnp.concatenate(tiles, axis=1)


def _nsa_sample(page_table, cache, qmat, ksn, vsn, kwn, vwn, win, gates, tsn, misc, tcs, tws,
                pos, w1bd, w2bd, gkc, avg):
    nb, npages = page_table.shape
    nstep = npages // PAGES_PER_STEP
    nsel = PAST_LEN // SEL_BLOCK

    def page_spec(j):
        return pl.BlockSpec((1, 512, PAGE), lambda b, s, pt: (pt[b, s * PAGES_PER_STEP + j], 0, 0))

    per_b = lambda shape: pl.BlockSpec((1,) + shape, lambda b, s, pt: (b, 0, 0))
    const = lambda shape: pl.BlockSpec(shape, lambda b, s, pt: (0,) * len(shape))
    cmp_rows = PAST_LEN // CMP_BLOCK * CMP_PITCH
    grid_spec = pltpu.PrefetchScalarGridSpec(
        num_scalar_prefetch=1,
        grid=(nb, nstep),
        in_specs=[page_spec(j) for j in range(PAGES_PER_STEP)]
        + [per_b((NSA_HEADS, LANES)), per_b((1, LANES)), per_b((1, LANES)), per_b((1, LANES)), per_b((1, LANES)),
           per_b((256, WINDOW)), per_b((1, LANES)),
           const((NSA_HEADS, PAGE)), const((NSA_HEADS, LANES)), const((NSA_HEADS, PAST_LEN // CMP_BLOCK)),
           const((NSA_HEADS, WINDOW)),
           const((CMP_BLOCK, 256)), const((2, CMP_BLOCK // 2, 256, 256)), const((2, 256, LANES)),
           const((1, LANES)), const((LANES, LANES))],
        out_specs=per_b((1, 512)),
        scratch_shapes=[pltpu.VMEM((cmp_rows, LANES), F32), pltpu.VMEM((cmp_rows, LANES), F32),
                        pltpu.VMEM((NSA_HEADS, LANES), F32),
                        pltpu.VMEM((NSA_HEADS, LANES), F32), pltpu.VMEM((nsel, NSA_HEADS, LANES), F32)],
    )
    return pl.pallas_call(
        _nsa_sample_kernel,
        grid_spec=grid_spec,
        out_shape=jax.ShapeDtypeStruct((nb, 1, 512), F32),
        compiler_params=_cparams(("arbitrary", "arbitrary")),
        name="nsa_sample",
    )(page_table, *([cache] * PAGES_PER_STEP), qmat, ksn, vsn, kwn, vwn, win, gates, tsn, misc, tcs, tws,
      pos, w1bd, w2bd, gkc, avg)


def _outproj_kernel(x_ref, mod_ref, om_ref, on_ref, w_ref, o_ref):
    y = _dot(om_ref[0].astype(BF16), w_ref[0:512, :]) + _dot(on_ref[0].astype(BF16), w_ref[512:1024, :])
    o_ref[0] = x_ref[0] + mod_ref[0][:, 2 * D:3 * D] * y


def _outproj(x, mod, o_m, o_n, w, tm):
    b, t, _ = x.shape
    tmod = mod.shape[1]
    row = lambda width: pl.BlockSpec((1, tm, width), lambda i, j: (i, j, 0))
    return pl.pallas_call(
        _outproj_kernel,
        grid=(b, t // tm),
        in_specs=[row(D), pl.BlockSpec((1, tmod, 3 * D), lambda i, j: (i, 0, 0)), row(512), row(512),
                  pl.BlockSpec((D, D), lambda i, j: (0, 0))],
        out_specs=row(D),
        out_shape=jax.ShapeDtypeStruct((b, t, D), F32),
        compiler_params=_cparams(("arbitrary", "arbitrary")),
        name="attn_outproj",
    )(x, mod, o_m, o_n, w)


def _mlp_kernel(x_ref, mod_ref, g_ref, w1_ref, w2_ref, o_ref, h_scr, acc_scr):
    kf = pl.program_id(2)

    @pl.when(kf == 0)
    def _():
        mod = mod_ref[0]
        h_scr[...] = _modulate(x_ref[0], g_ref[...], mod[:, 0:D], mod[:, D:2 * D]).astype(BF16)
        acc_scr[...] = jnp.zeros(acc_scr.shape, F32)

    a = jnp.square(jnp.maximum(_dot(h_scr[...], w1_ref[...]), 0.0))
    acc_scr[...] += _dot(a.astype(BF16), w2_ref[...])

    @pl.when(kf == pl.num_programs(2) - 1)
    def _():
        o_ref[0] = x_ref[0] + mod_ref[0][:, 2 * D:3 * D] * acc_scr[...]


def _mlp(x, mod, g, w1, w2, tm, tf):
    b, t, _ = x.shape
    tmod = mod.shape[1]
    return pl.pallas_call(
        _mlp_kernel,
        grid=(b, t // tm, D_FF // tf),
        in_specs=[pl.BlockSpec((1, tm, D), lambda i, j, kf: (i, j, 0)),
                  pl.BlockSpec((1, tmod, 3 * D), lambda i, j, kf: (i, 0, 0)),
                  pl.BlockSpec((1, D), lambda i, j, kf: (0, 0)),
                  pl.BlockSpec((D, tf), lambda i, j, kf: (0, kf)),
                  pl.BlockSpec((tf, D), lambda i, j, kf: (kf, 0))],
        out_specs=pl.BlockSpec((1, tm, D), lambda i, j, kf: (i, j, 0)),
        out_shape=jax.ShapeDtypeStruct((b, t, D), F32),
        scratch_shapes=[pltpu.VMEM((tm, D), BF16), pltpu.VMEM((tm, D), F32)],
        compiler_params=_cparams(("arbitrary", "arbitrary", "arbitrary")),
        name="mlp",
    )(x, mod, g, w1, w2)


def _s5_disc_kernel(are_ref, aim_ref, ldt_ref, bre_ref, bim_ref, abre_ref, abim_ref, bbre_ref, bbim_ref):
    a_re, a_im = are_ref[...], aim_ref[...]
    dt = jnp.exp(ldt_ref[...])
    decay = jnp.exp(dt * a_re)
    ab_re, ab_im = decay * jnp.cos(dt * a_im), decay * jnp.sin(dt * a_im)
    den = a_re * a_re + a_im * a_im
    f_re = ((ab_re - 1) * a_re + ab_im * a_im) / den
    f_im = (ab_im * a_re - (ab_re - 1) * a_im) / den
    br, bi = bre_ref[...], bim_ref[...]
    abre_ref[...] = ab_re
    abim_ref[...] = ab_im
    bbre_ref[...] = f_re * br - f_im * bi
    bbim_ref[...] = f_re * bi + f_im * br


def _s5_discretize(a_re, a_im, log_dt, b_re, b_im):
    rep = lambda a: jnp.repeat(a, S5_GROUP_CH, axis=1)
    shp = jax.ShapeDtypeStruct((S5_GROUPS, S5_STATE * S5_GROUP_CH), F32)
    ldt = jnp.broadcast_to(log_dt[:, None], (S5_GROUPS, S5_STATE * S5_GROUP_CH))
    flat = lambda a: a.reshape(S5_GROUPS, S5_STATE * S5_GROUP_CH)
    ab_re, ab_im, bb_re, bb_im = pl.pallas_call(
        _s5_disc_kernel, out_shape=[shp] * 4, name="s5_discretize",
    )(rep(a_re), rep(a_im), ldt, flat(b_re), flat(b_im))
    unrep = lambda a: a[:, ::S5_GROUP_CH]
    unflat = lambda a: a.reshape(S5_GROUPS, S5_STATE, S5_GROUP_CH)
    return unrep(ab_re), unrep(ab_im), unflat(bb_re), unflat(bb_im)


def _modulate_tm_kernel(x_ref, mod_ref, g_ref, o_ref):
    mod = mod_ref[0]
    o_ref[...] = _modulate(x_ref[0], g_ref[...], mod[:, 0:D], mod[:, D:2 * D])


def _modulate_time_major(x, mod, g, tl):
    b, t, _ = x.shape
    tmod = mod.shape[1]
    return pl.pallas_call(
        _modulate_tm_kernel,
        grid=(b, t // tl),
        in_specs=[pl.BlockSpec((1, tl, D), lambda i, j: (i, j, 0)),
                  pl.BlockSpec((1, tmod, 3 * D), lambda i, j: (i, 0, 0)),
                  pl.BlockSpec((1, D), lambda i, j: (0, 0))],
        out_specs=pl.BlockSpec((tl, D), lambda i, j: (j, i)),
        out_shape=jax.ShapeDtypeStruct((t, b * D), F32),
        compiler_params=_cparams(("arbitrary", "arbitrary")),
        name="s5_modulate",
    )(x, mod, g)


S5_CB = 256
S5_NS = S5_CB // S5_GROUP_CH * S5_STATE


def _s5_scan_kernel(h_ref, wb_ref, wc_ref, ar_ref, ai_ref, d_ref, h0_ref, y_ref, so_ref, xs_scr, st_scr, *, tl, r):
    i = pl.program_id(1)

    @pl.when(i == 0)
    def _():
        st_scr[...] = h0_ref[...]

    u = h_ref[...]
    xs_scr[...] = _dot(u.astype(BF16), wb_ref[0])
    ar = jnp.broadcast_to(ar_ref[0], (r, S5_NS))
    ai = jnp.broadcast_to(ai_ref[0], (r, S5_NS))

    def step(t, carry):
        xr, xi = carry
        r0 = pl.multiple_of(t * r, r)
        nr = ar * xr - ai * xi + xs_scr[pl.ds(r0, r), 0:S5_NS]
        ni = ar * xi + ai * xr + xs_scr[pl.ds(r0, r), S5_NS:2 * S5_NS]
        xs_scr[pl.ds(r0, r), 0:S5_NS] = nr
        xs_scr[pl.ds(r0, r), S5_NS:2 * S5_NS] = ni
        return nr, ni

    xr, xi = lax.fori_loop(0, tl, step, (st_scr[0], st_scr[1]))
    st_scr[0] = xr
    st_scr[1] = xi
    y_ref[...] = _dot(xs_scr[...].astype(BF16), wc_ref[0]) + d_ref[...] * u

    @pl.when(i == pl.num_programs(1) - 1)
    def _():
        so_ref[...] = st_scr[...]


def _s5_scan(h_tm, wb, wc, ar, ai, d_skip, h0, r, tl):
    rows = h_tm.shape[0]
    nj = D // S5_CB
    return pl.pallas_call(
        functools.partial(_s5_scan_kernel, tl=tl, r=r),
        grid=(nj, rows // (tl * r)),
        in_specs=[pl.BlockSpec((tl * r, S5_CB), lambda j, i: (i, j)),
                  pl.BlockSpec((1, S5_CB, 2 * S5_NS), lambda j, i: (j, 0, 0)),
                  pl.BlockSpec((1, 2 * S5_NS, S5_CB), lambda j, i: (j, 0, 0)),
                  pl.BlockSpec((1, 1, S5_NS), lambda j, i: (j, 0, 0)),
                  pl.BlockSpec((1, 1, S5_NS), lambda j, i: (j, 0, 0)),
                  pl.BlockSpec((1, S5_CB), lambda j, i: (0, j)),
                  pl.BlockSpec((2, r, S5_NS), lambda j, i: (0, 0, j))],
        out_specs=[pl.BlockSpec((tl * r, S5_CB), lambda j, i: (i, j)),
                   pl.BlockSpec((2, r, S5_NS), lambda j, i: (0, 0, j))],
        out_shape=[jax.ShapeDtypeStruct((rows, D), F32), jax.ShapeDtypeStruct((2, r, S5_GROUPS * S5_STATE), F32)],
        scratch_shapes=[pltpu.VMEM((tl * r, 2 * S5_NS), F32), pltpu.VMEM((2, r, S5_NS), F32)],
        compiler_params=_cparams(("arbitrary", "arbitrary")),
        name="s5_scan",
    )(h_tm, wb, wc, ar, ai, d_skip, h0)


def _glu_kernel(y_ref, x_ref, mod_ref, w_ref, o_ref):
    z = _dot(jax.nn.gelu(y_ref[...]).astype(BF16), w_ref[...])
    o_ref[0] = x_ref[0] + mod_ref[0][:, 2 * D:3 * D] * (z[:, 0:D] * jax.nn.sigmoid(z[:, D:2 * D]))


def _glu_residual(y_tm, x, mod, w, tl):
    b, t, _ = x.shape
    tmod = mod.shape[1]
    return pl.pallas_call(
        _glu_kernel,
        grid=(b, t // tl),
        in_specs=[pl.BlockSpec((tl, D), lambda i, j: (j, i)),
                  pl.BlockSpec((1, tl, D), lambda i, j: (i, j, 0)),
                  pl.BlockSpec((1, tmod, 3 * D), lambda i, j: (i, 0, 0)),
                  pl.BlockSpec((D, 2 * D), lambda i, j: (0, 0))],
        out_specs=pl.BlockSpec((1, tl, D), lambda i, j: (i, j, 0)),
        out_shape=jax.ShapeDtypeStruct((b, t, D), F32),
        compiler_params=_cparams(("arbitrary", "arbitrary")),
        name="s5_glu",
    )(y_tm, x, mod, w)


def _s5_block_weights(bb_re, bb_im, c_re, c_im):
    nj, ng = D // S5_CB, S5_CB // S5_GROUP_CH
    eye = jnp.eye(ng, dtype=F32)

    def wb_part(bb):
        t = bb.reshape(nj, ng, S5_STATE, S5_GROUP_CH).transpose(0, 1, 3, 2)
        return jnp.einsum("jgcn,gh->jgchn", t, eye).reshape(nj, S5_CB, S5_NS)

    def wc_part(c):
        t = c.reshape(nj, ng, S5_GROUP_CH, S5_STATE).transpose(0, 1, 3, 2)
        return jnp.einsum("jgnc,gh->jgnhc", t, eye).reshape(nj, S5_NS, S5_CB)

    wb = jnp.concatenate([wb_part(bb_re), wb_part(bb_im)], axis=2).astype(BF16)
    wc = jnp.concatenate([wc_part(c_re), -wc_part(c_im)], axis=1).astype(BF16)
    return wb, wc


def _s5_layer(x, mod, g, wb, wc, ar, ai, d_skip, h0, w_glu, r, tl):
    b, t, _ = x.shape
    h_tm = _modulate_time_major(x, mod, g, tl).reshape(t * b, D)
    h0_t = h0.reshape(b, 2, S5_GROUPS * S5_STATE).transpose(1, 0, 2)
    y, st = _s5_scan(h_tm, wb, wc, ar, ai, d_skip, h0_t, r, tl)
    x_new = _glu_residual(y.reshape(t, b * D), x, mod, w_glu, tl)
    return x_new, st.transpose(1, 0, 2).reshape(b, 2, S5_GROUPS, S5_STATE)


def _dist_tiles():
    r = jnp.arange(TQ, dtype=jnp.int32)[:, None]
    c = jnp.arange(TQ, dtype=jnp.int32)[None, :]
    d0 = r - c
    edge = 2 * TQ + r - c
    return jnp.concatenate([d0, TQ + d0, 2 * TQ + d0, jnp.where(edge <= WINDOW, edge, -1),
                            jnp.full((TQ, TQ), -1, jnp.int32)], axis=0)


def _dist_cmp(seq):
    q = jnp.arange(seq, dtype=jnp.int32)[:, None]
    n = jnp.arange(LANES, dtype=jnp.int32)[None, :]
    return jnp.where(n < seq // CMP_BLOCK, q - ((n + 1) * CMP_BLOCK - 1), -1)


_SAMPLE_TABLE_SIZES = (LANES, MOBA_BLOCK, PAGE, PAST_LEN // CMP_BLOCK, WINDOW)


def _dist_sample():
    ar = lambda n: jnp.arange(n, dtype=jnp.int32)
    misc = jnp.zeros((LANES,), jnp.int32).at[1].set(MAX_DISTANCE * 4)
    moba = MOBA_BLOCK - ar(MOBA_BLOCK)
    sel = PAGE - ar(PAGE)
    cmp_ = PAST_LEN - ((ar(PAST_LEN // CMP_BLOCK) + 1) * CMP_BLOCK - 1)
    win = WINDOW - ar(WINDOW)
    return jnp.concatenate([misc, moba, sel, cmp_, win])[None, :]


def _block_diag2(w):
    z = jnp.zeros_like(w)
    return jnp.concatenate([jnp.concatenate([w, z], axis=-1), jnp.concatenate([z, w], axis=-1)], axis=-2)


def kernel(x_prompt, x_sample, cache_moba_kv, cache_nsa_kv, state_nsa_win, state_s5, page_table, c_prompt, c_sample, rel_bias, attn_norm_g, attn_ada_w, attn_ada_b, attn_w_in, attn_qk_g, nsa_cmp_pos, nsa_cmp_w1, nsa_cmp_w2, attn_w_out, ssm_norm_g, ssm_ada_w, ssm_ada_b, s5_a_re, s5_a_im, s5_log_dt, s5_b_re, s5_b_im, s5_c_re, s5_c_im, s5_d, s5_w_glu, mlp_norm_g, mlp_ada_w, mlp_ada_b, mlp_w1, mlp_w2):
    bp, seq, _ = x_prompt.shape
    bs = x_sample.shape[0]
    assert seq % TQ == 0 and x_sample.shape[1] == 1
    n_pool = cache_moba_kv.shape[1]

    c_all = jnp.concatenate([c_prompt, c_sample], axis=0)
    split_mod = lambda m: (m[:bp, None, :], m[None, bp:, :])
    mod_attn = _adaln(c_all, attn_ada_w, attn_ada_b)
    mod_ssm = _adaln(c_all, ssm_ada_w, ssm_ada_b)
    mod_mlp = _adaln(c_all, mlp_ada_w, mlp_ada_b)

    xp = x_prompt
    xs = x_sample.reshape(1, bs, D)

    tb = _bias_table(rel_bias, _dist_tiles()).reshape(2 * MOBA_HEADS, N_BIAS_TILES, TQ, TQ)
    tc = _bias_table(rel_bias, _dist_cmp(seq))[MOBA_HEADS:]
    ts = _bias_table(rel_bias, _dist_sample())[:, 0, :]
    offs = [0]
    for size in _SAMPLE_TABLE_SIZES:
        offs.append(offs[-1] + size)
    part = lambda heads, t: ts[heads, offs[t]:offs[t + 1]]
    hm, hn = slice(0, MOBA_HEADS), slice(MOBA_HEADS, 2 * MOBA_HEADS)
    misc_m, misc_n, tsb, tsn, tcs, tws = part(hm, 0), part(hn, 0), part(hm, 1), part(hn, 2), part(hn, 3), part(hn, 4)

    w_in = jnp.pad(attn_w_in[0], ((0, 0), (0, IN_COLS_PAD - IN_COLS))).astype(BF16)
    qkg_t = jnp.pad(jnp.tile(attn_qk_g[0], (1, 2)), ((0, 2), (0, 0)))
    lr = jnp.arange(LANES)
    avg = jnp.where(lr[:, None] // HEAD_DIM == lr[None, :] // HEAD_DIM, 1.0 / HEAD_DIM, 0.0).astype(BF16)
    g_attn = attn_norm_g[0][None, :]
    w_out = attn_w_out[0].astype(BF16)
    pos = jnp.concatenate([nsa_cmp_pos[0, 0], nsa_cmp_pos[0, 0], nsa_cmp_pos[0, 1], nsa_cmp_pos[0, 1]], axis=1)
    w1bd = _block_diag2(nsa_cmp_w1[0].reshape(2, CMP_BLOCK, HEAD_DIM, CMP_HIDDEN)).astype(BF16)
    w1bd = w1bd.reshape(2, CMP_BLOCK // 2, 256, 256)
    w2bd = _block_diag2(nsa_cmp_w2[0]).astype(BF16)
    gkc = qkg_t[3:4]

    mp_attn, ms_attn = split_mod(mod_attn[0])
    mq, mkv, nq, nkv, wkv, gates, mkv_t, nkv_t = _attn_proj(xp, mp_attn, g_attn, w_in, qkg_t, avg, 512, page_major=True)
    o_moba = _moba_prompt(mq, mkv, tb[:MOBA_HEADS])
    kcmp, vcmp = _cmp_prompt(nkv, pos, w1bd, w2bd, gkc, avg)
    o_nsa = _nsa_prompt(nq, nkv, wkv, kcmp, vcmp, gates, tb[MOBA_HEADS:], tc)
    xp = _outproj(xp, mp_attn, o_moba, o_nsa, w_out, 512)
    npg = seq // PAGE
    moba_p = mkv_t.reshape(1, bp, npg, 2, MOBA_HEADS, HEAD_DIM, PAGE).transpose(0, 1, 2, 6, 3, 4, 5)
    nsa_p = nkv_t.reshape(1, bp, npg, 4, 2, HEAD_DIM, PAGE).transpose(0, 1, 2, 6, 3, 4, 5)
    win_p = wkv[:, seq - min(WINDOW, seq):].reshape(1, bp, min(WINDOW, seq), 2, 2, HEAD_DIM)
    mq_s, mkv_s, nq_s, nkv_s, wkv_s, gates_s = _attn_proj(xs, ms_attn, g_attn, w_in, qkg_t, avg, bs)
    cache_m_t = cache_moba_kv.transpose(0, 1, 3, 4, 5, 2).reshape(n_pool, 2, 512, PAGE)
    cache_n_t = cache_nsa_kv.transpose(0, 1, 3, 4, 5, 2).reshape(n_pool, 512, PAGE)
    win_t = state_nsa_win[0].transpose(0, 2, 3, 4, 1).reshape(bs, 256, WINDOW)
    lw = jnp.arange(512)
    qmat_m = jnp.where(lw[None, None, :] // HEAD_DIM == jnp.arange(MOBA_HEADS)[None, :, None], mq_s[0][:, None, :], 0.0)
    col3 = lambda a, lo, width: a[0][:, None, lo:lo + width]
    o_moba_s = _moba_sample(page_table, cache_m_t, qmat_m, col3(mkv_s, 0, 512), col3(mkv_s, 512, 512), tsb, misc_m)
    nq4 = nq_s[0].reshape(bs, NSA_HEADS, HEAD_DIM)
    kvh = jnp.arange(NSA_HEADS) // NSA_GROUP
    qmat_n = jnp.concatenate([jnp.where(kvh[None, :, None] == 0, nq4, 0.0),
                              jnp.where(kvh[None, :, None] == 1, nq4, 0.0)], axis=2)
    o_nsa_s = _nsa_sample(page_table, cache_n_t, qmat_n,
                          col3(nkv_s, 256, LANES), col3(nkv_s, 384, LANES), col3(wkv_s, 0, LANES),
                          col3(wkv_s, 128, LANES), win_t, gates_s.reshape(bs, 1, LANES),
                          tsn, misc_n, tcs, tws, pos, w1bd, w2bd, gkc, avg)
    xs = _outproj(xs, ms_attn, o_moba_s.reshape(1, bs, 512), o_nsa_s.reshape(1, bs, 512), w_out, bs)
    moba_s = mkv_s.reshape(1, bs, 1, 2, MOBA_HEADS, HEAD_DIM)
    nsa_s = nkv_s.reshape(1, bs, 1, 4, 2, HEAD_DIM)
    win_s = jnp.concatenate([state_nsa_win[0][:, 1:], wkv_s[0].reshape(bs, 1, 2, 2, HEAD_DIM)], axis=1)[None]

    w1_0, w2_0 = mlp_w1[0].astype(BF16), mlp_w2[0].astype(BF16)
    mp_mlp, ms_mlp = split_mod(mod_mlp[0])
    g_mlp0 = mlp_norm_g[0][None, :]
    xp = _mlp(xp, mp_mlp, g_mlp0, w1_0, w2_0, 512, 1024)
    xs = _mlp(xs, ms_mlp, g_mlp0, w1_0, w2_0, bs, 1024)

    ab_re, ab_im, bb_re, bb_im = _s5_discretize(s5_a_re[0], s5_a_im[0], s5_log_dt[0], s5_b_re[0], s5_b_im[0])
    wb, wc = _s5_block_weights(bb_re, bb_im, s5_c_re[0], s5_c_im[0])
    nj = D // S5_CB
    ar = ab_re.reshape(nj, 1, S5_NS)
    ai = ab_im.reshape(nj, 1, S5_NS)
    g_ssm = ssm_norm_g[0][None, :]
    d_skip = s5_d[0][None, :]
    w_glu = s5_w_glu[0].astype(BF16)
    mp_ssm, ms_ssm = split_mod(mod_ssm[0])
    xp, st_p = _s5_layer(xp, mp_ssm, g_ssm, wb, wc, ar, ai, d_skip,
                         jnp.zeros((bp, 2, S5_GROUPS, S5_STATE), F32), w_glu, bp, 128)
    h_s = _modulate_time_major(xs, ms_ssm, g_ssm, bs)
    y_s, st_s = _s5_scan(h_s, wb, wc, ar, ai, d_skip,
                         state_s5[0].reshape(bs, 2, S5_GROUPS * S5_STATE).transpose(1, 0, 2), bs, 1)
    xs = _glu_residual(y_s, xs, ms_ssm, w_glu, bs)
    st_s = st_s.transpose(1, 0, 2).reshape(bs, 2, S5_GROUPS, S5_STATE)

    w1_1, w2_1 = mlp_w1[1].astype(BF16), mlp_w2[1].astype(BF16)
    mp_mlp, ms_mlp = split_mod(mod_mlp[1])
    g_mlp1 = mlp_norm_g[1][None, :]
    xp = _mlp(xp, mp_mlp, g_mlp1, w1_1, w2_1, 512, 1024)
    xs = _mlp(xs, ms_mlp, g_mlp1, w1_1, w2_1, bs, 1024)

    return (xp, xs.reshape(bs, 1, D), moba_p, moba_s, nsa_p, nsa_s, win_p, win_s, st_p[None], st_s[None])
```

```python
import functools
import math

import jax
import jax.numpy as jnp
from jax import lax
from jax.experimental import pallas as pl
from jax.experimental.pallas import tpu as pltpu

F32 = jnp.float32
BF16 = jnp.bfloat16
HIGHEST = lax.Precision.HIGHEST

D = 1024
HEAD_DIM = 64
MOBA_HEADS = 8
NSA_HEADS = 8
NSA_GROUP = 4
MOBA_BLOCK = 256
MOBA_TOPK = 3
CMP_BLOCK = 32
CMP_HIDDEN = 128
SEL_BLOCK = 64
SEL_TOPK = 16
WINDOW = 512
NUM_BUCKETS = 32
MAX_DISTANCE = 128
PAGE = 128
PAST_LEN = 8192
D_FF = 4 * D
S5_GROUPS = 64
S5_STATE = 64
S5_GROUP_CH = 16
IN_COLS = 3 * 512 + 512 + 6 * 128 + 3 * NSA_HEADS
IN_COLS_PAD = 23 * 128
EPS = 1e-6
SCALE = HEAD_DIM ** -0.5
LANES = 128
TQ = 256
N_BIAS_TILES = 5
NEG = -1e30
M_INIT = -1e15
VMEM_LIMIT = 56 * 1024 * 1024

_NT = (((1,), (1,)), ((), ()))


def _cparams(sem):
    return pltpu.CompilerParams(dimension_semantics=sem, vmem_limit_bytes=VMEM_LIMIT)


def _dot(a, b, **kw):
    return jnp.dot(a, b, preferred_element_type=F32, **kw)


def _dot_nt(a, b, **kw):
    return lax.dot_general(a, b, _NT, preferred_element_type=F32, **kw)


def _modulate(x, g, shift, scale):
    ms = jnp.mean(x * x, axis=-1, keepdims=True)
    return x * lax.rsqrt(ms + EPS) * g * (1.0 + scale) + shift


def _group_mean_sq(z, avg):
    sq = z * z
    hi = sq.astype(BF16)
    lo = (sq - hi.astype(F32)).astype(BF16)
    return _dot(hi, avg) + _dot(lo, avg)


def _col(x, lane, idx):
    return jnp.sum(jnp.where(lane == idx, x, 0.0), axis=1, keepdims=True)


def _adaln_kernel(c_ref, w_ref, b_ref, o_ref):
    c = c_ref[...]
    s = c * jax.nn.sigmoid(c)
    o_ref[0] = _dot(s, w_ref[0], precision=HIGHEST) + b_ref[0]


def _adaln(c_all, w, b):
    nl, n = w.shape[0], c_all.shape[0]
    return pl.pallas_call(
        _adaln_kernel,
        grid=(nl, 3),
        in_specs=[pl.BlockSpec((n, D), lambda l, j: (0, 0)),
                  pl.BlockSpec((1, D, D), lambda l, j: (l, 0, j)),
                  pl.BlockSpec((1, 1, D), lambda l, j: (l, 0, j))],
        out_specs=pl.BlockSpec((1, n, D), lambda l, j: (l, 0, j)),
        out_shape=jax.ShapeDtypeStruct((nl, n, 3 * D), F32),
        compiler_params=_cparams(("arbitrary", "arbitrary")),
        name="adaln",
    )(c_all, w, b.reshape(nl, 1, 3 * D))


def _bias_kernel(rb_ref, d_ref, o_ref):
    h = pl.program_id(0)
    dist = d_ref[...]
    n = jnp.maximum(dist, 0)
    max_exact = NUM_BUCKETS // 2
    nf = jnp.maximum(n, 1).astype(F32)
    large = max_exact + (jnp.log(nf / max_exact) / math.log(MAX_DISTANCE / max_exact)
                         * (NUM_BUCKETS - max_exact)).astype(jnp.int32)
    large = jnp.minimum(large, NUM_BUCKETS - 1)
    bucket = jnp.where(n < max_exact, n, large)
    acc = jnp.zeros(dist.shape, F32)
    for k in range(NUM_BUCKETS):
        acc = jnp.where(bucket == k, rb_ref[k, h], acc)
    o_ref[0] = jnp.where(dist < 0, NEG, acc)


def _bias_table(rel_bias, dist):
    r, c = dist.shape
    nh = rel_bias.shape[1]
    return pl.pallas_call(
        _bias_kernel,
        grid=(nh,),
        in_specs=[pl.BlockSpec(memory_space=pltpu.SMEM),
                  pl.BlockSpec((r, c), lambda h: (0, 0))],
        out_specs=pl.BlockSpec((1, r, c), lambda h: (h, 0, 0)),
        out_shape=jax.ShapeDtypeStruct((nh, r, c), F32),
        compiler_params=_cparams(("arbitrary",)),
        name="bias_table",
    )(rel_bias, dist)


def _proj_kernel(x_ref, mod_ref, g_ref, w_ref, qkg_ref, avg_ref,
                 mq_ref, mkv_ref, nq_ref, nkv_ref, wkv_ref, gt_ref, *page_major_refs):
    x = x_ref[0]
    mod = mod_ref[0]
    h = _modulate(x, g_ref[...], mod[:, 0:D], mod[:, D:2 * D])
    z = _dot(h.astype(BF16), w_ref[...])
    avg = avg_ref[...]

    def normed(lo, gi):
        zs = z[:, lo:lo + LANES]
        return zs * lax.rsqrt(_group_mean_sq(zs, avg) + EPS) * qkg_ref[gi:gi + 1, :]

    for t in range(4):
        mq_ref[0, :, t * LANES:(t + 1) * LANES] = normed(t * LANES, 0) * SCALE
        mkv_ref[0, :, t * LANES:(t + 1) * LANES] = normed(512 + t * LANES, 1)
        nq_ref[0, :, t * LANES:(t + 1) * LANES] = normed(1536 + t * LANES, 2) * SCALE
    mkv_ref[0, :, 512:1024] = z[:, 1024:1536]
    nkv_ref[0, :, 0:256] = z[:, 2048:2304]
    nkv_ref[0, :, 256:384] = normed(2304, 4)
    nkv_ref[0, :, 384:512] = z[:, 2432:2560]
    wkv_ref[0, :, 0:128] = normed(2560, 5)
    wkv_ref[0, :, 128:256] = z[:, 2688:2816]
    gt_ref[0] = jax.nn.sigmoid(z[:, 2816:2944])
    if page_major_refs:
        mkv_t_ref, nkv_t_ref = page_major_refs
        for p in range(x.shape[0] // PAGE):
            mkv_t_ref[0, p] = mkv_ref[0, p * PAGE:(p + 1) * PAGE, :].T
            nkv_t_ref[0, p] = nkv_ref[0, p * PAGE:(p + 1) * PAGE, :].T


def _attn_proj(x, mod, g, w_pad, qkg_t, avg, tm, page_major=False):
    b, t, _ = x.shape
    tmod = mod.shape[1]
    row = lambda width: pl.BlockSpec((1, tm, width), lambda i, j: (i, j, 0))
    shp = lambda width: jax.ShapeDtypeStruct((b, t, width), F32)
    out_specs = [row(512), row(1024), row(512), row(512), row(256), row(128)]
    out_shape = [shp(512), shp(1024), shp(512), shp(512), shp(256), shp(128)]
    if page_major:
        for width in (1024, 512):
            out_specs.append(pl.BlockSpec((1, tm // PAGE, width, PAGE), lambda i, j: (i, j, 0, 0)))
            out_shape.append(jax.ShapeDtypeStruct((b, t // PAGE, width, PAGE), F32))
    return pl.pallas_call(
        _proj_kernel,
        grid=(b, t // tm),
        in_specs=[row(D),
                  pl.BlockSpec((1, tmod, 3 * D), lambda i, j: (i, 0, 0)),
                  pl.BlockSpec((1, D), lambda i, j: (0, 0)),
                  pl.BlockSpec((D, IN_COLS_PAD), lambda i, j: (0, 0)),
                  pl.BlockSpec((8, LANES), lambda i, j: (0, 0)),
                  pl.BlockSpec((LANES, LANES), lambda i, j: (0, 0))],
        out_specs=out_specs,
        out_shape=out_shape,
        compiler_params=_cparams(("arbitrary", "arbitrary")),
        name="attn_proj",
    )(x, mod, g, w_pad, qkg_t, avg)


def _rank_rows(score, rowi, ncand, step=1):
    rank = jnp.zeros(score.shape, F32)
    for m in range(0, ncand * step, step):
        rm = score[m:m + 1, :]
        rank = rank + jnp.where(rm > score, 1.0, 0.0) + jnp.where((rm == score) & (m < rowi), 1.0, 0.0)
    return rank


def _columns_from_rows(x_t):
    pad = jnp.zeros((LANES - x_t.shape[0], x_t.shape[1]), F32)
    return jnp.concatenate([x_t, pad], axis=0).T


def _softmax_pv(pieces, v_all):
    m = pieces[0]
    for s in pieces[1:]:
        m = jnp.maximum(m, s)
    m = jnp.maximum(jnp.max(m, axis=1, keepdims=True), M_INIT)
    ps = [jnp.exp(s - m) for s in pieces]
    tot = ps[0]
    for p in ps[1:]:
        tot = tot + p
    l = jnp.sum(tot, axis=1, keepdims=True)
    p_all = jnp.concatenate([p.astype(BF16) for p in ps], axis=1) if len(ps) > 1 else ps[0].astype(BF16)
    return _dot(p_all, v_all) / jnp.maximum(l, 1e-30)


def _moba_prompt_kernel(q_ref, k_ref, v_ref, t_ref, o_ref, km_scr, kb_scr, vb_scr):
    s_len = q_ref.shape[1]
    nblk = s_len // MOBA_BLOCK
    nq = s_len // TQ
    lane = lax.broadcasted_iota(jnp.int32, (TQ, LANES), 1)
    rowb = lax.broadcasted_iota(jnp.int32, (nblk, TQ), 0)
    km_scr[...] = jnp.zeros(km_scr.shape, F32)
    for n in range(nblk):
        km_scr[n:n + 1, :] = jnp.mean(k_ref[0, n * MOBA_BLOCK:(n + 1) * MOBA_BLOCK, :], axis=0, keepdims=True)
    kmean = km_scr[...]
    kb_scr[...] = k_ref[0].astype(BF16)
    vb_scr[...] = v_ref[0].astype(BF16)

    def qtile(i, _):
        r0 = pl.multiple_of(i * TQ, TQ)
        q2 = q_ref[0, pl.ds(r0, TQ), :]
        qbs, cbs = [], []
        for e in range(2):
            qe = jnp.where(lane // HEAD_DIM == e, q2, 0.0)
            gate_t = _dot_nt(kmean, qe, precision=HIGHEST)[0:nblk]
            gm = jnp.where(rowb < i, gate_t, -jnp.inf)
            sel = ((_rank_rows(gm, rowb, nblk) < MOBA_TOPK) & (rowb < i)) | (rowb == i)
            cbs.append(_columns_from_rows(jnp.where(sel, 0.0, NEG)))
            qbs.append(qe.astype(BF16))

        for c in range(1, nq // 2 + 1):
            @pl.when(i // 2 + 1 == c)
            def _():
                kall = kb_scr[0:2 * c * TQ, :]
                vall = vb_scr[0:2 * c * TQ, :]
                outs = []
                for e in range(2):
                    s = _dot_nt(qbs[e], kall)
                    pieces = [s[:, n * TQ:(n + 1) * TQ] + t_ref[e, jnp.clip(i - n, 0, 2)] + cbs[e][:, n:n + 1]
                              for n in range(2 * c)]
                    outs.append(_softmax_pv(pieces, vall))
                o_ref[0, pl.ds(r0, TQ), :] = jnp.where(lane < HEAD_DIM, outs[0], outs[1])
        return 0

    lax.fori_loop(0, nq, qtile, 0)


def _moba_prompt(mq, mkv, tb):
    b, s, _ = mq.shape
    npair = MOBA_HEADS // 2
    return pl.pallas_call(
        _moba_prompt_kernel,
        grid=(b, npair),
        in_specs=[pl.BlockSpec((1, s, LANES), lambda i, p: (i, 0, p)),
                  pl.BlockSpec((1, s, LANES), lambda i, p: (i, 0, p)),
                  pl.BlockSpec((1, s, LANES), lambda i, p: (i, 0, npair + p)),
                  pl.BlockSpec((2, N_BIAS_TILES, TQ, TQ), lambda i, p: (p, 0, 0, 0))],
        out_specs=pl.BlockSpec((1, s, LANES), lambda i, p: (i, 0, p)),
        out_shape=jax.ShapeDtypeStruct((b, s, 512), F32),
        scratch_shapes=[pltpu.VMEM((LANES, LANES), F32), pltpu.VMEM((s, LANES), BF16), pltpu.VMEM((s, LANES), BF16)],
        compiler_params=_cparams(("arbitrary", "arbitrary")),
        name="moba_prompt",
    )(mq, mkv, mkv, tb)


def _compress_tokens(load_k, load_v, pos_ref, w1_ref, w2_ref):
    hk = hv = None
    for r in range(0, CMP_BLOCK, 2):
        xk = [(load_k(r + t) + pos_ref[r + t:r + t + 1, 0:LANES]).astype(BF16) for t in range(2)]
        xv = [(load_v(r + t) + pos_ref[r + t:r + t + 1, LANES:2 * LANES]).astype(BF16) for t in range(2)]
        dk = _dot(jnp.concatenate(xk, axis=1), w1_ref[0, r // 2])
        dv = _dot(jnp.concatenate(xv, axis=1), w1_ref[1, r // 2])
        hk = dk if hk is None else hk + dk
        hv = dv if hv is None else hv + dv
    ck = _dot(jax.nn.gelu(hk).astype(BF16), w2_ref[0])
    cv = _dot(jax.nn.gelu(hv).astype(BF16), w2_ref[1])
    return ck, cv


def _cmp_prompt_kernel(xk_ref, xv_ref, pos_ref, w1_ref, w2_ref, gkc_ref, avg_ref, kc_ref, vc_ref):
    nblk = xk_ref.shape[1] // CMP_BLOCK
    ck, cv = _compress_tokens(lambda r: xk_ref[0, pl.ds(r, nblk, stride=CMP_BLOCK), :],
                              lambda r: xv_ref[0, pl.ds(r, nblk, stride=CMP_BLOCK), :], pos_ref, w1_ref, w2_ref)
    ck = ck * lax.rsqrt(_group_mean_sq(ck, avg_ref[...]) + EPS) * gkc_ref[...]
    kc_ref[0] = jnp.zeros((LANES, LANES), F32)
    vc_ref[0] = jnp.zeros((LANES, LANES), F32)
    kc_ref[0, 0:nblk, :] = ck
    vc_ref[0, 0:nblk, :] = cv


def _cmp_prompt(nkv, pos, w1bd, w2bd, gkc, avg):
    b, s, _ = nkv.shape
    const = lambda shape: pl.BlockSpec(shape, lambda i: (0,) * len(shape))
    return pl.pallas_call(
        _cmp_prompt_kernel,
        grid=(b,),
        in_specs=[pl.BlockSpec((1, s, LANES), lambda i: (i, 0, 0)), pl.BlockSpec((1, s, LANES), lambda i: (i, 0, 1)),
                  const((CMP_BLOCK, 256)), const((2, CMP_BLOCK // 2, 256, 256)), const((2, 256, LANES)),
                  const((1, LANES)), const((LANES, LANES))],
        out_specs=[pl.BlockSpec((1, LANES, LANES), lambda i: (i, 0, 0))] * 2,
        out_shape=[jax.ShapeDtypeStruct((b, LANES, LANES), F32)] * 2,
        compiler_params=_cparams(("arbitrary",)),
        name="nsa_compress_prompt",
    )(nkv, nkv, pos, w1bd, w2bd, gkc, avg)


def _nsa_prompt_kernel(q_ref, ks_ref, vs_ref, kw_ref, vw_ref, kc_ref, vc_ref, g_ref, t_ref, tc_ref,
                       o_ref, ksb_scr, vsb_scr, kwb_scr, vwb_scr):
    s_len = q_ref.shape[1]
    k = pl.program_id(1)
    lane = lax.broadcasted_iota(jnp.int32, (TQ, LANES), 1)
    kvmask = (lane // HEAD_DIM) == k
    kc = kc_ref[0].astype(BF16)
    vc = vc_ref[0].astype(BF16)
    nsel = s_len // SEL_BLOCK
    nq = s_len // TQ
    ncmp = s_len // CMP_BLOCK
    rowb = lax.broadcasted_iota(jnp.int32, (nsel, TQ), 0)
    qpos = lax.broadcasted_iota(jnp.int32, (nsel, TQ), 1)
    pair_r = lax.broadcasted_iota(jnp.int32, (nsel, LANES), 0)
    pair_c = lax.broadcasted_iota(jnp.int32, (nsel, LANES), 1)
    pair_t = jnp.where((pair_c // (SEL_BLOCK // CMP_BLOCK) == pair_r) & (pair_c < ncmp), 1.0, 0.0)
    ksb_scr[...] = ks_ref[0].astype(BF16)
    vsb_scr[...] = vs_ref[0].astype(BF16)
    kwb_scr[...] = kw_ref[0].astype(BF16)
    vwb_scr[...] = vw_ref[0].astype(BF16)

    def qtile(i, _):
        r0 = pl.multiple_of(i * TQ, TQ)
        qs = []
        for h in range(NSA_GROUP):
            q2 = q_ref[0, pl.ds(r0, TQ), (h // 2) * LANES:(h // 2 + 1) * LANES]
            qa = jnp.where(k == (h % 2), q2, pltpu.roll(q2, HEAD_DIM, 1))
            qs.append(jnp.where(kvmask, qa, 0.0).astype(BF16))

        imp = jnp.zeros((TQ, LANES), F32)
        o_cmp = []
        for h in range(NSA_GROUP):
            s = _dot_nt(qs[h], kc) + tc_ref[h, pl.ds(r0, TQ), :]
            m = jnp.maximum(jnp.max(s, axis=1, keepdims=True), M_INIT)
            p = jnp.exp(s - m)
            p = p / jnp.maximum(jnp.sum(p, axis=1, keepdims=True), 1e-30)
            imp = imp + p
            o_cmp.append(_dot(p.astype(BF16), vc))

        imp_t = _dot_nt(pair_t, imp, precision=HIGHEST)
        own = (r0 + qpos) // SEL_BLOCK
        sc = jnp.where(rowb < own, imp_t, -jnp.inf)
        sel = ((_rank_rows(sc, rowb, nsel) < SEL_TOPK) & (rowb < own)) | (rowb == own)
        selb = _columns_from_rows(jnp.where(sel, 1.0, 0.0)).astype(BF16)

        wk, wv, wt = [], [], []
        for j, tidx in enumerate((3, 1, 0)):
            n = i - 2 + j
            c0 = pl.multiple_of(jnp.maximum(n, 0) * TQ, TQ)
            wk.append(kwb_scr[pl.ds(c0, TQ), :])
            wv.append(vwb_scr[pl.ds(c0, TQ), :])
            wt.append(jnp.where(n < 0, N_BIAS_TILES - 1, tidx))
        kw_all = jnp.concatenate(wk, axis=0)
        vw_all = jnp.concatenate(wv, axis=0)
        g = g_ref[0, pl.ds(r0, TQ), :]
        o_win = []
        for h in range(NSA_GROUP):
            s = _dot_nt(qs[h], kw_all)
            o_win.append(_softmax_pv([s[:, j * TQ:(j + 1) * TQ] + t_ref[h, wt[j]] for j in range(3)], vw_all))

        for c in range(1, nq // 2 + 1):
            @pl.when(i // 2 + 1 == c)
            def _():
                nkeys = 2 * c * TQ
                e_r = lax.broadcasted_iota(jnp.int32, (LANES, nkeys), 0)
                e_c = lax.broadcasted_iota(jnp.int32, (LANES, nkeys), 1)
                expand = jnp.where(e_r == e_c // SEL_BLOCK, 1.0, 0.0).astype(BF16)
                addm = (_dot(selb, expand) - 1.0) * (-NEG)
                kall = ksb_scr[0:nkeys, :]
                vall = vsb_scr[0:nkeys, :]
                res = []
                for h in range(NSA_GROUP):
                    s = _dot_nt(qs[h], kall) + addm
                    pieces = [s[:, n * TQ:(n + 1) * TQ] + t_ref[h, jnp.clip(i - n, 0, 2)] for n in range(2 * c)]
                    o_sel = _softmax_pv(pieces, vall)
                    hg = (k * NSA_GROUP + h) * 3
                    o = (_col(g, lane, hg) * o_cmp[h] + _col(g, lane, hg + 1) * o_sel
                         + _col(g, lane, hg + 2) * o_win[h])
                    res.append(jnp.where(k == (h % 2), o, pltpu.roll(o, HEAD_DIM, 1)))
                for t in range(2):
                    o_ref[0, pl.ds(r0, TQ), t * LANES:(t + 1) * LANES] = jnp.where(
                        lane < HEAD_DIM, res[2 * t], res[2 * t + 1])
        return 0

    lax.fori_loop(0, nq, qtile, 0)


def _nsa_prompt(nq, nkv, wkv, kcmp, vcmp, gates, tb, tc):
    b, s, _ = nq.shape
    col = lambda arr_cols, cb: pl.BlockSpec((1, s, LANES), lambda i, k: (i, 0, cb))
    return pl.pallas_call(
        _nsa_prompt_kernel,
        grid=(b, 2),
        in_specs=[pl.BlockSpec((1, s, 256), lambda i, k: (i, 0, k)),
                  col(512, 2), col(512, 3), col(256, 0), col(256, 1),
                  pl.BlockSpec((1, LANES, LANES), lambda i, k: (i, 0, 0)),
                  pl.BlockSpec((1, LANES, LANES), lambda i, k: (i, 0, 0)),
                  pl.BlockSpec((1, s, LANES), lambda i, k: (i, 0, 0)),
                  pl.BlockSpec((NSA_GROUP, N_BIAS_TILES, TQ, TQ), lambda i, k: (k, 0, 0, 0)),
                  pl.BlockSpec((NSA_GROUP, s, LANES), lambda i, k: (k, 0, 0))],
        out_specs=pl.BlockSpec((1, s, 256), lambda i, k: (i, 0, k)),
        out_shape=jax.ShapeDtypeStruct((b, s, 512), F32),
        scratch_shapes=[pltpu.VMEM((s, LANES), BF16)] * 4,
        compiler_params=_cparams(("arbitrary", "arbitrary")),
        name="nsa_prompt",
    )(nq, nkv, nkv, wkv, wkv, kcmp, vcmp, gates, tb, tc)


PAGES_PER_STEP = 8
CMP_PITCH = 40


def _rank_lt(score, lane, ncand, topk):
    rank = jnp.zeros(score.shape, F32)
    for m in range(ncand):
        col = score[:, m:m + 1]
        beats = (col > score) | ((col == score) & (m < lane))
        rank = rank + jnp.where(beats, 1.0, 0.0)
    return rank < topk


def _merge_blocks(sel, m_all, l_all, acc_scr, nblk, s_self, v_self):
    mx = jnp.maximum(jnp.max(jnp.where(sel, m_all, NEG), axis=1, keepdims=True), s_self)
    w = jnp.exp(jnp.where(sel, m_all - mx, NEG))
    w_self = jnp.exp(s_self - mx)
    den = jnp.sum(w * l_all, axis=1, keepdims=True) + w_self
    num = w_self * v_self
    for j in range(nblk):
        num = num + w[:, j:j + 1] * acc_scr[j]
    return num / den


def _moba_sample_kernel(pt_ref, *refs):
    pages = refs[:PAGES_PER_STEP]
    qm_ref, kn_ref, vn_ref, tsb_ref, misc_ref, o_ref, g_scr, m_scr, l_scr, acc_scr = refs[PAGES_PER_STEP:]
    s = pl.program_id(1)
    nstep = pl.num_programs(1)
    nblk = PAST_LEN // MOBA_BLOCK
    width = MOBA_HEADS * HEAD_DIM
    qm = qm_ref[0]
    qb = qm.astype(BF16)
    lane = lax.broadcasted_iota(jnp.int32, (MOBA_HEADS, LANES), 1)

    @pl.when(s == 0)
    def _():
        g_scr[...] = jnp.zeros(g_scr.shape, F32)
        m_scr[...] = jnp.zeros(m_scr.shape, F32)
        l_scr[...] = jnp.zeros(l_scr.shape, F32)

    npb = PAGES_PER_STEP // 2
    blk0 = s * npb
    kt_all = jnp.concatenate([pages[t][0, 0].astype(BF16) for t in range(PAGES_PER_STEP)], axis=1)
    vt_all = jnp.concatenate([pages[t][0, 1].astype(BF16) for t in range(PAGES_PER_STEP)], axis=1)
    raw = _dot(qb, kt_all)
    far = jnp.broadcast_to(misc_ref[:, 1:2], (MOBA_HEADS, MOBA_BLOCK))
    g_new, m_new, l_new = g_scr[...], m_scr[...], l_scr[...]
    p_rows = []
    for j in range(npb):
        seg = raw[:, j * MOBA_BLOCK:(j + 1) * MOBA_BLOCK]
        gate = jnp.sum(seg, axis=1, keepdims=True)
        sc = seg + (jnp.where(s == nstep - 1, tsb_ref[...], far) if j == npb - 1 else far)
        mj = jnp.max(sc, axis=1, keepdims=True)
        p = jnp.exp(sc - mj)
        g_new = jnp.where(lane == blk0 + j, gate, g_new)
        m_new = jnp.where(lane == blk0 + j, mj, m_new)
        l_new = jnp.where(lane == blk0 + j, jnp.sum(p, axis=1, keepdims=True), l_new)
        zeros = jnp.zeros((MOBA_HEADS, MOBA_BLOCK), F32)
        p_rows.append(jnp.concatenate([p if t == j else zeros for t in range(npb)], axis=1))
    g_scr[...] = g_new
    m_scr[...] = m_new
    l_scr[...] = l_new
    acc = _dot_nt(jnp.concatenate(p_rows, axis=0).astype(BF16), vt_all)
    for j in range(npb):
        acc_scr[blk0 + j] = acc[j * MOBA_HEADS:(j + 1) * MOBA_HEADS]

    @pl.when(s == nstep - 1)
    def _():
        gm = jnp.where(lane < nblk, g_scr[...], -jnp.inf)
        sel = _rank_lt(gm, lane, nblk, MOBA_TOPK) & (lane < nblk)
        s_self = jnp.sum(qm * kn_ref[0], axis=1, keepdims=True) + misc_ref[:, 0:1]
        o = _merge_blocks(sel, m_scr[...], l_scr[...], acc_scr, nblk, s_self, vn_ref[0])
        hrow = lax.broadcasted_iota(jnp.int32, (MOBA_HEADS, width), 0)
        hlane = lax.broadcasted_iota(jnp.int32, (MOBA_HEADS, width), 1)
        o_ref[0] = jnp.sum(jnp.where(hlane // HEAD_DIM == hrow, o, 0.0), axis=0, keepdims=True)


def _moba_sample(page_table, cache_t, qmat, knew, vnew, tsb, misc):
    nb, npages = page_table.shape
    nstep = npages // PAGES_PER_STEP
    width = MOBA_HEADS * HEAD_DIM
    nblk = PAST_LEN // MOBA_BLOCK

    def page_spec(j):
        return pl.BlockSpec((1, 2, width, PAGE), lambda b, s, pt: (pt[b, s * PAGES_PER_STEP + j], 0, 0, 0))

    per_b = lambda shape: pl.BlockSpec((1,) + shape, lambda b, s, pt: (b, 0, 0))
    const = lambda shape: pl.BlockSpec(shape, lambda b, s, pt: (0,) * len(shape))
    grid_spec = pltpu.PrefetchScalarGridSpec(
        num_scalar_prefetch=1,
        grid=(nb, nstep),
        in_specs=[page_spec(j) for j in range(PAGES_PER_STEP)]
        + [per_b((MOBA_HEADS, width)), per_b((1, width)), per_b((1, width)),
           const((MOBA_HEADS, MOBA_BLOCK)), const((MOBA_HEADS, LANES))],
        out_specs=per_b((1, width)),
        scratch_shapes=[pltpu.VMEM((MOBA_HEADS, LANES), F32)] * 3 + [pltpu.VMEM((nblk, MOBA_HEADS, width), F32)],
    )
    return pl.pallas_call(
        _moba_sample_kernel,
        grid_spec=grid_spec,
        out_shape=jax.ShapeDtypeStruct((nb, 1, width), F32),
        compiler_params=_cparams(("arbitrary", "arbitrary")),
        name="moba_sample",
    )(page_table, *([cache_t] * PAGES_PER_STEP), qmat, knew, vnew, tsb, misc)


def _nsa_sample_kernel(pt_ref, *refs):
    pages = refs[:PAGES_PER_STEP]
    (qm_ref, ksn_ref, vsn_ref, kwn_ref, vwn_ref, win_ref, g_ref, tsn_ref, misc_ref, tcs_ref, tws_ref,
     pos_ref, w1_ref, w2_ref, gkc_ref, avg_ref, o_ref, xk_scr, xv_scr, m_scr, l_scr, acc_scr) = refs[PAGES_PER_STEP:]
    s = pl.program_id(1)
    nstep = pl.num_programs(1)
    nsel = PAST_LEN // SEL_BLOCK
    ncmp = PAST_LEN // CMP_BLOCK
    npage = PAST_LEN // PAGE
    qm = qm_ref[0]
    qb = qm.astype(BF16)
    lane = lax.broadcasted_iota(jnp.int32, (NSA_HEADS, LANES), 1)
    row = lax.broadcasted_iota(jnp.int32, (NSA_HEADS, LANES), 0)
    lo = lane < HEAD_DIM

    @pl.when(s == 0)
    def _():
        m_scr[...] = jnp.zeros(m_scr.shape, F32)
        l_scr[...] = jnp.zeros(l_scr.shape, F32)

    for j in range(PAGES_PER_STEP):
        pg = s * PAGES_PER_STEP + j
        kc = pages[j][0, 0:LANES, :].T
        vc = pages[j][0, LANES:2 * LANES, :].T
        for b4 in range(PAGE // CMP_BLOCK):
            r0 = pl.multiple_of((pg * (PAGE // CMP_BLOCK) + b4) * CMP_PITCH, 8)
            xk_scr[pl.ds(r0, CMP_BLOCK), :] = kc[b4 * CMP_BLOCK:(b4 + 1) * CMP_BLOCK, :]
            xv_scr[pl.ds(r0, CMP_BLOCK), :] = vc[b4 * CMP_BLOCK:(b4 + 1) * CMP_BLOCK, :]

    ks_all = jnp.concatenate([pages[t][0, 256:384, :].astype(BF16) for t in range(PAGES_PER_STEP)], axis=1)
    vs_all = jnp.concatenate([pages[t][0, 384:512, :].astype(BF16) for t in range(PAGES_PER_STEP)], axis=1)
    raw = _dot(qb, ks_all)
    far = jnp.broadcast_to(misc_ref[:, 1:2], (NSA_HEADS, PAGE))
    m_new, l_new = m_scr[...], l_scr[...]
    b0 = 2 * s * PAGES_PER_STEP
    zeros = jnp.zeros((NSA_HEADS, PAGE), F32)
    p_rows = []
    for j in range(PAGES_PER_STEP):
        sc = raw[:, j * PAGE:(j + 1) * PAGE]
        sc = sc + (jnp.where(s == nstep - 1, tsn_ref[...], far) if j == PAGES_PER_STEP - 1 else far)
        m0 = jnp.max(jnp.where(lo, sc, NEG), axis=1, keepdims=True)
        m1 = jnp.max(jnp.where(lo, NEG, sc), axis=1, keepdims=True)
        p = jnp.exp(sc - jnp.where(lo, m0, m1))
        p0, p1 = jnp.where(lo, p, 0.0), jnp.where(lo, 0.0, p)
        l0 = jnp.sum(p0, axis=1, keepdims=True)
        l1 = jnp.sum(p1, axis=1, keepdims=True)
        bj = b0 + 2 * j
        m_new = jnp.where(lane == bj, m0, jnp.where(lane == bj + 1, m1, m_new))
        l_new = jnp.where(lane == bj, l0, jnp.where(lane == bj + 1, l1, l_new))
        for ph in (p0, p1):
            p_rows.append(jnp.concatenate([ph if t == j else zeros for t in range(PAGES_PER_STEP)], axis=1))
    m_scr[...] = m_new
    l_scr[...] = l_new
    acc = _dot_nt(jnp.concatenate(p_rows, axis=0).astype(BF16), vs_all)
    for b in range(2 * PAGES_PER_STEP):
        acc_scr[b0 + b] = acc[b * NSA_HEADS:(b + 1) * NSA_HEADS]

    @pl.when(s == nstep - 1)
    def _():
        ck, cv = _compress_tokens(lambda r: xk_scr[pl.ds(r, ncmp, stride=CMP_PITCH), :],
                                  lambda r: xv_scr[pl.ds(r, ncmp, stride=CMP_PITCH), :], pos_ref, w1_ref, w2_ref)
        ck = ck * lax.rsqrt(_group_mean_sq(ck, avg_ref[...]) + EPS) * gkc_ref[...]
        sc = _dot_nt(qb, ck.astype(BF16)) + tcs_ref[...]
        m = jnp.maximum(jnp.max(sc, axis=1, keepdims=True), M_INIT)
        pc = jnp.exp(sc - m)
        pc = pc / jnp.maximum(jnp.sum(pc, axis=1, keepdims=True), 1e-30)
        o_cmp = _dot(pc.astype(BF16), cv.astype(BF16))
        g0 = pc[0:1] + pc[1:2] + pc[2:3] + pc[3:4]
        g1 = pc[4:5] + pc[5:6] + pc[6:7] + pc[7:8]
        rowc = lax.broadcasted_iota(jnp.int32, (NSA_HEADS, ncmp), 0)
        imp = jnp.where(rowc < NSA_GROUP, g0, g1)
        pr = lax.broadcasted_iota(jnp.int32, (ncmp, LANES), 0)
        pc_ = lax.broadcasted_iota(jnp.int32, (ncmp, LANES), 1)
        pair = jnp.where(pr // (SEL_BLOCK // CMP_BLOCK) == pc_, 1.0, 0.0)
        impb = _dot(imp, pair, precision=HIGHEST)
        own = PAST_LEN // SEL_BLOCK
        sel = _rank_lt(jnp.where(lane < own, impb, -jnp.inf), lane, nsel, SEL_TOPK) & (lane < own)
        s_self = jnp.sum(qm * ksn_ref[0], axis=1, keepdims=True) + misc_ref[:, 0:1]
        o_sel = _merge_blocks(sel, m_scr[...], l_scr[...], acc_scr, nsel, s_self, vsn_ref[0])
        kw_t = win_ref[0, 0:LANES, :].astype(BF16)
        vw_t = win_ref[0, LANES:2 * LANES, :].astype(BF16)
        sw = _dot(qb, kw_t) + tws_ref[...]
        sw_self = jnp.sum(qm * kwn_ref[0], axis=1, keepdims=True) + misc_ref[:, 0:1]
        mw = jnp.maximum(jnp.max(sw, axis=1, keepdims=True), sw_self)
        pw = jnp.exp(sw - mw)
        pw_self = jnp.exp(sw_self - mw)
        o_win = ((_dot_nt(pw.astype(BF16), vw_t) + pw_self * vwn_ref[0])
                 / (jnp.sum(pw, axis=1, keepdims=True) + pw_self))
        gt = jnp.broadcast_to(g_ref[0], (NSA_HEADS, LANES))
        o8 = (_col(gt, lane, 3 * row) * o_cmp + _col(gt, lane, 3 * row + 1) * o_sel
              + _col(gt, lane, 3 * row + 2) * o_win)
        lane1 = lax.broadcasted_iota(jnp.int32, (1, LANES), 1)
        tiles = []
        for t in range(NSA_HEADS // 2):
            ha, hb = 2 * t, 2 * t + 1
            ra = o8[ha:ha + 1, :]
            rb = o8[hb:hb + 1, :]
            if ha // NSA_GROUP == 1:
                ra = pltpu.roll(ra, HEAD_DIM, 1)
            if hb // NSA_GROUP == 0:
                rb = pltpu.roll(rb, HEAD_DIM, 1)
            tiles.append(jnp.where(lane1 < HEAD_DIM, ra, rb))
        o_ref[0] = jnp.concatenate(tiles, axis=1)


def _nsa_sample(page_table, cache, qmat, ksn, vsn, kwn, vwn, win, gates, tsn, misc, tcs, tws,
                pos, w1bd, w2bd, gkc, avg):
    nb, npages = page_table.shape
    nstep = npages // PAGES_PER_STEP
    nsel = PAST_LEN // SEL_BLOCK

    def page_spec(j):
        return pl.BlockSpec((1, 512, PAGE), lambda b, s, pt: (pt[b, s * PAGES_PER_STEP + j], 0, 0))

    per_b = lambda shape: pl.BlockSpec((1,) + shape, lambda b, s, pt: (b, 0, 0))
    const = lambda shape: pl.BlockSpec(shape, lambda b, s, pt: (0,) * len(shape))
    cmp_rows = PAST_LEN // CMP_BLOCK * CMP_PITCH
    grid_spec = pltpu.PrefetchScalarGridSpec(
        num_scalar_prefetch=1,
        grid=(nb, nstep),
        in_specs=[page_spec(j) for j in range(PAGES_PER_STEP)]
        + [per_b((NSA_HEADS, LANES)), per_b((1, LANES)), per_b((1, LANES)), per_b((1, LANES)), per_b((1, LANES)),
           per_b((256, WINDOW)), per_b((1, LANES)),
           const((NSA_HEADS, PAGE)), const((NSA_HEADS, LANES)), const((NSA_HEADS, PAST_LEN // CMP_BLOCK)),
           const((NSA_HEADS, WINDOW)),
           const((CMP_BLOCK, 256)), const((2, CMP_BLOCK // 2, 256, 256)), const((2, 256, LANES)),
           const((1, LANES)), const((LANES, LANES))],
        out_specs=per_b((1, 512)),
        scratch_shapes=[pltpu.VMEM((cmp_rows, LANES), F32), pltpu.VMEM((cmp_rows, LANES), F32),
                        pltpu.VMEM((NSA_HEADS, LANES), F32),
                        pltpu.VMEM((NSA_HEADS, LANES), F32), pltpu.VMEM((nsel, NSA_HEADS, LANES), F32)],
    )
    return pl.pallas_call(
        _nsa_sample_kernel,
        grid_spec=grid_spec,
        out_shape=jax.ShapeDtypeStruct((nb, 1, 512), F32),
        compiler_params=_cparams(("arbitrary", "arbitrary")),
        name="nsa_sample",
    )(page_table, *([cache] * PAGES_PER_STEP), qmat, ksn, vsn, kwn, vwn, win, gates, tsn, misc, tcs, tws,
      pos, w1bd, w2bd, gkc, avg)


def _outproj_kernel(x_ref, mod_ref, om_ref, on_ref, w_ref, o_ref):
    y = _dot(om_ref[0].astype(BF16), w_ref[0:512, :]) + _dot(on_ref[0].astype(BF16), w_ref[512:1024, :])
    o_ref[0] = x_ref[0] + mod_ref[0][:, 2 * D:3 * D] * y


def _outproj(x, mod, o_m, o_n, w, tm):
    b, t, _ = x.shape
    tmod = mod.shape[1]
    row = lambda width: pl.BlockSpec((1, tm, width), lambda i, j: (i, j, 0))
    return pl.pallas_call(
        _outproj_kernel,
        grid=(b, t // tm),
        in_specs=[row(D), pl.BlockSpec((1, tmod, 3 * D), lambda i, j: (i, 0, 0)), row(512), row(512),
                  pl.BlockSpec((D, D), lambda i, j: (0, 0))],
        out_specs=row(D),
        out_shape=jax.ShapeDtypeStruct((b, t, D), F32),
        compiler_params=_cparams(("arbitrary", "arbitrary")),
        name="attn_outproj",
    )(x, mod, o_m, o_n, w)


def _mlp_kernel(x_ref, mod_ref, g_ref, w1_ref, w2_ref, o_ref, h_scr, acc_scr):
    kf = pl.program_id(2)

    @pl.when(kf == 0)
    def _():
        mod = mod_ref[0]
        h_scr[...] = _modulate(x_ref[0], g_ref[...], mod[:, 0:D], mod[:, D:2 * D]).astype(BF16)
        acc_scr[...] = jnp.zeros(acc_scr.shape, F32)

    a = jnp.square(jnp.maximum(_dot(h_scr[...], w1_ref[...]), 0.0))
    acc_scr[...] += _dot(a.astype(BF16), w2_ref[...])

    @pl.when(kf == pl.num_programs(2) - 1)
    def _():
        o_ref[0] = x_ref[0] + mod_ref[0][:, 2 * D:3 * D] * acc_scr[...]


def _mlp(x, mod, g, w1, w2, tm, tf):
    b, t, _ = x.shape
    tmod = mod.shape[1]
    return pl.pallas_call(
        _mlp_kernel,
        grid=(b, t // tm, D_FF // tf),
        in_specs=[pl.BlockSpec((1, tm, D), lambda i, j, kf: (i, j, 0)),
                  pl.BlockSpec((1, tmod, 3 * D), lambda i, j, kf: (i, 0, 0)),
                  pl.BlockSpec((1, D), lambda i, j, kf: (0, 0)),
                  pl.BlockSpec((D, tf), lambda i, j, kf: (0, kf)),
                  pl.BlockSpec((tf, D), lambda i, j, kf: (kf, 0))],
        out_specs=pl.BlockSpec((1, tm, D), lambda i, j, kf: (i, j, 0)),
        out_shape=jax.ShapeDtypeStruct((b, t, D), F32),
        scratch_shapes=[pltpu.VMEM((tm, D), BF16), pltpu.VMEM((tm, D), F32)],
        compiler_params=_cparams(("arbitrary", "arbitrary", "arbitrary")),
        name="mlp",
    )(x, mod, g, w1, w2)


def _s5_disc_kernel(are_ref, aim_ref, ldt_ref, bre_ref, bim_ref, abre_ref, abim_ref, bbre_ref, bbim_ref):
    a_re, a_im = are_ref[...], aim_ref[...]
    dt = jnp.exp(ldt_ref[...])
    decay = jnp.exp(dt * a_re)
    ab_re, ab_im = decay * jnp.cos(dt * a_im), decay * jnp.sin(dt * a_im)
    den = a_re * a_re + a_im * a_im
    f_re = ((ab_re - 1) * a_re + ab_im * a_im) / den
    f_im = (ab_im * a_re - (ab_re - 1) * a_im) / den
    br, bi = bre_ref[...], bim_ref[...]
    abre_ref[...] = ab_re
    abim_ref[...] = ab_im
    bbre_ref[...] = f_re * br - f_im * bi
    bbim_ref[...] = f_re * bi + f_im * br


def _s5_discretize(a_re, a_im, log_dt, b_re, b_im):
    rep = lambda a: jnp.repeat(a, S5_GROUP_CH, axis=1)
    shp = jax.ShapeDtypeStruct((S5_GROUPS, S5_STATE * S5_GROUP_CH), F32)
    ldt = jnp.broadcast_to(log_dt[:, None], (S5_GROUPS, S5_STATE * S5_GROUP_CH))
    flat = lambda a: a.reshape(S5_GROUPS, S5_STATE * S5_GROUP_CH)
    ab_re, ab_im, bb_re, bb_im = pl.pallas_call(
        _s5_disc_kernel, out_shape=[shp] * 4, name="s5_discretize",
    )(rep(a_re), rep(a_im), ldt, flat(b_re), flat(b_im))
    unrep = lambda a: a[:, ::S5_GROUP_CH]
    unflat = lambda a: a.reshape(S5_GROUPS, S5_STATE, S5_GROUP_CH)
    return unrep(ab_re), unrep(ab_im), unflat(bb_re), unflat(bb_im)


def _modulate_tm_kernel(x_ref, mod_ref, g_ref, o_ref):
    mod = mod_ref[0]
    o_ref[...] = _modulate(x_ref[0], g_ref[...], mod[:, 0:D], mod[:, D:2 * D])


def _modulate_time_major(x, mod, g, tl):
    b, t, _ = x.shape
    tmod = mod.shape[1]
    return pl.pallas_call(
        _modulate_tm_kernel,
        grid=(b, t // tl),
        in_specs=[pl.BlockSpec((1, tl, D), lambda i, j: (i, j, 0)),
                  pl.BlockSpec((1, tmod, 3 * D), lambda i, j: (i, 0, 0)),
                  pl.BlockSpec((1, D), lambda i, j: (0, 0))],
        out_specs=pl.BlockSpec((tl, D), lambda i, j: (j, i)),
        out_shape=jax.ShapeDtypeStruct((t, b * D), F32),
        compiler_params=_cparams(("arbitrary", "arbitrary")),
        name="s5_modulate",
    )(x, mod, g)


S5_CB = 256
S5_NS = S5_CB // S5_GROUP_CH * S5_STATE


def _s5_scan_kernel(h_ref, wb_ref, wc_ref, ar_ref, ai_ref, d_ref, h0_ref, y_ref, so_ref, xs_scr, st_scr, *, tl, r):
    i = pl.program_id(1)

    @pl.when(i == 0)
    def _():
        st_scr[...] = h0_ref[...]

    u = h_ref[...]
    xs_scr[...] = _dot(u.astype(BF16), wb_ref[0])
    ar = jnp.broadcast_to(ar_ref[0], (r, S5_NS))
    ai = jnp.broadcast_to(ai_ref[0], (r, S5_NS))

    def step(t, carry):
        xr, xi = carry
        r0 = pl.multiple_of(t * r, r)
        nr = ar * xr - ai * xi + xs_scr[pl.ds(r0, r), 0:S5_NS]
        ni = ar * xi + ai * xr + xs_scr[pl.ds(r0, r), S5_NS:2 * S5_NS]
        xs_scr[pl.ds(r0, r), 0:S5_NS] = nr
        xs_scr[pl.ds(r0, r), S5_NS:2 * S5_NS] = ni
        return nr, ni

    xr, xi = lax.fori_loop(0, tl, step, (st_scr[0], st_scr[1]))
    st_scr[0] = xr
    st_scr[1] = xi
    y_ref[...] = _dot(xs_scr[...].astype(BF16), wc_ref[0]) + d_ref[...] * u

    @pl.when(i == pl.num_programs(1) - 1)
    def _():
        so_ref[...] = st_scr[...]


def _s5_scan(h_tm, wb, wc, ar, ai, d_skip, h0, r, tl):
    rows = h_tm.shape[0]
    nj = D // S5_CB
    return pl.pallas_call(
        functools.partial(_s5_scan_kernel, tl=tl, r=r),
        grid=(nj, rows // (tl * r)),
        in_specs=[pl.BlockSpec((tl * r, S5_CB), lambda j, i: (i, j)),
                  pl.BlockSpec((1, S5_CB, 2 * S5_NS), lambda j, i: (j, 0, 0)),
                  pl.BlockSpec((1, 2 * S5_NS, S5_CB), lambda j, i: (j, 0, 0)),
                  pl.BlockSpec((1, 1, S5_NS), lambda j, i: (j, 0, 0)),
                  pl.BlockSpec((1, 1, S5_NS), lambda j, i: (j, 0, 0)),
                  pl.BlockSpec((1, S5_CB), lambda j, i: (0, j)),
                  pl.BlockSpec((2, r, S5_NS), lambda j, i: (0, 0, j))],
        out_specs=[pl.BlockSpec((tl * r, S5_CB), lambda j, i: (i, j)),
                   pl.BlockSpec((2, r, S5_NS), lambda j, i: (0, 0, j))],
        out_shape=[jax.ShapeDtypeStruct((rows, D), F32), jax.ShapeDtypeStruct((2, r, S5_GROUPS * S5_STATE), F32)],
        scratch_shapes=[pltpu.VMEM((tl * r, 2 * S5_NS), F32), pltpu.VMEM((2, r, S5_NS), F32)],
        compiler_params=_cparams(("arbitrary", "arbitrary")),
        name="s5_scan",
    )(h_tm, wb, wc, ar, ai, d_skip, h0)


def _glu_kernel(y_ref, x_ref, mod_ref, w_ref, o_ref):
    z = _dot(jax.nn.gelu(y_ref[...]).astype(BF16), w_ref[...])
    o_ref[0] = x_ref[0] + mod_ref[0][:, 2 * D:3 * D] * (z[:, 0:D] * jax.nn.sigmoid(z[:, D:2 * D]))


def _glu_residual(y_tm, x, mod, w, tl):
    b, t, _ = x.shape
    tmod = mod.shape[1]
    return pl.pallas_call(
        _glu_kernel,
        grid=(b, t // tl),
        in_specs=[pl.BlockSpec((tl, D), lambda i, j: (j, i)),
                  pl.BlockSpec((1, tl, D), lambda i, j: (i, j, 0)),
                  pl.BlockSpec((1, tmod, 3 * D), lambda i, j: (i, 0, 0)),
                  pl.BlockSpec((D, 2 * D), lambda i, j: (0, 0))],
        out_specs=pl.BlockSpec((1, tl, D), lambda i, j: (i, j, 0)),
        out_shape=jax.ShapeDtypeStruct((b, t, D), F32),
        compiler_params=_cparams(("arbitrary", "arbitrary")),
        name="s5_glu",
    )(y_tm, x, mod, w)


def _s5_block_weights(bb_re, bb_im, c_re, c_im):
    nj, ng = D // S5_CB, S5_CB // S5_GROUP_CH
    eye = jnp.eye(ng, dtype=F32)

    def wb_part(bb):
        t = bb.reshape(nj, ng, S5_STATE, S5_GROUP_CH).transpose(0, 1, 3, 2)
        return jnp.einsum("jgcn,gh->jgchn", t, eye).reshape(nj, S5_CB, S5_NS)

    def wc_part(c):
        t = c.reshape(nj, ng, S5_GROUP_CH, S5_STATE).transpose(0, 1, 3, 2)
        return jnp.einsum("jgnc,gh->jgnhc", t, eye).reshape(nj, S5_NS, S5_CB)

    wb = jnp.concatenate([wb_part(bb_re), wb_part(bb_im)], axis=2).astype(BF16)
    wc = jnp.concatenate([wc_part(c_re), -wc_part(c_im)], axis=1).astype(BF16)
    return wb, wc


def _s5_layer(x, mod, g, wb, wc, ar, ai, d_skip, h0, w_glu, r, tl):
    b, t, _ = x.shape
    h_tm = _modulate_time_major(x, mod, g, tl).reshape(t * b, D)
    h0_t = h0.reshape(b, 2, S5_GROUPS * S5_STATE).transpose(1, 0, 2)
    y, st = _s5_scan(h_tm, wb, wc, ar, ai, d_skip, h0_t, r, tl)
    x_new = _glu_residual(y.reshape(t, b * D), x, mod, w_glu, tl)
    return x_new, st.transpose(1, 0, 2).reshape(b, 2, S5_GROUPS, S5_STATE)


def _dist_tiles():
    r = jnp.arange(TQ, dtype=jnp.int32)[:, None]
    c = jnp.arange(TQ, dtype=jnp.int32)[None, :]
    d0 = r - c
    edge = 2 * TQ + r - c
    return jnp.concatenate([d0, TQ + d0, 2 * TQ + d0, jnp.where(edge <= WINDOW, edge, -1),
                            jnp.full((TQ, TQ), -1, jnp.int32)], axis=0)


def _dist_cmp(seq):
    q = jnp.arange(seq, dtype=jnp.int32)[:, None]
    n = jnp.arange(LANES, dtype=jnp.int32)[None, :]
    return jnp.where(n < seq // CMP_BLOCK, q - ((n + 1) * CMP_BLOCK - 1), -1)


_SAMPLE_TABLE_SIZES = (LANES, MOBA_BLOCK, PAGE, PAST_LEN // CMP_BLOCK, WINDOW)


def _dist_sample():
    ar = lambda n: jnp.arange(n, dtype=jnp.int32)
    misc = jnp.zeros((LANES,), jnp.int32).at[1].set(MAX_DISTANCE * 4)
    moba = MOBA_BLOCK - ar(MOBA_BLOCK)
    sel = PAGE - ar(PAGE)
    cmp_ = PAST_LEN - ((ar(PAST_LEN // CMP_BLOCK) + 1) * CMP_BLOCK - 1)
    win = WINDOW - ar(WINDOW)
    return jnp.concatenate([misc, moba, sel, cmp_, win])[None, :]


def _block_diag2(w):
    z = jnp.zeros_like(w)
    return jnp.concatenate([jnp.concatenate([w, z], axis=-1), jnp.concatenate([z, w], axis=-1)], axis=-2)


def kernel(x_prompt, x_sample, cache_moba_kv, cache_nsa_kv, state_nsa_win, state_s5, page_table, c_prompt, c_sample, rel_bias, attn_norm_g, attn_ada_w, attn_ada_b, attn_w_in, attn_qk_g, nsa_cmp_pos, nsa_cmp_w1, nsa_cmp_w2, attn_w_out, ssm_norm_g, ssm_ada_w, ssm_ada_b, s5_a_re, s5_a_im, s5_log_dt, s5_b_re, s5_b_im, s5_c_re, s5_c_im, s5_d, s5_w_glu, mlp_norm_g, mlp_ada_w, mlp_ada_b, mlp_w1, mlp_w2):
    bp, seq, _ = x_prompt.shape
    bs = x_sample.shape[0]
    assert seq % TQ == 0 and x_sample.shape[1] == 1
    n_pool = cache_moba_kv.shape[1]

    c_all = jnp.concatenate([c_prompt, c_sample], axis=0)
    split_mod = lambda m: (m[:bp, None, :], m[None, bp:, :])
    mod_attn = _adaln(c_all, attn_ada_w, attn_ada_b)
    mod_ssm = _adaln(c_all, ssm_ada_w, ssm_ada_b)
    mod_mlp = _adaln(c_all, mlp_ada_w, mlp_ada_b)

    xp = x_prompt
    xs = x_sample.reshape(1, bs, D)

    tb = _bias_table(rel_bias, _dist_tiles()).reshape(2 * MOBA_HEADS, N_BIAS_TILES, TQ, TQ)
    tc = _bias_table(rel_bias, _dist_cmp(seq))[MOBA_HEADS:]
    ts = _bias_table(rel_bias, _dist_sample())[:, 0, :]
    offs = [0]
    for size in _SAMPLE_TABLE_SIZES:
        offs.append(offs[-1] + size)
    part = lambda heads, t: ts[heads, offs[t]:offs[t + 1]]
    hm, hn = slice(0, MOBA_HEADS), slice(MOBA_HEADS, 2 * MOBA_HEADS)
    misc_m, misc_n, tsb, tsn, tcs, tws = part(hm, 0), part(hn, 0), part(hm, 1), part(hn, 2), part(hn, 3), part(hn, 4)

    w_in = jnp.pad(attn_w_in[0], ((0, 0), (0, IN_COLS_PAD - IN_COLS))).astype(BF16)
    qkg_t = jnp.pad(jnp.tile(attn_qk_g[0], (1, 2)), ((0, 2), (0, 0)))
    lr = jnp.arange(LANES)
    avg = jnp.where(lr[:, None] // HEAD_DIM == lr[None, :] // HEAD_DIM, 1.0 / HEAD_DIM, 0.0).astype(BF16)
    g_attn = attn_norm_g[0][None, :]
    w_out = attn_w_out[0].astype(BF16)
    pos = jnp.concatenate([nsa_cmp_pos[0, 0], nsa_cmp_pos[0, 0], nsa_cmp_pos[0, 1], nsa_cmp_pos[0, 1]], axis=1)
    w1bd = _block_diag2(nsa_cmp_w1[0].reshape(2, CMP_BLOCK, HEAD_DIM, CMP_HIDDEN)).astype(BF16)
    w1bd = w1bd.reshape(2, CMP_BLOCK // 2, 256, 256)
    w2bd = _block_diag2(nsa_cmp_w2[0]).astype(BF16)
    gkc = qkg_t[3:4]

    mp_attn, ms_attn = split_mod(mod_attn[0])
    mq, mkv, nq, nkv, wkv, gates, mkv_t, nkv_t = _attn_proj(xp, mp_attn, g_attn, w_in, qkg_t, avg, 512, page_major=True)
    o_moba = _moba_prompt(mq, mkv, tb[:MOBA_HEADS])
    kcmp, vcmp = _cmp_prompt(nkv, pos, w1bd, w2bd, gkc, avg)
    o_nsa = _nsa_prompt(nq, nkv, wkv, kcmp, vcmp, gates, tb[MOBA_HEADS:], tc)
    xp = _outproj(xp, mp_attn, o_moba, o_nsa, w_out, 512)
    npg = seq // PAGE
    moba_p = mkv_t.reshape(1, bp, npg, 2, MOBA_HEADS, HEAD_DIM, PAGE).transpose(0, 1, 2, 6, 3, 4, 5)
    nsa_p = nkv_t.reshape(1, bp, npg, 4, 2, HEAD_DIM, PAGE).transpose(0, 1, 2, 6, 3, 4, 5)
    win_p = wkv[:, seq - min(WINDOW, seq):].reshape(1, bp, min(WINDOW, seq), 2, 2, HEAD_DIM)
    mq_s, mkv_s, nq_s, nkv_s, wkv_s, gates_s = _attn_proj(xs, ms_attn, g_attn, w_in, qkg_t, avg, bs)
    cache_m_t = cache_moba_kv.transpose(0, 1, 3, 4, 5, 2).reshape(n_pool, 2, 512, PAGE)
    cache_n_t = cache_nsa_kv.transpose(0, 1, 3, 4, 5, 2).reshape(n_pool, 512, PAGE)
    win_t = state_nsa_win[0].transpose(0, 2, 3, 4, 1).reshape(bs, 256, WINDOW)
    lw = jnp.arange(512)
    qmat_m = jnp.where(lw[None, None, :] // HEAD_DIM == jnp.arange(MOBA_HEADS)[None, :, None], mq_s[0][:, None, :], 0.0)
    col3 = lambda a, lo, width: a[0][:, None, lo:lo + width]
    o_moba_s = _moba_sample(page_table, cache_m_t, qmat_m, col3(mkv_s, 0, 512), col3(mkv_s, 512, 512), tsb, misc_m)
    nq4 = nq_s[0].reshape(bs, NSA_HEADS, HEAD_DIM)
    kvh = jnp.arange(NSA_HEADS) // NSA_GROUP
    qmat_n = jnp.concatenate([jnp.where(kvh[None, :, None] == 0, nq4, 0.0),
                              jnp.where(kvh[None, :, None] == 1, nq4, 0.0)], axis=2)
    o_nsa_s = _nsa_sample(page_table, cache_n_t, qmat_n,
                          col3(nkv_s, 256, LANES), col3(nkv_s, 384, LANES), col3(wkv_s, 0, LANES),
                          col3(wkv_s, 128, LANES), win_t, gates_s.reshape(bs, 1, LANES),
                          tsn, misc_n, tcs, tws, pos, w1bd, w2bd, gkc, avg)
    xs = _outproj(xs, ms_attn, o_moba_s.reshape(1, bs, 512), o_nsa_s.reshape(1, bs, 512), w_out, bs)
    moba_s = mkv_s.reshape(1, bs, 1, 2, MOBA_HEADS, HEAD_DIM)
    nsa_s = nkv_s.reshape(1, bs, 1, 4, 2, HEAD_DIM)
    win_s = jnp.concatenate([state_nsa_win[0][:, 1:], wkv_s[0].reshape(bs, 1, 2, 2, HEAD_DIM)], axis=1)[None]

    w1_0, w2_0 = mlp_w1[0].astype(BF16), mlp_w2[0].astype(BF16)
    mp_mlp, ms_mlp = split_mod(mod_mlp[0])
    g_mlp0 = mlp_norm_g[0][None, :]
    xp = _mlp(xp, mp_mlp, g_mlp0, w1_0, w2_0, 512, 1024)
    xs = _mlp(xs, ms_mlp, g_mlp0, w1_0, w2_0, bs, 1024)

    ab_re, ab_im, bb_re, bb_im = _s5_discretize(s5_a_re[0], s5_a_im[0], s5_log_dt[0], s5_b_re[0], s5_b_im[0])
    wb, wc = _s5_block_weights(bb_re, bb_im, s5_c_re[0], s5_c_im[0])
    nj = D // S5_CB
    ar = ab_re.reshape(nj, 1, S5_NS)
    ai = ab_im.reshape(nj, 1, S5_NS)
    g_ssm = ssm_norm_g[0][None, :]
    d_skip = s5_d[0][None, :]
    w_glu = s5_w_glu[0].astype(BF16)
    mp_ssm, ms_ssm = split_mod(mod_ssm[0])
    xp, st_p = _s5_layer(xp, mp_ssm, g_ssm, wb, wc, ar, ai, d_skip,
                         jnp.zeros((bp, 2, S5_GROUPS, S5_STATE), F32), w_glu, bp, 128)
    h_s = _modulate_time_major(xs, ms_ssm, g_ssm, bs)
    y_s, st_s = _s5_scan(h_s, wb, wc, ar, ai, d_skip,
                         state_s5[0].reshape(bs, 2, S5_GROUPS * S5_STATE).transpose(1, 0, 2), bs, 1)
    xs = _glu_residual(y_s, xs, ms_ssm, w_glu, bs)
    st_s = st_s.transpose(1, 0, 2).reshape(bs, 2, S5_GROUPS, S5_STATE)

    w1_1, w2_1 = mlp_w1[1].astype(BF16), mlp_w2[1].astype(BF16)
    mp_mlp, ms_mlp = split_mod(mod_mlp[1])
    g_mlp1 = mlp_norm_g[1][None, :]
    xp = _mlp(xp, mp_mlp, g_mlp1, w1_1, w2_1, 512, 1024)
    xs = _mlp(xs, ms_mlp, g_mlp1, w1_1, w2_1, bs, 1024)

    return (xp, xs.reshape(bs, 1, D), moba_p, moba_s, nsa_p, nsa_s, win_p, win_s, st_p[None], st_s[None])
```

```python
import functools
import math

import jax
import jax.numpy as jnp
from jax import lax
from jax.experimental import pallas as pl
from jax.experimental.pallas import tpu as pltpu

F32 = jnp.float32
BF16 = jnp.bfloat16
HIGHEST = lax.Precision.HIGHEST

D = 1024
HEAD_DIM = 64
MOBA_HEADS = 8
NSA_HEADS = 8
NSA_GROUP = 4
MOBA_BLOCK = 256
MOBA_TOPK = 3
CMP_BLOCK = 32
CMP_HIDDEN = 128
SEL_BLOCK = 64
SEL_TOPK = 16
WINDOW = 512
NUM_BUCKETS = 32
MAX_DISTANCE = 128
PAGE = 128
PAST_LEN = 8192
D_FF = 4 * D
S5_GROUPS = 64
S5_STATE = 64
S5_GROUP_CH = 16
IN_COLS = 3 * 512 + 512 + 6 * 128 + 3 * NSA_HEADS
IN_COLS_PAD = 23 * 128
EPS = 1e-6
SCALE = HEAD_DIM ** -0.5
LOG2E = math.log2(math.e)
LANES = 128
TQ = 256
N_BIAS_TILES = 5
NEG = -1e30
M_INIT = -1e15
VMEM_LIMIT = 56 * 1024 * 1024

_NT = (((1,), (1,)), ((), ()))


def _cparams(sem):
    return pltpu.CompilerParams(dimension_semantics=sem, vmem_limit_bytes=VMEM_LIMIT)


def _dot(a, b, **kw):
    return jnp.dot(a, b, preferred_element_type=F32, **kw)


def _dot_nt(a, b, **kw):
    return lax.dot_general(a, b, _NT, preferred_element_type=F32, **kw)


def _modulate(x, g, shift, scale):
    ms = jnp.mean(x * x, axis=-1, keepdims=True)
    return x * lax.rsqrt(ms + EPS) * g * (1.0 + scale) + shift


def _group_mean_sq(z, avg):
    sq = z * z
    hi = sq.astype(BF16)
    lo = (sq - hi.astype(F32)).astype(BF16)
    return _dot(hi, avg) + _dot(lo, avg)


def _col(x, lane, idx):
    return jnp.sum(jnp.where(lane == idx, x, 0.0), axis=1, keepdims=True)


def _adaln_kernel(c_ref, w_ref, b_ref, o_ref):
    c = c_ref[...]
    s = c * jax.nn.sigmoid(c)
    o_ref[0] = _dot(s, w_ref[0], precision=HIGHEST) + b_ref[0]


def _adaln(c_all, w, b):
    nl, n = w.shape[0], c_all.shape[0]
    return pl.pallas_call(
        _adaln_kernel,
        grid=(nl, 3),
        in_specs=[pl.BlockSpec((n, D), lambda l, j: (0, 0)),
                  pl.BlockSpec((1, D, D), lambda l, j: (l, 0, j)),
                  pl.BlockSpec((1, 1, D), lambda l, j: (l, 0, j))],
        out_specs=pl.BlockSpec((1, n, D), lambda l, j: (l, 0, j)),
        out_shape=jax.ShapeDtypeStruct((nl, n, 3 * D), F32),
        compiler_params=_cparams(("arbitrary", "arbitrary")),
        name="adaln",
    )(c_all, w, b.reshape(nl, 1, 3 * D))


def _bias_kernel(rb_ref, d_ref, o_ref, *, scale):
    h = pl.program_id(0)
    dist = d_ref[...]
    n = jnp.maximum(dist, 0)
    max_exact = NUM_BUCKETS // 2
    nf = jnp.maximum(n, 1).astype(F32)
    large = max_exact + (jnp.log(nf / max_exact) / math.log(MAX_DISTANCE / max_exact)
                         * (NUM_BUCKETS - max_exact)).astype(jnp.int32)
    large = jnp.minimum(large, NUM_BUCKETS - 1)
    bucket = jnp.where(n < max_exact, n, large)
    acc = jnp.zeros(dist.shape, F32)
    for k in range(NUM_BUCKETS):
        acc = jnp.where(bucket == k, rb_ref[k, h], acc)
    o_ref[0] = jnp.where(dist < 0, NEG, acc * scale)


def _bias_table(rel_bias, dist, scale=1.0):
    r, c = dist.shape
    nh = rel_bias.shape[1]
    return pl.pallas_call(
        functools.partial(_bias_kernel, scale=scale),
        grid=(nh,),
        in_specs=[pl.BlockSpec(memory_space=pltpu.SMEM),
                  pl.BlockSpec((r, c), lambda h: (0, 0))],
        out_specs=pl.BlockSpec((1, r, c), lambda h: (h, 0, 0)),
        out_shape=jax.ShapeDtypeStruct((nh, r, c), F32),
        compiler_params=_cparams(("arbitrary",)),
        name="bias_table",
    )(rel_bias, dist)


def _proj_kernel(x_ref, mod_ref, g_ref, w_ref, qkg_ref, avg_ref,
                 mq_ref, mkv_ref, nq_ref, nkv_ref, wkv_ref, gt_ref, *page_major_refs, q_scale):
    x = x_ref[0]
    mod = mod_ref[0]
    h = _modulate(x, g_ref[...], mod[:, 0:D], mod[:, D:2 * D])
    z = _dot(h.astype(BF16), w_ref[...])
    avg = avg_ref[...]

    def normed(lo, gi):
        zs = z[:, lo:lo + LANES]
        return zs * lax.rsqrt(_group_mean_sq(zs, avg) + EPS) * qkg_ref[gi:gi + 1, :]

    for t in range(4):
        mq_ref[0, :, t * LANES:(t + 1) * LANES] = normed(t * LANES, 0) * q_scale
        mkv_ref[0, :, t * LANES:(t + 1) * LANES] = normed(512 + t * LANES, 1)
        nq_ref[0, :, t * LANES:(t + 1) * LANES] = normed(1536 + t * LANES, 2) * q_scale
    mkv_ref[0, :, 512:1024] = z[:, 1024:1536]
    nkv_ref[0, :, 0:256] = z[:, 2048:2304]
    nkv_ref[0, :, 256:384] = normed(2304, 4)
    nkv_ref[0, :, 384:512] = z[:, 2432:2560]
    wkv_ref[0, :, 0:128] = normed(2560, 5)
    wkv_ref[0, :, 128:256] = z[:, 2688:2816]
    gt_ref[0] = jax.nn.sigmoid(z[:, 2816:2944])
    if page_major_refs:
        mkv_t_ref, nkv_t_ref = page_major_refs
        for p in range(x.shape[0] // PAGE):
            mkv_t_ref[0, p] = mkv_ref[0, p * PAGE:(p + 1) * PAGE, :].T
            nkv_t_ref[0, p] = nkv_ref[0, p * PAGE:(p + 1) * PAGE, :].T


def _attn_proj(x, mod, g, w_pad, qkg_t, avg, tm, q_scale, page_major=False):
    b, t, _ = x.shape
    tmod = mod.shape[1]
    row = lambda width: pl.BlockSpec((1, tm, width), lambda i, j: (i, j, 0))
    shp = lambda width: jax.ShapeDtypeStruct((b, t, width), F32)
    out_specs = [row(512), row(1024), row(512), row(512), row(256), row(128)]
    out_shape = [shp(512), shp(1024), shp(512), shp(512), shp(256), shp(128)]
    if page_major:
        for width in (1024, 512):
            out_specs.append(pl.BlockSpec((1, tm // PAGE, width, PAGE), lambda i, j: (i, j, 0, 0)))
            out_shape.append(jax.ShapeDtypeStruct((b, t // PAGE, width, PAGE), F32))
    return pl.pallas_call(
        functools.partial(_proj_kernel, q_scale=q_scale),
        grid=(b, t // tm),
        in_specs=[row(D),
                  pl.BlockSpec((1, tmod, 3 * D), lambda i, j: (i, 0, 0)),
                  pl.BlockSpec((1, D), lambda i, j: (0, 0)),
                  pl.BlockSpec((D, IN_COLS_PAD), lambda i, j: (0, 0)),
                  pl.BlockSpec((8, LANES), lambda i, j: (0, 0)),
                  pl.BlockSpec((LANES, LANES), lambda i, j: (0, 0))],
        out_specs=out_specs,
        out_shape=out_shape,
        compiler_params=_cparams(("arbitrary", "arbitrary")),
        name="attn_proj",
    )(x, mod, g, w_pad, qkg_t, avg)


def _rank_rows(score, rowi, ncand, step=1):
    rank = jnp.zeros(score.shape, F32)
    for m in range(0, ncand * step, step):
        rm = score[m:m + 1, :]
        rank = rank + jnp.where(rm > score, 1.0, 0.0) + jnp.where((rm == score) & (m < rowi), 1.0, 0.0)
    return rank


def _columns_from_rows(x_t):
    pad = jnp.zeros((LANES - x_t.shape[0], x_t.shape[1]), F32)
    return jnp.concatenate([x_t, pad], axis=0).T


def _softmax_pv(pieces, v_all):
    m = pieces[0]
    for s in pieces[1:]:
        m = jnp.maximum(m, s)
    m = jnp.maximum(jnp.max(m, axis=1, keepdims=True), M_INIT)
    ps = [jnp.exp2(s - m) for s in pieces]
    tot = ps[0]
    for p in ps[1:]:
        tot = tot + p
    l = jnp.sum(tot, axis=1, keepdims=True)
    p_all = jnp.concatenate([p.astype(BF16) for p in ps], axis=1) if len(ps) > 1 else ps[0].astype(BF16)
    return _dot(p_all, v_all) / jnp.maximum(l, 1e-30)


def _moba_prompt_kernel(q_ref, k_ref, v_ref, t_ref, o_ref, km_scr, kb_scr, vb_scr):
    s_len = q_ref.shape[1]
    nblk = s_len // MOBA_BLOCK
    nq = s_len // TQ
    lane = lax.broadcasted_iota(jnp.int32, (TQ, LANES), 1)
    rowb = lax.broadcasted_iota(jnp.int32, (nblk, TQ), 0)
    km_scr[...] = jnp.zeros(km_scr.shape, F32)
    for n in range(nblk):
        km_scr[n:n + 1, :] = jnp.mean(k_ref[0, n * MOBA_BLOCK:(n + 1) * MOBA_BLOCK, :], axis=0, keepdims=True)
    kmean = km_scr[...]
    kb_scr[...] = k_ref[0].astype(BF16)
    vb_scr[...] = v_ref[0].astype(BF16)

    def qtile(i, _):
        r0 = pl.multiple_of(i * TQ, TQ)
        q2 = q_ref[0, pl.ds(r0, TQ), :]
        qbs, cbs = [], []
        for e in range(2):
            qe = jnp.where(lane // HEAD_DIM == e, q2, 0.0)
            gate_t = _dot_nt(kmean, qe, precision=HIGHEST)[0:nblk]
            gm = jnp.where(rowb < i, gate_t, -jnp.inf)
            sel = ((_rank_rows(gm, rowb, nblk) < MOBA_TOPK) & (rowb < i)) | (rowb == i)
            cbs.append(_columns_from_rows(jnp.where(sel, 0.0, NEG)))
            qbs.append(qe.astype(BF16))

        for c in range(1, nq // 2 + 1):
            @pl.when(i // 2 + 1 == c)
            def _():
                kall = kb_scr[0:2 * c * TQ, :]
                vall = vb_scr[0:2 * c * TQ, :]
                outs = []
                for e in range(2):
                    s = _dot_nt(qbs[e], kall)
                    far_bias = t_ref[e, 2, 0:1, 0:1]
                    pieces = []
                    for n in range(2 * c):
                        seg = s[:, n * TQ:(n + 1) * TQ]
                        if n >= 2 * c - 3:
                            pieces.append(seg + t_ref[e, jnp.clip(i - n, 0, 2)] + cbs[e][:, n:n + 1])
                        else:
                            pieces.append(seg + (cbs[e][:, n:n + 1] + far_bias))
                    outs.append(_softmax_pv(pieces, vall))
                o_ref[0, pl.ds(r0, TQ), :] = jnp.where(lane < HEAD_DIM, outs[0], outs[1])
        return 0

    lax.fori_loop(0, nq, qtile, 0)


def _moba_prompt(mq, mkv, tb):
    b, s, _ = mq.shape
    npair = MOBA_HEADS // 2
    return pl.pallas_call(
        _moba_prompt_kernel,
        grid=(b, npair),
        in_specs=[pl.BlockSpec((1, s, LANES), lambda i, p: (i, 0, p)),
                  pl.BlockSpec((1, s, LANES), lambda i, p: (i, 0, p)),
                  pl.BlockSpec((1, s, LANES), lambda i, p: (i, 0, npair + p)),
                  pl.BlockSpec((2, N_BIAS_TILES, TQ, TQ), lambda i, p: (p, 0, 0, 0))],
        out_specs=pl.BlockSpec((1, s, LANES), lambda i, p: (i, 0, p)),
        out_shape=jax.ShapeDtypeStruct((b, s, 512), F32),
        scratch_shapes=[pltpu.VMEM((LANES, LANES), F32), pltpu.VMEM((s, LANES), BF16), pltpu.VMEM((s, LANES), BF16)],
        compiler_params=_cparams(("arbitrary", "arbitrary")),
        name="moba_prompt",
    )(mq, mkv, mkv, tb)


def _compress_tokens(load_k, load_v, pos_ref, w1_ref, w2_ref):
    hk = hv = None
    for r in range(0, CMP_BLOCK, 2):
        xk = [(load_k(r + t) + pos_ref[r + t:r + t + 1, 0:LANES]).astype(BF16) for t in range(2)]
        xv = [(load_v(r + t) + pos_ref[r + t:r + t + 1, LANES:2 * LANES]).astype(BF16) for t in range(2)]
        dk = _dot(jnp.concatenate(xk, axis=1), w1_ref[0, r // 2])
        dv = _dot(jnp.concatenate(xv, axis=1), w1_ref[1, r // 2])
        hk = dk if hk is None else hk + dk
        hv = dv if hv is None else hv + dv
    ck = _dot(jax.nn.gelu(hk).astype(BF16), w2_ref[0])
    cv = _dot(jax.nn.gelu(hv).astype(BF16), w2_ref[1])
    return ck, cv


def _cmp_prompt_kernel(xk_ref, xv_ref, pos_ref, w1_ref, w2_ref, gkc_ref, avg_ref, kc_ref, vc_ref):
    nblk = xk_ref.shape[1] // CMP_BLOCK
    ck, cv = _compress_tokens(lambda r: xk_ref[0, pl.ds(r, nblk, stride=CMP_BLOCK), :],
                              lambda r: xv_ref[0, pl.ds(r, nblk, stride=CMP_BLOCK), :], pos_ref, w1_ref, w2_ref)
    ck = ck * lax.rsqrt(_group_mean_sq(ck, avg_ref[...]) + EPS) * gkc_ref[...]
    kc_ref[0] = jnp.zeros((LANES, LANES), F32)
    vc_ref[0] = jnp.zeros((LANES, LANES), F32)
    kc_ref[0, 0:nblk, :] = ck
    vc_ref[0, 0:nblk, :] = cv


def _cmp_prompt(nkv, pos, w1bd, w2bd, gkc, avg):
    b, s, _ = nkv.shape
    const = lambda shape: pl.BlockSpec(shape, lambda i: (0,) * len(shape))
    return pl.pallas_call(
        _cmp_prompt_kernel,
        grid=(b,),
        in_specs=[pl.BlockSpec((1, s, LANES), lambda i: (i, 0, 0)), pl.BlockSpec((1, s, LANES), lambda i: (i, 0, 1)),
                  const((CMP_BLOCK, 256)), const((2, CMP_BLOCK // 2, 256, 256)), const((2, 256, LANES)),
                  const((1, LANES)), const((LANES, LANES))],
        out_specs=[pl.BlockSpec((1, LANES, LANES), lambda i: (i, 0, 0))] * 2,
        out_shape=[jax.ShapeDtypeStruct((b, LANES, LANES), F32)] * 2,
        compiler_params=_cparams(("arbitrary",)),
        name="nsa_compress_prompt",
    )(nkv, nkv, pos, w1bd, w2bd, gkc, avg)


def _nsa_prompt_kernel(q_ref, ks_ref, vs_ref, kw_ref, vw_ref, kc_ref, vc_ref, g_ref, t_ref, tc_ref,
                       o_ref, ksb_scr, vsb_scr, kwb_scr, vwb_scr):
    s_len = q_ref.shape[1]
    k = pl.program_id(1)
    lane = lax.broadcasted_iota(jnp.int32, (TQ, LANES), 1)
    kvmask = (lane // HEAD_DIM) == k
    kc = kc_ref[0].astype(BF16)
    vc = vc_ref[0].astype(BF16)
    nsel = s_len // SEL_BLOCK
    nq = s_len // TQ
    ncmp = s_len // CMP_BLOCK
    rowb = lax.broadcasted_iota(jnp.int32, (nsel, TQ), 0)
    qpos = lax.broadcasted_iota(jnp.int32, (nsel, TQ), 1)
    pair_r = lax.broadcasted_iota(jnp.int32, (nsel, LANES), 0)
    pair_c = lax.broadcasted_iota(jnp.int32, (nsel, LANES), 1)
    pair_t = jnp.where((pair_c // (SEL_BLOCK // CMP_BLOCK) == pair_r) & (pair_c < ncmp), 1.0, 0.0)
    ksb_scr[...] = ks_ref[0].astype(BF16)
    vsb_scr[...] = vs_ref[0].astype(BF16)
    kwb_scr[...] = kw_ref[0].astype(BF16)
    vwb_scr[...] = vw_ref[0].astype(BF16)

    def qtile(i, _):
        r0 = pl.multiple_of(i * TQ, TQ)
        qs = []
        for h in range(NSA_GROUP):
            q2 = q_ref[0, pl.ds(r0, TQ), (h // 2) * LANES:(h // 2 + 1) * LANES]
            qa = jnp.where(k == (h % 2), q2, pltpu.roll(q2, HEAD_DIM, 1))
            qs.append(jnp.where(kvmask, qa, 0.0).astype(BF16))

        imp = jnp.zeros((TQ, LANES), F32)
        o_cmp = []
        for h in range(NSA_GROUP):
            s = _dot_nt(qs[h], kc) + tc_ref[h, pl.ds(r0, TQ), :]
            m = jnp.maximum(jnp.max(s, axis=1, keepdims=True), M_INIT)
            p = jnp.exp2(s - m)
            p = p / jnp.maximum(jnp.sum(p, axis=1, keepdims=True), 1e-30)
            imp = imp + p
            o_cmp.append(_dot(p.astype(BF16), vc))

        imp_t = _dot_nt(pair_t, imp, precision=HIGHEST)
        own = (r0 + qpos) // SEL_BLOCK
        sc = jnp.where(rowb < own, imp_t, -jnp.inf)
        sel = ((_rank_rows(sc, rowb, nsel) < SEL_TOPK) & (rowb < own)) | (rowb == own)
        selb = _columns_from_rows(jnp.where(sel, 1.0, 0.0)).astype(BF16)

        wk, wv, wt = [], [], []
        for j, tidx in enumerate((3, 1, 0)):
            n = i - 2 + j
            c0 = pl.multiple_of(jnp.maximum(n, 0) * TQ, TQ)
            wk.append(kwb_scr[pl.ds(c0, TQ), :])
            wv.append(vwb_scr[pl.ds(c0, TQ), :])
            wt.append(jnp.where(n < 0, N_BIAS_TILES - 1, tidx))
        kw_all = jnp.concatenate(wk, axis=0)
        vw_all = jnp.concatenate(wv, axis=0)
        g = g_ref[0, pl.ds(r0, TQ), :]
        o_win = []
        for h in range(NSA_GROUP):
            s = _dot_nt(qs[h], kw_all)
            o_win.append(_softmax_pv([s[:, j * TQ:(j + 1) * TQ] + t_ref[h, wt[j]] for j in range(3)], vw_all))

        for c in range(1, nq // 2 + 1):
            @pl.when(i // 2 + 1 == c)
            def _():
                nkeys = 2 * c * TQ
                e_r = lax.broadcasted_iota(jnp.int32, (LANES, nkeys), 0)
                e_c = lax.broadcasted_iota(jnp.int32, (LANES, nkeys), 1)
                expand = jnp.where(e_r == e_c // SEL_BLOCK, 1.0, 0.0).astype(BF16)
                addm = (_dot(selb, expand) - 1.0) * (-NEG)
                kall = ksb_scr[0:nkeys, :]
                vall = vsb_scr[0:nkeys, :]
                res = []
                for h in range(NSA_GROUP):
                    s = _dot_nt(qs[h], kall) + addm
                    pieces = [s[:, n * TQ:(n + 1) * TQ] + t_ref[h, jnp.clip(i - n, 0, 2)] for n in range(2 * c)]
                    o_sel = _softmax_pv(pieces, vall)
                    hg = (k * NSA_GROUP + h) * 3
                    o = (_col(g, lane, hg) * o_cmp[h] + _col(g, lane, hg + 1) * o_sel
                         + _col(g, lane, hg + 2) * o_win[h])
                    res.append(jnp.where(k == (h % 2), o, pltpu.roll(o, HEAD_DIM, 1)))
                for t in range(2):
                    o_ref[0, pl.ds(r0, TQ), t * LANES:(t + 1) * LANES] = jnp.where(
                        lane < HEAD_DIM, res[2 * t], res[2 * t + 1])
        return 0

    lax.fori_loop(0, nq, qtile, 0)


def _nsa_prompt(nq, nkv, wkv, kcmp, vcmp, gates, tb, tc):
    b, s, _ = nq.shape
    col = lambda arr_cols, cb: pl.BlockSpec((1, s, LANES), lambda i, k: (i, 0, cb))
    return pl.pallas_call(
        _nsa_prompt_kernel,
        grid=(b, 2),
        in_specs=[pl.BlockSpec((1, s, 256), lambda i, k: (i, 0, k)),
                  col(512, 2), col(512, 3), col(256, 0), col(256, 1),
                  pl.BlockSpec((1, LANES, LANES), lambda i, k: (i, 0, 0)),
                  pl.BlockSpec((1, LANES, LANES), lambda i, k: (i, 0, 0)),
                  pl.BlockSpec((1, s, LANES), lambda i, k: (i, 0, 0)),
                  pl.BlockSpec((NSA_GROUP, N_BIAS_TILES, TQ, TQ), lambda i, k: (k, 0, 0, 0)),
                  pl.BlockSpec((NSA_GROUP, s, LANES), lambda i, k: (k, 0, 0))],
        out_specs=pl.BlockSpec((1, s, 256), lambda i, k: (i, 0, k)),
        out_shape=jax.ShapeDtypeStruct((b, s, 512), F32),
        scratch_shapes=[pltpu.VMEM((s, LANES), BF16)] * 4,
        compiler_params=_cparams(("arbitrary", "arbitrary")),
        name="nsa_prompt",
    )(nq, nkv, nkv, wkv, wkv, kcmp, vcmp, gates, tb, tc)


PAGES_PER_STEP = 8
CMP_PITCH = 40


def _rank_lt(score, lane, ncand, topk):
    rank = jnp.zeros(score.shape, F32)
    for m in range(ncand):
        col = score[:, m:m + 1]
        beats = (col > score) | ((col == score) & (m < lane))
        rank = rank + jnp.where(beats, 1.0, 0.0)
    return rank < topk


def _merge_blocks(sel, m_all, l_all, acc_scr, nblk, s_self, v_self):
    mx = jnp.maximum(jnp.max(jnp.where(sel, m_all, NEG), axis=1, keepdims=True), s_self)
    w = jnp.exp(jnp.where(sel, m_all - mx, NEG))
    w_self = jnp.exp(s_self - mx)
    den = jnp.sum(w * l_all, axis=1, keepdims=True) + w_self
    num = w_self * v_self
    for j in range(nblk):
        num = num + w[:, j:j + 1] * acc_scr[j]
    return num / den


def _moba_sample_kernel(pt_ref, *refs):
    pages = refs[:PAGES_PER_STEP]
    qm_ref, kn_ref, vn_ref, tsb_ref, misc_ref, o_ref, g_scr, m_scr, l_scr, acc_scr = refs[PAGES_PER_STEP:]
    s = pl.program_id(1)
    nstep = pl.num_programs(1)
    nblk = PAST_LEN // MOBA_BLOCK
    width = MOBA_HEADS * HEAD_DIM
    qm = qm_ref[0]
    qb = qm.astype(BF16)
    lane = lax.broadcasted_iota(jnp.int32, (MOBA_HEADS, LANES), 1)

    @pl.when(s == 0)
    def _():
        g_scr[...] = jnp.zeros(g_scr.shape, F32)
        m_scr[...] = jnp.zeros(m_scr.shape, F32)
        l_scr[...] = jnp.zeros(l_scr.shape, F32)

    npb = PAGES_PER_STEP // 2
    blk0 = s * npb
    kt_all = jnp.concatenate([pages[t][0, 0].astype(BF16) for t in range(PAGES_PER_STEP)], axis=1)
    vt_all = jnp.concatenate([pages[t][0, 1].astype(BF16) for t in range(PAGES_PER_STEP)], axis=1)
    raw = _dot(qb, kt_all)
    far = jnp.broadcast_to(misc_ref[:, 1:2], (MOBA_HEADS, MOBA_BLOCK))
    g_new, m_new, l_new = g_scr[...], m_scr[...], l_scr[...]
    p_rows = []
    for j in range(npb):
        seg = raw[:, j * MOBA_BLOCK:(j + 1) * MOBA_BLOCK]
        gate = jnp.sum(seg, axis=1, keepdims=True)
        sc = seg + (jnp.where(s == nstep - 1, tsb_ref[...], far) if j == npb - 1 else far)
        mj = jnp.max(sc, axis=1, keepdims=True)
        p = jnp.exp(sc - mj)
        g_new = jnp.where(lane == blk0 + j, gate, g_new)
        m_new = jnp.where(lane == blk0 + j, mj, m_new)
        l_new = jnp.where(lane == blk0 + j, jnp.sum(p, axis=1, keepdims=True), l_new)
        zeros = jnp.zeros((MOBA_HEADS, MOBA_BLOCK), F32)
        p_rows.append(jnp.concatenate([p if t == j else zeros for t in range(npb)], axis=1))
    g_scr[...] = g_new
    m_scr[...] = m_new
    l_scr[...] = l_new
    acc = _dot_nt(jnp.concatenate(p_rows, axis=0).astype(BF16), vt_all)
    for j in range(npb):
        acc_scr[blk0 + j] = acc[j * MOBA_HEADS:(j + 1) * MOBA_HEADS]

    @pl.when(s == nstep - 1)
    def _():
        gm = jnp.where(lane < nblk, g_scr[...], -jnp.inf)
        sel = _rank_lt(gm, lane, nblk, MOBA_TOPK) & (lane < nblk)
        s_self = jnp.sum(qm * kn_ref[0], axis=1, keepdims=True) + misc_ref[:, 0:1]
        o = _merge_blocks(sel, m_scr[...], l_scr[...], acc_scr, nblk, s_self, vn_ref[0])
        hrow = lax.broadcasted_iota(jnp.int32, (MOBA_HEADS, width), 0)
        hlane = lax.broadcasted_iota(jnp.int32, (MOBA_HEADS, width), 1)
        o_ref[0] = jnp.sum(jnp.where(hlane // HEAD_DIM == hrow, o, 0.0), axis=0, keepdims=True)


def _moba_sample(page_table, cache_t, qmat, knew, vnew, tsb, misc):
    nb, npages = page_table.shape
    nstep = npages // PAGES_PER_STEP
    width = MOBA_HEADS * HEAD_DIM
    nblk = PAST_LEN // MOBA_BLOCK

    def page_spec(j):
        return pl.BlockSpec((1, 2, width, PAGE), lambda b, s, pt: (pt[b, s * PAGES_PER_STEP + j], 0, 0, 0))

    per_b = lambda shape: pl.BlockSpec((1,) + shape, lambda b, s, pt: (b, 0, 0))
    const = lambda shape: pl.BlockSpec(shape, lambda b, s, pt: (0,) * len(shape))
    grid_spec = pltpu.PrefetchScalarGridSpec(
        num_scalar_prefetch=1,
        grid=(nb, nstep),
        in_specs=[page_spec(j) for j in range(PAGES_PER_STEP)]
        + [per_b((MOBA_HEADS, width)), per_b((1, width)), per_b((1, width)),
           const((MOBA_HEADS, MOBA_BLOCK)), const((MOBA_HEADS, LANES))],
        out_specs=per_b((1, width)),
        scratch_shapes=[pltpu.VMEM((MOBA_HEADS, LANES), F32)] * 3 + [pltpu.VMEM((nblk, MOBA_HEADS, width), F32)],
    )
    return pl.pallas_call(
        _moba_sample_kernel,
        grid_spec=grid_spec,
        out_shape=jax.ShapeDtypeStruct((nb, 1, width), F32),
        compiler_params=_cparams(("arbitrary", "arbitrary")),
        name="moba_sample",
    )(page_table, *([cache_t] * PAGES_PER_STEP), qmat, knew, vnew, tsb, misc)


def _nsa_sample_kernel(pt_ref, *refs):
    pages = refs[:PAGES_PER_STEP]
    (qm_ref, ksn_ref, vsn_ref, kwn_ref, vwn_ref, win_ref, g_ref, tsn_ref, misc_ref, tcs_ref, tws_ref,
     pos_ref, w1_ref, w2_ref, gkc_ref, avg_ref, o_ref, xk_scr, xv_scr, m_scr, l_scr, acc_scr) = refs[PAGES_PER_STEP:]
    s = pl.program_id(1)
    nstep = pl.num_programs(1)
    nsel = PAST_LEN // SEL_BLOCK
    ncmp = PAST_LEN // CMP_BLOCK
    npage = PAST_LEN // PAGE
    qm = qm_ref[0]
    qb = qm.astype(BF16)
    lane = lax.broadcasted_iota(jnp.int32, (NSA_HEADS, LANES), 1)
    row = lax.broadcasted_iota(jnp.int32, (NSA_HEADS, LANES), 0)
    lo = lane < HEAD_DIM

    @pl.when(s == 0)
    def _():
        m_scr[...] = jnp.zeros(m_scr.shape, F32)
        l_scr[...] = jnp.zeros(l_scr.shape, F32)

    for j in range(PAGES_PER_STEP):
        pg = s * PAGES_PER_STEP + j
        kc = pages[j][0, 0:LANES, :].T
        vc = pages[j][0, LANES:2 * LANES, :].T
        for b4 in range(PAGE // CMP_BLOCK):
            r0 = pl.multiple_of((pg * (PAGE // CMP_BLOCK) + b4) * CMP_PITCH, 8)
            xk_scr[pl.ds(r0, CMP_BLOCK), :] = kc[b4 * CMP_BLOCK:(b4 + 1) * CMP_BLOCK, :]
            xv_scr[pl.ds(r0, CMP_BLOCK), :] = vc[b4 * CMP_BLOCK:(b4 + 1) * CMP_BLOCK, :]

    ks_all = jnp.concatenate([pages[t][0, 256:384, :].astype(BF16) for t in range(PAGES_PER_STEP)], axis=1)
    vs_all = jnp.concatenate([pages[t][0, 384:512, :].astype(BF16) for t in range(PAGES_PER_STEP)], axis=1)
    raw = _dot(qb, ks_all)
    far = jnp.broadcast_to(misc_ref[:, 1:2], (NSA_HEADS, PAGE))
    m_new, l_new = m_scr[...], l_scr[...]
    b0 = 2 * s * PAGES_PER_STEP
    zeros = jnp.zeros((NSA_HEADS, PAGE), F32)
    p_rows = []
    for j in range(PAGES_PER_STEP):
        sc = raw[:, j * PAGE:(j + 1) * PAGE]
        sc = sc + (jnp.where(s == nstep - 1, tsn_ref[...], far) if j == PAGES_PER_STEP - 1 else far)
        m0 = jnp.max(jnp.where(lo, sc, NEG), axis=1, keepdims=True)
        m1 = jnp.max(jnp.where(lo, NEG, sc), axis=1, keepdims=True)
        p = jnp.exp(sc - jnp.where(lo, m0, m1))
        p0, p1 = jnp.where(lo, p, 0.0), jnp.where(lo, 0.0, p)
        l0 = jnp.sum(p0, axis=1, keepdims=True)
        l1 = jnp.sum(p1, axis=1, keepdims=True)
        bj = b0 + 2 * j
        m_new = jnp.where(lane == bj, m0, jnp.where(lane == bj + 1, m1, m_new))
        l_new = jnp.where(lane == bj, l0, jnp.where(lane == bj + 1, l1, l_new))
        for ph in (p0, p1):
            p_rows.append(jnp.concatenate([ph if t == j else zeros for t in range(PAGES_PER_STEP)], axis=1))
    m_scr[...] = m_new
    l_scr[...] = l_new
    acc = _dot_nt(jnp.concatenate(p_rows, axis=0).astype(BF16), vs_all)
    for b in range(2 * PAGES_PER_STEP):
        acc_scr[b0 + b] = acc[b * NSA_HEADS:(b + 1) * NSA_HEADS]

    @pl.when(s == nstep - 1)
    def _():
        ck, cv = _compress_tokens(lambda r: xk_scr[pl.ds(r, ncmp, stride=CMP_PITCH), :],
                                  lambda r: xv_scr[pl.ds(r, ncmp, stride=CMP_PITCH), :], pos_ref, w1_ref, w2_ref)
        ck = ck * lax.rsqrt(_group_mean_sq(ck, avg_ref[...]) + EPS) * gkc_ref[...]
        sc = _dot_nt(qb, ck.astype(BF16)) + tcs_ref[...]
        m = jnp.maximum(jnp.max(sc, axis=1, keepdims=True), M_INIT)
        pc = jnp.exp(sc - m)
        pc = pc / jnp.maximum(jnp.sum(pc, axis=1, keepdims=True), 1e-30)
        o_cmp = _dot(pc.astype(BF16), cv.astype(BF16))
        g0 = pc[0:1] + pc[1:2] + pc[2:3] + pc[3:4]
        g1 = pc[4:5] + pc[5:6] + pc[6:7] + pc[7:8]
        rowc = lax.broadcasted_iota(jnp.int32, (NSA_HEADS, ncmp), 0)
        imp = jnp.where(rowc < NSA_GROUP, g0, g1)
        pr = lax.broadcasted_iota(jnp.int32, (ncmp, LANES), 0)
        pc_ = lax.broadcasted_iota(jnp.int32, (ncmp, LANES), 1)
        pair = jnp.where(pr // (SEL_BLOCK // CMP_BLOCK) == pc_, 1.0, 0.0)
        impb = _dot(imp, pair, precision=HIGHEST)
        own = PAST_LEN // SEL_BLOCK
        sel = _rank_lt(jnp.where(lane < own, impb, -jnp.inf), lane, nsel, SEL_TOPK) & (lane < own)
        s_self = jnp.sum(qm * ksn_ref[0], axis=1, keepdims=True) + misc_ref[:, 0:1]
        o_sel = _merge_blocks(sel, m_scr[...], l_scr[...], acc_scr, nsel, s_self, vsn_ref[0])
        kw_t = win_ref[0, 0:LANES, :].astype(BF16)
        vw_t = win_ref[0, LANES:2 * LANES, :].astype(BF16)
        sw = _dot(qb, kw_t) + tws_ref[...]
        sw_self = jnp.sum(qm * kwn_ref[0], axis=1, keepdims=True) + misc_ref[:, 0:1]
        mw = jnp.maximum(jnp.max(sw, axis=1, keepdims=True), sw_self)
        pw = jnp.exp(sw - mw)
        pw_self = jnp.exp(sw_self - mw)
        o_win = ((_dot_nt(pw.astype(BF16), vw_t) + pw_self * vwn_ref[0])
                 / (jnp.sum(pw, axis=1, keepdims=True) + pw_self))
        gt = jnp.broadcast_to(g_ref[0], (NSA_HEADS, LANES))
        o8 = (_col(gt, lane, 3 * row) * o_cmp + _col(gt, lane, 3 * row + 1) * o_sel
              + _col(gt, lane, 3 * row + 2) * o_win)
        lane1 = lax.broadcasted_iota(jnp.int32, (1, LANES), 1)
        tiles = []
        for t in range(NSA_HEADS // 2):
            ha, hb = 2 * t, 2 * t + 1
            ra = o8[ha:ha + 1, :]
            rb = o8[hb:hb + 1, :]
            if ha // NSA_GROUP == 1:
                ra = pltpu.roll(ra, HEAD_DIM, 1)
            if hb // NSA_GROUP == 0:
                rb = pltpu.roll(rb, HEAD_DIM, 1)
            tiles.append(jnp.where(lane1 < HEAD_DIM, ra, rb))
        o_ref[0] = jnp.concatenate(tiles, axis=1)


def _nsa_sample(page_table, cache, qmat, ksn, vsn, kwn, vwn, win, gates, tsn, misc, tcs, tws,
                pos, w1bd, w2bd, gkc, avg):
    nb, npages = page_table.shape
    nstep = npages // PAGES_PER_STEP
    nsel = PAST_LEN // SEL_BLOCK

    def page_spec(j):
        return pl.BlockSpec((1, 512, PAGE), lambda b, s, pt: (pt[b, s * PAGES_PER_STEP + j], 0, 0))

    per_b = lambda shape: pl.BlockSpec((1,) + shape, lambda b, s, pt: (b, 0, 0))
    const = lambda shape: pl.BlockSpec(shape, lambda b, s, pt: (0,) * len(shape))
    cmp_rows = PAST_LEN // CMP_BLOCK * CMP_PITCH
    grid_spec = pltpu.PrefetchScalarGridSpec(
        num_scalar_prefetch=1,
        grid=(nb, nstep),
        in_specs=[page_spec(j) for j in range(PAGES_PER_STEP)]
        + [per_b((NSA_HEADS, LANES)), per_b((1, LANES)), per_b((1, LANES)), per_b((1, LANES)), per_b((1, LANES)),
           per_b((256, WINDOW)), per_b((1, LANES)),
           const((NSA_HEADS, PAGE)), const((NSA_HEADS, LANES)), const((NSA_HEADS, PAST_LEN // CMP_BLOCK)),
           const((NSA_HEADS, WINDOW)),
           const((CMP_BLOCK, 256)), const((2, CMP_BLOCK // 2, 256, 256)), const((2, 256, LANES)),
           const((1, LANES)), const((LANES, LANES))],
        out_specs=per_b((1, 512)),
        scratch_shapes=[pltpu.VMEM((cmp_rows, LANES), F32), pltpu.VMEM((cmp_rows, LANES), F32),
                        pltpu.VMEM((NSA_HEADS, LANES), F32),
                        pltpu.VMEM((NSA_HEADS, LANES), F32), pltpu.VMEM((nsel, NSA_HEADS, LANES), F32)],
    )
    return pl.pallas_call(
        _nsa_sample_kernel,
        grid_spec=grid_spec,
        out_shape=jax.ShapeDtypeStruct((nb, 1, 512), F32),
        compiler_params=_cparams(("arbitrary", "arbitrary")),
        name="nsa_sample",
    )(page_table, *([cache] * PAGES_PER_STEP), qmat, ksn, vsn, kwn, vwn, win, gates, tsn, misc, tcs, tws,
      pos, w1bd, w2bd, gkc, avg)


def _outproj_kernel(x_ref, mod_ref, om_ref, on_ref, w_ref, o_ref):
    y = _dot(om_ref[0].astype(BF16), w_ref[0:512, :]) + _dot(on_ref[0].astype(BF16), w_ref[512:1024, :])
    o_ref[0] = x_ref[0] + mod_ref[0][:, 2 * D:3 * D] * y


def _outproj(x, mod, o_m, o_n, w, tm):
    b, t, _ = x.shape
    tmod = mod.shape[1]
    row = lambda width: pl.BlockSpec((1, tm, width), lambda i, j: (i, j, 0))
    return pl.pallas_call(
        _outproj_kernel,
        grid=(b, t // tm),
        in_specs=[row(D), pl.BlockSpec((1, tmod, 3 * D), lambda i, j: (i, 0, 0)), row(512), row(512),
                  pl.BlockSpec((D, D), lambda i, j: (0, 0))],
        out_specs=row(D),
        out_shape=jax.ShapeDtypeStruct((b, t, D), F32),
        compiler_params=_cparams(("arbitrary", "arbitrary")),
        name="attn_outproj",
    )(x, mod, o_m, o_n, w)


def _mlp_kernel(x_ref, mod_ref, g_ref, w1_ref, w2_ref, o_ref, h_scr, acc_scr):
    kf = pl.program_id(2)

    @pl.when(kf == 0)
    def _():
        mod = mod_ref[0]
        h_scr[...] = _modulate(x_ref[0], g_ref[...], mod[:, 0:D], mod[:, D:2 * D]).astype(BF16)
        acc_scr[...] = jnp.zeros(acc_scr.shape, F32)

    a = jnp.square(jnp.maximum(_dot(h_scr[...], w1_ref[...]), 0.0))
    acc_scr[...] += _dot(a.astype(BF16), w2_ref[...])

    @pl.when(kf == pl.num_programs(2) - 1)
    def _():
        o_ref[0] = x_ref[0] + mod_ref[0][:, 2 * D:3 * D] * acc_scr[...]


def _mlp(x, mod, g, w1, w2, tm, tf):
    b, t, _ = x.shape
    tmod = mod.shape[1]
    return pl.pallas_call(
        _mlp_kernel,
        grid=(b, t // tm, D_FF // tf),
        in_specs=[pl.BlockSpec((1, tm, D), lambda i, j, kf: (i, j, 0)),
                  pl.BlockSpec((1, tmod, 3 * D), lambda i, j, kf: (i, 0, 0)),
                  pl.BlockSpec((1, D), lambda i, j, kf: (0, 0)),
                  pl.BlockSpec((D, tf), lambda i, j, kf: (0, kf)),
                  pl.BlockSpec((tf, D), lambda i, j, kf: (kf, 0))],
        out_specs=pl.BlockSpec((1, tm, D), lambda i, j, kf: (i, j, 0)),
        out_shape=jax.ShapeDtypeStruct((b, t, D), F32),
        scratch_shapes=[pltpu.VMEM((tm, D), BF16), pltpu.VMEM((tm, D), F32)],
        compiler_params=_cparams(("arbitrary", "arbitrary", "arbitrary")),
        name="mlp",
    )(x, mod, g, w1, w2)


def _s5_disc_kernel(are_ref, aim_ref, ldt_ref, bre_ref, bim_ref, abre_ref, abim_ref, bbre_ref, bbim_ref):
    a_re, a_im = are_ref[...], aim_ref[...]
    dt = jnp.exp(ldt_ref[...])
    decay = jnp.exp(dt * a_re)
    ab_re, ab_im = decay * jnp.cos(dt * a_im), decay * jnp.sin(dt * a_im)
    den = a_re * a_re + a_im * a_im
    f_re = ((ab_re - 1) * a_re + ab_im * a_im) / den
    f_im = (ab_im * a_re - (ab_re - 1) * a_im) / den
    br, bi = bre_ref[...], bim_ref[...]
    abre_ref[...] = ab_re
    abim_ref[...] = ab_im
    bbre_ref[...] = f_re * br - f_im * bi
    bbim_ref[...] = f_re * bi + f_im * br


def _s5_discretize(a_re, a_im, log_dt, b_re, b_im):
    rep = lambda a: jnp.repeat(a, S5_GROUP_CH, axis=1)
    shp = jax.ShapeDtypeStruct((S5_GROUPS, S5_STATE * S5_GROUP_CH), F32)
    ldt = jnp.broadcast_to(log_dt[:, None], (S5_GROUPS, S5_STATE * S5_GROUP_CH))
    flat = lambda a: a.reshape(S5_GROUPS, S5_STATE * S5_GROUP_CH)
    ab_re, ab_im, bb_re, bb_im = pl.pallas_call(
        _s5_disc_kernel, out_shape=[shp] * 4, name="s5_discretize",
    )(rep(a_re), rep(a_im), ldt, flat(b_re), flat(b_im))
    unrep = lambda a: a[:, ::S5_GROUP_CH]
    unflat = lambda a: a.reshape(S5_GROUPS, S5_STATE, S5_GROUP_CH)
    return unrep(ab_re), unrep(ab_im), unflat(bb_re), unflat(bb_im)


def _modulate_tm_kernel(x_ref, mod_ref, g_ref, o_ref):
    mod = mod_ref[0]
    o_ref[...] = _modulate(x_ref[0], g_ref[...], mod[:, 0:D], mod[:, D:2 * D])


def _modulate_time_major(x, mod, g, tl):
    b, t, _ = x.shape
    tmod = mod.shape[1]
    return pl.pallas_call(
        _modulate_tm_kernel,
        grid=(b, t // tl),
        in_specs=[pl.BlockSpec((1, tl, D), lambda i, j: (i, j, 0)),
                  pl.BlockSpec((1, tmod, 3 * D), lambda i, j: (i, 0, 0)),
                  pl.BlockSpec((1, D), lambda i, j: (0, 0))],
        out_specs=pl.BlockSpec((tl, D), lambda i, j: (j, i)),
        out_shape=jax.ShapeDtypeStruct((t, b * D), F32),
        compiler_params=_cparams(("arbitrary", "arbitrary")),
        name="s5_modulate",
    )(x, mod, g)


S5_CB = 256
S5_NS = S5_CB // S5_GROUP_CH * S5_STATE


def _s5_scan_kernel(h_ref, wb_ref, wc_ref, ar_ref, ai_ref, d_ref, h0_ref, y_ref, so_ref, xs_scr, st_scr, *, tl, r):
    i = pl.program_id(1)

    @pl.when(i == 0)
    def _():
        st_scr[...] = h0_ref[...]

    u = h_ref[...]
    xs_scr[...] = _dot(u.astype(BF16), wb_ref[0])
    ar = jnp.broadcast_to(ar_ref[0], (r, S5_NS))
    ai = jnp.broadcast_to(ai_ref[0], (r, S5_NS))

    def step(t, carry):
        xr, xi = carry
        r0 = pl.multiple_of(t * r, r)
        nr = ar * xr - ai * xi + xs_scr[pl.ds(r0, r), 0:S5_NS]
        ni = ar * xi + ai * xr + xs_scr[pl.ds(r0, r), S5_NS:2 * S5_NS]
        xs_scr[pl.ds(r0, r), 0:S5_NS] = nr
        xs_scr[pl.ds(r0, r), S5_NS:2 * S5_NS] = ni
        return nr, ni

    xr, xi = lax.fori_loop(0, tl, step, (st_scr[0], st_scr[1]))
    st_scr[0] = xr
    st_scr[1] = xi
    y_ref[...] = _dot(xs_scr[...].astype(BF16), wc_ref[0]) + d_ref[...] * u

    @pl.when(i == pl.num_programs(1) - 1)
    def _():
        so_ref[...] = st_scr[...]


def _s5_scan(h_tm, wb, wc, ar, ai, d_skip, h0, r, tl):
    rows = h_tm.shape[0]
    nj = D // S5_CB
    return pl.pallas_call(
        functools.partial(_s5_scan_kernel, tl=tl, r=r),
        grid=(nj, rows // (tl * r)),
        in_specs=[pl.BlockSpec((tl * r, S5_CB), lambda j, i: (i, j)),
                  pl.BlockSpec((1, S5_CB, 2 * S5_NS), lambda j, i: (j, 0, 0)),
                  pl.BlockSpec((1, 2 * S5_NS, S5_CB), lambda j, i: (j, 0, 0)),
                  pl.BlockSpec((1, 1, S5_NS), lambda j, i: (j, 0, 0)),
                  pl.BlockSpec((1, 1, S5_NS), lambda j, i: (j, 0, 0)),
                  pl.BlockSpec((1, S5_CB), lambda j, i: (0, j)),
                  pl.BlockSpec((2, r, S5_NS), lambda j, i: (0, 0, j))],
        out_specs=[pl.BlockSpec((tl * r, S5_CB), lambda j, i: (i, j)),
                   pl.BlockSpec((2, r, S5_NS), lambda j, i: (0, 0, j))],
        out_shape=[jax.ShapeDtypeStruct((rows, D), F32), jax.ShapeDtypeStruct((2, r, S5_GROUPS * S5_STATE), F32)],
        scratch_shapes=[pltpu.VMEM((tl * r, 2 * S5_NS), F32), pltpu.VMEM((2, r, S5_NS), F32)],
        compiler_params=_cparams(("arbitrary", "arbitrary")),
        name="s5_scan",
    )(h_tm, wb, wc, ar, ai, d_skip, h0)


def _glu_kernel(y_ref, x_ref, mod_ref, w_ref, o_ref):
    z = _dot(jax.nn.gelu(y_ref[...]).astype(BF16), w_ref[...])
    o_ref[0] = x_ref[0] + mod_ref[0][:, 2 * D:3 * D] * (z[:, 0:D] * jax.nn.sigmoid(z[:, D:2 * D]))


def _glu_residual(y_tm, x, mod, w, tl):
    b, t, _ = x.shape
    tmod = mod.shape[1]
    return pl.pallas_call(
        _glu_kernel,
        grid=(b, t // tl),
        in_specs=[pl.BlockSpec((tl, D), lambda i, j: (j, i)),
                  pl.BlockSpec((1, tl, D), lambda i, j: (i, j, 0)),
                  pl.BlockSpec((1, tmod, 3 * D), lambda i, j: (i, 0, 0)),
                  pl.BlockSpec((D, 2 * D), lambda i, j: (0, 0))],
        out_specs=pl.BlockSpec((1, tl, D), lambda i, j: (i, j, 0)),
        out_shape=jax.ShapeDtypeStruct((b, t, D), F32),
        compiler_params=_cparams(("arbitrary", "arbitrary")),
        name="s5_glu",
    )(y_tm, x, mod, w)


def _s5_block_weights(bb_re, bb_im, c_re, c_im):
    nj, ng = D // S5_CB, S5_CB // S5_GROUP_CH
    eye = jnp.eye(ng, dtype=F32)

    def wb_part(bb):
        t = bb.reshape(nj, ng, S5_STATE, S5_GROUP_CH).transpose(0, 1, 3, 2)
        return jnp.einsum("jgcn,gh->jgchn", t, eye).reshape(nj, S5_CB, S5_NS)

    def wc_part(c):
        t = c.reshape(nj, ng, S5_GROUP_CH, S5_STATE).transpose(0, 1, 3, 2)
        return jnp.einsum("jgnc,gh->jgnhc", t, eye).reshape(nj, S5_NS, S5_CB)

    wb = jnp.concatenate([wb_part(bb_re), wb_part(bb_im)], axis=2).astype(BF16)
    wc = jnp.concatenate([wc_part(c_re), -wc_part(c_im)], axis=1).astype(BF16)
    return wb, wc


S5_TL = 32
S5_PITCH = 40
S5_NSLAB = 2 * S5_GROUPS * S5_STATE // LANES


def _s5_fused_kernel(x_ref, mod_ref, g_ref, wb_ref, wc_ref, ar_ref, ai_ref, d_ref, h0_ref, wg_ref,
                     o_ref, so_ref, xs_scr, st_scr):
    i = pl.program_id(0)
    nb = x_ref.shape[0]
    nj = D // S5_CB
    tiles = S5_NS // LANES

    @pl.when(i == 0)
    def _():
        xs_scr[...] = jnp.zeros(xs_scr.shape, F32)
        for j in range(nj):
            for comp in range(2):
                for q in range(tiles):
                    st_scr[(2 * j + comp) * tiles + q] = h0_ref[comp, :, j * S5_NS + q * LANES:j * S5_NS + (q + 1) * LANES]

    h = jnp.concatenate([_modulate(x_ref[b], g_ref[...], mod_ref[b][:, 0:D], mod_ref[b][:, D:2 * D])
                         for b in range(nb)], axis=0)
    hb = h.astype(BF16)
    ys = []
    for j in range(nj):
        bu = _dot(hb[:, j * S5_CB:(j + 1) * S5_CB], wb_ref[j])
        base = 2 * j * tiles
        for lt in range(2 * tiles):
            for b in range(nb):
                xs_scr[base + lt, b * S5_PITCH:b * S5_PITCH + S5_TL, :] = bu[b * S5_TL:(b + 1) * S5_TL,
                                                                             lt * LANES:(lt + 1) * LANES]
        ar = [jnp.broadcast_to(ar_ref[j][:, q * LANES:(q + 1) * LANES], (nb, LANES)) for q in range(tiles)]
        ai = [jnp.broadcast_to(ai_ref[j][:, q * LANES:(q + 1) * LANES], (nb, LANES)) for q in range(tiles)]

        def step(t, carry):
            new = []
            for q in range(tiles):
                xr, xi = carry[2 * q], carry[2 * q + 1]
                rows = pl.ds(t, nb, stride=S5_PITCH)
                nr = ar[q] * xr - ai[q] * xi + xs_scr[base + q, rows, :]
                ni = ar[q] * xi + ai[q] * xr + xs_scr[base + tiles + q, rows, :]
                xs_scr[base + q, rows, :] = nr
                xs_scr[base + tiles + q, rows, :] = ni
                new += [nr, ni]
            return tuple(new)

        init = []
        for q in range(tiles):
            init += [st_scr[base + q], st_scr[base + tiles + q]]
        fin = lax.fori_loop(0, S5_TL, step, tuple(init))
        for q in range(tiles):
            st_scr[base + q] = fin[2 * q]
            st_scr[base + tiles + q] = fin[2 * q + 1]
        states = jnp.concatenate([xs_scr[base + lt] for lt in range(2 * tiles)], axis=1)
        ys.append(_dot(states.astype(BF16), wc_ref[j]))
    y_all = jnp.concatenate(ys, axis=1)
    y = jnp.concatenate([y_all[b * S5_PITCH:b * S5_PITCH + S5_TL] for b in range(nb)], axis=0) + d_ref[...] * h
    z = _dot(jax.nn.gelu(y).astype(BF16), wg_ref[...])
    out = z[:, 0:D] * jax.nn.sigmoid(z[:, D:2 * D])
    for b in range(nb):
        o_ref[b] = x_ref[b] + mod_ref[b][:, 2 * D:3 * D] * out[b * S5_TL:(b + 1) * S5_TL]

    @pl.when(i == pl.num_programs(0) - 1)
    def _():
        for j in range(nj):
            for comp in range(2):
                for q in range(tiles):
                    so_ref[comp, :, j * S5_NS + q * LANES:j * S5_NS + (q + 1) * LANES] = st_scr[(2 * j + comp) * tiles + q]


def _s5_layer(x, mod, g, wb, wc, ar, ai, d_skip, h0, w_glu):
    b, t, _ = x.shape
    ns = S5_GROUPS * S5_STATE
    const = lambda shape: pl.BlockSpec(shape, lambda i: (0,) * len(shape))
    x_new, st = pl.pallas_call(
        _s5_fused_kernel,
        grid=(t // S5_TL,),
        in_specs=[pl.BlockSpec((b, S5_TL, D), lambda i: (0, i, 0)),
                  const((b, 1, 3 * D)), const((1, D)),
                  const((D // S5_CB, S5_CB, 2 * S5_NS)), const((D // S5_CB, 2 * S5_NS, S5_CB)),
                  const((D // S5_CB, 1, S5_NS)), const((D // S5_CB, 1, S5_NS)),
                  const((1, D)), const((2, b, ns)), const((D, 2 * D))],
        out_specs=[pl.BlockSpec((b, S5_TL, D), lambda i: (0, i, 0)), const((2, b, ns))],
        out_shape=[jax.ShapeDtypeStruct((b, t, D), F32), jax.ShapeDtypeStruct((2, b, ns), F32)],
        scratch_shapes=[pltpu.VMEM((S5_NSLAB, b * S5_PITCH, LANES), F32), pltpu.VMEM((S5_NSLAB, b, LANES), F32)],
        compiler_params=_cparams(("arbitrary",)),
        name="s5_fused",
    )(x, mod, g, wb, wc, ar, ai, d_skip, h0.reshape(b, 2, ns).transpose(1, 0, 2), w_glu)
    return x_new, st.transpose(1, 0, 2).reshape(b, 2, S5_GROUPS, S5_STATE)


def _dist_tiles():
    r = jnp.arange(TQ, dtype=jnp.int32)[:, None]
    c = jnp.arange(TQ, dtype=jnp.int32)[None, :]
    d0 = r - c
    edge = 2 * TQ + r - c
    return jnp.concatenate([d0, TQ + d0, 2 * TQ + d0, jnp.where(edge <= WINDOW, edge, -1),
                            jnp.full((TQ, TQ), -1, jnp.int32)], axis=0)


def _dist_cmp(seq):
    q = jnp.arange(seq, dtype=jnp.int32)[:, None]
    n = jnp.arange(LANES, dtype=jnp.int32)[None, :]
    return jnp.where(n < seq // CMP_BLOCK, q - ((n + 1) * CMP_BLOCK - 1), -1)


_SAMPLE_TABLE_SIZES = (LANES, MOBA_BLOCK, PAGE, PAST_LEN // CMP_BLOCK, WINDOW)


def _dist_sample():
    ar = lambda n: jnp.arange(n, dtype=jnp.int32)
    misc = jnp.zeros((LANES,), jnp.int32).at[1].set(MAX_DISTANCE * 4)
    moba = MOBA_BLOCK - ar(MOBA_BLOCK)
    sel = PAGE - ar(PAGE)
    cmp_ = PAST_LEN - ((ar(PAST_LEN // CMP_BLOCK) + 1) * CMP_BLOCK - 1)
    win = WINDOW - ar(WINDOW)
    return jnp.concatenate([misc, moba, sel, cmp_, win])[None, :]


def _block_diag2(w):
    z = jnp.zeros_like(w)
    return jnp.concatenate([jnp.concatenate([w, z], axis=-1), jnp.concatenate([z, w], axis=-1)], axis=-2)


def kernel(x_prompt, x_sample, cache_moba_kv, cache_nsa_kv, state_nsa_win, state_s5, page_table, c_prompt, c_sample, rel_bias, attn_norm_g, attn_ada_w, attn_ada_b, attn_w_in, attn_qk_g, nsa_cmp_pos, nsa_cmp_w1, nsa_cmp_w2, attn_w_out, ssm_norm_g, ssm_ada_w, ssm_ada_b, s5_a_re, s5_a_im, s5_log_dt, s5_b_re, s5_b_im, s5_c_re, s5_c_im, s5_d, s5_w_glu, mlp_norm_g, mlp_ada_w, mlp_ada_b, mlp_w1, mlp_w2):
    bp, seq, _ = x_prompt.shape
    bs = x_sample.shape[0]
    assert seq % TQ == 0 and x_sample.shape[1] == 1
    n_pool = cache_moba_kv.shape[1]

    c_all = jnp.concatenate([c_prompt, c_sample], axis=0)
    split_mod = lambda m: (m[:bp, None, :], m[None, bp:, :])
    mod_attn = _adaln(c_all, attn_ada_w, attn_ada_b)
    mod_ssm = _adaln(c_all, ssm_ada_w, ssm_ada_b)
    mod_mlp = _adaln(c_all, mlp_ada_w, mlp_ada_b)

    xp = x_prompt
    xs = x_sample.reshape(1, bs, D)

    tb = _bias_table(rel_bias, _dist_tiles(), LOG2E).reshape(2 * MOBA_HEADS, N_BIAS_TILES, TQ, TQ)
    tc = _bias_table(rel_bias, _dist_cmp(seq), LOG2E)[MOBA_HEADS:]
    ts = _bias_table(rel_bias, _dist_sample())[:, 0, :]
    offs = [0]
    for size in _SAMPLE_TABLE_SIZES:
        offs.append(offs[-1] + size)
    part = lambda heads, t: ts[heads, offs[t]:offs[t + 1]]
    hm, hn = slice(0, MOBA_HEADS), slice(MOBA_HEADS, 2 * MOBA_HEADS)
    misc_m, misc_n, tsb, tsn, tcs, tws = part(hm, 0), part(hn, 0), part(hm, 1), part(hn, 2), part(hn, 3), part(hn, 4)

    w_in = jnp.pad(attn_w_in[0], ((0, 0), (0, IN_COLS_PAD - IN_COLS))).astype(BF16)
    qkg_t = jnp.pad(jnp.tile(attn_qk_g[0], (1, 2)), ((0, 2), (0, 0)))
    lr = jnp.arange(LANES)
    avg = jnp.where(lr[:, None] // HEAD_DIM == lr[None, :] // HEAD_DIM, 1.0 / HEAD_DIM, 0.0).astype(BF16)
    g_attn = attn_norm_g[0][None, :]
    w_out = attn_w_out[0].astype(BF16)
    pos = jnp.concatenate([nsa_cmp_pos[0, 0], nsa_cmp_pos[0, 0], nsa_cmp_pos[0, 1], nsa_cmp_pos[0, 1]], axis=1)
    w1bd = _block_diag2(nsa_cmp_w1[0].reshape(2, CMP_BLOCK, HEAD_DIM, CMP_HIDDEN)).astype(BF16)
    w1bd = w1bd.reshape(2, CMP_BLOCK // 2, 256, 256)
    w2bd = _block_diag2(nsa_cmp_w2[0]).astype(BF16)
    gkc = qkg_t[3:4]

    mp_attn, ms_attn = split_mod(mod_attn[0])
    mq, mkv, nq, nkv, wkv, gates, mkv_t, nkv_t = _attn_proj(xp, mp_attn, g_attn, w_in, qkg_t, avg, 512,
                                                            SCALE * LOG2E, page_major=True)
    o_moba = _moba_prompt(mq, mkv, tb[:MOBA_HEADS])
    kcmp, vcmp = _cmp_prompt(nkv, pos, w1bd, w2bd, gkc, avg)
    o_nsa = _nsa_prompt(nq, nkv, wkv, kcmp, vcmp, gates, tb[MOBA_HEADS:], tc)
    xp = _outproj(xp, mp_attn, o_moba, o_nsa, w_out, 512)
    npg = seq // PAGE
    moba_p = mkv_t.reshape(1, bp, npg, 2, MOBA_HEADS, HEAD_DIM, PAGE).transpose(0, 1, 2, 6, 3, 4, 5)
    nsa_p = nkv_t.reshape(1, bp, npg, 4, 2, HEAD_DIM, PAGE).transpose(0, 1, 2, 6, 3, 4, 5)
    win_p = wkv[:, seq - min(WINDOW, seq):].reshape(1, bp, min(WINDOW, seq), 2, 2, HEAD_DIM)
    mq_s, mkv_s, nq_s, nkv_s, wkv_s, gates_s = _attn_proj(xs, ms_attn, g_attn, w_in, qkg_t, avg, bs, SCALE)
    cache_m_t = cache_moba_kv.transpose(0, 1, 3, 4, 5, 2).reshape(n_pool, 2, 512, PAGE)
    cache_n_t = cache_nsa_kv.transpose(0, 1, 3, 4, 5, 2).reshape(n_pool, 512, PAGE)
    win_t = state_nsa_win[0].transpose(0, 2, 3, 4, 1).reshape(bs, 256, WINDOW)
    lw = jnp.arange(512)
    qmat_m = jnp.where(lw[None, None, :] // HEAD_DIM == jnp.arange(MOBA_HEADS)[None, :, None], mq_s[0][:, None, :], 0.0)
    col3 = lambda a, lo, width: a[0][:, None, lo:lo + width]
    o_moba_s = _moba_sample(page_table, cache_m_t, qmat_m, col3(mkv_s, 0, 512), col3(mkv_s, 512, 512), tsb, misc_m)
    nq4 = nq_s[0].reshape(bs, NSA_HEADS, HEAD_DIM)
    kvh = jnp.arange(NSA_HEADS) // NSA_GROUP
    qmat_n = jnp.concatenate([jnp.where(kvh[None, :, None] == 0, nq4, 0.0),
                              jnp.where(kvh[None, :, None] == 1, nq4, 0.0)], axis=2)
    o_nsa_s = _nsa_sample(page_table, cache_n_t, qmat_n,
                          col3(nkv_s, 256, LANES), col3(nkv_s, 384, LANES), col3(wkv_s, 0, LANES),
                          col3(wkv_s, 128, LANES), win_t, gates_s.reshape(bs, 1, LANES),
                          tsn, misc_n, tcs, tws, pos, w1bd, w2bd, gkc, avg)
    xs = _outproj(xs, ms_attn, o_moba_s.reshape(1, bs, 512), o_nsa_s.reshape(1, bs, 512), w_out, bs)
    moba_s = mkv_s.reshape(1, bs, 1, 2, MOBA_HEADS, HEAD_DIM)
    nsa_s = nkv_s.reshape(1, bs, 1, 4, 2, HEAD_DIM)
    win_s = jnp.concatenate([state_nsa_win[0][:, 1:], wkv_s[0].reshape(bs, 1, 2, 2, HEAD_DIM)], axis=1)[None]

    w1_0, w2_0 = mlp_w1[0].astype(BF16), mlp_w2[0].astype(BF16)
    mp_mlp, ms_mlp = split_mod(mod_mlp[0])
    g_mlp0 = mlp_norm_g[0][None, :]
    xp = _mlp(xp, mp_mlp, g_mlp0, w1_0, w2_0, 512, 1024)
    xs = _mlp(xs, ms_mlp, g_mlp0, w1_0, w2_0, bs, 1024)

    ab_re, ab_im, bb_re, bb_im = _s5_discretize(s5_a_re[0], s5_a_im[0], s5_log_dt[0], s5_b_re[0], s5_b_im[0])
    wb, wc = _s5_block_weights(bb_re, bb_im, s5_c_re[0], s5_c_im[0])
    nj = D // S5_CB
    ar = ab_re.reshape(nj, 1, S5_NS)
    ai = ab_im.reshape(nj, 1, S5_NS)
    g_ssm = ssm_norm_g[0][None, :]
    d_skip = s5_d[0][None, :]
    w_glu = s5_w_glu[0].astype(BF16)
    mp_ssm, ms_ssm = split_mod(mod_ssm[0])
    xp, st_p = _s5_layer(xp, mp_ssm, g_ssm, wb, wc, ar, ai, d_skip,
                         jnp.zeros((bp, 2, S5_GROUPS, S5_STATE), F32), w_glu)
    h_s = _modulate_time_major(xs, ms_ssm, g_ssm, bs)
    y_s, st_s = _s5_scan(h_s, wb, wc, ar, ai, d_skip,
                         state_s5[0].reshape(bs, 2, S5_GROUPS * S5_STATE).transpose(1, 0, 2), bs, 1)
    xs = _glu_residual(y_s, xs, ms_ssm, w_glu, bs)
    st_s = st_s.transpose(1, 0, 2).reshape(bs, 2, S5_GROUPS, S5_STATE)

    w1_1, w2_1 = mlp_w1[1].astype(BF16), mlp_w2[1].astype(BF16)
    mp_mlp, ms_mlp = split_mod(mod_mlp[1])
    g_mlp1 = mlp_norm_g[1][None, :]
    xp = _mlp(xp, mp_mlp, g_mlp1, w1_1, w2_1, 512, 1024)
    xs = _mlp(xs, ms_mlp, g_mlp1, w1_1, w2_1, bs, 1024)

    return (xp, xs.reshape(bs, 1, D), moba_p, moba_s, nsa_p, nsa_s, win_p, win_s, st_p[None], st_s[None])
```

```python
import functools
import math

import jax
import jax.numpy as jnp
import numpy as np
from jax import lax
from jax.experimental import pallas as pl
from jax.experimental.pallas import tpu as pltpu

F32 = jnp.float32
BF16 = jnp.bfloat16
HIGHEST = lax.Precision.HIGHEST

D = 1024
HEAD_DIM = 64
MOBA_HEADS = 8
NSA_HEADS = 8
NSA_GROUP = 4
MOBA_BLOCK = 256
MOBA_TOPK = 3
CMP_BLOCK = 32
CMP_HIDDEN = 128
SEL_BLOCK = 64
SEL_TOPK = 16
WINDOW = 512
NUM_BUCKETS = 32
MAX_DISTANCE = 128
PAGE = 128
PAST_LEN = 8192
D_FF = 4 * D
S5_GROUPS = 64
S5_STATE = 64
S5_GROUP_CH = 16
IN_COLS = 3 * 512 + 512 + 6 * 128 + 3 * NSA_HEADS
IN_COLS_PAD = 23 * 128
EPS = 1e-6
SCALE = HEAD_DIM ** -0.5
LOG2E = math.log2(math.e)
LANES = 128
TQ = 256
N_BIAS_TILES = 5
CASE_TILES = 1
MLP_TM = 1024
MLP_TF = 1024
NEG = -1e30
M_INIT = -1e15
VMEM_LIMIT = 56 * 1024 * 1024

_NT = (((1,), (1,)), ((), ()))


def _cparams(sem):
    return pltpu.CompilerParams(dimension_semantics=sem, vmem_limit_bytes=VMEM_LIMIT)


def _dot(a, b, **kw):
    return jnp.dot(a, b, preferred_element_type=F32, **kw)


def _dot_nt(a, b, **kw):
    return lax.dot_general(a, b, _NT, preferred_element_type=F32, **kw)


def _modulate(x, g, shift, scale):
    ms = jnp.mean(x * x, axis=-1, keepdims=True)
    return x * lax.rsqrt(ms + EPS) * g * (1.0 + scale) + shift


def _group_mean_sq(z, avg):
    sq = z * z
    hi = sq.astype(BF16)
    lo = (sq - hi.astype(F32)).astype(BF16)
    return _dot(hi, avg) + _dot(lo, avg)


def _col(x, lane, idx):
    return jnp.sum(jnp.where(lane == idx, x, 0.0), axis=1, keepdims=True)


def _adaln_kernel(c_ref, w_ref, b_ref, o_ref):
    c = c_ref[...]
    s = c * jax.nn.sigmoid(c)
    o_ref[0] = _dot(s, w_ref[0], precision=HIGHEST) + b_ref[0]


def _adaln(c_all, w, b):
    nl, n = w.shape[0], c_all.shape[0]
    return pl.pallas_call(
        _adaln_kernel,
        grid=(nl, 3),
        in_specs=[pl.BlockSpec((n, D), lambda l, j: (0, 0)),
                  pl.BlockSpec((1, D, D), lambda l, j: (l, 0, j)),
                  pl.BlockSpec((1, 1, D), lambda l, j: (l, 0, j))],
        out_specs=pl.BlockSpec((1, n, D), lambda l, j: (l, 0, j)),
        out_shape=jax.ShapeDtypeStruct((nl, n, 3 * D), F32),
        compiler_params=_cparams(("arbitrary", "arbitrary")),
        name="adaln",
    )(c_all, w, b.reshape(nl, 1, 3 * D))


def _log_bucket_starts():
    max_exact = NUM_BUCKETS // 2
    n = np.arange(max_exact, 4 * MAX_DISTANCE, dtype=np.float32)
    large = max_exact + (np.log(n / np.float32(max_exact)) / np.float32(math.log(MAX_DISTANCE / max_exact))
                         * np.float32(NUM_BUCKETS - max_exact)).astype(np.int32)
    large = np.minimum(large, NUM_BUCKETS - 1)
    return [int(np.argmax(large >= b)) + max_exact for b in range(max_exact, NUM_BUCKETS)]


def _bias_kernel(rb_ref, d_ref, o_ref, *, scale):
    h = pl.program_id(0)
    dist = d_ref[...]
    n = jnp.maximum(dist, 0)
    max_exact = NUM_BUCKETS // 2
    acc = jnp.zeros(dist.shape, F32)
    for k in range(max_exact):
        acc = jnp.where(n == k, rb_ref[k, h], acc)
    for j, start in enumerate(_log_bucket_starts()):
        acc = jnp.where(n >= start, rb_ref[max_exact + j, h], acc)
    o_ref[0] = jnp.where(dist < 0, NEG, acc * scale)


def _bias_table(rel_bias, dist, scale=1.0):
    r, c = dist.shape
    nh = rel_bias.shape[1]
    return pl.pallas_call(
        functools.partial(_bias_kernel, scale=scale),
        grid=(nh,),
        in_specs=[pl.BlockSpec(memory_space=pltpu.SMEM),
                  pl.BlockSpec((r, c), lambda h: (0, 0))],
        out_specs=pl.BlockSpec((1, r, c), lambda h: (h, 0, 0)),
        out_shape=jax.ShapeDtypeStruct((nh, r, c), F32),
        compiler_params=_cparams(("arbitrary",)),
        name="bias_table",
    )(rel_bias, dist)


def _proj_kernel(x_ref, mod_ref, g_ref, w_ref, qkg_ref, avg_ref,
                 mq_ref, mkv_ref, nq_ref, nkv_ref, wkv_ref, gt_ref, *page_major_refs, q_scale):
    x = x_ref[0]
    mod = mod_ref[0]
    h = _modulate(x, g_ref[...], mod[:, 0:D], mod[:, D:2 * D])
    z = _dot(h.astype(BF16), w_ref[...])
    avg = avg_ref[...]

    def normed(lo, gi):
        zs = z[:, lo:lo + LANES]
        return zs * lax.rsqrt(_group_mean_sq(zs, avg) + EPS) * qkg_ref[gi:gi + 1, :]

    for t in range(4):
        mq_ref[0, :, t * LANES:(t + 1) * LANES] = normed(t * LANES, 0) * q_scale
        mkv_ref[0, :, t * LANES:(t + 1) * LANES] = normed(512 + t * LANES, 1)
        nq_ref[0, :, t * LANES:(t + 1) * LANES] = normed(1536 + t * LANES, 2) * q_scale
    mkv_ref[0, :, 512:1024] = z[:, 1024:1536]
    nkv_ref[0, :, 0:256] = z[:, 2048:2304]
    nkv_ref[0, :, 256:384] = normed(2304, 4)
    nkv_ref[0, :, 384:512] = z[:, 2432:2560]
    wkv_ref[0, :, 0:128] = normed(2560, 5)
    wkv_ref[0, :, 128:256] = z[:, 2688:2816]
    gt_ref[0] = jax.nn.sigmoid(z[:, 2816:2944])
    if page_major_refs:
        mkv_t_ref, nkv_t_ref = page_major_refs
        for p in range(x.shape[0] // PAGE):
            mkv_t_ref[0, p] = mkv_ref[0, p * PAGE:(p + 1) * PAGE, :].T
            nkv_t_ref[0, p] = nkv_ref[0, p * PAGE:(p + 1) * PAGE, :].T


def _attn_proj(x, mod, g, w_pad, qkg_t, avg, tm, q_scale, page_major=False):
    b, t, _ = x.shape
    tmod = mod.shape[1]
    row = lambda width: pl.BlockSpec((1, tm, width), lambda i, j: (i, j, 0))
    shp = lambda width: jax.ShapeDtypeStruct((b, t, width), F32)
    out_specs = [row(512), row(1024), row(512), row(512), row(256), row(128)]
    out_shape = [shp(512), shp(1024), shp(512), shp(512), shp(256), shp(128)]
    if page_major:
        for width in (1024, 512):
            out_specs.append(pl.BlockSpec((1, tm // PAGE, width, PAGE), lambda i, j: (i, j, 0, 0)))
            out_shape.append(jax.ShapeDtypeStruct((b, t // PAGE, width, PAGE), F32))
    return pl.pallas_call(
        functools.partial(_proj_kernel, q_scale=q_scale),
        grid=(b, t // tm),
        in_specs=[row(D),
                  pl.BlockSpec((1, tmod, 3 * D), lambda i, j: (i, 0, 0)),
                  pl.BlockSpec((1, D), lambda i, j: (0, 0)),
                  pl.BlockSpec((D, IN_COLS_PAD), lambda i, j: (0, 0)),
                  pl.BlockSpec((8, LANES), lambda i, j: (0, 0)),
                  pl.BlockSpec((LANES, LANES), lambda i, j: (0, 0))],
        out_specs=out_specs,
        out_shape=out_shape,
        compiler_params=_cparams(("arbitrary", "arbitrary")),
        name="attn_proj",
    )(x, mod, g, w_pad, qkg_t, avg)


def _rank_rows(score, rowi, ncand, step=1):
    rank = jnp.zeros(score.shape, F32)
    for m in range(0, ncand * step, step):
        rm = score[m:m + 1, :]
        rank = rank + jnp.where(rm > score, 1.0, 0.0) + jnp.where((rm == score) & (m < rowi), 1.0, 0.0)
    return rank


def _columns_from_rows(x_t):
    pad = jnp.zeros((LANES - x_t.shape[0], x_t.shape[1]), F32)
    return jnp.concatenate([x_t, pad], axis=0).T


def _softmax_pv(pieces, v_all):
    m = pieces[0]
    for s in pieces[1:]:
        m = jnp.maximum(m, s)
    m = jnp.maximum(jnp.max(m, axis=1, keepdims=True), M_INIT)
    ps = [jnp.exp2(s - m) for s in pieces]
    tot = ps[0]
    for p in ps[1:]:
        tot = tot + p
    l = jnp.sum(tot, axis=1, keepdims=True)
    p_all = jnp.concatenate([p.astype(BF16) for p in ps], axis=1) if len(ps) > 1 else ps[0].astype(BF16)
    return _dot(p_all, v_all) / jnp.maximum(l, 1e-30)


def _moba_prompt_kernel(q_ref, k_ref, v_ref, t_ref, o_ref, km_scr, kb_scr, vb_scr):
    s_len = q_ref.shape[1]
    nblk = s_len // MOBA_BLOCK
    nq = s_len // TQ
    lane = lax.broadcasted_iota(jnp.int32, (TQ, LANES), 1)
    rowb = lax.broadcasted_iota(jnp.int32, (nblk, TQ), 0)
    km_scr[...] = jnp.zeros(km_scr.shape, F32)
    for n in range(nblk):
        km_scr[n:n + 1, :] = jnp.mean(k_ref[0, n * MOBA_BLOCK:(n + 1) * MOBA_BLOCK, :], axis=0, keepdims=True)
    kmean = km_scr[...]
    kb_scr[...] = k_ref[0].astype(BF16)
    vb_scr[...] = v_ref[0].astype(BF16)

    def qtile(i, _):
        r0 = pl.multiple_of(i * TQ, TQ)
        q2 = q_ref[0, pl.ds(r0, TQ), :]
        qbs, cbs = [], []
        for e in range(2):
            qe = jnp.where(lane // HEAD_DIM == e, q2, 0.0)
            gate_t = _dot_nt(kmean, qe, precision=HIGHEST)[0:nblk]
            gm = jnp.where(rowb < i, gate_t, -jnp.inf)
            sel = ((_rank_rows(gm, rowb, nblk) < MOBA_TOPK) & (rowb < i)) | (rowb == i)
            cbs.append(_columns_from_rows(jnp.where(sel, 0.0, NEG)))
            qbs.append(qe.astype(BF16))

        for c in range(1, nq // CASE_TILES + 1):
            @pl.when(i // CASE_TILES + 1 == c)
            def _():
                ntile = CASE_TILES * c
                kall = kb_scr[0:ntile * TQ, :]
                vall = vb_scr[0:ntile * TQ, :]
                outs = []
                for e in range(2):
                    s = _dot_nt(qbs[e], kall)
                    far_bias = t_ref[e, 2, 0:1, 0:1]
                    pieces = []
                    for n in range(ntile):
                        seg = s[:, n * TQ:(n + 1) * TQ]
                        if n >= ntile - CASE_TILES - 1:
                            pieces.append(seg + t_ref[e, jnp.clip(i - n, 0, 2)] + cbs[e][:, n:n + 1])
                        else:
                            pieces.append(seg + (cbs[e][:, n:n + 1] + far_bias))
                    outs.append(_softmax_pv(pieces, vall))
                o_ref[0, pl.ds(r0, TQ), :] = jnp.where(lane < HEAD_DIM, outs[0], outs[1])
        return 0

    lax.fori_loop(0, nq, qtile, 0)


def _moba_prompt(mq, mkv, tb):
    b, s, _ = mq.shape
    npair = MOBA_HEADS // 2
    return pl.pallas_call(
        _moba_prompt_kernel,
        grid=(b, npair),
        in_specs=[pl.BlockSpec((1, s, LANES), lambda i, p: (i, 0, p)),
                  pl.BlockSpec((1, s, LANES), lambda i, p: (i, 0, p)),
                  pl.BlockSpec((1, s, LANES), lambda i, p: (i, 0, npair + p)),
                  pl.BlockSpec((2, N_BIAS_TILES, TQ, TQ), lambda i, p: (p, 0, 0, 0))],
        out_specs=pl.BlockSpec((1, s, LANES), lambda i, p: (i, 0, p)),
        out_shape=jax.ShapeDtypeStruct((b, s, 512), F32),
        scratch_shapes=[pltpu.VMEM((LANES, LANES), F32), pltpu.VMEM((s, LANES), BF16), pltpu.VMEM((s, LANES), BF16)],
        compiler_params=_cparams(("arbitrary", "arbitrary")),
        name="moba_prompt",
    )(mq, mkv, mkv, tb)


def _compress_tokens(load_k, load_v, pos_ref, w1_ref, w2_ref):
    hk = hv = None
    for r in range(0, CMP_BLOCK, 2):
        xk = [(load_k(r + t) + pos_ref[r + t:r + t + 1, 0:LANES]).astype(BF16) for t in range(2)]
        xv = [(load_v(r + t) + pos_ref[r + t:r + t + 1, LANES:2 * LANES]).astype(BF16) for t in range(2)]
        dk = _dot(jnp.concatenate(xk, axis=1), w1_ref[0, r // 2])
        dv = _dot(jnp.concatenate(xv, axis=1), w1_ref[1, r // 2])
        hk = dk if hk is None else hk + dk
        hv = dv if hv is None else hv + dv
    ck = _dot(jax.nn.gelu(hk).astype(BF16), w2_ref[0])
    cv = _dot(jax.nn.gelu(hv).astype(BF16), w2_ref[1])
    return ck, cv


def _cmp_prompt_kernel(xk_ref, xv_ref, pos_ref, w1_ref, w2_ref, gkc_ref, avg_ref, kc_ref, vc_ref):
    nblk = xk_ref.shape[1] // CMP_BLOCK
    ck, cv = _compress_tokens(lambda r: xk_ref[0, pl.ds(r, nblk, stride=CMP_BLOCK), :],
                              lambda r: xv_ref[0, pl.ds(r, nblk, stride=CMP_BLOCK), :], pos_ref, w1_ref, w2_ref)
    ck = ck * lax.rsqrt(_group_mean_sq(ck, avg_ref[...]) + EPS) * gkc_ref[...]
    kc_ref[0] = jnp.zeros((LANES, LANES), F32)
    vc_ref[0] = jnp.zeros((LANES, LANES), F32)
    kc_ref[0, 0:nblk, :] = ck
    vc_ref[0, 0:nblk, :] = cv


def _cmp_prompt(nkv, pos, w1bd, w2bd, gkc, avg):
    b, s, _ = nkv.shape
    const = lambda shape: pl.BlockSpec(shape, lambda i: (0,) * len(shape))
    return pl.pallas_call(
        _cmp_prompt_kernel,
        grid=(b,),
        in_specs=[pl.BlockSpec((1, s, LANES), lambda i: (i, 0, 0)), pl.BlockSpec((1, s, LANES), lambda i: (i, 0, 1)),
                  const((CMP_BLOCK, 256)), const((2, CMP_BLOCK // 2, 256, 256)), const((2, 256, LANES)),
                  const((1, LANES)), const((LANES, LANES))],
        out_specs=[pl.BlockSpec((1, LANES, LANES), lambda i: (i, 0, 0))] * 2,
        out_shape=[jax.ShapeDtypeStruct((b, LANES, LANES), F32)] * 2,
        compiler_params=_cparams(("arbitrary",)),
        name="nsa_compress_prompt",
    )(nkv, nkv, pos, w1bd, w2bd, gkc, avg)


def _nsa_prompt_kernel(q_ref, ks_ref, vs_ref, kw_ref, vw_ref, kc_ref, vc_ref, g_ref, t_ref, tc_ref,
                       o_ref, ksb_scr, vsb_scr, kwb_scr, vwb_scr):
    s_len = q_ref.shape[1]
    k = pl.program_id(1)
    lane = lax.broadcasted_iota(jnp.int32, (TQ, LANES), 1)
    kvmask = (lane // HEAD_DIM) == k
    kc = kc_ref[0].astype(BF16)
    vc = vc_ref[0].astype(BF16)
    nsel = s_len // SEL_BLOCK
    nq = s_len // TQ
    ncmp = s_len // CMP_BLOCK
    rowb = lax.broadcasted_iota(jnp.int32, (nsel, TQ), 0)
    qpos = lax.broadcasted_iota(jnp.int32, (nsel, TQ), 1)
    pair_r = lax.broadcasted_iota(jnp.int32, (nsel, LANES), 0)
    pair_c = lax.broadcasted_iota(jnp.int32, (nsel, LANES), 1)
    pair_t = jnp.where((pair_c // (SEL_BLOCK // CMP_BLOCK) == pair_r) & (pair_c < ncmp), 1.0, 0.0)
    ksb_scr[...] = ks_ref[0].astype(BF16)
    vsb_scr[...] = vs_ref[0].astype(BF16)
    kwb_scr[...] = kw_ref[0].astype(BF16)
    vwb_scr[...] = vw_ref[0].astype(BF16)

    def qtile(i, _):
        r0 = pl.multiple_of(i * TQ, TQ)
        qs = []
        for h in range(NSA_GROUP):
            q2 = q_ref[0, pl.ds(r0, TQ), (h // 2) * LANES:(h // 2 + 1) * LANES]
            qa = jnp.where(k == (h % 2), q2, pltpu.roll(q2, HEAD_DIM, 1))
            qs.append(jnp.where(kvmask, qa, 0.0).astype(BF16))

        imp = jnp.zeros((TQ, LANES), F32)
        o_cmp = []
        for h in range(NSA_GROUP):
            s = _dot_nt(qs[h], kc) + tc_ref[h, pl.ds(r0, TQ), :]
            m = jnp.maximum(jnp.max(s, axis=1, keepdims=True), M_INIT)
            p = jnp.exp2(s - m)
            p = p / jnp.maximum(jnp.sum(p, axis=1, keepdims=True), 1e-30)
            imp = imp + p
            o_cmp.append(_dot(p.astype(BF16), vc))

        imp_t = _dot_nt(pair_t, imp, precision=HIGHEST)
        own = (r0 + qpos) // SEL_BLOCK
        sc = jnp.where(rowb < own, imp_t, -jnp.inf)
        sel = ((_rank_rows(sc, rowb, nsel) < SEL_TOPK) & (rowb < own)) | (rowb == own)
        selb = _columns_from_rows(jnp.where(sel, 1.0, 0.0)).astype(BF16)

        wk, wv, wt = [], [], []
        for j, tidx in enumerate((3, 1, 0)):
            n = i - 2 + j
            c0 = pl.multiple_of(jnp.maximum(n, 0) * TQ, TQ)
            wk.append(kwb_scr[pl.ds(c0, TQ), :])
            wv.append(vwb_scr[pl.ds(c0, TQ), :])
            wt.append(jnp.where(n < 0, N_BIAS_TILES - 1, tidx))
        kw_all = jnp.concatenate(wk, axis=0)
        vw_all = jnp.concatenate(wv, axis=0)
        g = g_ref[0, pl.ds(r0, TQ), :]
        o_win = []
        for h in range(NSA_GROUP):
            s = _dot_nt(qs[h], kw_all)
            o_win.append(_softmax_pv([s[:, j * TQ:(j + 1) * TQ] + t_ref[h, wt[j]] for j in range(3)], vw_all))

        for c in range(1, nq // CASE_TILES + 1):
            @pl.when(i // CASE_TILES + 1 == c)
            def _():
                nkeys = CASE_TILES * c * TQ
                e_r = lax.broadcasted_iota(jnp.int32, (LANES, nkeys), 0)
                e_c = lax.broadcasted_iota(jnp.int32, (LANES, nkeys), 1)
                expand = jnp.where(e_r == e_c // SEL_BLOCK, 1.0, 0.0).astype(BF16)
                addm = (_dot(selb, expand) - 1.0) * (-NEG)
                kall = ksb_scr[0:nkeys, :]
                vall = vsb_scr[0:nkeys, :]
                res = []
                for h in range(NSA_GROUP):
                    s = _dot_nt(qs[h], kall) + addm
                    pieces = [s[:, n * TQ:(n + 1) * TQ] + t_ref[h, jnp.clip(i - n, 0, 2)]
                              for n in range(CASE_TILES * c)]
                    o_sel = _softmax_pv(pieces, vall)
                    hg = (k * NSA_GROUP + h) * 3
                    o = (_col(g, lane, hg) * o_cmp[h] + _col(g, lane, hg + 1) * o_sel
                         + _col(g, lane, hg + 2) * o_win[h])
                    res.append(jnp.where(k == (h % 2), o, pltpu.roll(o, HEAD_DIM, 1)))
                for t in range(2):
                    o_ref[0, pl.ds(r0, TQ), t * LANES:(t + 1) * LANES] = jnp.where(
                        lane < HEAD_DIM, res[2 * t], res[2 * t + 1])
        return 0

    lax.fori_loop(0, nq, qtile, 0)


def _nsa_prompt(nq, nkv, wkv, kcmp, vcmp, gates, tb, tc):
    b, s, _ = nq.shape
    col = lambda arr_cols, cb: pl.BlockSpec((1, s, LANES), lambda i, k: (i, 0, cb))
    return pl.pallas_call(
        _nsa_prompt_kernel,
        grid=(b, 2),
        in_specs=[pl.BlockSpec((1, s, 256), lambda i, k: (i, 0, k)),
                  col(512, 2), col(512, 3), col(256, 0), col(256, 1),
                  pl.BlockSpec((1, LANES, LANES), lambda i, k: (i, 0, 0)),
                  pl.BlockSpec((1, LANES, LANES), lambda i, k: (i, 0, 0)),
                  pl.BlockSpec((1, s, LANES), lambda i, k: (i, 0, 0)),
                  pl.BlockSpec((NSA_GROUP, N_BIAS_TILES, TQ, TQ), lambda i, k: (k, 0, 0, 0)),
                  pl.BlockSpec((NSA_GROUP, s, LANES), lambda i, k: (k, 0, 0))],
        out_specs=pl.BlockSpec((1, s, 256), lambda i, k: (i, 0, k)),
        out_shape=jax.ShapeDtypeStruct((b, s, 512), F32),
        scratch_shapes=[pltpu.VMEM((s, LANES), BF16)] * 4,
        compiler_params=_cparams(("arbitrary", "arbitrary")),
        name="nsa_prompt",
    )(nq, nkv, nkv, wkv, wkv, kcmp, vcmp, gates, tb, tc)


PAGES_PER_STEP = 16
CMP_PITCH = 40


def _rank_lt(score, lane, ncand, topk):
    rank = jnp.zeros(score.shape, F32)
    for m in range(ncand):
        col = score[:, m:m + 1]
        beats = (col > score) | ((col == score) & (m < lane))
        rank = rank + jnp.where(beats, 1.0, 0.0)
    return rank < topk


def _merge_blocks(sel, m_all, l_all, acc_scr, nblk, s_self, v_self):
    mx = jnp.maximum(jnp.max(jnp.where(sel, m_all, NEG), axis=1, keepdims=True), s_self)
    w = jnp.exp(jnp.where(sel, m_all - mx, NEG))
    w_self = jnp.exp(s_self - mx)
    den = jnp.sum(w * l_all, axis=1, keepdims=True) + w_self
    num = w_self * v_self
    for j in range(nblk):
        num = num + w[:, j:j + 1] * acc_scr[j]
    return num / den


def _moba_sample_kernel(pt_ref, *refs):
    pages = refs[:PAGES_PER_STEP]
    qm_ref, kn_ref, vn_ref, tsb_ref, misc_ref, o_ref, g_scr, m_scr, l_scr, acc_scr = refs[PAGES_PER_STEP:]
    s = pl.program_id(1)
    nstep = pl.num_programs(1)
    nblk = PAST_LEN // MOBA_BLOCK
    width = MOBA_HEADS * HEAD_DIM
    qm = qm_ref[0]
    qb = qm.astype(BF16)
    lane = lax.broadcasted_iota(jnp.int32, (MOBA_HEADS, LANES), 1)

    @pl.when(s == 0)
    def _():
        g_scr[...] = jnp.zeros(g_scr.shape, F32)
        m_scr[...] = jnp.zeros(m_scr.shape, F32)
        l_scr[...] = jnp.zeros(l_scr.shape, F32)

    npb = PAGES_PER_STEP // 2
    blk0 = s * npb
    kt_all = jnp.concatenate([pages[t][0, 0].astype(BF16) for t in range(PAGES_PER_STEP)], axis=1)
    vt_all = jnp.concatenate([pages[t][0, 1].astype(BF16) for t in range(PAGES_PER_STEP)], axis=1)
    raw = _dot(qb, kt_all)
    far = jnp.broadcast_to(misc_ref[:, 1:2], (MOBA_HEADS, MOBA_BLOCK))
    g_new, m_new, l_new = g_scr[...], m_scr[...], l_scr[...]
    p_rows = []
    for j in range(npb):
        seg = raw[:, j * MOBA_BLOCK:(j + 1) * MOBA_BLOCK]
        gate = jnp.sum(seg, axis=1, keepdims=True)
        sc = seg + (jnp.where(s == nstep - 1, tsb_ref[...], far) if j == npb - 1 else far)
        mj = jnp.max(sc, axis=1, keepdims=True)
        p = jnp.exp(sc - mj)
        g_new = jnp.where(lane == blk0 + j, gate, g_new)
        m_new = jnp.where(lane == blk0 + j, mj, m_new)
        l_new = jnp.where(lane == blk0 + j, jnp.sum(p, axis=1, keepdims=True), l_new)
        zeros = jnp.zeros((MOBA_HEADS, MOBA_BLOCK), F32)
        p_rows.append(jnp.concatenate([p if t == j else zeros for t in range(npb)], axis=1))
    g_scr[...] = g_new
    m_scr[...] = m_new
    l_scr[...] = l_new
    acc = _dot_nt(jnp.concatenate(p_rows, axis=0).astype(BF16), vt_all)
    for j in range(npb):
        acc_scr[blk0 + j] = acc[j * MOBA_HEADS:(j + 1) * MOBA_HEADS]

    @pl.when(s == nstep - 1)
    def _():
        gm = jnp.where(lane < nblk, g_scr[...], -jnp.inf)
        sel = _rank_lt(gm, lane, nblk, MOBA_TOPK) & (lane < nblk)
        s_self = jnp.sum(qm * kn_ref[0], axis=1, keepdims=True) + misc_ref[:, 0:1]
        o = _merge_blocks(sel, m_scr[...], l_scr[...], acc_scr, nblk, s_self, vn_ref[0])
        hrow = lax.broadcasted_iota(jnp.int32, (MOBA_HEADS, width), 0)
        hlane = lax.broadcasted_iota(jnp.int32, (MOBA_HEADS, width), 1)
        o_ref[0] = jnp.sum(jnp.where(hlane // HEAD_DIM == hrow, o, 0.0), axis=0, keepdims=True)


def _moba_sample(page_table, cache_t, qmat, knew, vnew, tsb, misc):
    nb, npages = page_table.shape
    nstep = npages // PAGES_PER_STEP
    width = MOBA_HEADS * HEAD_DIM
    nblk = PAST_LEN // MOBA_BLOCK

    def page_spec(j):
        return pl.BlockSpec((1, 2, width, PAGE), lambda b, s, pt: (pt[b, s * PAGES_PER_STEP + j], 0, 0, 0))

    per_b = lambda shape: pl.BlockSpec((1,) + shape, lambda b, s, pt: (b, 0, 0))
    const = lambda shape: pl.BlockSpec(shape, lambda b, s, pt: (0,) * len(shape))
    grid_spec = pltpu.PrefetchScalarGridSpec(
        num_scalar_prefetch=1,
        grid=(nb, nstep),
        in_specs=[page_spec(j) for j in range(PAGES_PER_STEP)]
        + [per_b((MOBA_HEADS, width)), per_b((1, width)), per_b((1, width)),
           const((MOBA_HEADS, MOBA_BLOCK)), const((MOBA_HEADS, LANES))],
        out_specs=per_b((1, width)),
        scratch_shapes=[pltpu.VMEM((MOBA_HEADS, LANES), F32)] * 3 + [pltpu.VMEM((nblk, MOBA_HEADS, width), F32)],
    )
    return pl.pallas_call(
        _moba_sample_kernel,
        grid_spec=grid_spec,
        out_shape=jax.ShapeDtypeStruct((nb, 1, width), F32),
        compiler_params=_cparams(("arbitrary", "arbitrary")),
        name="moba_sample",
    )(page_table, *([cache_t] * PAGES_PER_STEP), qmat, knew, vnew, tsb, misc)


def _nsa_sample_kernel(pt_ref, *refs):
    pages = refs[:PAGES_PER_STEP]
    (qm_ref, ksn_ref, vsn_ref, kwn_ref, vwn_ref, win_ref, g_ref, tsn_ref, misc_ref, tcs_ref, tws_ref,
     pos_ref, w1_ref, w2_ref, gkc_ref, avg_ref, o_ref, xk_scr, xv_scr, m_scr, l_scr, acc_scr) = refs[PAGES_PER_STEP:]
    s = pl.program_id(1)
    nstep = pl.num_programs(1)
    nsel = PAST_LEN // SEL_BLOCK
    ncmp = PAST_LEN // CMP_BLOCK
    npage = PAST_LEN // PAGE
    qm = qm_ref[0]
    qb = qm.astype(BF16)
    lane = lax.broadcasted_iota(jnp.int32, (NSA_HEADS, LANES), 1)
    row = lax.broadcasted_iota(jnp.int32, (NSA_HEADS, LANES), 0)
    lo = lane < HEAD_DIM

    @pl.when(s == 0)
    def _():
        m_scr[...] = jnp.zeros(m_scr.shape, F32)
        l_scr[...] = jnp.zeros(l_scr.shape, F32)

    for j in range(PAGES_PER_STEP):
        pg = s * PAGES_PER_STEP + j
        kc = pages[j][0, 0:LANES, :].T
        vc = pages[j][0, LANES:2 * LANES, :].T
        for b4 in range(PAGE // CMP_BLOCK):
            r0 = pl.multiple_of((pg * (PAGE // CMP_BLOCK) + b4) * CMP_PITCH, 8)
            xk_scr[pl.ds(r0, CMP_BLOCK), :] = kc[b4 * CMP_BLOCK:(b4 + 1) * CMP_BLOCK, :]
            xv_scr[pl.ds(r0, CMP_BLOCK), :] = vc[b4 * CMP_BLOCK:(b4 + 1) * CMP_BLOCK, :]

    ks_all = jnp.concatenate([pages[t][0, 256:384, :].astype(BF16) for t in range(PAGES_PER_STEP)], axis=1)
    vs_all = jnp.concatenate([pages[t][0, 384:512, :].astype(BF16) for t in range(PAGES_PER_STEP)], axis=1)
    raw = _dot(qb, ks_all)
    far = jnp.broadcast_to(misc_ref[:, 1:2], (NSA_HEADS, PAGE))
    m_new, l_new = m_scr[...], l_scr[...]
    b0 = 2 * s * PAGES_PER_STEP
    zeros = jnp.zeros((NSA_HEADS, PAGE), F32)
    p_rows = []
    for j in range(PAGES_PER_STEP):
        sc = raw[:, j * PAGE:(j + 1) * PAGE]
        sc = sc + (jnp.where(s == nstep - 1, tsn_ref[...], far) if j == PAGES_PER_STEP - 1 else far)
        m0 = jnp.max(jnp.where(lo, sc, NEG), axis=1, keepdims=True)
        m1 = jnp.max(jnp.where(lo, NEG, sc), axis=1, keepdims=True)
        p = jnp.exp(sc - jnp.where(lo, m0, m1))
        p0, p1 = jnp.where(lo, p, 0.0), jnp.where(lo, 0.0, p)
        l0 = jnp.sum(p0, axis=1, keepdims=True)
        l1 = jnp.sum(p1, axis=1, keepdims=True)
        bj = b0 + 2 * j
        m_new = jnp.where(lane == bj, m0, jnp.where(lane == bj + 1, m1, m_new))
        l_new = jnp.where(lane == bj, l0, jnp.where(lane == bj + 1, l1, l_new))
        for ph in (p0, p1):
            p_rows.append(jnp.concatenate([ph if t == j else zeros for t in range(PAGES_PER_STEP)], axis=1))
    m_scr[...] = m_new
    l_scr[...] = l_new
    acc = _dot_nt(jnp.concatenate(p_rows, axis=0).astype(BF16), vs_all)
    for b in range(2 * PAGES_PER_STEP):
        acc_scr[b0 + b] = acc[b * NSA_HEADS:(b + 1) * NSA_HEADS]

    @pl.when(s == nstep - 1)
    def _():
        ck, cv = _compress_tokens(lambda r: xk_scr[pl.ds(r, ncmp, stride=CMP_PITCH), :],
                                  lambda r: xv_scr[pl.ds(r, ncmp, stride=CMP_PITCH), :], pos_ref, w1_ref, w2_ref)
        ck = ck * lax.rsqrt(_group_mean_sq(ck, avg_ref[...]) + EPS) * gkc_ref[...]
        sc = _dot_nt(qb, ck.astype(BF16)) + tcs_ref[...]
        m = jnp.maximum(jnp.max(sc, axis=1, keepdims=True), M_INIT)
        pc = jnp.exp(sc - m)
        pc = pc / jnp.maximum(jnp.sum(pc, axis=1, keepdims=True), 1e-30)
        o_cmp = _dot(pc.astype(BF16), cv.astype(BF16))
        g0 = pc[0:1] + pc[1:2] + pc[2:3] + pc[3:4]
        g1 = pc[4:5] + pc[5:6] + pc[6:7] + pc[7:8]
        rowc = lax.broadcasted_iota(jnp.int32, (NSA_HEADS, ncmp), 0)
        imp = jnp.where(rowc < NSA_GROUP, g0, g1)
        pr = lax.broadcasted_iota(jnp.int32, (ncmp, LANES), 0)
        pc_ = lax.broadcasted_iota(jnp.int32, (ncmp, LANES), 1)
        pair = jnp.where(pr // (SEL_BLOCK // CMP_BLOCK) == pc_, 1.0, 0.0)
        impb = _dot(imp, pair, precision=HIGHEST)
        own = PAST_LEN // SEL_BLOCK
        sel = _rank_lt(jnp.where(lane < own, impb, -jnp.inf), lane, nsel, SEL_TOPK) & (lane < own)
        s_self = jnp.sum(qm * ksn_ref[0], axis=1, keepdims=True) + misc_ref[:, 0:1]
        o_sel = _merge_blocks(sel, m_scr[...], l_scr[...], acc_scr, nsel, s_self, vsn_ref[0])
        kw_t = win_ref[0, 0:LANES, :].astype(BF16)
        vw_t = win_ref[0, LANES:2 * LANES, :].astype(BF16)
        sw = _dot(qb, kw_t) + tws_ref[...]
        sw_self = jnp.sum(qm * kwn_ref[0], axis=1, keepdims=True) + misc_ref[:, 0:1]
        mw = jnp.maximum(jnp.max(sw, axis=1, keepdims=True), sw_self)
        pw = jnp.exp(sw - mw)
        pw_self = jnp.exp(sw_self - mw)
        o_win = ((_dot_nt(pw.astype(BF16), vw_t) + pw_self * vwn_ref[0])
                 / (jnp.sum(pw, axis=1, keepdims=True) + pw_self))
        gt = jnp.broadcast_to(g_ref[0], (NSA_HEADS, LANES))
        o8 = (_col(gt, lane, 3 * row) * o_cmp + _col(gt, lane, 3 * row + 1) * o_sel
              + _col(gt, lane, 3 * row + 2) * o_win)
        lane1 = lax.broadcasted_iota(jnp.int32, (1, LANES), 1)
        tiles = []
        for t in range(NSA_HEADS // 2):
            ha, hb = 2 * t, 2 * t + 1
            ra = o8[ha:ha + 1, :]
            rb = o8[hb:hb + 1, :]
            if ha // NSA_GROUP == 1:
                ra = pltpu.roll(ra, HEAD_DIM, 1)
            if hb // NSA_GROUP == 0:
                rb = pltpu.roll(rb, HEAD_DIM, 1)
            tiles.append(jnp.where(lane1 < HEAD_DIM, ra, rb))
        o_ref[0] = jnp.concatenate(tiles, axis=1)


def _nsa_sample(page_table, cache, qmat, ksn, vsn, kwn, vwn, win, gates, tsn, misc, tcs, tws,
                pos, w1bd, w2bd, gkc, avg):
    nb, npages = page_table.shape
    nstep = npages // PAGES_PER_STEP
    nsel = PAST_LEN // SEL_BLOCK

    def page_spec(j):
        return pl.BlockSpec((1, 512, PAGE), lambda b, s, pt: (pt[b, s * PAGES_PER_STEP + j], 0, 0))

    per_b = lambda shape: pl.BlockSpec((1,) + shape, lambda b, s, pt: (b, 0, 0))
    const = lambda shape: pl.BlockSpec(shape, lambda b, s, pt: (0,) * len(shape))
    cmp_rows = PAST_LEN // CMP_BLOCK * CMP_PITCH
    grid_spec = pltpu.PrefetchScalarGridSpec(
        num_scalar_prefetch=1,
        grid=(nb, nstep),
        in_specs=[page_spec(j) for j in range(PAGES_PER_STEP)]
        + [per_b((NSA_HEADS, LANES)), per_b((1, LANES)), per_b((1, LANES)), per_b((1, LANES)), per_b((1, LANES)),
           per_b((256, WINDOW)), per_b((1, LANES)),
           const((NSA_HEADS, PAGE)), const((NSA_HEADS, LANES)), const((NSA_HEADS, PAST_LEN // CMP_BLOCK)),
           const((NSA_HEADS, WINDOW)),
           const((CMP_BLOCK, 256)), const((2, CMP_BLOCK // 2, 256, 256)), const((2, 256, LANES)),
           const((1, LANES)), const((LANES, LANES))],
        out_specs=per_b((1, 512)),
        scratch_shapes=[pltpu.VMEM((cmp_rows, LANES), F32), pltpu.VMEM((cmp_rows, LANES), F32),
                        pltpu.VMEM((NSA_HEADS, LANES), F32),
                        pltpu.VMEM((NSA_HEADS, LANES), F32), pltpu.VMEM((nsel, NSA_HEADS, LANES), F32)],
    )
    return pl.pallas_call(
        _nsa_sample_kernel,
        grid_spec=grid_spec,
        out_shape=jax.ShapeDtypeStruct((nb, 1, 512), F32),
        compiler_params=_cparams(("arbitrary", "arbitrary")),
        name="nsa_sample",
    )(page_table, *([cache] * PAGES_PER_STEP), qmat, ksn, vsn, kwn, vwn, win, gates, tsn, misc, tcs, tws,
      pos, w1bd, w2bd, gkc, avg)


def _outproj_kernel(x_ref, mod_ref, om_ref, on_ref, w_ref, o_ref):
    y = _dot(om_ref[0].astype(BF16), w_ref[0:512, :]) + _dot(on_ref[0].astype(BF16), w_ref[512:1024, :])
    o_ref[0] = x_ref[0] + mod_ref[0][:, 2 * D:3 * D] * y


def _outproj(x, mod, o_m, o_n, w, tm):
    b, t, _ = x.shape
    tmod = mod.shape[1]
    row = lambda width: pl.BlockSpec((1, tm, width), lambda i, j: (i, j, 0))
    return pl.pallas_call(
        _outproj_kernel,
        grid=(b, t // tm),
        in_specs=[row(D), pl.BlockSpec((1, tmod, 3 * D), lambda i, j: (i, 0, 0)), row(512), row(512),
                  pl.BlockSpec((D, D), lambda i, j: (0, 0))],
        out_specs=row(D),
        out_shape=jax.ShapeDtypeStruct((b, t, D), F32),
        compiler_params=_cparams(("arbitrary", "arbitrary")),
        name="attn_outproj",
    )(x, mod, o_m, o_n, w)


def _mlp_kernel(x_ref, mod_ref, g_ref, w1_ref, w2_ref, o_ref, h_scr, acc_scr):
    kf = pl.program_id(2)

    @pl.when(kf == 0)
    def _():
        mod = mod_ref[0]
        h_scr[...] = _modulate(x_ref[0], g_ref[...], mod[:, 0:D], mod[:, D:2 * D]).astype(BF16)
        acc_scr[...] = jnp.zeros(acc_scr.shape, F32)

    a = jnp.square(jnp.maximum(_dot(h_scr[...], w1_ref[...]), 0.0))
    acc_scr[...] += _dot(a.astype(BF16), w2_ref[...])

    @pl.when(kf == pl.num_programs(2) - 1)
    def _():
        o_ref[0] = x_ref[0] + mod_ref[0][:, 2 * D:3 * D] * acc_scr[...]


def _mlp(x, mod, g, w1, w2, tm, tf):
    b, t, _ = x.shape
    tmod = mod.shape[1]
    return pl.pallas_call(
        _mlp_kernel,
        grid=(b, t // tm, D_FF // tf),
        in_specs=[pl.BlockSpec((1, tm, D), lambda i, j, kf: (i, j, 0)),
                  pl.BlockSpec((1, tmod, 3 * D), lambda i, j, kf: (i, 0, 0)),
                  pl.BlockSpec((1, D), lambda i, j, kf: (0, 0)),
                  pl.BlockSpec((D, tf), lambda i, j, kf: (0, kf)),
                  pl.BlockSpec((tf, D), lambda i, j, kf: (kf, 0))],
        out_specs=pl.BlockSpec((1, tm, D), lambda i, j, kf: (i, j, 0)),
        out_shape=jax.ShapeDtypeStruct((b, t, D), F32),
        scratch_shapes=[pltpu.VMEM((tm, D), BF16), pltpu.VMEM((tm, D), F32)],
        compiler_params=_cparams(("arbitrary", "arbitrary", "arbitrary")),
        name="mlp",
    )(x, mod, g, w1, w2)


def _s5_disc_kernel(are_ref, aim_ref, ldt_ref, bre_ref, bim_ref, abre_ref, abim_ref, bbre_ref, bbim_ref):
    a_re, a_im = are_ref[...], aim_ref[...]
    dt = jnp.exp(ldt_ref[...])
    decay = jnp.exp(dt * a_re)
    ab_re, ab_im = decay * jnp.cos(dt * a_im), decay * jnp.sin(dt * a_im)
    den = a_re * a_re + a_im * a_im
    f_re = ((ab_re - 1) * a_re + ab_im * a_im) / den
    f_im = (ab_im * a_re - (ab_re - 1) * a_im) / den
    br, bi = bre_ref[...], bim_ref[...]
    abre_ref[...] = ab_re
    abim_ref[...] = ab_im
    bbre_ref[...] = f_re * br - f_im * bi
    bbim_ref[...] = f_re * bi + f_im * br


def _s5_discretize(a_re, a_im, log_dt, b_re, b_im):
    rep = lambda a: jnp.repeat(a, S5_GROUP_CH, axis=1)
    shp = jax.ShapeDtypeStruct((S5_GROUPS, S5_STATE * S5_GROUP_CH), F32)
    ldt = jnp.broadcast_to(log_dt[:, None], (S5_GROUPS, S5_STATE * S5_GROUP_CH))
    flat = lambda a: a.reshape(S5_GROUPS, S5_STATE * S5_GROUP_CH)
    ab_re, ab_im, bb_re, bb_im = pl.pallas_call(
        _s5_disc_kernel, out_shape=[shp] * 4, name="s5_discretize",
    )(rep(a_re), rep(a_im), ldt, flat(b_re), flat(b_im))
    unrep = lambda a: a[:, ::S5_GROUP_CH]
    unflat = lambda a: a.reshape(S5_GROUPS, S5_STATE, S5_GROUP_CH)
    return unrep(ab_re), unrep(ab_im), unflat(bb_re), unflat(bb_im)


def _modulate_tm_kernel(x_ref, mod_ref, g_ref, o_ref):
    mod = mod_ref[0]
    o_ref[...] = _modulate(x_ref[0], g_ref[...], mod[:, 0:D], mod[:, D:2 * D])


def _modulate_time_major(x, mod, g, tl):
    b, t, _ = x.shape
    tmod = mod.shape[1]
    return pl.pallas_call(
        _modulate_tm_kernel,
        grid=(b, t // tl),
        in_specs=[pl.BlockSpec((1, tl, D), lambda i, j: (i, j, 0)),
                  pl.BlockSpec((1, tmod, 3 * D), lambda i, j: (i, 0, 0)),
                  pl.BlockSpec((1, D), lambda i, j: (0, 0))],
        out_specs=pl.BlockSpec((tl, D), lambda i, j: (j, i)),
        out_shape=jax.ShapeDtypeStruct((t, b * D), F32),
        compiler_params=_cparams(("arbitrary", "arbitrary")),
        name="s5_modulate",
    )(x, mod, g)


S5_CB = 256
S5_NS = S5_CB // S5_GROUP_CH * S5_STATE


def _s5_scan_kernel(h_ref, wb_ref, wc_ref, ar_ref, ai_ref, d_ref, h0_ref, y_ref, so_ref, xs_scr, st_scr, *, tl, r):
    i = pl.program_id(1)

    @pl.when(i == 0)
    def _():
        st_scr[...] = h0_ref[...]

    u = h_ref[...]
    xs_scr[...] = _dot(u.astype(BF16), wb_ref[0])
    ar = jnp.broadcast_to(ar_ref[0], (r, S5_NS))
    ai = jnp.broadcast_to(ai_ref[0], (r, S5_NS))

    def step(t, carry):
        xr, xi = carry
        r0 = pl.multiple_of(t * r, r)
        nr = ar * xr - ai * xi + xs_scr[pl.ds(r0, r), 0:S5_NS]
        ni = ar * xi + ai * xr + xs_scr[pl.ds(r0, r), S5_NS:2 * S5_NS]
        xs_scr[pl.ds(r0, r), 0:S5_NS] = nr
        xs_scr[pl.ds(r0, r), S5_NS:2 * S5_NS] = ni
        return nr, ni

    xr, xi = lax.fori_loop(0, tl, step, (st_scr[0], st_scr[1]))
    st_scr[0] = xr
    st_scr[1] = xi
    y_ref[...] = _dot(xs_scr[...].astype(BF16), wc_ref[0]) + d_ref[...] * u

    @pl.when(i == pl.num_programs(1) - 1)
    def _():
        so_ref[...] = st_scr[...]


def _s5_scan(h_tm, wb, wc, ar, ai, d_skip, h0, r, tl):
    rows = h_tm.shape[0]
    nj = D // S5_CB
    return pl.pallas_call(
        functools.partial(_s5_scan_kernel, tl=tl, r=r),
        grid=(nj, rows // (tl * r)),
        in_specs=[pl.BlockSpec((tl * r, S5_CB), lambda j, i: (i, j)),
                  pl.BlockSpec((1, S5_CB, 2 * S5_NS), lambda j, i: (j, 0, 0)),
                  pl.BlockSpec((1, 2 * S5_NS, S5_CB), lambda j, i: (j, 0, 0)),
                  pl.BlockSpec((1, 1, S5_NS), lambda j, i: (j, 0, 0)),
                  pl.BlockSpec((1, 1, S5_NS), lambda j, i: (j, 0, 0)),
                  pl.BlockSpec((1, S5_CB), lambda j, i: (0, j)),
                  pl.BlockSpec((2, r, S5_NS), lambda j, i: (0, 0, j))],
        out_specs=[pl.BlockSpec((tl * r, S5_CB), lambda j, i: (i, j)),
                   pl.BlockSpec((2, r, S5_NS), lambda j, i: (0, 0, j))],
        out_shape=[jax.ShapeDtypeStruct((rows, D), F32), jax.ShapeDtypeStruct((2, r, S5_GROUPS * S5_STATE), F32)],
        scratch_shapes=[pltpu.VMEM((tl * r, 2 * S5_NS), F32), pltpu.VMEM((2, r, S5_NS), F32)],
        compiler_params=_cparams(("arbitrary", "arbitrary")),
        name="s5_scan",
    )(h_tm, wb, wc, ar, ai, d_skip, h0)


def _glu_kernel(y_ref, x_ref, mod_ref, w_ref, o_ref):
    z = _dot(jax.nn.gelu(y_ref[...]).astype(BF16), w_ref[...])
    o_ref[0] = x_ref[0] + mod_ref[0][:, 2 * D:3 * D] * (z[:, 0:D] * jax.nn.sigmoid(z[:, D:2 * D]))


def _glu_residual(y_tm, x, mod, w, tl):
    b, t, _ = x.shape
    tmod = mod.shape[1]
    return pl.pallas_call(
        _glu_kernel,
        grid=(b, t // tl),
        in_specs=[pl.BlockSpec((tl, D), lambda i, j: (j, i)),
                  pl.BlockSpec((1, tl, D), lambda i, j: (i, j, 0)),
                  pl.BlockSpec((1, tmod, 3 * D), lambda i, j: (i, 0, 0)),
                  pl.BlockSpec((D, 2 * D), lambda i, j: (0, 0))],
        out_specs=pl.BlockSpec((1, tl, D), lambda i, j: (i, j, 0)),
        out_shape=jax.ShapeDtypeStruct((b, t, D), F32),
        compiler_params=_cparams(("arbitrary", "arbitrary")),
        name="s5_glu",
    )(y_tm, x, mod, w)


def _s5_block_weights(bb_re, bb_im, c_re, c_im):
    nj, ng = D // S5_CB, S5_CB // S5_GROUP_CH
    eye = jnp.eye(ng, dtype=F32)

    def wb_part(bb):
        t = bb.reshape(nj, ng, S5_STATE, S5_GROUP_CH).transpose(0, 1, 3, 2)
        return jnp.einsum("jgcn,gh->jgchn", t, eye).reshape(nj, S5_CB, S5_NS)

    def wc_part(c):
        t = c.reshape(nj, ng, S5_GROUP_CH, S5_STATE).transpose(0, 1, 3, 2)
        return jnp.einsum("jgnc,gh->jgnhc", t, eye).reshape(nj, S5_NS, S5_CB)

    wb = jnp.concatenate([wb_part(bb_re), wb_part(bb_im)], axis=2).astype(BF16)
    wc = jnp.concatenate([wc_part(c_re), -wc_part(c_im)], axis=1).astype(BF16)
    return wb, wc


S5_TL = 32
S5_PITCH = 40
S5_NSLAB = 2 * S5_GROUPS * S5_STATE // LANES


def _s5_fused_kernel(x_ref, mod_ref, g_ref, wb_ref, wc_ref, ar_ref, ai_ref, d_ref, h0_ref, wg_ref,
                     o_ref, so_ref, xs_scr, st_scr):
    i = pl.program_id(0)
    nb = x_ref.shape[0]
    nj = D // S5_CB
    tiles = S5_NS // LANES

    @pl.when(i == 0)
    def _():
        xs_scr[...] = jnp.zeros(xs_scr.shape, F32)
        for j in range(nj):
            for comp in range(2):
                for q in range(tiles):
                    st_scr[(2 * j + comp) * tiles + q] = h0_ref[comp, :, j * S5_NS + q * LANES:j * S5_NS + (q + 1) * LANES]

    h = jnp.concatenate([_modulate(x_ref[b], g_ref[...], mod_ref[b][:, 0:D], mod_ref[b][:, D:2 * D])
                         for b in range(nb)], axis=0)
    hb = h.astype(BF16)
    ys = []
    for j in range(nj):
        bu = _dot(hb[:, j * S5_CB:(j + 1) * S5_CB], wb_ref[j])
        base = 2 * j * tiles
        for lt in range(2 * tiles):
            for b in range(nb):
                xs_scr[base + lt, b * S5_PITCH:b * S5_PITCH + S5_TL, :] = bu[b * S5_TL:(b + 1) * S5_TL,
                                                                             lt * LANES:(lt + 1) * LANES]
        ar = [jnp.broadcast_to(ar_ref[j][:, q * LANES:(q + 1) * LANES], (nb, LANES)) for q in range(tiles)]
        ai = [jnp.broadcast_to(ai_ref[j][:, q * LANES:(q + 1) * LANES], (nb, LANES)) for q in range(tiles)]

        def step(t, carry):
            new = []
            for q in range(tiles):
                xr, xi = carry[2 * q], carry[2 * q + 1]
                rows = pl.ds(t, nb, stride=S5_PITCH)
                nr = ar[q] * xr - ai[q] * xi + xs_scr[base + q, rows, :]
                ni = ar[q] * xi + ai[q] * xr + xs_scr[base + tiles + q, rows, :]
                xs_scr[base + q, rows, :] = nr
                xs_scr[base + tiles + q, rows, :] = ni
                new += [nr, ni]
            return tuple(new)

        init = []
        for q in range(tiles):
            init += [st_scr[base + q], st_scr[base + tiles + q]]
        fin = lax.fori_loop(0, S5_TL, step, tuple(init))
        for q in range(tiles):
            st_scr[base + q] = fin[2 * q]
            st_scr[base + tiles + q] = fin[2 * q + 1]
        states = jnp.concatenate([xs_scr[base + lt] for lt in range(2 * tiles)], axis=1)
        ys.append(_dot(states.astype(BF16), wc_ref[j]))
    y_all = jnp.concatenate(ys, axis=1)
    y = jnp.concatenate([y_all[b * S5_PITCH:b * S5_PITCH + S5_TL] for b in range(nb)], axis=0) + d_ref[...] * h
    z = _dot(jax.nn.gelu(y).astype(BF16), wg_ref[...])
    out = z[:, 0:D] * jax.nn.sigmoid(z[:, D:2 * D])
    for b in range(nb):
        o_ref[b] = x_ref[b] + mod_ref[b][:, 2 * D:3 * D] * out[b * S5_TL:(b + 1) * S5_TL]

    @pl.when(i == pl.num_programs(0) - 1)
    def _():
        for j in range(nj):
            for comp in range(2):
                for q in range(tiles):
                    so_ref[comp, :, j * S5_NS + q * LANES:j * S5_NS + (q + 1) * LANES] = st_scr[(2 * j + comp) * tiles + q]


def _s5_layer(x, mod, g, wb, wc, ar, ai, d_skip, h0, w_glu):
    b, t, _ = x.shape
    ns = S5_GROUPS * S5_STATE
    const = lambda shape: pl.BlockSpec(shape, lambda i: (0,) * len(shape))
    x_new, st = pl.pallas_call(
        _s5_fused_kernel,
        grid=(t // S5_TL,),
        in_specs=[pl.BlockSpec((b, S5_TL, D), lambda i: (0, i, 0)),
                  const((b, 1, 3 * D)), const((1, D)),
                  const((D // S5_CB, S5_CB, 2 * S5_NS)), const((D // S5_CB, 2 * S5_NS, S5_CB)),
                  const((D // S5_CB, 1, S5_NS)), const((D // S5_CB, 1, S5_NS)),
                  const((1, D)), const((2, b, ns)), const((D, 2 * D))],
        out_specs=[pl.BlockSpec((b, S5_TL, D), lambda i: (0, i, 0)), const((2, b, ns))],
        out_shape=[jax.ShapeDtypeStruct((b, t, D), F32), jax.ShapeDtypeStruct((2, b, ns), F32)],
        scratch_shapes=[pltpu.VMEM((S5_NSLAB, b * S5_PITCH, LANES), F32), pltpu.VMEM((S5_NSLAB, b, LANES), F32)],
        compiler_params=_cparams(("arbitrary",)),
        name="s5_fused",
    )(x, mod, g, wb, wc, ar, ai, d_skip, h0.reshape(b, 2, ns).transpose(1, 0, 2), w_glu)
    return x_new, st.transpose(1, 0, 2).reshape(b, 2, S5_GROUPS, S5_STATE)


def _dist_tiles():
    r = jnp.arange(TQ, dtype=jnp.int32)[:, None]
    c = jnp.arange(TQ, dtype=jnp.int32)[None, :]
    d0 = r - c
    edge = 2 * TQ + r - c
    return jnp.concatenate([d0, TQ + d0, 2 * TQ + d0, jnp.where(edge <= WINDOW, edge, -1),
                            jnp.full((TQ, TQ), -1, jnp.int32)], axis=0)


def _dist_cmp(seq):
    q = jnp.arange(seq, dtype=jnp.int32)[:, None]
    n = jnp.arange(LANES, dtype=jnp.int32)[None, :]
    return jnp.where(n < seq // CMP_BLOCK, q - ((n + 1) * CMP_BLOCK - 1), -1)


_SAMPLE_TABLE_SIZES = (LANES, MOBA_BLOCK, PAGE, PAST_LEN // CMP_BLOCK, WINDOW)


def _dist_sample():
    ar = lambda n: jnp.arange(n, dtype=jnp.int32)
    misc = jnp.zeros((LANES,), jnp.int32).at[1].set(MAX_DISTANCE * 4)
    moba = MOBA_BLOCK - ar(MOBA_BLOCK)
    sel = PAGE - ar(PAGE)
    cmp_ = PAST_LEN - ((ar(PAST_LEN // CMP_BLOCK) + 1) * CMP_BLOCK - 1)
    win = WINDOW - ar(WINDOW)
    return jnp.concatenate([misc, moba, sel, cmp_, win])[None, :]


def _block_diag2(w):
    z = jnp.zeros_like(w)
    return jnp.concatenate([jnp.concatenate([w, z], axis=-1), jnp.concatenate([z, w], axis=-1)], axis=-2)


def kernel(x_prompt, x_sample, cache_moba_kv, cache_nsa_kv, state_nsa_win, state_s5, page_table, c_prompt, c_sample, rel_bias, attn_norm_g, attn_ada_w, attn_ada_b, attn_w_in, attn_qk_g, nsa_cmp_pos, nsa_cmp_w1, nsa_cmp_w2, attn_w_out, ssm_norm_g, ssm_ada_w, ssm_ada_b, s5_a_re, s5_a_im, s5_log_dt, s5_b_re, s5_b_im, s5_c_re, s5_c_im, s5_d, s5_w_glu, mlp_norm_g, mlp_ada_w, mlp_ada_b, mlp_w1, mlp_w2):
    bp, seq, _ = x_prompt.shape
    bs = x_sample.shape[0]
    assert seq % TQ == 0 and x_sample.shape[1] == 1
    n_pool = cache_moba_kv.shape[1]

    c_all = jnp.concatenate([c_prompt, c_sample], axis=0)
    split_mod = lambda m: (m[:bp, None, :], m[None, bp:, :])
    mod_attn = _adaln(c_all, attn_ada_w, attn_ada_b)
    mod_ssm = _adaln(c_all, ssm_ada_w, ssm_ada_b)
    mod_mlp = _adaln(c_all, mlp_ada_w, mlp_ada_b)

    xp = x_prompt
    xs = x_sample.reshape(1, bs, D)

    tb = _bias_table(rel_bias, _dist_tiles(), LOG2E).reshape(2 * MOBA_HEADS, N_BIAS_TILES, TQ, TQ)
    tc = _bias_table(rel_bias, _dist_cmp(seq), LOG2E)[MOBA_HEADS:]
    ts = _bias_table(rel_bias, _dist_sample())[:, 0, :]
    offs = [0]
    for size in _SAMPLE_TABLE_SIZES:
        offs.append(offs[-1] + size)
    part = lambda heads, t: ts[heads, offs[t]:offs[t + 1]]
    hm, hn = slice(0, MOBA_HEADS), slice(MOBA_HEADS, 2 * MOBA_HEADS)
    misc_m, misc_n, tsb, tsn, tcs, tws = part(hm, 0), part(hn, 0), part(hm, 1), part(hn, 2), part(hn, 3), part(hn, 4)

    w_in = jnp.pad(attn_w_in[0], ((0, 0), (0, IN_COLS_PAD - IN_COLS))).astype(BF16)
    qkg_t = jnp.pad(jnp.tile(attn_qk_g[0], (1, 2)), ((0, 2), (0, 0)))
    lr = jnp.arange(LANES)
    avg = jnp.where(lr[:, None] // HEAD_DIM == lr[None, :] // HEAD_DIM, 1.0 / HEAD_DIM, 0.0).astype(BF16)
    g_attn = attn_norm_g[0][None, :]
    w_out = attn_w_out[0].astype(BF16)
    pos = jnp.concatenate([nsa_cmp_pos[0, 0], nsa_cmp_pos[0, 0], nsa_cmp_pos[0, 1], nsa_cmp_pos[0, 1]], axis=1)
    w1bd = _block_diag2(nsa_cmp_w1[0].reshape(2, CMP_BLOCK, HEAD_DIM, CMP_HIDDEN)).astype(BF16)
    w1bd = w1bd.reshape(2, CMP_BLOCK // 2, 256, 256)
    w2bd = _block_diag2(nsa_cmp_w2[0]).astype(BF16)
    gkc = qkg_t[3:4]

    mp_attn, ms_attn = split_mod(mod_attn[0])
    mq, mkv, nq, nkv, wkv, gates, mkv_t, nkv_t = _attn_proj(xp, mp_attn, g_attn, w_in, qkg_t, avg, 512,
                                                            SCALE * LOG2E, page_major=True)
    o_moba = _moba_prompt(mq, mkv, tb[:MOBA_HEADS])
    kcmp, vcmp = _cmp_prompt(nkv, pos, w1bd, w2bd, gkc, avg)
    o_nsa = _nsa_prompt(nq, nkv, wkv, kcmp, vcmp, gates, tb[MOBA_HEADS:], tc)
    xp = _outproj(xp, mp_attn, o_moba, o_nsa, w_out, 512)
    npg = seq // PAGE
    moba_p = mkv_t.reshape(1, bp, npg, 2, MOBA_HEADS, HEAD_DIM, PAGE).transpose(0, 1, 2, 6, 3, 4, 5)
    nsa_p = nkv_t.reshape(1, bp, npg, 4, 2, HEAD_DIM, PAGE).transpose(0, 1, 2, 6, 3, 4, 5)
    win_p = wkv[:, seq - min(WINDOW, seq):].reshape(1, bp, min(WINDOW, seq), 2, 2, HEAD_DIM)
    mq_s, mkv_s, nq_s, nkv_s, wkv_s, gates_s = _attn_proj(xs, ms_attn, g_attn, w_in, qkg_t, avg, bs, SCALE)
    cache_m_t = cache_moba_kv.transpose(0, 1, 3, 4, 5, 2).reshape(n_pool, 2, 512, PAGE)
    cache_n_t = cache_nsa_kv.transpose(0, 1, 3, 4, 5, 2).reshape(n_pool, 512, PAGE)
    win_t = state_nsa_win[0].transpose(0, 2, 3, 4, 1).reshape(bs, 256, WINDOW)
    lw = jnp.arange(512)
    qmat_m = jnp.where(lw[None, None, :] // HEAD_DIM == jnp.arange(MOBA_HEADS)[None, :, None], mq_s[0][:, None, :], 0.0)
    col3 = lambda a, lo, width: a[0][:, None, lo:lo + width]
    o_moba_s = _moba_sample(page_table, cache_m_t, qmat_m, col3(mkv_s, 0, 512), col3(mkv_s, 512, 512), tsb, misc_m)
    nq4 = nq_s[0].reshape(bs, NSA_HEADS, HEAD_DIM)
    kvh = jnp.arange(NSA_HEADS) // NSA_GROUP
    qmat_n = jnp.concatenate([jnp.where(kvh[None, :, None] == 0, nq4, 0.0),
                              jnp.where(kvh[None, :, None] == 1, nq4, 0.0)], axis=2)
    o_nsa_s = _nsa_sample(page_table, cache_n_t, qmat_n,
                          col3(nkv_s, 256, LANES), col3(nkv_s, 384, LANES), col3(wkv_s, 0, LANES),
                          col3(wkv_s, 128, LANES), win_t, gates_s.reshape(bs, 1, LANES),
                          tsn, misc_n, tcs, tws, pos, w1bd, w2bd, gkc, avg)
    xs = _outproj(xs, ms_attn, o_moba_s.reshape(1, bs, 512), o_nsa_s.reshape(1, bs, 512), w_out, bs)
    moba_s = mkv_s.reshape(1, bs, 1, 2, MOBA_HEADS, HEAD_DIM)
    nsa_s = nkv_s.reshape(1, bs, 1, 4, 2, HEAD_DIM)
    win_s = jnp.concatenate([state_nsa_win[0][:, 1:], wkv_s[0].reshape(bs, 1, 2, 2, HEAD_DIM)], axis=1)[None]

    w1_0, w2_0 = mlp_w1[0].astype(BF16), mlp_w2[0].astype(BF16)
    mp_mlp, ms_mlp = split_mod(mod_mlp[0])
    g_mlp0 = mlp_norm_g[0][None, :]
    xp = _mlp(xp, mp_mlp, g_mlp0, w1_0, w2_0, MLP_TM, MLP_TF)
    xs = _mlp(xs, ms_mlp, g_mlp0, w1_0, w2_0, bs, MLP_TF)

    ab_re, ab_im, bb_re, bb_im = _s5_discretize(s5_a_re[0], s5_a_im[0], s5_log_dt[0], s5_b_re[0], s5_b_im[0])
    wb, wc = _s5_block_weights(bb_re, bb_im, s5_c_re[0], s5_c_im[0])
    nj = D // S5_CB
    ar = ab_re.reshape(nj, 1, S5_NS)
    ai = ab_im.reshape(nj, 1, S5_NS)
    g_ssm = ssm_norm_g[0][None, :]
    d_skip = s5_d[0][None, :]
    w_glu = s5_w_glu[0].astype(BF16)
    mp_ssm, ms_ssm = split_mod(mod_ssm[0])
    xp, st_p = _s5_layer(xp, mp_ssm, g_ssm, wb, wc, ar, ai, d_skip,
                         jnp.zeros((bp, 2, S5_GROUPS, S5_STATE), F32), w_glu)
    h_s = _modulate_time_major(xs, ms_ssm, g_ssm, bs)
    y_s, st_s = _s5_scan(h_s, wb, wc, ar, ai, d_skip,
                         state_s5[0].reshape(bs, 2, S5_GROUPS * S5_STATE).transpose(1, 0, 2), bs, 1)
    xs = _glu_residual(y_s, xs, ms_ssm, w_glu, bs)
    st_s = st_s.transpose(1, 0, 2).reshape(bs, 2, S5_GROUPS, S5_STATE)

    w1_1, w2_1 = mlp_w1[1].astype(BF16), mlp_w2[1].astype(BF16)
    mp_mlp, ms_mlp = split_mod(mod_mlp[1])
    g_mlp1 = mlp_norm_g[1][None, :]
    xp = _mlp(xp, mp_mlp, g_mlp1, w1_1, w2_1, MLP_TM, MLP_TF)
    xs = _mlp(xs, ms_mlp, g_mlp1, w1_1, w2_1, bs, MLP_TF)

    return (xp, xs.reshape(bs, 1, D), moba_p, moba_s, nsa_p, nsa_s, win_p, win_s, st_p[None], st_s[None])
```

```python
import functools
import math

import jax
import jax.numpy as jnp
import numpy as np
from jax import lax
from jax.experimental import pallas as pl
from jax.experimental.pallas import tpu as pltpu

F32 = jnp.float32
BF16 = jnp.bfloat16
HIGHEST = lax.Precision.HIGHEST

D = 1024
HEAD_DIM = 64
MOBA_HEADS = 8
NSA_HEADS = 8
NSA_GROUP = 4
MOBA_BLOCK = 256
MOBA_TOPK = 3
CMP_BLOCK = 32
CMP_HIDDEN = 128
SEL_BLOCK = 64
SEL_TOPK = 16
WINDOW = 512
NUM_BUCKETS = 32
MAX_DISTANCE = 128
PAGE = 128
PAST_LEN = 8192
D_FF = 4 * D
S5_GROUPS = 64
S5_STATE = 64
S5_GROUP_CH = 16
IN_COLS = 3 * 512 + 512 + 6 * 128 + 3 * NSA_HEADS
IN_COLS_PAD = 23 * 128
EPS = 1e-6
SCALE = HEAD_DIM ** -0.5
LOG2E = math.log2(math.e)
LANES = 128
TQ = 256
N_BIAS_TILES = 5
CASE_TILES = 1
MLP_TM = 1024
MLP_TF = 1024
NEG = -1e30
M_INIT = -1e15
VMEM_LIMIT = 56 * 1024 * 1024

_NT = (((1,), (1,)), ((), ()))


def _cparams(sem):
    return pltpu.CompilerParams(dimension_semantics=sem, vmem_limit_bytes=VMEM_LIMIT)


def _dot(a, b, **kw):
    return jnp.dot(a, b, preferred_element_type=F32, **kw)


def _dot_nt(a, b, **kw):
    return lax.dot_general(a, b, _NT, preferred_element_type=F32, **kw)


def _modulate(x, g, shift, scale):
    ms = jnp.mean(x * x, axis=-1, keepdims=True)
    return x * lax.rsqrt(ms + EPS) * g * (1.0 + scale) + shift


def _group_mean_sq(z, avg):
    sq = z * z
    hi = sq.astype(BF16)
    lo = (sq - hi.astype(F32)).astype(BF16)
    return _dot(hi, avg) + _dot(lo, avg)


def _col(x, lane, idx):
    return jnp.sum(jnp.where(lane == idx, x, 0.0), axis=1, keepdims=True)


def _adaln_kernel(c_ref, w_ref, b_ref, o_ref):
    c = c_ref[...]
    s = c * jax.nn.sigmoid(c)
    o_ref[0] = _dot(s, w_ref[0], precision=HIGHEST) + b_ref[0]


def _adaln(c_all, w, b):
    nl, n = w.shape[0], c_all.shape[0]
    return pl.pallas_call(
        _adaln_kernel,
        grid=(nl, 3),
        in_specs=[pl.BlockSpec((n, D), lambda l, j: (0, 0)),
                  pl.BlockSpec((1, D, D), lambda l, j: (l, 0, j)),
                  pl.BlockSpec((1, 1, D), lambda l, j: (l, 0, j))],
        out_specs=pl.BlockSpec((1, n, D), lambda l, j: (l, 0, j)),
        out_shape=jax.ShapeDtypeStruct((nl, n, 3 * D), F32),
        compiler_params=_cparams(("arbitrary", "arbitrary")),
        name="adaln",
    )(c_all, w, b.reshape(nl, 1, 3 * D))


def _log_bucket_starts():
    max_exact = NUM_BUCKETS // 2
    n = np.arange(max_exact, 4 * MAX_DISTANCE, dtype=np.float32)
    large = max_exact + (np.log(n / np.float32(max_exact)) / np.float32(math.log(MAX_DISTANCE / max_exact))
                         * np.float32(NUM_BUCKETS - max_exact)).astype(np.int32)
    large = np.minimum(large, NUM_BUCKETS - 1)
    return [int(np.argmax(large >= b)) + max_exact for b in range(max_exact, NUM_BUCKETS)]


def _bias_kernel(rb_ref, d_ref, o_ref, *, scale, first_head):
    h = pl.program_id(0) + first_head
    dist = d_ref[...]
    n = jnp.maximum(dist, 0)
    max_exact = NUM_BUCKETS // 2
    acc = jnp.zeros(dist.shape, F32)
    for k in range(max_exact):
        acc = jnp.where(n == k, rb_ref[k, h], acc)
    for j, start in enumerate(_log_bucket_starts()):
        acc = jnp.where(n >= start, rb_ref[max_exact + j, h], acc)
    o_ref[0] = jnp.where(dist < 0, NEG, acc * scale)


def _bias_table(rel_bias, dist, scale=1.0, first_head=0, nh=None):
    r, c = dist.shape
    nh = rel_bias.shape[1] - first_head if nh is None else nh
    return pl.pallas_call(
        functools.partial(_bias_kernel, scale=scale, first_head=first_head),
        grid=(nh,),
        in_specs=[pl.BlockSpec(memory_space=pltpu.SMEM),
                  pl.BlockSpec((r, c), lambda h: (0, 0))],
        out_specs=pl.BlockSpec((1, r, c), lambda h: (h, 0, 0)),
        out_shape=jax.ShapeDtypeStruct((nh, r, c), F32),
        compiler_params=_cparams(("arbitrary",)),
        name="bias_table",
    )(rel_bias, dist)


def _proj_kernel(x_ref, mod_ref, g_ref, w_ref, qkg_ref, avg_ref,
                 mq_ref, mkv_ref, nq_ref, nkv_ref, wkv_ref, gt_ref, *page_major_refs, q_scale):
    x = x_ref[0]
    mod = mod_ref[0]
    h = _modulate(x, g_ref[...], mod[:, 0:D], mod[:, D:2 * D])
    z = _dot(h.astype(BF16), w_ref[...])
    avg = avg_ref[...]

    def normed(lo, gi):
        zs = z[:, lo:lo + LANES]
        return zs * lax.rsqrt(_group_mean_sq(zs, avg) + EPS) * qkg_ref[gi:gi + 1, :]

    for t in range(4):
        mq_ref[0, :, t * LANES:(t + 1) * LANES] = normed(t * LANES, 0) * q_scale
        mkv_ref[0, :, t * LANES:(t + 1) * LANES] = normed(512 + t * LANES, 1)
        nq_ref[0, :, t * LANES:(t + 1) * LANES] = normed(1536 + t * LANES, 2) * q_scale
    mkv_ref[0, :, 512:1024] = z[:, 1024:1536]
    nkv_ref[0, :, 0:256] = z[:, 2048:2304]
    nkv_ref[0, :, 256:384] = normed(2304, 4)
    nkv_ref[0, :, 384:512] = z[:, 2432:2560]
    wkv_ref[0, :, 0:128] = normed(2560, 5)
    wkv_ref[0, :, 128:256] = z[:, 2688:2816]
    gt_ref[0] = jax.nn.sigmoid(z[:, 2816:2944])
    if page_major_refs:
        mkv_t_ref, nkv_t_ref = page_major_refs
        for p in range(x.shape[0] // PAGE):
            mkv_t_ref[0, p] = mkv_ref[0, p * PAGE:(p + 1) * PAGE, :].T
            nkv_t_ref[0, p] = nkv_ref[0, p * PAGE:(p + 1) * PAGE, :].T


def _attn_proj(x, mod, g, w_pad, qkg_t, avg, tm, q_scale, page_major=False):
    b, t, _ = x.shape
    tmod = mod.shape[1]
    row = lambda width: pl.BlockSpec((1, tm, width), lambda i, j: (i, j, 0))
    shp = lambda width: jax.ShapeDtypeStruct((b, t, width), F32)
    out_specs = [row(512), row(1024), row(512), row(512), row(256), row(128)]
    out_shape = [shp(512), shp(1024), shp(512), shp(512), shp(256), shp(128)]
    if page_major:
        for width in (1024, 512):
            out_specs.append(pl.BlockSpec((1, tm // PAGE, width, PAGE), lambda i, j: (i, j, 0, 0)))
            out_shape.append(jax.ShapeDtypeStruct((b, t // PAGE, width, PAGE), F32))
    return pl.pallas_call(
        functools.partial(_proj_kernel, q_scale=q_scale),
        grid=(b, t // tm),
        in_specs=[row(D),
                  pl.BlockSpec((1, tmod, 3 * D), lambda i, j: (i, 0, 0)),
                  pl.BlockSpec((1, D), lambda i, j: (0, 0)),
                  pl.BlockSpec((D, IN_COLS_PAD), lambda i, j: (0, 0)),
                  pl.BlockSpec((8, LANES), lambda i, j: (0, 0)),
                  pl.BlockSpec((LANES, LANES), lambda i, j: (0, 0))],
        out_specs=out_specs,
        out_shape=out_shape,
        compiler_params=_cparams(("arbitrary", "arbitrary")),
        name="attn_proj",
    )(x, mod, g, w_pad, qkg_t, avg)


def _rank_rows(score, rowi, ncand):
    rank = jnp.zeros(score.shape, F32)
    for m in range(ncand):
        rm = score[m:m + 1, :]
        rank = rank + jnp.where(rm > score, 1.0, 0.0) + jnp.where((rm == score) & (m < rowi), 1.0, 0.0)
    return rank


def _columns_from_rows(x_t):
    pad = jnp.zeros((LANES - x_t.shape[0], x_t.shape[1]), F32)
    return jnp.concatenate([x_t, pad], axis=0).T


def _softmax_pv(pieces, v_all):
    m = pieces[0]
    for s in pieces[1:]:
        m = jnp.maximum(m, s)
    m = jnp.maximum(jnp.max(m, axis=1, keepdims=True), M_INIT)
    ps = [jnp.exp2(s - m) for s in pieces]
    tot = ps[0]
    for p in ps[1:]:
        tot = tot + p
    l = jnp.sum(tot, axis=1, keepdims=True)
    p_all = jnp.concatenate([p.astype(BF16) for p in ps], axis=1) if len(ps) > 1 else ps[0].astype(BF16)
    return _dot(p_all, v_all) / jnp.maximum(l, 1e-30)


def _moba_prompt_kernel(q_ref, k_ref, v_ref, t_ref, o_ref, km_scr, kb_scr, vb_scr):
    s_len = q_ref.shape[1]
    nblk = s_len // MOBA_BLOCK
    nq = s_len // TQ
    lane = lax.broadcasted_iota(jnp.int32, (TQ, LANES), 1)
    rowb = lax.broadcasted_iota(jnp.int32, (nblk, TQ), 0)
    km_scr[...] = jnp.zeros(km_scr.shape, F32)
    for n in range(nblk):
        km_scr[n:n + 1, :] = jnp.mean(k_ref[0, n * MOBA_BLOCK:(n + 1) * MOBA_BLOCK, :], axis=0, keepdims=True)
    kmean = km_scr[...]
    kb_scr[...] = k_ref[0].astype(BF16)
    vb_scr[...] = v_ref[0].astype(BF16)

    def qtile(i, _):
        r0 = pl.multiple_of(i * TQ, TQ)
        q2 = q_ref[0, pl.ds(r0, TQ), :]
        qbs, cbs = [], []
        for e in range(2):
            qe = jnp.where(lane // HEAD_DIM == e, q2, 0.0)
            gate_t = _dot_nt(kmean, qe, precision=HIGHEST)[0:nblk]
            gm = jnp.where(rowb < i, gate_t, -jnp.inf)
            sel = ((_rank_rows(gm, rowb, nblk) < MOBA_TOPK) & (rowb < i)) | (rowb == i)
            cbs.append(_columns_from_rows(jnp.where(sel, 0.0, NEG)))
            qbs.append(qe.astype(BF16))

        for c in range(1, nq // CASE_TILES + 1):
            @pl.when(i // CASE_TILES + 1 == c)
            def _():
                ntile = CASE_TILES * c
                kall = kb_scr[0:ntile * TQ, :]
                vall = vb_scr[0:ntile * TQ, :]
                outs = []
                for e in range(2):
                    s = _dot_nt(qbs[e], kall)
                    far_bias = t_ref[e, 2, 0:1, 0:1]
                    pieces = []
                    for n in range(ntile):
                        seg = s[:, n * TQ:(n + 1) * TQ]
                        if n >= ntile - CASE_TILES - 1:
                            pieces.append(seg + t_ref[e, jnp.clip(i - n, 0, 2)] + cbs[e][:, n:n + 1])
                        else:
                            pieces.append(seg + (cbs[e][:, n:n + 1] + far_bias))
                    outs.append(_softmax_pv(pieces, vall))
                o_ref[0, pl.ds(r0, TQ), :] = jnp.where(lane < HEAD_DIM, outs[0], outs[1])
        return 0

    lax.fori_loop(0, nq, qtile, 0)


def _moba_prompt(mq, mkv, tb):
    b, s, _ = mq.shape
    npair = MOBA_HEADS // 2
    return pl.pallas_call(
        _moba_prompt_kernel,
        grid=(b, npair),
        in_specs=[pl.BlockSpec((1, s, LANES), lambda i, p: (i, 0, p)),
                  pl.BlockSpec((1, s, LANES), lambda i, p: (i, 0, p)),
                  pl.BlockSpec((1, s, LANES), lambda i, p: (i, 0, npair + p)),
                  pl.BlockSpec((2, N_BIAS_TILES, TQ, TQ), lambda i, p: (p, 0, 0, 0))],
        out_specs=pl.BlockSpec((1, s, LANES), lambda i, p: (i, 0, p)),
        out_shape=jax.ShapeDtypeStruct((b, s, 512), F32),
        scratch_shapes=[pltpu.VMEM((LANES, LANES), F32), pltpu.VMEM((s, LANES), BF16), pltpu.VMEM((s, LANES), BF16)],
        compiler_params=_cparams(("arbitrary", "arbitrary")),
        name="moba_prompt",
    )(mq, mkv, mkv, tb)


def _compress_tokens(load_k, load_v, pos_ref, w1_ref, w2_ref):
    hk = hv = None
    for r in range(0, CMP_BLOCK, 2):
        xk = [(load_k(r + t) + pos_ref[r + t:r + t + 1, 0:LANES]).astype(BF16) for t in range(2)]
        xv = [(load_v(r + t) + pos_ref[r + t:r + t + 1, LANES:2 * LANES]).astype(BF16) for t in range(2)]
        dk = _dot(jnp.concatenate(xk, axis=1), w1_ref[0, r // 2])
        dv = _dot(jnp.concatenate(xv, axis=1), w1_ref[1, r // 2])
        hk = dk if hk is None else hk + dk
        hv = dv if hv is None else hv + dv
    ck = _dot(jax.nn.gelu(hk).astype(BF16), w2_ref[0])
    cv = _dot(jax.nn.gelu(hv).astype(BF16), w2_ref[1])
    return ck, cv


def _cmp_prompt_kernel(xk_ref, xv_ref, pos_ref, w1_ref, w2_ref, gkc_ref, avg_ref, kc_ref, vc_ref):
    nblk = xk_ref.shape[1] // CMP_BLOCK
    ck, cv = _compress_tokens(lambda r: xk_ref[0, pl.ds(r, nblk, stride=CMP_BLOCK), :],
                              lambda r: xv_ref[0, pl.ds(r, nblk, stride=CMP_BLOCK), :], pos_ref, w1_ref, w2_ref)
    ck = ck * lax.rsqrt(_group_mean_sq(ck, avg_ref[...]) + EPS) * gkc_ref[...]
    kc_ref[0] = jnp.zeros((LANES, LANES), F32)
    vc_ref[0] = jnp.zeros((LANES, LANES), F32)
    kc_ref[0, 0:nblk, :] = ck
    vc_ref[0, 0:nblk, :] = cv


def _cmp_prompt(nkv, pos, w1bd, w2bd, gkc, avg):
    b, s, _ = nkv.shape
    const = lambda shape: pl.BlockSpec(shape, lambda i: (0,) * len(shape))
    return pl.pallas_call(
        _cmp_prompt_kernel,
        grid=(b,),
        in_specs=[pl.BlockSpec((1, s, LANES), lambda i: (i, 0, 0)), pl.BlockSpec((1, s, LANES), lambda i: (i, 0, 1)),
                  const((CMP_BLOCK, 256)), const((2, CMP_BLOCK // 2, 256, 256)), const((2, 256, LANES)),
                  const((1, LANES)), const((LANES, LANES))],
        out_specs=[pl.BlockSpec((1, LANES, LANES), lambda i: (i, 0, 0))] * 2,
        out_shape=[jax.ShapeDtypeStruct((b, LANES, LANES), F32)] * 2,
        compiler_params=_cparams(("arbitrary",)),
        name="nsa_compress_prompt",
    )(nkv, nkv, pos, w1bd, w2bd, gkc, avg)


def _nsa_prompt_kernel(q_ref, ks_ref, vs_ref, kw_ref, vw_ref, kc_ref, vc_ref, g_ref, t_ref, tc_ref,
                       o_ref, ksb_scr, vsb_scr, kwb_scr, vwb_scr, ex_scr):
    s_len = q_ref.shape[1]
    k = pl.program_id(1)
    lane = lax.broadcasted_iota(jnp.int32, (TQ, LANES), 1)
    kvmask = (lane // HEAD_DIM) == k
    kc = kc_ref[0].astype(BF16)
    vc = vc_ref[0].astype(BF16)
    nsel = s_len // SEL_BLOCK
    nq = s_len // TQ
    ncmp = s_len // CMP_BLOCK
    rowb = lax.broadcasted_iota(jnp.int32, (nsel, TQ), 0)
    qpos = lax.broadcasted_iota(jnp.int32, (nsel, TQ), 1)
    pair_r = lax.broadcasted_iota(jnp.int32, (nsel, LANES), 0)
    pair_c = lax.broadcasted_iota(jnp.int32, (nsel, LANES), 1)
    pair_t = jnp.where((pair_c // (SEL_BLOCK // CMP_BLOCK) == pair_r) & (pair_c < ncmp), 1.0, 0.0)
    e_r = lax.broadcasted_iota(jnp.int32, (LANES, s_len), 0)
    e_c = lax.broadcasted_iota(jnp.int32, (LANES, s_len), 1)
    ex_scr[...] = jnp.where(e_r == e_c // SEL_BLOCK, 1.0, 0.0).astype(BF16)
    ksb_scr[...] = ks_ref[0].astype(BF16)
    vsb_scr[...] = vs_ref[0].astype(BF16)
    kwb_scr[...] = kw_ref[0].astype(BF16)
    vwb_scr[...] = vw_ref[0].astype(BF16)

    def qtile(i, _):
        r0 = pl.multiple_of(i * TQ, TQ)
        qs = []
        for h in range(NSA_GROUP):
            q2 = q_ref[0, pl.ds(r0, TQ), (h // 2) * LANES:(h // 2 + 1) * LANES]
            qa = jnp.where(k == (h % 2), q2, pltpu.roll(q2, HEAD_DIM, 1))
            qs.append(jnp.where(kvmask, qa, 0.0).astype(BF16))

        imp = jnp.zeros((TQ, LANES), F32)
        o_cmp = []
        for h in range(NSA_GROUP):
            s = _dot_nt(qs[h], kc) + tc_ref[h, pl.ds(r0, TQ), :]
            m = jnp.maximum(jnp.max(s, axis=1, keepdims=True), M_INIT)
            p = jnp.exp2(s - m)
            p = p / jnp.maximum(jnp.sum(p, axis=1, keepdims=True), 1e-30)
            imp = imp + p
            o_cmp.append(_dot(p.astype(BF16), vc))

        imp_t = _dot_nt(pair_t, imp, precision=HIGHEST)
        own = (r0 + qpos) // SEL_BLOCK
        sc = jnp.where(rowb < own, imp_t, -jnp.inf)
        sel = ((_rank_rows(sc, rowb, nsel) < SEL_TOPK) & (rowb < own)) | (rowb == own)
        selb = _columns_from_rows(jnp.where(sel, 1.0, 0.0)).astype(BF16)

        wk, wv, wt = [], [], []
        for j, tidx in enumerate((3, 1, 0)):
            n = i - 2 + j
            c0 = pl.multiple_of(jnp.maximum(n, 0) * TQ, TQ)
            wk.append(kwb_scr[pl.ds(c0, TQ), :])
            wv.append(vwb_scr[pl.ds(c0, TQ), :])
            wt.append(jnp.where(n < 0, N_BIAS_TILES - 1, tidx))
        kw_all = jnp.concatenate(wk, axis=0)
        vw_all = jnp.concatenate(wv, axis=0)
        g = g_ref[0, pl.ds(r0, TQ), :]
        o_win = []
        for h in range(NSA_GROUP):
            s = _dot_nt(qs[h], kw_all)
            o_win.append(_softmax_pv([s[:, j * TQ:(j + 1) * TQ] + t_ref[h, wt[j]] for j in range(3)], vw_all))

        for c in range(1, nq // CASE_TILES + 1):
            @pl.when(i // CASE_TILES + 1 == c)
            def _():
                nkeys = CASE_TILES * c * TQ
                addm = (_dot(selb, ex_scr[:, 0:nkeys]) - 1.0) * (-NEG)
                kall = ksb_scr[0:nkeys, :]
                vall = vsb_scr[0:nkeys, :]
                res = []
                for h in range(NSA_GROUP):
                    s = _dot_nt(qs[h], kall) + addm
                    pieces = [s[:, n * TQ:(n + 1) * TQ] + t_ref[h, jnp.clip(i - n, 0, 2)]
                              for n in range(CASE_TILES * c)]
                    o_sel = _softmax_pv(pieces, vall)
                    hg = (k * NSA_GROUP + h) * 3
                    o = (_col(g, lane, hg) * o_cmp[h] + _col(g, lane, hg + 1) * o_sel
                         + _col(g, lane, hg + 2) * o_win[h])
                    res.append(jnp.where(k == (h % 2), o, pltpu.roll(o, HEAD_DIM, 1)))
                for t in range(2):
                    o_ref[0, pl.ds(r0, TQ), t * LANES:(t + 1) * LANES] = jnp.where(
                        lane < HEAD_DIM, res[2 * t], res[2 * t + 1])
        return 0

    lax.fori_loop(0, nq, qtile, 0)


def _nsa_prompt(nq, nkv, wkv, kcmp, vcmp, gates, tb, tc):
    b, s, _ = nq.shape
    col = lambda arr_cols, cb: pl.BlockSpec((1, s, LANES), lambda i, k: (i, 0, cb))
    return pl.pallas_call(
        _nsa_prompt_kernel,
        grid=(b, 2),
        in_specs=[pl.BlockSpec((1, s, 256), lambda i, k: (i, 0, k)),
                  col(512, 2), col(512, 3), col(256, 0), col(256, 1),
                  pl.BlockSpec((1, LANES, LANES), lambda i, k: (i, 0, 0)),
                  pl.BlockSpec((1, LANES, LANES), lambda i, k: (i, 0, 0)),
                  pl.BlockSpec((1, s, LANES), lambda i, k: (i, 0, 0)),
                  pl.BlockSpec((NSA_GROUP, N_BIAS_TILES, TQ, TQ), lambda i, k: (k, 0, 0, 0)),
                  pl.BlockSpec((NSA_GROUP, s, LANES), lambda i, k: (k, 0, 0))],
        out_specs=pl.BlockSpec((1, s, 256), lambda i, k: (i, 0, k)),
        out_shape=jax.ShapeDtypeStruct((b, s, 512), F32),
        scratch_shapes=[pltpu.VMEM((s, LANES), BF16)] * 4 + [pltpu.VMEM((LANES, s), BF16)],
        compiler_params=_cparams(("arbitrary", "arbitrary")),
        name="nsa_prompt",
    )(nq, nkv, nkv, wkv, wkv, kcmp, vcmp, gates, tb, tc)


PAGES_PER_STEP = 16
CMP_PITCH = 40


def _rank_lt(score, lane, ncand, topk):
    rank = jnp.zeros(score.shape, F32)
    for m in range(ncand):
        col = score[:, m:m + 1]
        beats = (col > score) | ((col == score) & (m < lane))
        rank = rank + jnp.where(beats, 1.0, 0.0)
    return rank < topk


def _merge_blocks(sel, m_all, l_all, acc_scr, nblk, s_self, v_self):
    mx = jnp.maximum(jnp.max(jnp.where(sel, m_all, NEG), axis=1, keepdims=True), s_self)
    w = jnp.exp(jnp.where(sel, m_all - mx, NEG))
    w_self = jnp.exp(s_self - mx)
    den = jnp.sum(w * l_all, axis=1, keepdims=True) + w_self
    num = w_self * v_self
    for j in range(nblk):
        num = num + w[:, j:j + 1] * acc_scr[j]
    return num / den


def _moba_sample_kernel(pt_ref, *refs):
    pages = refs[:PAGES_PER_STEP]
    qm_ref, kn_ref, vn_ref, tsb_ref, misc_ref, o_ref, g_scr, m_scr, l_scr, acc_scr = refs[PAGES_PER_STEP:]
    s = pl.program_id(1)
    nstep = pl.num_programs(1)
    nblk = PAST_LEN // MOBA_BLOCK
    width = MOBA_HEADS * HEAD_DIM
    qm = qm_ref[0]
    qb = qm.astype(BF16)
    lane = lax.broadcasted_iota(jnp.int32, (MOBA_HEADS, LANES), 1)

    @pl.when(s == 0)
    def _():
        g_scr[...] = jnp.zeros(g_scr.shape, F32)
        m_scr[...] = jnp.zeros(m_scr.shape, F32)
        l_scr[...] = jnp.zeros(l_scr.shape, F32)

    npb = PAGES_PER_STEP // 2
    blk0 = s * npb
    kt_all = jnp.concatenate([pages[t][0, 0].astype(BF16) for t in range(PAGES_PER_STEP)], axis=1)
    vt_all = jnp.concatenate([pages[t][0, 1].astype(BF16) for t in range(PAGES_PER_STEP)], axis=1)
    raw = _dot(qb, kt_all)
    far = jnp.broadcast_to(misc_ref[:, 1:2], (MOBA_HEADS, MOBA_BLOCK))
    g_new, m_new, l_new = g_scr[...], m_scr[...], l_scr[...]
    p_rows = []
    for j in range(npb):
        seg = raw[:, j * MOBA_BLOCK:(j + 1) * MOBA_BLOCK]
        gate = jnp.sum(seg, axis=1, keepdims=True)
        sc = seg + (jnp.where(s == nstep - 1, tsb_ref[...], far) if j == npb - 1 else far)
        mj = jnp.max(sc, axis=1, keepdims=True)
        p = jnp.exp(sc - mj)
        g_new = jnp.where(lane == blk0 + j, gate, g_new)
        m_new = jnp.where(lane == blk0 + j, mj, m_new)
        l_new = jnp.where(lane == blk0 + j, jnp.sum(p, axis=1, keepdims=True), l_new)
        zeros = jnp.zeros((MOBA_HEADS, MOBA_BLOCK), F32)
        p_rows.append(jnp.concatenate([p if t == j else zeros for t in range(npb)], axis=1))
    g_scr[...] = g_new
    m_scr[...] = m_new
    l_scr[...] = l_new
    acc = _dot_nt(jnp.concatenate(p_rows, axis=0).astype(BF16), vt_all)
    for j in range(npb):
        acc_scr[blk0 + j] = acc[j * MOBA_HEADS:(j + 1) * MOBA_HEADS]

    @pl.when(s == nstep - 1)
    def _():
        gm = jnp.where(lane < nblk, g_scr[...], -jnp.inf)
        sel = _rank_lt(gm, lane, nblk, MOBA_TOPK) & (lane < nblk)
        s_self = jnp.sum(qm * kn_ref[0], axis=1, keepdims=True) + misc_ref[:, 0:1]
        o = _merge_blocks(sel, m_scr[...], l_scr[...], acc_scr, nblk, s_self, vn_ref[0])
        hrow = lax.broadcasted_iota(jnp.int32, (MOBA_HEADS, width), 0)
        hlane = lax.broadcasted_iota(jnp.int32, (MOBA_HEADS, width), 1)
        o_ref[0] = jnp.sum(jnp.where(hlane // HEAD_DIM == hrow, o, 0.0), axis=0, keepdims=True)


def _moba_sample(page_table, cache_t, qmat, knew, vnew, tsb, misc):
    nb, npages = page_table.shape
    nstep = npages // PAGES_PER_STEP
    width = MOBA_HEADS * HEAD_DIM
    nblk = PAST_LEN // MOBA_BLOCK

    def page_spec(j):
        return pl.BlockSpec((1, 2, width, PAGE), lambda b, s, pt: (pt[b, s * PAGES_PER_STEP + j], 0, 0, 0))

    per_b = lambda shape: pl.BlockSpec((1,) + shape, lambda b, s, pt: (b, 0, 0))
    const = lambda shape: pl.BlockSpec(shape, lambda b, s, pt: (0,) * len(shape))
    grid_spec = pltpu.PrefetchScalarGridSpec(
        num_scalar_prefetch=1,
        grid=(nb, nstep),
        in_specs=[page_spec(j) for j in range(PAGES_PER_STEP)]
        + [per_b((MOBA_HEADS, width)), per_b((1, width)), per_b((1, width)),
           const((MOBA_HEADS, MOBA_BLOCK)), const((MOBA_HEADS, LANES))],
        out_specs=per_b((1, width)),
        scratch_shapes=[pltpu.VMEM((MOBA_HEADS, LANES), F32)] * 3 + [pltpu.VMEM((nblk, MOBA_HEADS, width), F32)],
    )
    return pl.pallas_call(
        _moba_sample_kernel,
        grid_spec=grid_spec,
        out_shape=jax.ShapeDtypeStruct((nb, 1, width), F32),
        compiler_params=_cparams(("arbitrary", "arbitrary")),
        name="moba_sample",
    )(page_table, *([cache_t] * PAGES_PER_STEP), qmat, knew, vnew, tsb, misc)


def _nsa_sample_kernel(pt_ref, *refs):
    pages = refs[:PAGES_PER_STEP]
    (qm_ref, ksn_ref, vsn_ref, kwn_ref, vwn_ref, win_ref, g_ref, tsn_ref, misc_ref, tcs_ref, tws_ref,
     pos_ref, w1_ref, w2_ref, gkc_ref, avg_ref, o_ref, xk_scr, xv_scr, m_scr, l_scr, acc_scr) = refs[PAGES_PER_STEP:]
    s = pl.program_id(1)
    nstep = pl.num_programs(1)
    nsel = PAST_LEN // SEL_BLOCK
    ncmp = PAST_LEN // CMP_BLOCK
    qm = qm_ref[0]
    qb = qm.astype(BF16)
    lane = lax.broadcasted_iota(jnp.int32, (NSA_HEADS, LANES), 1)
    row = lax.broadcasted_iota(jnp.int32, (NSA_HEADS, LANES), 0)
    lo = lane < HEAD_DIM

    @pl.when(s == 0)
    def _():
        m_scr[...] = jnp.zeros(m_scr.shape, F32)
        l_scr[...] = jnp.zeros(l_scr.shape, F32)

    for j in range(PAGES_PER_STEP):
        pg = s * PAGES_PER_STEP + j
        kc = pages[j][0, 0:LANES, :].T
        vc = pages[j][0, LANES:2 * LANES, :].T
        for b4 in range(PAGE // CMP_BLOCK):
            r0 = pl.multiple_of((pg * (PAGE // CMP_BLOCK) + b4) * CMP_PITCH, 8)
            xk_scr[pl.ds(r0, CMP_BLOCK), :] = kc[b4 * CMP_BLOCK:(b4 + 1) * CMP_BLOCK, :]
            xv_scr[pl.ds(r0, CMP_BLOCK), :] = vc[b4 * CMP_BLOCK:(b4 + 1) * CMP_BLOCK, :]

    ks_all = jnp.concatenate([pages[t][0, 256:384, :].astype(BF16) for t in range(PAGES_PER_STEP)], axis=1)
    vs_all = jnp.concatenate([pages[t][0, 384:512, :].astype(BF16) for t in range(PAGES_PER_STEP)], axis=1)
    raw = _dot(qb, ks_all)
    far = jnp.broadcast_to(misc_ref[:, 1:2], (NSA_HEADS, PAGE))
    m_new, l_new = m_scr[...], l_scr[...]
    b0 = 2 * s * PAGES_PER_STEP
    zeros = jnp.zeros((NSA_HEADS, PAGE), F32)
    p_rows = []
    for j in range(PAGES_PER_STEP):
        sc = raw[:, j * PAGE:(j + 1) * PAGE]
        sc = sc + (jnp.where(s == nstep - 1, tsn_ref[...], far) if j == PAGES_PER_STEP - 1 else far)
        m0 = jnp.max(jnp.where(lo, sc, NEG), axis=1, keepdims=True)
        m1 = jnp.max(jnp.where(lo, NEG, sc), axis=1, keepdims=True)
        p = jnp.exp(sc - jnp.where(lo, m0, m1))
        p0, p1 = jnp.where(lo, p, 0.0), jnp.where(lo, 0.0, p)
        l0 = jnp.sum(p0, axis=1, keepdims=True)
        l1 = jnp.sum(p1, axis=1, keepdims=True)
        bj = b0 + 2 * j
        m_new = jnp.where(lane == bj, m0, jnp.where(lane == bj + 1, m1, m_new))
        l_new = jnp.where(lane == bj, l0, jnp.where(lane == bj + 1, l1, l_new))
        for ph in (p0, p1):
            p_rows.append(jnp.concatenate([ph if t == j else zeros for t in range(PAGES_PER_STEP)], axis=1))
    m_scr[...] = m_new
    l_scr[...] = l_new
    acc = _dot_nt(jnp.concatenate(p_rows, axis=0).astype(BF16), vs_all)
    for b in range(2 * PAGES_PER_STEP):
        acc_scr[b0 + b] = acc[b * NSA_HEADS:(b + 1) * NSA_HEADS]

    @pl.when(s == nstep - 1)
    def _():
        ck, cv = _compress_tokens(lambda r: xk_scr[pl.ds(r, ncmp, stride=CMP_PITCH), :],
                                  lambda r: xv_scr[pl.ds(r, ncmp, stride=CMP_PITCH), :], pos_ref, w1_ref, w2_ref)
        ck = ck * lax.rsqrt(_group_mean_sq(ck, avg_ref[...]) + EPS) * gkc_ref[...]
        sc = _dot_nt(qb, ck.astype(BF16)) + tcs_ref[...]
        m = jnp.maximum(jnp.max(sc, axis=1, keepdims=True), M_INIT)
        pc = jnp.exp(sc - m)
        pc = pc / jnp.maximum(jnp.sum(pc, axis=1, keepdims=True), 1e-30)
        o_cmp = _dot(pc.astype(BF16), cv.astype(BF16))
        g0 = pc[0:1] + pc[1:2] + pc[2:3] + pc[3:4]
        g1 = pc[4:5] + pc[5:6] + pc[6:7] + pc[7:8]
        rowc = lax.broadcasted_iota(jnp.int32, (NSA_HEADS, ncmp), 0)
        imp = jnp.where(rowc < NSA_GROUP, g0, g1)
        pr = lax.broadcasted_iota(jnp.int32, (ncmp, LANES), 0)
        pc_ = lax.broadcasted_iota(jnp.int32, (ncmp, LANES), 1)
        pair = jnp.where(pr // (SEL_BLOCK // CMP_BLOCK) == pc_, 1.0, 0.0)
        impb = _dot(imp, pair, precision=HIGHEST)
        own = PAST_LEN // SEL_BLOCK
        sel = _rank_lt(jnp.where(lane < own, impb, -jnp.inf), lane, nsel, SEL_TOPK) & (lane < own)
        s_self = jnp.sum(qm * ksn_ref[0], axis=1, keepdims=True) + misc_ref[:, 0:1]
        o_sel = _merge_blocks(sel, m_scr[...], l_scr[...], acc_scr, nsel, s_self, vsn_ref[0])
        kw_t = win_ref[0, 0:LANES, :].astype(BF16)
        vw_t = win_ref[0, LANES:2 * LANES, :].astype(BF16)
        sw = _dot(qb, kw_t) + tws_ref[...]
        sw_self = jnp.sum(qm * kwn_ref[0], axis=1, keepdims=True) + misc_ref[:, 0:1]
        mw = jnp.maximum(jnp.max(sw, axis=1, keepdims=True), sw_self)
        pw = jnp.exp(sw - mw)
        pw_self = jnp.exp(sw_self - mw)
        o_win = ((_dot_nt(pw.astype(BF16), vw_t) + pw_self * vwn_ref[0])
                 / (jnp.sum(pw, axis=1, keepdims=True) + pw_self))
        gt = jnp.broadcast_to(g_ref[0], (NSA_HEADS, LANES))
        o8 = (_col(gt, lane, 3 * row) * o_cmp + _col(gt, lane, 3 * row + 1) * o_sel
              + _col(gt, lane, 3 * row + 2) * o_win)
        lane1 = lax.broadcasted_iota(jnp.int32, (1, LANES), 1)
        tiles = []
        for t in range(NSA_HEADS // 2):
            ha, hb = 2 * t, 2 * t + 1
            ra = o8[ha:ha + 1, :]
            rb = o8[hb:hb + 1, :]
            if ha // NSA_GROUP == 1:
                ra = pltpu.roll(ra, HEAD_DIM, 1)
            if hb // NSA_GROUP == 0:
                rb = pltpu.roll(rb, HEAD_DIM, 1)
            tiles.append(jnp.where(lane1 < HEAD_DIM, ra, rb))
        o_ref[0] = jnp.concatenate(tiles, axis=1)


def _nsa_sample(page_table, cache, qmat, ksn, vsn, kwn, vwn, win, gates, tsn, misc, tcs, tws,
                pos, w1bd, w2bd, gkc, avg):
    nb, npages = page_table.shape
    nstep = npages // PAGES_PER_STEP
    nsel = PAST_LEN // SEL_BLOCK

    def page_spec(j):
        return pl.BlockSpec((1, 512, PAGE), lambda b, s, pt: (pt[b, s * PAGES_PER_STEP + j], 0, 0))

    per_b = lambda shape: pl.BlockSpec((1,) + shape, lambda b, s, pt: (b, 0, 0))
    const = lambda shape: pl.BlockSpec(shape, lambda b, s, pt: (0,) * len(shape))
    cmp_rows = PAST_LEN // CMP_BLOCK * CMP_PITCH
    grid_spec = pltpu.PrefetchScalarGridSpec(
        num_scalar_prefetch=1,
        grid=(nb, nstep),
        in_specs=[page_spec(j) for j in range(PAGES_PER_STEP)]
        + [per_b((NSA_HEADS, LANES)), per_b((1, LANES)), per_b((1, LANES)), per_b((1, LANES)), per_b((1, LANES)),
           per_b((256, WINDOW)), per_b((1, LANES)),
           const((NSA_HEADS, PAGE)), const((NSA_HEADS, LANES)), const((NSA_HEADS, PAST_LEN // CMP_BLOCK)),
           const((NSA_HEADS, WINDOW)),
           const((CMP_BLOCK, 256)), const((2, CMP_BLOCK // 2, 256, 256)), const((2, 256, LANES)),
           const((1, LANES)), const((LANES, LANES))],
        out_specs=per_b((1, 512)),
        scratch_shapes=[pltpu.VMEM((cmp_rows, LANES), F32), pltpu.VMEM((cmp_rows, LANES), F32),
                        pltpu.VMEM((NSA_HEADS, LANES), F32),
                        pltpu.VMEM((NSA_HEADS, LANES), F32), pltpu.VMEM((nsel, NSA_HEADS, LANES), F32)],
    )
    return pl.pallas_call(
        _nsa_sample_kernel,
        grid_spec=grid_spec,
        out_shape=jax.ShapeDtypeStruct((nb, 1, 512), F32),
        compiler_params=_cparams(("arbitrary", "arbitrary")),
        name="nsa_sample",
    )(page_table, *([cache] * PAGES_PER_STEP), qmat, ksn, vsn, kwn, vwn, win, gates, tsn, misc, tcs, tws,
      pos, w1bd, w2bd, gkc, avg)


def _outproj_kernel(x_ref, mod_ref, om_ref, on_ref, w_ref, o_ref):
    y = _dot(om_ref[0].astype(BF16), w_ref[0:512, :]) + _dot(on_ref[0].astype(BF16), w_ref[512:1024, :])
    o_ref[0] = x_ref[0] + mod_ref[0][:, 2 * D:3 * D] * y


def _outproj(x, mod, o_m, o_n, w, tm):
    b, t, _ = x.shape
    tmod = mod.shape[1]
    row = lambda width: pl.BlockSpec((1, tm, width), lambda i, j: (i, j, 0))
    return pl.pallas_call(
        _outproj_kernel,
        grid=(b, t // tm),
        in_specs=[row(D), pl.BlockSpec((1, tmod, 3 * D), lambda i, j: (i, 0, 0)), row(512), row(512),
                  pl.BlockSpec((D, D), lambda i, j: (0, 0))],
        out_specs=row(D),
        out_shape=jax.ShapeDtypeStruct((b, t, D), F32),
        compiler_params=_cparams(("arbitrary", "arbitrary")),
        name="attn_outproj",
    )(x, mod, o_m, o_n, w)


def _mlp_kernel(x_ref, mod_ref, g_ref, w1_ref, w2_ref, o_ref, h_scr, acc_scr):
    kf = pl.program_id(2)

    @pl.when(kf == 0)
    def _():
        mod = mod_ref[0]
        h_scr[...] = _modulate(x_ref[0], g_ref[...], mod[:, 0:D], mod[:, D:2 * D]).astype(BF16)
        acc_scr[...] = jnp.zeros(acc_scr.shape, F32)

    a = jnp.square(jnp.maximum(_dot(h_scr[...], w1_ref[...]), 0.0))
    acc_scr[...] += _dot(a.astype(BF16), w2_ref[...])

    @pl.when(kf == pl.num_programs(2) - 1)
    def _():
        o_ref[0] = x_ref[0] + mod_ref[0][:, 2 * D:3 * D] * acc_scr[...]


def _mlp(x, mod, g, w1, w2, tm, tf):
    b, t, _ = x.shape
    tmod = mod.shape[1]
    return pl.pallas_call(
        _mlp_kernel,
        grid=(b, t // tm, D_FF // tf),
        in_specs=[pl.BlockSpec((1, tm, D), lambda i, j, kf: (i, j, 0)),
                  pl.BlockSpec((1, tmod, 3 * D), lambda i, j, kf: (i, 0, 0)),
                  pl.BlockSpec((1, D), lambda i, j, kf: (0, 0)),
                  pl.BlockSpec((D, tf), lambda i, j, kf: (0, kf)),
                  pl.BlockSpec((tf, D), lambda i, j, kf: (kf, 0))],
        out_specs=pl.BlockSpec((1, tm, D), lambda i, j, kf: (i, j, 0)),
        out_shape=jax.ShapeDtypeStruct((b, t, D), F32),
        scratch_shapes=[pltpu.VMEM((tm, D), BF16), pltpu.VMEM((tm, D), F32)],
        compiler_params=_cparams(("arbitrary", "arbitrary", "arbitrary")),
        name="mlp",
    )(x, mod, g, w1, w2)


def _s5_disc_kernel(are_ref, aim_ref, ldt_ref, bre_ref, bim_ref, abre_ref, abim_ref, bbre_ref, bbim_ref):
    a_re, a_im = are_ref[...], aim_ref[...]
    dt = jnp.exp(ldt_ref[...])
    decay = jnp.exp(dt * a_re)
    ab_re, ab_im = decay * jnp.cos(dt * a_im), decay * jnp.sin(dt * a_im)
    den = a_re * a_re + a_im * a_im
    f_re = ((ab_re - 1) * a_re + ab_im * a_im) / den
    f_im = (ab_im * a_re - (ab_re - 1) * a_im) / den
    br, bi = bre_ref[...], bim_ref[...]
    abre_ref[...] = ab_re
    abim_ref[...] = ab_im
    bbre_ref[...] = f_re * br - f_im * bi
    bbim_ref[...] = f_re * bi + f_im * br


def _s5_discretize(a_re, a_im, log_dt, b_re, b_im):
    rep = lambda a: jnp.repeat(a, S5_GROUP_CH, axis=1)
    shp = jax.ShapeDtypeStruct((S5_GROUPS, S5_STATE * S5_GROUP_CH), F32)
    ldt = jnp.broadcast_to(log_dt[:, None], (S5_GROUPS, S5_STATE * S5_GROUP_CH))
    flat = lambda a: a.reshape(S5_GROUPS, S5_STATE * S5_GROUP_CH)
    ab_re, ab_im, bb_re, bb_im = pl.pallas_call(
        _s5_disc_kernel, out_shape=[shp] * 4, name="s5_discretize",
    )(rep(a_re), rep(a_im), ldt, flat(b_re), flat(b_im))
    unrep = lambda a: a[:, ::S5_GROUP_CH]
    unflat = lambda a: a.reshape(S5_GROUPS, S5_STATE, S5_GROUP_CH)
    return unrep(ab_re), unrep(ab_im), unflat(bb_re), unflat(bb_im)


def _modulate_tm_kernel(x_ref, mod_ref, g_ref, o_ref):
    mod = mod_ref[0]
    o_ref[...] = _modulate(x_ref[0], g_ref[...], mod[:, 0:D], mod[:, D:2 * D])


def _modulate_time_major(x, mod, g, tl):
    b, t, _ = x.shape
    tmod = mod.shape[1]
    return pl.pallas_call(
        _modulate_tm_kernel,
        grid=(b, t // tl),
        in_specs=[pl.BlockSpec((1, tl, D), lambda i, j: (i, j, 0)),
                  pl.BlockSpec((1, tmod, 3 * D), lambda i, j: (i, 0, 0)),
                  pl.BlockSpec((1, D), lambda i, j: (0, 0))],
        out_specs=pl.BlockSpec((tl, D), lambda i, j: (j, i)),
        out_shape=jax.ShapeDtypeStruct((t, b * D), F32),
        compiler_params=_cparams(("arbitrary", "arbitrary")),
        name="s5_modulate",
    )(x, mod, g)


S5_CB = 256
S5_NS = S5_CB // S5_GROUP_CH * S5_STATE


def _s5_scan_kernel(h_ref, wb_ref, wc_ref, ar_ref, ai_ref, d_ref, h0_ref, y_ref, so_ref, xs_scr, st_scr, *, tl, r):
    i = pl.program_id(1)

    @pl.when(i == 0)
    def _():
        st_scr[...] = h0_ref[...]

    u = h_ref[...]
    xs_scr[...] = _dot(u.astype(BF16), wb_ref[0])
    ar = jnp.broadcast_to(ar_ref[0], (r, S5_NS))
    ai = jnp.broadcast_to(ai_ref[0], (r, S5_NS))

    def step(t, carry):
        xr, xi = carry
        r0 = pl.multiple_of(t * r, r)
        nr = ar * xr - ai * xi + xs_scr[pl.ds(r0, r), 0:S5_NS]
        ni = ar * xi + ai * xr + xs_scr[pl.ds(r0, r), S5_NS:2 * S5_NS]
        xs_scr[pl.ds(r0, r), 0:S5_NS] = nr
        xs_scr[pl.ds(r0, r), S5_NS:2 * S5_NS] = ni
        return nr, ni

    xr, xi = lax.fori_loop(0, tl, step, (st_scr[0], st_scr[1]))
    st_scr[0] = xr
    st_scr[1] = xi
    y_ref[...] = _dot(xs_scr[...].astype(BF16), wc_ref[0]) + d_ref[...] * u

    @pl.when(i == pl.num_programs(1) - 1)
    def _():
        so_ref[...] = st_scr[...]


def _s5_scan(h_tm, wb, wc, ar, ai, d_skip, h0, r, tl):
    rows = h_tm.shape[0]
    nj = D // S5_CB
    return pl.pallas_call(
        functools.partial(_s5_scan_kernel, tl=tl, r=r),
        grid=(nj, rows // (tl * r)),
        in_specs=[pl.BlockSpec((tl * r, S5_CB), lambda j, i: (i, j)),
                  pl.BlockSpec((1, S5_CB, 2 * S5_NS), lambda j, i: (j, 0, 0)),
                  pl.BlockSpec((1, 2 * S5_NS, S5_CB), lambda j, i: (j, 0, 0)),
                  pl.BlockSpec((1, 1, S5_NS), lambda j, i: (j, 0, 0)),
                  pl.BlockSpec((1, 1, S5_NS), lambda j, i: (j, 0, 0)),
                  pl.BlockSpec((1, S5_CB), lambda j, i: (0, j)),
                  pl.BlockSpec((2, r, S5_NS), lambda j, i: (0, 0, j))],
        out_specs=[pl.BlockSpec((tl * r, S5_CB), lambda j, i: (i, j)),
                   pl.BlockSpec((2, r, S5_NS), lambda j, i: (0, 0, j))],
        out_shape=[jax.ShapeDtypeStruct((rows, D), F32), jax.ShapeDtypeStruct((2, r, S5_GROUPS * S5_STATE), F32)],
        scratch_shapes=[pltpu.VMEM((tl * r, 2 * S5_NS), F32), pltpu.VMEM((2, r, S5_NS), F32)],
        compiler_params=_cparams(("arbitrary", "arbitrary")),
        name="s5_scan",
    )(h_tm, wb, wc, ar, ai, d_skip, h0)


def _glu_kernel(y_ref, x_ref, mod_ref, w_ref, o_ref):
    z = _dot(jax.nn.gelu(y_ref[...]).astype(BF16), w_ref[...])
    o_ref[0] = x_ref[0] + mod_ref[0][:, 2 * D:3 * D] * (z[:, 0:D] * jax.nn.sigmoid(z[:, D:2 * D]))


def _glu_residual(y_tm, x, mod, w, tl):
    b, t, _ = x.shape
    tmod = mod.shape[1]
    return pl.pallas_call(
        _glu_kernel,
        grid=(b, t // tl),
        in_specs=[pl.BlockSpec((tl, D), lambda i, j: (j, i)),
                  pl.BlockSpec((1, tl, D), lambda i, j: (i, j, 0)),
                  pl.BlockSpec((1, tmod, 3 * D), lambda i, j: (i, 0, 0)),
                  pl.BlockSpec((D, 2 * D), lambda i, j: (0, 0))],
        out_specs=pl.BlockSpec((1, tl, D), lambda i, j: (i, j, 0)),
        out_shape=jax.ShapeDtypeStruct((b, t, D), F32),
        compiler_params=_cparams(("arbitrary", "arbitrary")),
        name="s5_glu",
    )(y_tm, x, mod, w)


def _s5_block_weights(bb_re, bb_im, c_re, c_im):
    nj, ng = D // S5_CB, S5_CB // S5_GROUP_CH
    eye = jnp.eye(ng, dtype=F32)

    def wb_part(bb):
        t = bb.reshape(nj, ng, S5_STATE, S5_GROUP_CH).transpose(0, 1, 3, 2)
        return jnp.einsum("jgcn,gh->jgchn", t, eye).reshape(nj, S5_CB, S5_NS)

    def wc_part(c):
        t = c.reshape(nj, ng, S5_GROUP_CH, S5_STATE).transpose(0, 1, 3, 2)
        return jnp.einsum("jgnc,gh->jgnhc", t, eye).reshape(nj, S5_NS, S5_CB)

    wb = jnp.concatenate([wb_part(bb_re), wb_part(bb_im)], axis=2).astype(BF16)
    wc = jnp.concatenate([wc_part(c_re), -wc_part(c_im)], axis=1).astype(BF16)
    return wb, wc


S5_TL = 32
S5_PITCH = 40
S5_NSLAB = 2 * S5_GROUPS * S5_STATE // LANES


def _s5_fused_kernel(x_ref, mod_ref, g_ref, wb_ref, wc_ref, ar_ref, ai_ref, d_ref, h0_ref, wg_ref,
                     o_ref, so_ref, xs_scr, st_scr):
    i = pl.program_id(0)
    nb = x_ref.shape[0]
    nj = D // S5_CB
    tiles = S5_NS // LANES

    @pl.when(i == 0)
    def _():
        xs_scr[...] = jnp.zeros(xs_scr.shape, F32)
        for j in range(nj):
            for comp in range(2):
                for q in range(tiles):
                    st_scr[(2 * j + comp) * tiles + q] = h0_ref[comp, :, j * S5_NS + q * LANES:j * S5_NS + (q + 1) * LANES]

    h = jnp.concatenate([_modulate(x_ref[b], g_ref[...], mod_ref[b][:, 0:D], mod_ref[b][:, D:2 * D])
                         for b in range(nb)], axis=0)
    hb = h.astype(BF16)
    ys = []
    for j in range(nj):
        bu = _dot(hb[:, j * S5_CB:(j + 1) * S5_CB], wb_ref[j])
        base = 2 * j * tiles
        for lt in range(2 * tiles):
            for b in range(nb):
                xs_scr[base + lt, b * S5_PITCH:b * S5_PITCH + S5_TL, :] = bu[b * S5_TL:(b + 1) * S5_TL,
                                                                             lt * LANES:(lt + 1) * LANES]
        ar = [jnp.broadcast_to(ar_ref[j][:, q * LANES:(q + 1) * LANES], (nb, LANES)) for q in range(tiles)]
        ai = [jnp.broadcast_to(ai_ref[j][:, q * LANES:(q + 1) * LANES], (nb, LANES)) for q in range(tiles)]

        def step(t, carry):
            new = []
            for q in range(tiles):
                xr, xi = carry[2 * q], carry[2 * q + 1]
                rows = pl.ds(t, nb, stride=S5_PITCH)
                nr = ar[q] * xr - ai[q] * xi + xs_scr[base + q, rows, :]
                ni = ar[q] * xi + ai[q] * xr + xs_scr[base + tiles + q, rows, :]
                xs_scr[base + q, rows, :] = nr
                xs_scr[base + tiles + q, rows, :] = ni
                new += [nr, ni]
            return tuple(new)

        init = []
        for q in range(tiles):
            init += [st_scr[base + q], st_scr[base + tiles + q]]
        fin = lax.fori_loop(0, S5_TL, step, tuple(init))
        for q in range(tiles):
            st_scr[base + q] = fin[2 * q]
            st_scr[base + tiles + q] = fin[2 * q + 1]
        states = jnp.concatenate([xs_scr[base + lt] for lt in range(2 * tiles)], axis=1)
        ys.append(_dot(states.astype(BF16), wc_ref[j]))
    y_all = jnp.concatenate(ys, axis=1)
    y = jnp.concatenate([y_all[b * S5_PITCH:b * S5_PITCH + S5_TL] for b in range(nb)], axis=0) + d_ref[...] * h
    z = _dot(jax.nn.gelu(y).astype(BF16), wg_ref[...])
    out = z[:, 0:D] * jax.nn.sigmoid(z[:, D:2 * D])
    for b in range(nb):
        o_ref[b] = x_ref[b] + mod_ref[b][:, 2 * D:3 * D] * out[b * S5_TL:(b + 1) * S5_TL]

    @pl.when(i == pl.num_programs(0) - 1)
    def _():
        for j in range(nj):
            for comp in range(2):
                for q in range(tiles):
                    so_ref[comp, :, j * S5_NS + q * LANES:j * S5_NS + (q + 1) * LANES] = st_scr[(2 * j + comp) * tiles + q]


def _s5_layer(x, mod, g, wb, wc, ar, ai, d_skip, h0, w_glu):
    b, t, _ = x.shape
    ns = S5_GROUPS * S5_STATE
    const = lambda shape: pl.BlockSpec(shape, lambda i: (0,) * len(shape))
    x_new, st = pl.pallas_call(
        _s5_fused_kernel,
        grid=(t // S5_TL,),
        in_specs=[pl.BlockSpec((b, S5_TL, D), lambda i: (0, i, 0)),
                  const((b, 1, 3 * D)), const((1, D)),
                  const((D // S5_CB, S5_CB, 2 * S5_NS)), const((D // S5_CB, 2 * S5_NS, S5_CB)),
                  const((D // S5_CB, 1, S5_NS)), const((D // S5_CB, 1, S5_NS)),
                  const((1, D)), const((2, b, ns)), const((D, 2 * D))],
        out_specs=[pl.BlockSpec((b, S5_TL, D), lambda i: (0, i, 0)), const((2, b, ns))],
        out_shape=[jax.ShapeDtypeStruct((b, t, D), F32), jax.ShapeDtypeStruct((2, b, ns), F32)],
        scratch_shapes=[pltpu.VMEM((S5_NSLAB, b * S5_PITCH, LANES), F32), pltpu.VMEM((S5_NSLAB, b, LANES), F32)],
        compiler_params=_cparams(("arbitrary",)),
        name="s5_fused",
    )(x, mod, g, wb, wc, ar, ai, d_skip, h0.reshape(b, 2, ns).transpose(1, 0, 2), w_glu)
    return x_new, st.transpose(1, 0, 2).reshape(b, 2, S5_GROUPS, S5_STATE)


def _dist_tiles():
    r = jnp.arange(TQ, dtype=jnp.int32)[:, None]
    c = jnp.arange(TQ, dtype=jnp.int32)[None, :]
    d0 = r - c
    edge = 2 * TQ + r - c
    return jnp.concatenate([d0, TQ + d0, 2 * TQ + d0, jnp.where(edge <= WINDOW, edge, -1),
                            jnp.full((TQ, TQ), -1, jnp.int32)], axis=0)


def _dist_cmp(seq):
    q = jnp.arange(seq, dtype=jnp.int32)[:, None]
    n = jnp.arange(LANES, dtype=jnp.int32)[None, :]
    return jnp.where(n < seq // CMP_BLOCK, q - ((n + 1) * CMP_BLOCK - 1), -1)


_SAMPLE_TABLE_SIZES = (LANES, MOBA_BLOCK, PAGE, PAST_LEN // CMP_BLOCK, WINDOW)


def _dist_sample():
    ar = lambda n: jnp.arange(n, dtype=jnp.int32)
    misc = jnp.zeros((LANES,), jnp.int32).at[1].set(MAX_DISTANCE * 4)
    moba = MOBA_BLOCK - ar(MOBA_BLOCK)
    sel = PAGE - ar(PAGE)
    cmp_ = PAST_LEN - ((ar(PAST_LEN // CMP_BLOCK) + 1) * CMP_BLOCK - 1)
    win = WINDOW - ar(WINDOW)
    return jnp.concatenate([misc, moba, sel, cmp_, win])[None, :]


def _block_diag2(w):
    z = jnp.zeros_like(w)
    return jnp.concatenate([jnp.concatenate([w, z], axis=-1), jnp.concatenate([z, w], axis=-1)], axis=-2)


def kernel(x_prompt, x_sample, cache_moba_kv, cache_nsa_kv, state_nsa_win, state_s5, page_table, c_prompt, c_sample, rel_bias, attn_norm_g, attn_ada_w, attn_ada_b, attn_w_in, attn_qk_g, nsa_cmp_pos, nsa_cmp_w1, nsa_cmp_w2, attn_w_out, ssm_norm_g, ssm_ada_w, ssm_ada_b, s5_a_re, s5_a_im, s5_log_dt, s5_b_re, s5_b_im, s5_c_re, s5_c_im, s5_d, s5_w_glu, mlp_norm_g, mlp_ada_w, mlp_ada_b, mlp_w1, mlp_w2):
    bp, seq, _ = x_prompt.shape
    bs = x_sample.shape[0]
    assert seq % TQ == 0 and x_sample.shape[1] == 1
    n_pool = cache_moba_kv.shape[1]

    c_all = jnp.concatenate([c_prompt, c_sample], axis=0)
    split_mod = lambda m: (m[:bp, None, :], m[None, bp:, :])
    mod_attn = _adaln(c_all, attn_ada_w, attn_ada_b)
    mod_ssm = _adaln(c_all, ssm_ada_w, ssm_ada_b)
    mod_mlp = _adaln(c_all, mlp_ada_w, mlp_ada_b)

    xp = x_prompt
    xs = x_sample.reshape(1, bs, D)

    tb = _bias_table(rel_bias, _dist_tiles(), LOG2E).reshape(2 * MOBA_HEADS, N_BIAS_TILES, TQ, TQ)
    tc = _bias_table(rel_bias, _dist_cmp(seq), LOG2E, first_head=MOBA_HEADS)
    ts = _bias_table(rel_bias, _dist_sample())[:, 0, :]
    offs = [0]
    for size in _SAMPLE_TABLE_SIZES:
        offs.append(offs[-1] + size)
    part = lambda heads, t: ts[heads, offs[t]:offs[t + 1]]
    hm, hn = slice(0, MOBA_HEADS), slice(MOBA_HEADS, 2 * MOBA_HEADS)
    misc_m, misc_n, tsb, tsn, tcs, tws = part(hm, 0), part(hn, 0), part(hm, 1), part(hn, 2), part(hn, 3), part(hn, 4)

    w_in = jnp.pad(attn_w_in[0], ((0, 0), (0, IN_COLS_PAD - IN_COLS))).astype(BF16)
    qkg_t = jnp.pad(jnp.tile(attn_qk_g[0], (1, 2)), ((0, 2), (0, 0)))
    lr = jnp.arange(LANES)
    avg = jnp.where(lr[:, None] // HEAD_DIM == lr[None, :] // HEAD_DIM, 1.0 / HEAD_DIM, 0.0).astype(BF16)
    g_attn = attn_norm_g[0][None, :]
    w_out = attn_w_out[0].astype(BF16)
    pos = jnp.concatenate([nsa_cmp_pos[0, 0], nsa_cmp_pos[0, 0], nsa_cmp_pos[0, 1], nsa_cmp_pos[0, 1]], axis=1)
    w1bd = _block_diag2(nsa_cmp_w1[0].reshape(2, CMP_BLOCK, HEAD_DIM, CMP_HIDDEN)).astype(BF16)
    w1bd = w1bd.reshape(2, CMP_BLOCK // 2, 256, 256)
    w2bd = _block_diag2(nsa_cmp_w2[0]).astype(BF16)
    gkc = qkg_t[3:4]

    mp_attn, ms_attn = split_mod(mod_attn[0])
    mq, mkv, nq, nkv, wkv, gates, mkv_t, nkv_t = _attn_proj(xp, mp_attn, g_attn, w_in, qkg_t, avg, 512,
                                                            SCALE * LOG2E, page_major=True)
    o_moba = _moba_prompt(mq, mkv, tb[:MOBA_HEADS])
    kcmp, vcmp = _cmp_prompt(nkv, pos, w1bd, w2bd, gkc, avg)
    o_nsa = _nsa_prompt(nq, nkv, wkv, kcmp, vcmp, gates, tb[MOBA_HEADS:], tc)
    xp = _outproj(xp, mp_attn, o_moba, o_nsa, w_out, 512)
    npg = seq // PAGE
    moba_p = mkv_t.reshape(1, bp, npg, 2, MOBA_HEADS, HEAD_DIM, PAGE).transpose(0, 1, 2, 6, 3, 4, 5)
    nsa_p = nkv_t.reshape(1, bp, npg, 4, 2, HEAD_DIM, PAGE).transpose(0, 1, 2, 6, 3, 4, 5)
    win_p = wkv[:, seq - min(WINDOW, seq):].reshape(1, bp, min(WINDOW, seq), 2, 2, HEAD_DIM)
    mq_s, mkv_s, nq_s, nkv_s, wkv_s, gates_s = _attn_proj(xs, ms_attn, g_attn, w_in, qkg_t, avg, bs, SCALE)
    cache_m_t = cache_moba_kv.transpose(0, 1, 3, 4, 5, 2).reshape(n_pool, 2, 512, PAGE)
    cache_n_t = cache_nsa_kv.transpose(0, 1, 3, 4, 5, 2).reshape(n_pool, 512, PAGE)
    win_t = state_nsa_win[0].transpose(0, 2, 3, 4, 1).reshape(bs, 256, WINDOW)
    lw = jnp.arange(512)
    qmat_m = jnp.where(lw[None, None, :] // HEAD_DIM == jnp.arange(MOBA_HEADS)[None, :, None], mq_s[0][:, None, :], 0.0)
    col3 = lambda a, lo, width: a[0][:, None, lo:lo + width]
    o_moba_s = _moba_sample(page_table, cache_m_t, qmat_m, col3(mkv_s, 0, 512), col3(mkv_s, 512, 512), tsb, misc_m)
    nq4 = nq_s[0].reshape(bs, NSA_HEADS, HEAD_DIM)
    kvh = jnp.arange(NSA_HEADS) // NSA_GROUP
    qmat_n = jnp.concatenate([jnp.where(kvh[None, :, None] == 0, nq4, 0.0),
                              jnp.where(kvh[None, :, None] == 1, nq4, 0.0)], axis=2)
    o_nsa_s = _nsa_sample(page_table, cache_n_t, qmat_n,
                          col3(nkv_s, 256, LANES), col3(nkv_s, 384, LANES), col3(wkv_s, 0, LANES),
                          col3(wkv_s, 128, LANES), win_t, gates_s.reshape(bs, 1, LANES),
                          tsn, misc_n, tcs, tws, pos, w1bd, w2bd, gkc, avg)
    xs = _outproj(xs, ms_attn, o_moba_s.reshape(1, bs, 512), o_nsa_s.reshape(1, bs, 512), w_out, bs)
    moba_s = mkv_s.reshape(1, bs, 1, 2, MOBA_HEADS, HEAD_DIM)
    nsa_s = nkv_s.reshape(1, bs, 1, 4, 2, HEAD_DIM)
    win_s = jnp.concatenate([state_nsa_win[0][:, 1:], wkv_s[0].reshape(bs, 1, 2, 2, HEAD_DIM)], axis=1)[None]

    w1_0, w2_0 = mlp_w1[0].astype(BF16), mlp_w2[0].astype(BF16)
    mp_mlp, ms_mlp = split_mod(mod_mlp[0])
    g_mlp0 = mlp_norm_g[0][None, :]
    xp = _mlp(xp, mp_mlp, g_mlp0, w1_0, w2_0, MLP_TM, MLP_TF)
    xs = _mlp(xs, ms_mlp, g_mlp0, w1_0, w2_0, bs, MLP_TF)

    ab_re, ab_im, bb_re, bb_im = _s5_discretize(s5_a_re[0], s5_a_im[0], s5_log_dt[0], s5_b_re[0], s5_b_im[0])
    wb, wc = _s5_block_weights(bb_re, bb_im, s5_c_re[0], s5_c_im[0])
    nj = D // S5_CB
    ar = ab_re.reshape(nj, 1, S5_NS)
    ai = ab_im.reshape(nj, 1, S5_NS)
    g_ssm = ssm_norm_g[0][None, :]
    d_skip = s5_d[0][None, :]
    w_glu = s5_w_glu[0].astype(BF16)
    mp_ssm, ms_ssm = split_mod(mod_ssm[0])
    xp, st_p = _s5_layer(xp, mp_ssm, g_ssm, wb, wc, ar, ai, d_skip,
                         jnp.zeros((bp, 2, S5_GROUPS, S5_STATE), F32), w_glu)
    h_s = _modulate_time_major(xs, ms_ssm, g_ssm, bs)
    y_s, st_s = _s5_scan(h_s, wb, wc, ar, ai, d_skip,
                         state_s5[0].reshape(bs, 2, S5_GROUPS * S5_STATE).transpose(1, 0, 2), bs, 1)
    xs = _glu_residual(y_s, xs, ms_ssm, w_glu, bs)
    st_s = st_s.transpose(1, 0, 2).reshape(bs, 2, S5_GROUPS, S5_STATE)

    w1_1, w2_1 = mlp_w1[1].astype(BF16), mlp_w2[1].astype(BF16)
    mp_mlp, ms_mlp = split_mod(mod_mlp[1])
    g_mlp1 = mlp_norm_g[1][None, :]
    xp = _mlp(xp, mp_mlp, g_mlp1, w1_1, w2_1, MLP_TM, MLP_TF)
    xs = _mlp(xs, ms_mlp, g_mlp1, w1_1, w2_1, bs, MLP_TF)

    return (xp, xs.reshape(bs, 1, D), moba_p, moba_s, nsa_p, nsa_s, win_p, win_s, st_p[None], st_s[None])
```

```python
import functools
import math

import jax
import jax.numpy as jnp
import numpy as np
from jax import lax
from jax.experimental import pallas as pl
from jax.experimental.pallas import tpu as pltpu

F32 = jnp.float32
BF16 = jnp.bfloat16
HIGHEST = lax.Precision.HIGHEST

D = 1024
HEAD_DIM = 64
MOBA_HEADS = 8
NSA_HEADS = 8
NSA_GROUP = 4
MOBA_BLOCK = 256
MOBA_TOPK = 3
CMP_BLOCK = 32
CMP_HIDDEN = 128
SEL_BLOCK = 64
SEL_TOPK = 16
WINDOW = 512
NUM_BUCKETS = 32
MAX_DISTANCE = 128
PAGE = 128
PAST_LEN = 8192
D_FF = 4 * D
S5_GROUPS = 64
S5_STATE = 64
S5_GROUP_CH = 16
IN_COLS = 3 * 512 + 512 + 6 * 128 + 3 * NSA_HEADS
IN_COLS_PAD = 23 * 128
EPS = 1e-6
SCALE = HEAD_DIM ** -0.5
LOG2E = math.log2(math.e)
LANES = 128
TQ = 256
N_BIAS_TILES = 5
CASE_TILES = 1
MLP_TM = 1024
MLP_TF = 1024
NEG = -1e30
M_INIT = -1e15
VMEM_LIMIT = 56 * 1024 * 1024

_NT = (((1,), (1,)), ((), ()))


def _cparams(sem):
    return pltpu.CompilerParams(dimension_semantics=sem, vmem_limit_bytes=VMEM_LIMIT)


def _dot(a, b, **kw):
    return jnp.dot(a, b, preferred_element_type=F32, **kw)


def _dot_nt(a, b, **kw):
    return lax.dot_general(a, b, _NT, preferred_element_type=F32, **kw)


def _modulate(x, g, shift, scale):
    ms = jnp.mean(x * x, axis=-1, keepdims=True)
    return x * lax.rsqrt(ms + EPS) * g * (1.0 + scale) + shift


def _group_mean_sq(z, avg):
    sq = z * z
    hi = sq.astype(BF16)
    lo = (sq - hi.astype(F32)).astype(BF16)
    return _dot(hi, avg) + _dot(lo, avg)


def _col(x, lane, idx):
    return jnp.sum(jnp.where(lane == idx, x, 0.0), axis=1, keepdims=True)


def _adaln_kernel(c_ref, w_ref, b_ref, o_ref):
    c = c_ref[...]
    s = c * jax.nn.sigmoid(c)
    o_ref[0] = _dot(s, w_ref[0], precision=HIGHEST) + b_ref[0]


def _adaln(c_all, w, b):
    nl, n = w.shape[0], c_all.shape[0]
    return pl.pallas_call(
        _adaln_kernel,
        grid=(nl, 3),
        in_specs=[pl.BlockSpec((n, D), lambda l, j: (0, 0)),
                  pl.BlockSpec((1, D, D), lambda l, j: (l, 0, j)),
                  pl.BlockSpec((1, 1, D), lambda l, j: (l, 0, j))],
        out_specs=pl.BlockSpec((1, n, D), lambda l, j: (l, 0, j)),
        out_shape=jax.ShapeDtypeStruct((nl, n, 3 * D), F32),
        compiler_params=_cparams(("arbitrary", "arbitrary")),
        name="adaln",
    )(c_all, w, b.reshape(nl, 1, 3 * D))


def _log_bucket_starts():
    max_exact = NUM_BUCKETS // 2
    n = np.arange(max_exact, 4 * MAX_DISTANCE, dtype=np.float32)
    large = max_exact + (np.log(n / np.float32(max_exact)) / np.float32(math.log(MAX_DISTANCE / max_exact))
                         * np.float32(NUM_BUCKETS - max_exact)).astype(np.int32)
    large = np.minimum(large, NUM_BUCKETS - 1)
    return [int(np.argmax(large >= b)) + max_exact for b in range(max_exact, NUM_BUCKETS)]


def _bias_kernel(rb_ref, d_ref, o_ref, *, scale, first_head):
    h = pl.program_id(0) + first_head
    dist = d_ref[...]
    n = jnp.maximum(dist, 0)
    max_exact = NUM_BUCKETS // 2
    acc = jnp.zeros(dist.shape, F32)
    for k in range(max_exact):
        acc = jnp.where(n == k, rb_ref[k, h], acc)
    for j, start in enumerate(_log_bucket_starts()):
        acc = jnp.where(n >= start, rb_ref[max_exact + j, h], acc)
    o_ref[0] = jnp.where(dist < 0, NEG, acc * scale)


def _bias_table(rel_bias, dist, scale=1.0, first_head=0, nh=None):
    r, c = dist.shape
    nh = rel_bias.shape[1] - first_head if nh is None else nh
    return pl.pallas_call(
        functools.partial(_bias_kernel, scale=scale, first_head=first_head),
        grid=(nh,),
        in_specs=[pl.BlockSpec(memory_space=pltpu.SMEM),
                  pl.BlockSpec((r, c), lambda h: (0, 0))],
        out_specs=pl.BlockSpec((1, r, c), lambda h: (h, 0, 0)),
        out_shape=jax.ShapeDtypeStruct((nh, r, c), F32),
        compiler_params=_cparams(("arbitrary",)),
        name="bias_table",
    )(rel_bias, dist)


def _proj_kernel(x_ref, mod_ref, g_ref, w_ref, qkg_ref, avg_ref,
                 mq_ref, mkv_ref, nq_ref, nkv_ref, wkv_ref, gt_ref, *page_major_refs, q_scale):
    x = x_ref[0]
    mod = mod_ref[0]
    h = _modulate(x, g_ref[...], mod[:, 0:D], mod[:, D:2 * D])
    z = _dot(h.astype(BF16), w_ref[...])
    avg = avg_ref[...]

    def normed(lo, gi):
        zs = z[:, lo:lo + LANES]
        return zs * lax.rsqrt(_group_mean_sq(zs, avg) + EPS) * qkg_ref[gi:gi + 1, :]

    for t in range(4):
        mq_ref[0, :, t * LANES:(t + 1) * LANES] = normed(t * LANES, 0) * q_scale
        mkv_ref[0, :, t * LANES:(t + 1) * LANES] = normed(512 + t * LANES, 1)
        nq_ref[0, :, t * LANES:(t + 1) * LANES] = normed(1536 + t * LANES, 2) * q_scale
    mkv_ref[0, :, 512:1024] = z[:, 1024:1536]
    nkv_ref[0, :, 0:256] = z[:, 2048:2304]
    nkv_ref[0, :, 256:384] = normed(2304, 4)
    nkv_ref[0, :, 384:512] = z[:, 2432:2560]
    wkv_ref[0, :, 0:128] = normed(2560, 5)
    wkv_ref[0, :, 128:256] = z[:, 2688:2816]
    gt_ref[0] = jax.nn.sigmoid(z[:, 2816:2944])
    if page_major_refs:
        mkv_t_ref, nkv_t_ref = page_major_refs
        for p in range(x.shape[0] // PAGE):
            mkv_t_ref[0, p] = mkv_ref[0, p * PAGE:(p + 1) * PAGE, :].T
            nkv_t_ref[0, p] = nkv_ref[0, p * PAGE:(p + 1) * PAGE, :].T


def _attn_proj(x, mod, g, w_pad, qkg_t, avg, tm, q_scale, page_major=False):
    b, t, _ = x.shape
    tmod = mod.shape[1]
    row = lambda width: pl.BlockSpec((1, tm, width), lambda i, j: (i, j, 0))
    shp = lambda width: jax.ShapeDtypeStruct((b, t, width), F32)
    out_specs = [row(512), row(1024), row(512), row(512), row(256), row(128)]
    out_shape = [shp(512), shp(1024), shp(512), shp(512), shp(256), shp(128)]
    if page_major:
        for width in (1024, 512):
            out_specs.append(pl.BlockSpec((1, tm // PAGE, width, PAGE), lambda i, j: (i, j, 0, 0)))
            out_shape.append(jax.ShapeDtypeStruct((b, t // PAGE, width, PAGE), F32))
    return pl.pallas_call(
        functools.partial(_proj_kernel, q_scale=q_scale),
        grid=(b, t // tm),
        in_specs=[row(D),
                  pl.BlockSpec((1, tmod, 3 * D), lambda i, j: (i, 0, 0)),
                  pl.BlockSpec((1, D), lambda i, j: (0, 0)),
                  pl.BlockSpec((D, IN_COLS_PAD), lambda i, j: (0, 0)),
                  pl.BlockSpec((8, LANES), lambda i, j: (0, 0)),
                  pl.BlockSpec((LANES, LANES), lambda i, j: (0, 0))],
        out_specs=out_specs,
        out_shape=out_shape,
        compiler_params=_cparams(("arbitrary", "arbitrary")),
        name="attn_proj",
    )(x, mod, g, w_pad, qkg_t, avg)


def _rank_rows(score, rowi, ncand):
    rank = jnp.zeros(score.shape, F32)
    for m in range(ncand):
        rm = score[m:m + 1, :]
        rank = rank + jnp.where(rm > score, 1.0, 0.0) + jnp.where((rm == score) & (m < rowi), 1.0, 0.0)
    return rank


def _columns_from_rows(x_t):
    pad = jnp.zeros((LANES - x_t.shape[0], x_t.shape[1]), F32)
    return jnp.concatenate([x_t, pad], axis=0).T


def _softmax_pv(pieces, v_all):
    m = pieces[0]
    for s in pieces[1:]:
        m = jnp.maximum(m, s)
    m = jnp.maximum(jnp.max(m, axis=1, keepdims=True), M_INIT)
    ps = [jnp.exp2(s - m) for s in pieces]
    tot = ps[0]
    for p in ps[1:]:
        tot = tot + p
    l = jnp.sum(tot, axis=1, keepdims=True)
    p_all = jnp.concatenate([p.astype(BF16) for p in ps], axis=1) if len(ps) > 1 else ps[0].astype(BF16)
    return _dot(p_all, v_all) / jnp.maximum(l, 1e-30)


def _moba_prompt_kernel(q_ref, k_ref, v_ref, t_ref, o_ref, km_scr, kb_scr, vb_scr):
    s_len = q_ref.shape[1]
    nblk = s_len // MOBA_BLOCK
    nq = s_len // TQ
    lane = lax.broadcasted_iota(jnp.int32, (TQ, LANES), 1)
    rowb = lax.broadcasted_iota(jnp.int32, (nblk, TQ), 0)
    km_scr[...] = jnp.zeros(km_scr.shape, F32)
    for n in range(nblk):
        km_scr[n:n + 1, :] = jnp.mean(k_ref[0, n * MOBA_BLOCK:(n + 1) * MOBA_BLOCK, :], axis=0, keepdims=True)
    kmean = km_scr[...]
    kb_scr[...] = k_ref[0].astype(BF16)
    vb_scr[...] = v_ref[0].astype(BF16)

    def qtile(i, _):
        r0 = pl.multiple_of(i * TQ, TQ)
        q2 = q_ref[0, pl.ds(r0, TQ), :]
        qbs, cbs = [], []
        for e in range(2):
            qe = jnp.where(lane // HEAD_DIM == e, q2, 0.0)
            gate_t = _dot_nt(kmean, qe, precision=HIGHEST)[0:nblk]
            gm = jnp.where(rowb < i, gate_t, -jnp.inf)
            sel = ((_rank_rows(gm, rowb, nblk) < MOBA_TOPK) & (rowb < i)) | (rowb == i)
            cbs.append(_columns_from_rows(jnp.where(sel, 0.0, NEG)))
            qbs.append(qe.astype(BF16))

        for c in range(1, nq // CASE_TILES + 1):
            @pl.when(i // CASE_TILES + 1 == c)
            def _():
                ntile = CASE_TILES * c
                kall = kb_scr[0:ntile * TQ, :]
                vall = vb_scr[0:ntile * TQ, :]
                outs = []
                for e in range(2):
                    s = _dot_nt(qbs[e], kall)
                    far_bias = t_ref[e, 2, 0:1, 0:1]
                    pieces = []
                    for n in range(ntile):
                        seg = s[:, n * TQ:(n + 1) * TQ]
                        if n >= ntile - CASE_TILES - 1:
                            pieces.append(seg + t_ref[e, jnp.clip(i - n, 0, 2)] + cbs[e][:, n:n + 1])
                        else:
                            pieces.append(seg + (cbs[e][:, n:n + 1] + far_bias))
                    outs.append(_softmax_pv(pieces, vall))
                o_ref[0, pl.ds(r0, TQ), :] = jnp.where(lane < HEAD_DIM, outs[0], outs[1])
        return 0

    lax.fori_loop(0, nq, qtile, 0)


def _moba_prompt(mq, mkv, tb):
    b, s, _ = mq.shape
    npair = MOBA_HEADS // 2
    return pl.pallas_call(
        _moba_prompt_kernel,
        grid=(b, npair),
        in_specs=[pl.BlockSpec((1, s, LANES), lambda i, p: (i, 0, p)),
                  pl.BlockSpec((1, s, LANES), lambda i, p: (i, 0, p)),
                  pl.BlockSpec((1, s, LANES), lambda i, p: (i, 0, npair + p)),
                  pl.BlockSpec((2, N_BIAS_TILES, TQ, TQ), lambda i, p: (p, 0, 0, 0))],
        out_specs=pl.BlockSpec((1, s, LANES), lambda i, p: (i, 0, p)),
        out_shape=jax.ShapeDtypeStruct((b, s, 512), F32),
        scratch_shapes=[pltpu.VMEM((LANES, LANES), F32), pltpu.VMEM((s, LANES), BF16), pltpu.VMEM((s, LANES), BF16)],
        compiler_params=_cparams(("arbitrary", "arbitrary")),
        name="moba_prompt",
    )(mq, mkv, mkv, tb)


def _compress_tokens(load_k, load_v, pos_ref, w1_ref, w2_ref):
    hk = hv = None
    for r in range(0, CMP_BLOCK, 2):
        xk = [(load_k(r + t) + pos_ref[r + t:r + t + 1, 0:LANES]).astype(BF16) for t in range(2)]
        xv = [(load_v(r + t) + pos_ref[r + t:r + t + 1, LANES:2 * LANES]).astype(BF16) for t in range(2)]
        dk = _dot(jnp.concatenate(xk, axis=1), w1_ref[0, r // 2])
        dv = _dot(jnp.concatenate(xv, axis=1), w1_ref[1, r // 2])
        hk = dk if hk is None else hk + dk
        hv = dv if hv is None else hv + dv
    ck = _dot(jax.nn.gelu(hk).astype(BF16), w2_ref[0])
    cv = _dot(jax.nn.gelu(hv).astype(BF16), w2_ref[1])
    return ck, cv


def _cmp_prompt_kernel(xk_ref, xv_ref, pos_ref, w1_ref, w2_ref, gkc_ref, avg_ref, kc_ref, vc_ref):
    nblk = xk_ref.shape[1] // CMP_BLOCK
    ck, cv = _compress_tokens(lambda r: xk_ref[0, pl.ds(r, nblk, stride=CMP_BLOCK), :],
                              lambda r: xv_ref[0, pl.ds(r, nblk, stride=CMP_BLOCK), :], pos_ref, w1_ref, w2_ref)
    ck = ck * lax.rsqrt(_group_mean_sq(ck, avg_ref[...]) + EPS) * gkc_ref[...]
    kc_ref[0] = jnp.zeros((LANES, LANES), F32)
    vc_ref[0] = jnp.zeros((LANES, LANES), F32)
    kc_ref[0, 0:nblk, :] = ck
    vc_ref[0, 0:nblk, :] = cv


def _cmp_prompt(nkv, pos, w1bd, w2bd, gkc, avg):
    b, s, _ = nkv.shape
    const = lambda shape: pl.BlockSpec(shape, lambda i: (0,) * len(shape))
    return pl.pallas_call(
        _cmp_prompt_kernel,
        grid=(b,),
        in_specs=[pl.BlockSpec((1, s, LANES), lambda i: (i, 0, 0)), pl.BlockSpec((1, s, LANES), lambda i: (i, 0, 1)),
                  const((CMP_BLOCK, 256)), const((2, CMP_BLOCK // 2, 256, 256)), const((2, 256, LANES)),
                  const((1, LANES)), const((LANES, LANES))],
        out_specs=[pl.BlockSpec((1, LANES, LANES), lambda i: (i, 0, 0))] * 2,
        out_shape=[jax.ShapeDtypeStruct((b, LANES, LANES), F32)] * 2,
        compiler_params=_cparams(("arbitrary",)),
        name="nsa_compress_prompt",
    )(nkv, nkv, pos, w1bd, w2bd, gkc, avg)


def _nsa_prompt_kernel(q_ref, ks_ref, vs_ref, kw_ref, vw_ref, kc_ref, vc_ref, g_ref, t_ref, tc_ref,
                       o_ref, ksb_scr, vsb_scr, kwb_scr, vwb_scr, ex_scr):
    s_len = q_ref.shape[1]
    k = pl.program_id(1)
    lane = lax.broadcasted_iota(jnp.int32, (TQ, LANES), 1)
    kvmask = (lane // HEAD_DIM) == k
    kc = kc_ref[0].astype(BF16)
    vc = vc_ref[0].astype(BF16)
    nsel = s_len // SEL_BLOCK
    nq = s_len // TQ
    ncmp = s_len // CMP_BLOCK
    rowb = lax.broadcasted_iota(jnp.int32, (nsel, TQ), 0)
    qpos = lax.broadcasted_iota(jnp.int32, (nsel, TQ), 1)
    pair_r = lax.broadcasted_iota(jnp.int32, (nsel, LANES), 0)
    pair_c = lax.broadcasted_iota(jnp.int32, (nsel, LANES), 1)
    pair_t = jnp.where((pair_c // (SEL_BLOCK // CMP_BLOCK) == pair_r) & (pair_c < ncmp), 1.0, 0.0)
    e_r = lax.broadcasted_iota(jnp.int32, (LANES, s_len), 0)
    e_c = lax.broadcasted_iota(jnp.int32, (LANES, s_len), 1)
    ex_scr[...] = jnp.where(e_r == e_c // SEL_BLOCK, 1.0, 0.0).astype(BF16)
    ksb_scr[...] = ks_ref[0].astype(BF16)
    vsb_scr[...] = vs_ref[0].astype(BF16)
    kwb_scr[...] = kw_ref[0].astype(BF16)
    vwb_scr[...] = vw_ref[0].astype(BF16)

    def qtile(i, _):
        r0 = pl.multiple_of(i * TQ, TQ)
        qs = []
        for h in range(NSA_GROUP):
            q2 = q_ref[0, pl.ds(r0, TQ), (h // 2) * LANES:(h // 2 + 1) * LANES]
            qa = jnp.where(k == (h % 2), q2, pltpu.roll(q2, HEAD_DIM, 1))
            qs.append(jnp.where(kvmask, qa, 0.0).astype(BF16))

        imp = jnp.zeros((TQ, LANES), F32)
        o_cmp = []
        for h in range(NSA_GROUP):
            s = _dot_nt(qs[h], kc) + tc_ref[h, pl.ds(r0, TQ), :]
            m = jnp.maximum(jnp.max(s, axis=1, keepdims=True), M_INIT)
            p = jnp.exp2(s - m)
            p = p / jnp.maximum(jnp.sum(p, axis=1, keepdims=True), 1e-30)
            imp = imp + p
            o_cmp.append(_dot(p.astype(BF16), vc))

        imp_t = _dot_nt(pair_t, imp, precision=HIGHEST)
        own = (r0 + qpos) // SEL_BLOCK
        sc = jnp.where(rowb < own, imp_t, -jnp.inf)
        sel = ((_rank_rows(sc, rowb, nsel) < SEL_TOPK) & (rowb < own)) | (rowb == own)
        selb = _columns_from_rows(jnp.where(sel, 1.0, 0.0)).astype(BF16)

        wk, wv, wt = [], [], []
        for j, tidx in enumerate((3, 1, 0)):
            n = i - 2 + j
            c0 = pl.multiple_of(jnp.maximum(n, 0) * TQ, TQ)
            wk.append(kwb_scr[pl.ds(c0, TQ), :])
            wv.append(vwb_scr[pl.ds(c0, TQ), :])
            wt.append(jnp.where(n < 0, N_BIAS_TILES - 1, tidx))
        kw_all = jnp.concatenate(wk, axis=0)
        vw_all = jnp.concatenate(wv, axis=0)
        g = g_ref[0, pl.ds(r0, TQ), :]
        o_win = []
        for h in range(NSA_GROUP):
            s = _dot_nt(qs[h], kw_all)
            o_win.append(_softmax_pv([s[:, j * TQ:(j + 1) * TQ] + t_ref[h, wt[j]] for j in range(3)], vw_all))

        for c in range(1, nq // CASE_TILES + 1):
            @pl.when(i // CASE_TILES + 1 == c)
            def _():
                nkeys = CASE_TILES * c * TQ
                addm = (_dot(selb, ex_scr[:, 0:nkeys]) - 1.0) * (-NEG)
                kall = ksb_scr[0:nkeys, :]
                vall = vsb_scr[0:nkeys, :]
                res = []
                for h in range(NSA_GROUP):
                    s = _dot_nt(qs[h], kall) + addm
                    pieces = [s[:, n * TQ:(n + 1) * TQ] + t_ref[h, jnp.clip(i - n, 0, 2)]
                              for n in range(CASE_TILES * c)]
                    o_sel = _softmax_pv(pieces, vall)
                    hg = (k * NSA_GROUP + h) * 3
                    o = (_col(g, lane, hg) * o_cmp[h] + _col(g, lane, hg + 1) * o_sel
                         + _col(g, lane, hg + 2) * o_win[h])
                    res.append(jnp.where(k == (h % 2), o, pltpu.roll(o, HEAD_DIM, 1)))
                for t in range(2):
                    o_ref[0, pl.ds(r0, TQ), t * LANES:(t + 1) * LANES] = jnp.where(
                        lane < HEAD_DIM, res[2 * t], res[2 * t + 1])
        return 0

    lax.fori_loop(0, nq, qtile, 0)


def _nsa_prompt(nq, nkv, wkv, kcmp, vcmp, gates, tb, tc):
    b, s, _ = nq.shape
    col = lambda arr_cols, cb: pl.BlockSpec((1, s, LANES), lambda i, k: (i, 0, cb))
    return pl.pallas_call(
        _nsa_prompt_kernel,
        grid=(b, 2),
        in_specs=[pl.BlockSpec((1, s, 256), lambda i, k: (i, 0, k)),
                  col(512, 2), col(512, 3), col(256, 0), col(256, 1),
                  pl.BlockSpec((1, LANES, LANES), lambda i, k: (i, 0, 0)),
                  pl.BlockSpec((1, LANES, LANES), lambda i, k: (i, 0, 0)),
                  pl.BlockSpec((1, s, LANES), lambda i, k: (i, 0, 0)),
                  pl.BlockSpec((NSA_GROUP, N_BIAS_TILES, TQ, TQ), lambda i, k: (k, 0, 0, 0)),
                  pl.BlockSpec((NSA_GROUP, s, LANES), lambda i, k: (k, 0, 0))],
        out_specs=pl.BlockSpec((1, s, 256), lambda i, k: (i, 0, k)),
        out_shape=jax.ShapeDtypeStruct((b, s, 512), F32),
        scratch_shapes=[pltpu.VMEM((s, LANES), BF16)] * 4 + [pltpu.VMEM((LANES, s), BF16)],
        compiler_params=_cparams(("arbitrary", "arbitrary")),
        name="nsa_prompt",
    )(nq, nkv, nkv, wkv, wkv, kcmp, vcmp, gates, tb, tc)


PAGES_PER_STEP = 16
CMP_PITCH = 40


def _rank_lt(score, lane, ncand, topk):
    rank = jnp.zeros(score.shape, F32)
    for m in range(ncand):
        col = score[:, m:m + 1]
        beats = (col > score) | ((col == score) & (m < lane))
        rank = rank + jnp.where(beats, 1.0, 0.0)
    return rank < topk


def _merge_blocks(sel, m_all, l_all, acc_scr, nblk, s_self, v_self):
    mx = jnp.maximum(jnp.max(jnp.where(sel, m_all, NEG), axis=1, keepdims=True), s_self)
    w = jnp.exp(jnp.where(sel, m_all - mx, NEG))
    w_self = jnp.exp(s_self - mx)
    den = jnp.sum(w * l_all, axis=1, keepdims=True) + w_self
    num = w_self * v_self
    for j in range(nblk):
        num = num + w[:, j:j + 1] * acc_scr[j]
    return num / den


def _moba_sample_kernel(pt_ref, *refs):
    pages = refs[:PAGES_PER_STEP]
    qm_ref, kn_ref, vn_ref, tsb_ref, misc_ref, o_ref, g_scr, m_scr, l_scr, acc_scr = refs[PAGES_PER_STEP:]
    s = pl.program_id(1)
    nstep = pl.num_programs(1)
    nblk = PAST_LEN // MOBA_BLOCK
    width = MOBA_HEADS * HEAD_DIM
    qm = qm_ref[0]
    qb = qm.astype(BF16)
    lane = lax.broadcasted_iota(jnp.int32, (MOBA_HEADS, LANES), 1)

    @pl.when(s == 0)
    def _():
        g_scr[...] = jnp.zeros(g_scr.shape, F32)
        m_scr[...] = jnp.zeros(m_scr.shape, F32)
        l_scr[...] = jnp.zeros(l_scr.shape, F32)

    npb = PAGES_PER_STEP // 2
    blk0 = s * npb
    kt_all = jnp.concatenate([pages[t][0, 0].astype(BF16) for t in range(PAGES_PER_STEP)], axis=1)
    vt_all = jnp.concatenate([pages[t][0, 1].astype(BF16) for t in range(PAGES_PER_STEP)], axis=1)
    raw = _dot(qb, kt_all)
    far = jnp.broadcast_to(misc_ref[:, 1:2], (MOBA_HEADS, MOBA_BLOCK))
    g_new, m_new, l_new = g_scr[...], m_scr[...], l_scr[...]
    p_rows = []
    for j in range(npb):
        seg = raw[:, j * MOBA_BLOCK:(j + 1) * MOBA_BLOCK]
        gate = jnp.sum(seg, axis=1, keepdims=True)
        sc = seg + (jnp.where(s == nstep - 1, tsb_ref[...], far) if j == npb - 1 else far)
        mj = jnp.max(sc, axis=1, keepdims=True)
        p = jnp.exp(sc - mj)
        g_new = jnp.where(lane == blk0 + j, gate, g_new)
        m_new = jnp.where(lane == blk0 + j, mj, m_new)
        l_new = jnp.where(lane == blk0 + j, jnp.sum(p, axis=1, keepdims=True), l_new)
        zeros = jnp.zeros((MOBA_HEADS, MOBA_BLOCK), F32)
        p_rows.append(jnp.concatenate([p if t == j else zeros for t in range(npb)], axis=1))
    g_scr[...] = g_new
    m_scr[...] = m_new
    l_scr[...] = l_new
    acc = _dot_nt(jnp.concatenate(p_rows, axis=0).astype(BF16), vt_all)
    for j in range(npb):
        acc_scr[blk0 + j] = acc[j * MOBA_HEADS:(j + 1) * MOBA_HEADS]

    @pl.when(s == nstep - 1)
    def _():
        gm = jnp.where(lane < nblk, g_scr[...], -jnp.inf)
        sel = _rank_lt(gm, lane, nblk, MOBA_TOPK) & (lane < nblk)
        s_self = jnp.sum(qm * kn_ref[0], axis=1, keepdims=True) + misc_ref[:, 0:1]
        o = _merge_blocks(sel, m_scr[...], l_scr[...], acc_scr, nblk, s_self, vn_ref[0])
        hrow = lax.broadcasted_iota(jnp.int32, (MOBA_HEADS, width), 0)
        hlane = lax.broadcasted_iota(jnp.int32, (MOBA_HEADS, width), 1)
        o_ref[0] = jnp.sum(jnp.where(hlane // HEAD_DIM == hrow, o, 0.0), axis=0, keepdims=True)


def _moba_sample(page_table, cache_t, qmat, knew, vnew, tsb, misc):
    nb, npages = page_table.shape
    nstep = npages // PAGES_PER_STEP
    width = MOBA_HEADS * HEAD_DIM
    nblk = PAST_LEN // MOBA_BLOCK

    def page_spec(j):
        return pl.BlockSpec((1, 2, width, PAGE), lambda b, s, pt: (pt[b, s * PAGES_PER_STEP + j], 0, 0, 0))

    per_b = lambda shape: pl.BlockSpec((1,) + shape, lambda b, s, pt: (b, 0, 0))
    const = lambda shape: pl.BlockSpec(shape, lambda b, s, pt: (0,) * len(shape))
    grid_spec = pltpu.PrefetchScalarGridSpec(
        num_scalar_prefetch=1,
        grid=(nb, nstep),
        in_specs=[page_spec(j) for j in range(PAGES_PER_STEP)]
        + [per_b((MOBA_HEADS, width)), per_b((1, width)), per_b((1, width)),
           const((MOBA_HEADS, MOBA_BLOCK)), const((MOBA_HEADS, LANES))],
        out_specs=per_b((1, width)),
        scratch_shapes=[pltpu.VMEM((MOBA_HEADS, LANES), F32)] * 3 + [pltpu.VMEM((nblk, MOBA_HEADS, width), F32)],
    )
    return pl.pallas_call(
        _moba_sample_kernel,
        grid_spec=grid_spec,
        out_shape=jax.ShapeDtypeStruct((nb, 1, width), F32),
        compiler_params=_cparams(("arbitrary", "arbitrary")),
        name="moba_sample",
    )(page_table, *([cache_t] * PAGES_PER_STEP), qmat, knew, vnew, tsb, misc)


def _nsa_sample_kernel(pt_ref, *refs):
    pages = refs[:PAGES_PER_STEP]
    (qm_ref, ksn_ref, vsn_ref, kwn_ref, vwn_ref, win_ref, g_ref, tsn_ref, misc_ref, tcs_ref, tws_ref,
     pos_ref, w1_ref, w2_ref, gkc_ref, avg_ref, o_ref, xk_scr, xv_scr, m_scr, l_scr, acc_scr) = refs[PAGES_PER_STEP:]
    s = pl.program_id(1)
    nstep = pl.num_programs(1)
    nsel = PAST_LEN // SEL_BLOCK
    ncmp = PAST_LEN // CMP_BLOCK
    qm = qm_ref[0]
    qb = qm.astype(BF16)
    lane = lax.broadcasted_iota(jnp.int32, (NSA_HEADS, LANES), 1)
    row = lax.broadcasted_iota(jnp.int32, (NSA_HEADS, LANES), 0)
    lo = lane < HEAD_DIM

    @pl.when(s == 0)
    def _():
        m_scr[...] = jnp.zeros(m_scr.shape, F32)
        l_scr[...] = jnp.zeros(l_scr.shape, F32)

    for j in range(PAGES_PER_STEP):
        pg = s * PAGES_PER_STEP + j
        kc = pages[j][0, 0:LANES, :].T
        vc = pages[j][0, LANES:2 * LANES, :].T
        for b4 in range(PAGE // CMP_BLOCK):
            r0 = pl.multiple_of((pg * (PAGE // CMP_BLOCK) + b4) * CMP_PITCH, 8)
            xk_scr[pl.ds(r0, CMP_BLOCK), :] = kc[b4 * CMP_BLOCK:(b4 + 1) * CMP_BLOCK, :]
            xv_scr[pl.ds(r0, CMP_BLOCK), :] = vc[b4 * CMP_BLOCK:(b4 + 1) * CMP_BLOCK, :]

    ks_all = jnp.concatenate([pages[t][0, 256:384, :].astype(BF16) for t in range(PAGES_PER_STEP)], axis=1)
    vs_all = jnp.concatenate([pages[t][0, 384:512, :].astype(BF16) for t in range(PAGES_PER_STEP)], axis=1)
    raw = _dot(qb, ks_all)
    far = jnp.broadcast_to(misc_ref[:, 1:2], (NSA_HEADS, PAGE))
    m_new, l_new = m_scr[...], l_scr[...]
    b0 = 2 * s * PAGES_PER_STEP
    zeros = jnp.zeros((NSA_HEADS, PAGE), F32)
    p_rows = []
    for j in range(PAGES_PER_STEP):
        sc = raw[:, j * PAGE:(j + 1) * PAGE]
        sc = sc + (jnp.where(s == nstep - 1, tsn_ref[...], far) if j == PAGES_PER_STEP - 1 else far)
        m0 = jnp.max(jnp.where(lo, sc, NEG), axis=1, keepdims=True)
        m1 = jnp.max(jnp.where(lo, NEG, sc), axis=1, keepdims=True)
        p = jnp.exp(sc - jnp.where(lo, m0, m1))
        p0, p1 = jnp.where(lo, p, 0.0), jnp.where(lo, 0.0, p)
        l0 = jnp.sum(p0, axis=1, keepdims=True)
        l1 = jnp.sum(p1, axis=1, keepdims=True)
        bj = b0 + 2 * j
        m_new = jnp.where(lane == bj, m0, jnp.where(lane == bj + 1, m1, m_new))
        l_new = jnp.where(lane == bj, l0, jnp.where(lane == bj + 1, l1, l_new))
        for ph in (p0, p1):
            p_rows.append(jnp.concatenate([ph if t == j else zeros for t in range(PAGES_PER_STEP)], axis=1))
    m_scr[...] = m_new
    l_scr[...] = l_new
    acc = _dot_nt(jnp.concatenate(p_rows, axis=0).astype(BF16), vs_all)
    for b in range(2 * PAGES_PER_STEP):
        acc_scr[b0 + b] = acc[b * NSA_HEADS:(b + 1) * NSA_HEADS]

    @pl.when(s == nstep - 1)
    def _():
        ck, cv = _compress_tokens(lambda r: xk_scr[pl.ds(r, ncmp, stride=CMP_PITCH), :],
                                  lambda r: xv_scr[pl.ds(r, ncmp, stride=CMP_PITCH), :], pos_ref, w1_ref, w2_ref)
        ck = ck * lax.rsqrt(_group_mean_sq(ck, avg_ref[...]) + EPS) * gkc_ref[...]
        sc = _dot_nt(qb, ck.astype(BF16)) + tcs_ref[...]
        m = jnp.maximum(jnp.max(sc, axis=1, keepdims=True), M_INIT)
        pc = jnp.exp(sc - m)
        pc = pc / jnp.maximum(jnp.sum(pc, axis=1, keepdims=True), 1e-30)
        o_cmp = _dot(pc.astype(BF16), cv.astype(BF16))
        g0 = pc[0:1] + pc[1:2] + pc[2:3] + pc[3:4]
        g1 = pc[4:5] + pc[5:6] + pc[6:7] + pc[7:8]
        rowc = lax.broadcasted_iota(jnp.int32, (NSA_HEADS, ncmp), 0)
        imp = jnp.where(rowc < NSA_GROUP, g0, g1)
        pr = lax.broadcasted_iota(jnp.int32, (ncmp, LANES), 0)
        pc_ = lax.broadcasted_iota(jnp.int32, (ncmp, LANES), 1)
        pair = jnp.where(pr // (SEL_BLOCK // CMP_BLOCK) == pc_, 1.0, 0.0)
        impb = _dot(imp, pair, precision=HIGHEST)
        own = PAST_LEN // SEL_BLOCK
        sel = _rank_lt(jnp.where(lane < own, impb, -jnp.inf), lane, nsel, SEL_TOPK) & (lane < own)
        s_self = jnp.sum(qm * ksn_ref[0], axis=1, keepdims=True) + misc_ref[:, 0:1]
        o_sel = _merge_blocks(sel, m_scr[...], l_scr[...], acc_scr, nsel, s_self, vsn_ref[0])
        kw_t = win_ref[0, 0:LANES, :].astype(BF16)
        vw_t = win_ref[0, LANES:2 * LANES, :].astype(BF16)
        sw = _dot(qb, kw_t) + tws_ref[...]
        sw_self = jnp.sum(qm * kwn_ref[0], axis=1, keepdims=True) + misc_ref[:, 0:1]
        mw = jnp.maximum(jnp.max(sw, axis=1, keepdims=True), sw_self)
        pw = jnp.exp(sw - mw)
        pw_self = jnp.exp(sw_self - mw)
        o_win = ((_dot_nt(pw.astype(BF16), vw_t) + pw_self * vwn_ref[0])
                 / (jnp.sum(pw, axis=1, keepdims=True) + pw_self))
        gt = jnp.broadcast_to(g_ref[0], (NSA_HEADS, LANES))
        o8 = (_col(gt, lane, 3 * row) * o_cmp + _col(gt, lane, 3 * row + 1) * o_sel
              + _col(gt, lane, 3 * row + 2) * o_win)
        lane1 = lax.broadcasted_iota(jnp.int32, (1, LANES), 1)
        tiles = []
        for t in range(NSA_HEADS // 2):
            ha, hb = 2 * t, 2 * t + 1
            ra = o8[ha:ha + 1, :]
            rb = o8[hb:hb + 1, :]
            if ha // NSA_GROUP == 1:
                ra = pltpu.roll(ra, HEAD_DIM, 1)
            if hb // NSA_GROUP == 0:
                rb = pltpu.roll(rb, HEAD_DIM, 1)
            tiles.append(jnp.where(lane1 < HEAD_DIM, ra, rb))
        o_ref[0] = jnp.concatenate(tiles, axis=1)


def _nsa_sample(page_table, cache, qmat, ksn, vsn, kwn, vwn, win, gates, tsn, misc, tcs, tws,
                pos, w1bd, w2bd, gkc, avg):
    nb, npages = page_table.shape
    nstep = npages // PAGES_PER_STEP
    nsel = PAST_LEN // SEL_BLOCK

    def page_spec(j):
        return pl.BlockSpec((1, 512, PAGE), lambda b, s, pt: (pt[b, s * PAGES_PER_STEP + j], 0, 0))

    per_b = lambda shape: pl.BlockSpec((1,) + shape, lambda b, s, pt: (b, 0, 0))
    const = lambda shape: pl.BlockSpec(shape, lambda b, s, pt: (0,) * len(shape))
    cmp_rows = PAST_LEN // CMP_BLOCK * CMP_PITCH
    grid_spec = pltpu.PrefetchScalarGridSpec(
        num_scalar_prefetch=1,
        grid=(nb, nstep),
        in_specs=[page_spec(j) for j in range(PAGES_PER_STEP)]
        + [per_b((NSA_HEADS, LANES)), per_b((1, LANES)), per_b((1, LANES)), per_b((1, LANES)), per_b((1, LANES)),
           per_b((256, WINDOW)), per_b((1, LANES)),
           const((NSA_HEADS, PAGE)), const((NSA_HEADS, LANES)), const((NSA_HEADS, PAST_LEN // CMP_BLOCK)),
           const((NSA_HEADS, WINDOW)),
           const((CMP_BLOCK, 256)), const((2, CMP_BLOCK // 2, 256, 256)), const((2, 256, LANES)),
           const((1, LANES)), const((LANES, LANES))],
        out_specs=per_b((1, 512)),
        scratch_shapes=[pltpu.VMEM((cmp_rows, LANES), F32), pltpu.VMEM((cmp_rows, LANES), F32),
                        pltpu.VMEM((NSA_HEADS, LANES), F32),
                        pltpu.VMEM((NSA_HEADS, LANES), F32), pltpu.VMEM((nsel, NSA_HEADS, LANES), F32)],
    )
    return pl.pallas_call(
        _nsa_sample_kernel,
        grid_spec=grid_spec,
        out_shape=jax.ShapeDtypeStruct((nb, 1, 512), F32),
        compiler_params=_cparams(("arbitrary", "arbitrary")),
        name="nsa_sample",
    )(page_table, *([cache] * PAGES_PER_STEP), qmat, ksn, vsn, kwn, vwn, win, gates, tsn, misc, tcs, tws,
      pos, w1bd, w2bd, gkc, avg)


def _outproj_kernel(x_ref, mod_ref, om_ref, on_ref, w_ref, o_ref):
    y = _dot(om_ref[0].astype(BF16), w_ref[0:512, :]) + _dot(on_ref[0].astype(BF16), w_ref[512:1024, :])
    o_ref[0] = x_ref[0] + mod_ref[0][:, 2 * D:3 * D] * y


def _outproj(x, mod, o_m, o_n, w, tm):
    b, t, _ = x.shape
    tmod = mod.shape[1]
    row = lambda width: pl.BlockSpec((1, tm, width), lambda i, j: (i, j, 0))
    return pl.pallas_call(
        _outproj_kernel,
        grid=(b, t // tm),
        in_specs=[row(D), pl.BlockSpec((1, tmod, 3 * D), lambda i, j: (i, 0, 0)), row(512), row(512),
                  pl.BlockSpec((D, D), lambda i, j: (0, 0))],
        out_specs=row(D),
        out_shape=jax.ShapeDtypeStruct((b, t, D), F32),
        compiler_params=_cparams(("arbitrary", "arbitrary")),
        name="attn_outproj",
    )(x, mod, o_m, o_n, w)


def _mlp_kernel(x_ref, mod_ref, g_ref, w1_ref, w2_ref, o_ref, h_scr, acc_scr):
    kf = pl.program_id(2)

    @pl.when(kf == 0)
    def _():
        mod = mod_ref[0]
        h_scr[...] = _modulate(x_ref[0], g_ref[...], mod[:, 0:D], mod[:, D:2 * D]).astype(BF16)
        acc_scr[...] = jnp.zeros(acc_scr.shape, F32)

    a = jnp.square(jnp.maximum(_dot(h_scr[...], w1_ref[...]), 0.0))
    acc_scr[...] += _dot(a.astype(BF16), w2_ref[...])

    @pl.when(kf == pl.num_programs(2) - 1)
    def _():
        o_ref[0] = x_ref[0] + mod_ref[0][:, 2 * D:3 * D] * acc_scr[...]


def _mlp(x, mod, g, w1, w2, tm, tf):
    b, t, _ = x.shape
    tmod = mod.shape[1]
    return pl.pallas_call(
        _mlp_kernel,
        grid=(b, t // tm, D_FF // tf),
        in_specs=[pl.BlockSpec((1, tm, D), lambda i, j, kf: (i, j, 0)),
                  pl.BlockSpec((1, tmod, 3 * D), lambda i, j, kf: (i, 0, 0)),
                  pl.BlockSpec((1, D), lambda i, j, kf: (0, 0)),
                  pl.BlockSpec((D, tf), lambda i, j, kf: (0, kf)),
                  pl.BlockSpec((tf, D), lambda i, j, kf: (kf, 0))],
        out_specs=pl.BlockSpec((1, tm, D), lambda i, j, kf: (i, j, 0)),
        out_shape=jax.ShapeDtypeStruct((b, t, D), F32),
        scratch_shapes=[pltpu.VMEM((tm, D), BF16), pltpu.VMEM((tm, D), F32)],
        compiler_params=_cparams(("arbitrary", "arbitrary", "arbitrary")),
        name="mlp",
    )(x, mod, g, w1, w2)


def _s5_disc_kernel(are_ref, aim_ref, ldt_ref, bre_ref, bim_ref, abre_ref, abim_ref, bbre_ref, bbim_ref):
    a_re, a_im = are_ref[...], aim_ref[...]
    dt = jnp.exp(ldt_ref[...])
    decay = jnp.exp(dt * a_re)
    ab_re, ab_im = decay * jnp.cos(dt * a_im), decay * jnp.sin(dt * a_im)
    den = a_re * a_re + a_im * a_im
    f_re = ((ab_re - 1) * a_re + ab_im * a_im) / den
    f_im = (ab_im * a_re - (ab_re - 1) * a_im) / den
    br, bi = bre_ref[...], bim_ref[...]
    abre_ref[...] = ab_re
    abim_ref[...] = ab_im
    bbre_ref[...] = f_re * br - f_im * bi
    bbim_ref[...] = f_re * bi + f_im * br


def _s5_discretize(a_re, a_im, log_dt, b_re, b_im):
    rep = lambda a: jnp.repeat(a, S5_GROUP_CH, axis=1)
    shp = jax.ShapeDtypeStruct((S5_GROUPS, S5_STATE * S5_GROUP_CH), F32)
    ldt = jnp.broadcast_to(log_dt[:, None], (S5_GROUPS, S5_STATE * S5_GROUP_CH))
    flat = lambda a: a.reshape(S5_GROUPS, S5_STATE * S5_GROUP_CH)
    ab_re, ab_im, bb_re, bb_im = pl.pallas_call(
        _s5_disc_kernel, out_shape=[shp] * 4, name="s5_discretize",
    )(rep(a_re), rep(a_im), ldt, flat(b_re), flat(b_im))
    unrep = lambda a: a[:, ::S5_GROUP_CH]
    unflat = lambda a: a.reshape(S5_GROUPS, S5_STATE, S5_GROUP_CH)
    return unrep(ab_re), unrep(ab_im), unflat(bb_re), unflat(bb_im)


def _modulate_tm_kernel(x_ref, mod_ref, g_ref, o_ref):
    mod = mod_ref[0]
    o_ref[...] = _modulate(x_ref[0], g_ref[...], mod[:, 0:D], mod[:, D:2 * D])


def _modulate_time_major(x, mod, g, tl):
    b, t, _ = x.shape
    tmod = mod.shape[1]
    return pl.pallas_call(
        _modulate_tm_kernel,
        grid=(b, t // tl),
        in_specs=[pl.BlockSpec((1, tl, D), lambda i, j: (i, j, 0)),
                  pl.BlockSpec((1, tmod, 3 * D), lambda i, j: (i, 0, 0)),
                  pl.BlockSpec((1, D), lambda i, j: (0, 0))],
        out_specs=pl.BlockSpec((tl, D), lambda i, j: (j, i)),
        out_shape=jax.ShapeDtypeStruct((t, b * D), F32),
        compiler_params=_cparams(("arbitrary", "arbitrary")),
        name="s5_modulate",
    )(x, mod, g)


S5_CB = 256
S5_NS = S5_CB // S5_GROUP_CH * S5_STATE


def _s5_scan_kernel(h_ref, wb_ref, wc_ref, ar_ref, ai_ref, d_ref, h0_ref, y_ref, so_ref, xs_scr, st_scr, *, tl, r):
    i = pl.program_id(1)

    @pl.when(i == 0)
    def _():
        st_scr[...] = h0_ref[...]

    u = h_ref[...]
    xs_scr[...] = _dot(u.astype(BF16), wb_ref[0])
    ar = jnp.broadcast_to(ar_ref[0], (r, S5_NS))
    ai = jnp.broadcast_to(ai_ref[0], (r, S5_NS))

    def step(t, carry):
        xr, xi = carry
        r0 = pl.multiple_of(t * r, r)
        nr = ar * xr - ai * xi + xs_scr[pl.ds(r0, r), 0:S5_NS]
        ni = ar * xi + ai * xr + xs_scr[pl.ds(r0, r), S5_NS:2 * S5_NS]
        xs_scr[pl.ds(r0, r), 0:S5_NS] = nr
        xs_scr[pl.ds(r0, r), S5_NS:2 * S5_NS] = ni
        return nr, ni

    xr, xi = lax.fori_loop(0, tl, step, (st_scr[0], st_scr[1]))
    st_scr[0] = xr
    st_scr[1] = xi
    y_ref[...] = _dot(xs_scr[...].astype(BF16), wc_ref[0]) + d_ref[...] * u

    @pl.when(i == pl.num_programs(1) - 1)
    def _():
        so_ref[...] = st_scr[...]


def _s5_scan(h_tm, wb, wc, ar, ai, d_skip, h0, r, tl):
    rows = h_tm.shape[0]
    nj = D // S5_CB
    return pl.pallas_call(
        functools.partial(_s5_scan_kernel, tl=tl, r=r),
        grid=(nj, rows // (tl * r)),
        in_specs=[pl.BlockSpec((tl * r, S5_CB), lambda j, i: (i, j)),
                  pl.BlockSpec((1, S5_CB, 2 * S5_NS), lambda j, i: (j, 0, 0)),
                  pl.BlockSpec((1, 2 * S5_NS, S5_CB), lambda j, i: (j, 0, 0)),
                  pl.BlockSpec((1, 1, S5_NS), lambda j, i: (j, 0, 0)),
                  pl.BlockSpec((1, 1, S5_NS), lambda j, i: (j, 0, 0)),
                  pl.BlockSpec((1, S5_CB), lambda j, i: (0, j)),
                  pl.BlockSpec((2, r, S5_NS), lambda j, i: (0, 0, j))],
        out_specs=[pl.BlockSpec((tl * r, S5_CB), lambda j, i: (i, j)),
                   pl.BlockSpec((2, r, S5_NS), lambda j, i: (0, 0, j))],
        out_shape=[jax.ShapeDtypeStruct((rows, D), F32), jax.ShapeDtypeStruct((2, r, S5_GROUPS * S5_STATE), F32)],
        scratch_shapes=[pltpu.VMEM((tl * r, 2 * S5_NS), F32), pltpu.VMEM((2, r, S5_NS), F32)],
        compiler_params=_cparams(("arbitrary", "arbitrary")),
        name="s5_scan",
    )(h_tm, wb, wc, ar, ai, d_skip, h0)


def _glu_kernel(y_ref, x_ref, mod_ref, w_ref, o_ref):
    z = _dot(jax.nn.gelu(y_ref[...]).astype(BF16), w_ref[...])
    o_ref[0] = x_ref[0] + mod_ref[0][:, 2 * D:3 * D] * (z[:, 0:D] * jax.nn.sigmoid(z[:, D:2 * D]))


def _glu_residual(y_tm, x, mod, w, tl):
    b, t, _ = x.shape
    tmod = mod.shape[1]
    return pl.pallas_call(
        _glu_kernel,
        grid=(b, t // tl),
        in_specs=[pl.BlockSpec((tl, D), lambda i, j: (j, i)),
                  pl.BlockSpec((1, tl, D), lambda i, j: (i, j, 0)),
                  pl.BlockSpec((1, tmod, 3 * D), lambda i, j: (i, 0, 0)),
                  pl.BlockSpec((D, 2 * D), lambda i, j: (0, 0))],
        out_specs=pl.BlockSpec((1, tl, D), lambda i, j: (i, j, 0)),
        out_shape=jax.ShapeDtypeStruct((b, t, D), F32),
        compiler_params=_cparams(("arbitrary", "arbitrary")),
        name="s5_glu",
    )(y_tm, x, mod, w)


def _s5_block_weights(bb_re, bb_im, c_re, c_im):
    nj, ng = D // S5_CB, S5_CB // S5_GROUP_CH
    eye = jnp.eye(ng, dtype=F32)

    def wb_part(bb):
        t = bb.reshape(nj, ng, S5_STATE, S5_GROUP_CH).transpose(0, 1, 3, 2)
        return jnp.einsum("jgcn,gh->jgchn", t, eye).reshape(nj, S5_CB, S5_NS)

    def wc_part(c):
        t = c.reshape(nj, ng, S5_GROUP_CH, S5_STATE).transpose(0, 1, 3, 2)
        return jnp.einsum("jgnc,gh->jgnhc", t, eye).reshape(nj, S5_NS, S5_CB)

    wb = jnp.concatenate([wb_part(bb_re), wb_part(bb_im)], axis=2).astype(BF16)
    wc = jnp.concatenate([wc_part(c_re), -wc_part(c_im)], axis=1).astype(BF16)
    return wb, wc


S5_TL = 32
S5_PITCH = 40
S5_NSLAB = 2 * S5_GROUPS * S5_STATE // LANES


def _s5_fused_kernel(x_ref, mod_ref, g_ref, wb_ref, wc_ref, ar_ref, ai_ref, d_ref, h0_ref, wg_ref,
                     o_ref, so_ref, xs_scr, st_scr):
    i = pl.program_id(0)
    nb = x_ref.shape[0]
    nj = D // S5_CB
    tiles = S5_NS // LANES

    @pl.when(i == 0)
    def _():
        xs_scr[...] = jnp.zeros(xs_scr.shape, F32)
        for j in range(nj):
            for comp in range(2):
                for q in range(tiles):
                    st_scr[(2 * j + comp) * tiles + q] = h0_ref[comp, :, j * S5_NS + q * LANES:j * S5_NS + (q + 1) * LANES]

    h = jnp.concatenate([_modulate(x_ref[b], g_ref[...], mod_ref[b][:, 0:D], mod_ref[b][:, D:2 * D])
                         for b in range(nb)], axis=0)
    hb = h.astype(BF16)
    ys = []
    for j in range(nj):
        bu = _dot(hb[:, j * S5_CB:(j + 1) * S5_CB], wb_ref[j])
        base = 2 * j * tiles
        for lt in range(2 * tiles):
            for b in range(nb):
                xs_scr[base + lt, b * S5_PITCH:b * S5_PITCH + S5_TL, :] = bu[b * S5_TL:(b + 1) * S5_TL,
                                                                             lt * LANES:(lt + 1) * LANES]
        ar = [jnp.broadcast_to(ar_ref[j][:, q * LANES:(q + 1) * LANES], (nb, LANES)) for q in range(tiles)]
        ai = [jnp.broadcast_to(ai_ref[j][:, q * LANES:(q + 1) * LANES], (nb, LANES)) for q in range(tiles)]

        def step(t, carry):
            new = []
            for q in range(tiles):
                xr, xi = carry[2 * q], carry[2 * q + 1]
                rows = pl.ds(t, nb, stride=S5_PITCH)
                nr = ar[q] * xr - ai[q] * xi + xs_scr[base + q, rows, :]
                ni = ar[q] * xi + ai[q] * xr + xs_scr[base + tiles + q, rows, :]
                xs_scr[base + q, rows, :] = nr
                xs_scr[base + tiles + q, rows, :] = ni
                new += [nr, ni]
            return tuple(new)

        init = []
        for q in range(tiles):
            init += [st_scr[base + q], st_scr[base + tiles + q]]
        fin = lax.fori_loop(0, S5_TL, step, tuple(init), unroll=True)
        for q in range(tiles):
            st_scr[base + q] = fin[2 * q]
            st_scr[base + tiles + q] = fin[2 * q + 1]
        states = jnp.concatenate([xs_scr[base + lt] for lt in range(2 * tiles)], axis=1)
        ys.append(_dot(states.astype(BF16), wc_ref[j]))
    y_all = jnp.concatenate(ys, axis=1)
    y = jnp.concatenate([y_all[b * S5_PITCH:b * S5_PITCH + S5_TL] for b in range(nb)], axis=0) + d_ref[...] * h
    z = _dot(jax.nn.gelu(y).astype(BF16), wg_ref[...])
    out = z[:, 0:D] * jax.nn.sigmoid(z[:, D:2 * D])
    for b in range(nb):
        o_ref[b] = x_ref[b] + mod_ref[b][:, 2 * D:3 * D] * out[b * S5_TL:(b + 1) * S5_TL]

    @pl.when(i == pl.num_programs(0) - 1)
    def _():
        for j in range(nj):
            for comp in range(2):
                for q in range(tiles):
                    so_ref[comp, :, j * S5_NS + q * LANES:j * S5_NS + (q + 1) * LANES] = st_scr[(2 * j + comp) * tiles + q]


def _s5_layer(x, mod, g, wb, wc, ar, ai, d_skip, h0, w_glu):
    b, t, _ = x.shape
    ns = S5_GROUPS * S5_STATE
    const = lambda shape: pl.BlockSpec(shape, lambda i: (0,) * len(shape))
    x_new, st = pl.pallas_call(
        _s5_fused_kernel,
        grid=(t // S5_TL,),
        in_specs=[pl.BlockSpec((b, S5_TL, D), lambda i: (0, i, 0)),
                  const((b, 1, 3 * D)), const((1, D)),
                  const((D // S5_CB, S5_CB, 2 * S5_NS)), const((D // S5_CB, 2 * S5_NS, S5_CB)),
                  const((D // S5_CB, 1, S5_NS)), const((D // S5_CB, 1, S5_NS)),
                  const((1, D)), const((2, b, ns)), const((D, 2 * D))],
        out_specs=[pl.BlockSpec((b, S5_TL, D), lambda i: (0, i, 0)), const((2, b, ns))],
        out_shape=[jax.ShapeDtypeStruct((b, t, D), F32), jax.ShapeDtypeStruct((2, b, ns), F32)],
        scratch_shapes=[pltpu.VMEM((S5_NSLAB, b * S5_PITCH, LANES), F32), pltpu.VMEM((S5_NSLAB, b, LANES), F32)],
        compiler_params=_cparams(("arbitrary",)),
        name="s5_fused",
    )(x, mod, g, wb, wc, ar, ai, d_skip, h0.reshape(b, 2, ns).transpose(1, 0, 2), w_glu)
    return x_new, st.transpose(1, 0, 2).reshape(b, 2, S5_GROUPS, S5_STATE)


def _dist_tiles():
    r = jnp.arange(TQ, dtype=jnp.int32)[:, None]
    c = jnp.arange(TQ, dtype=jnp.int32)[None, :]
    d0 = r - c
    edge = 2 * TQ + r - c
    return jnp.concatenate([d0, TQ + d0, 2 * TQ + d0, jnp.where(edge <= WINDOW, edge, -1),
                            jnp.full((TQ, TQ), -1, jnp.int32)], axis=0)


def _dist_cmp(seq):
    q = jnp.arange(seq, dtype=jnp.int32)[:, None]
    n = jnp.arange(LANES, dtype=jnp.int32)[None, :]
    return jnp.where(n < seq // CMP_BLOCK, q - ((n + 1) * CMP_BLOCK - 1), -1)


_SAMPLE_TABLE_SIZES = (LANES, MOBA_BLOCK, PAGE, PAST_LEN // CMP_BLOCK, WINDOW)


def _dist_sample():
    ar = lambda n: jnp.arange(n, dtype=jnp.int32)
    misc = jnp.zeros((LANES,), jnp.int32).at[1].set(MAX_DISTANCE * 4)
    moba = MOBA_BLOCK - ar(MOBA_BLOCK)
    sel = PAGE - ar(PAGE)
    cmp_ = PAST_LEN - ((ar(PAST_LEN // CMP_BLOCK) + 1) * CMP_BLOCK - 1)
    win = WINDOW - ar(WINDOW)
    return jnp.concatenate([misc, moba, sel, cmp_, win])[None, :]


def _block_diag2(w):
    z = jnp.zeros_like(w)
    return jnp.concatenate([jnp.concatenate([w, z], axis=-1), jnp.concatenate([z, w], axis=-1)], axis=-2)


def kernel(x_prompt, x_sample, cache_moba_kv, cache_nsa_kv, state_nsa_win, state_s5, page_table, c_prompt, c_sample, rel_bias, attn_norm_g, attn_ada_w, attn_ada_b, attn_w_in, attn_qk_g, nsa_cmp_pos, nsa_cmp_w1, nsa_cmp_w2, attn_w_out, ssm_norm_g, ssm_ada_w, ssm_ada_b, s5_a_re, s5_a_im, s5_log_dt, s5_b_re, s5_b_im, s5_c_re, s5_c_im, s5_d, s5_w_glu, mlp_norm_g, mlp_ada_w, mlp_ada_b, mlp_w1, mlp_w2):
    bp, seq, _ = x_prompt.shape
    bs = x_sample.shape[0]
    assert seq % TQ == 0 and x_sample.shape[1] == 1
    n_pool = cache_moba_kv.shape[1]

    c_all = jnp.concatenate([c_prompt, c_sample], axis=0)
    split_mod = lambda m: (m[:bp, None, :], m[None, bp:, :])
    mod_attn = _adaln(c_all, attn_ada_w, attn_ada_b)
    mod_ssm = _adaln(c_all, ssm_ada_w, ssm_ada_b)
    mod_mlp = _adaln(c_all, mlp_ada_w, mlp_ada_b)

    xp = x_prompt
    xs = x_sample.reshape(1, bs, D)

    tb = _bias_table(rel_bias, _dist_tiles(), LOG2E).reshape(2 * MOBA_HEADS, N_BIAS_TILES, TQ, TQ)
    tc = _bias_table(rel_bias, _dist_cmp(seq), LOG2E, first_head=MOBA_HEADS)
    ts = _bias_table(rel_bias, _dist_sample())[:, 0, :]
    offs = [0]
    for size in _SAMPLE_TABLE_SIZES:
        offs.append(offs[-1] + size)
    part = lambda heads, t: ts[heads, offs[t]:offs[t + 1]]
    hm, hn = slice(0, MOBA_HEADS), slice(MOBA_HEADS, 2 * MOBA_HEADS)
    misc_m, misc_n, tsb, tsn, tcs, tws = part(hm, 0), part(hn, 0), part(hm, 1), part(hn, 2), part(hn, 3), part(hn, 4)

    w_in = jnp.pad(attn_w_in[0], ((0, 0), (0, IN_COLS_PAD - IN_COLS))).astype(BF16)
    qkg_t = jnp.pad(jnp.tile(attn_qk_g[0], (1, 2)), ((0, 2), (0, 0)))
    lr = jnp.arange(LANES)
    avg = jnp.where(lr[:, None] // HEAD_DIM == lr[None, :] // HEAD_DIM, 1.0 / HEAD_DIM, 0.0).astype(BF16)
    g_attn = attn_norm_g[0][None, :]
    w_out = attn_w_out[0].astype(BF16)
    pos = jnp.concatenate([nsa_cmp_pos[0, 0], nsa_cmp_pos[0, 0], nsa_cmp_pos[0, 1], nsa_cmp_pos[0, 1]], axis=1)
    w1bd = _block_diag2(nsa_cmp_w1[0].reshape(2, CMP_BLOCK, HEAD_DIM, CMP_HIDDEN)).astype(BF16)
    w1bd = w1bd.reshape(2, CMP_BLOCK // 2, 256, 256)
    w2bd = _block_diag2(nsa_cmp_w2[0]).astype(BF16)
    gkc = qkg_t[3:4]

    mp_attn, ms_attn = split_mod(mod_attn[0])
    mq, mkv, nq, nkv, wkv, gates, mkv_t, nkv_t = _attn_proj(xp, mp_attn, g_attn, w_in, qkg_t, avg, 512,
                                                            SCALE * LOG2E, page_major=True)
    o_moba = _moba_prompt(mq, mkv, tb[:MOBA_HEADS])
    kcmp, vcmp = _cmp_prompt(nkv, pos, w1bd, w2bd, gkc, avg)
    o_nsa = _nsa_prompt(nq, nkv, wkv, kcmp, vcmp, gates, tb[MOBA_HEADS:], tc)
    xp = _outproj(xp, mp_attn, o_moba, o_nsa, w_out, 512)
    npg = seq // PAGE
    moba_p = mkv_t.reshape(1, bp, npg, 2, MOBA_HEADS, HEAD_DIM, PAGE).transpose(0, 1, 2, 6, 3, 4, 5)
    nsa_p = nkv_t.reshape(1, bp, npg, 4, 2, HEAD_DIM, PAGE).transpose(0, 1, 2, 6, 3, 4, 5)
    win_p = wkv[:, seq - min(WINDOW, seq):].reshape(1, bp, min(WINDOW, seq), 2, 2, HEAD_DIM)
    mq_s, mkv_s, nq_s, nkv_s, wkv_s, gates_s = _attn_proj(xs, ms_attn, g_attn, w_in, qkg_t, avg, bs, SCALE)
    cache_m_t = cache_moba_kv.transpose(0, 1, 3, 4, 5, 2).reshape(n_pool, 2, 512, PAGE)
    cache_n_t = cache_nsa_kv.transpose(0, 1, 3, 4, 5, 2).reshape(n_pool, 512, PAGE)
    win_t = state_nsa_win[0].transpose(0, 2, 3, 4, 1).reshape(bs, 256, WINDOW)
    lw = jnp.arange(512)
    qmat_m = jnp.where(lw[None, None, :] // HEAD_DIM == jnp.arange(MOBA_HEADS)[None, :, None], mq_s[0][:, None, :], 0.0)
    col3 = lambda a, lo, width: a[0][:, None, lo:lo + width]
    o_moba_s = _moba_sample(page_table, cache_m_t, qmat_m, col3(mkv_s, 0, 512), col3(mkv_s, 512, 512), tsb, misc_m)
    nq4 = nq_s[0].reshape(bs, NSA_HEADS, HEAD_DIM)
    kvh = jnp.arange(NSA_HEADS) // NSA_GROUP
    qmat_n = jnp.concatenate([jnp.where(kvh[None, :, None] == 0, nq4, 0.0),
                              jnp.where(kvh[None, :, None] == 1, nq4, 0.0)], axis=2)
    o_nsa_s = _nsa_sample(page_table, cache_n_t, qmat_n,
                          col3(nkv_s, 256, LANES), col3(nkv_s, 384, LANES), col3(wkv_s, 0, LANES),
                          col3(wkv_s, 128, LANES), win_t, gates_s.reshape(bs, 1, LANES),
                          tsn, misc_n, tcs, tws, pos, w1bd, w2bd, gkc, avg)
    xs = _outproj(xs, ms_attn, o_moba_s.reshape(1, bs, 512), o_nsa_s.reshape(1, bs, 512), w_out, bs)
    moba_s = mkv_s.reshape(1, bs, 1, 2, MOBA_HEADS, HEAD_DIM)
    nsa_s = nkv_s.reshape(1, bs, 1, 4, 2, HEAD_DIM)
    win_s = jnp.concatenate([state_nsa_win[0][:, 1:], wkv_s[0].reshape(bs, 1, 2, 2, HEAD_DIM)], axis=1)[None]

    w1_0, w2_0 = mlp_w1[0].astype(BF16), mlp_w2[0].astype(BF16)
    mp_mlp, ms_mlp = split_mod(mod_mlp[0])
    g_mlp0 = mlp_norm_g[0][None, :]
    xp = _mlp(xp, mp_mlp, g_mlp0, w1_0, w2_0, MLP_TM, MLP_TF)
    xs = _mlp(xs, ms_mlp, g_mlp0, w1_0, w2_0, bs, MLP_TF)

    ab_re, ab_im, bb_re, bb_im = _s5_discretize(s5_a_re[0], s5_a_im[0], s5_log_dt[0], s5_b_re[0], s5_b_im[0])
    wb, wc = _s5_block_weights(bb_re, bb_im, s5_c_re[0], s5_c_im[0])
    nj = D // S5_CB
    ar = ab_re.reshape(nj, 1, S5_NS)
    ai = ab_im.reshape(nj, 1, S5_NS)
    g_ssm = ssm_norm_g[0][None, :]
    d_skip = s5_d[0][None, :]
    w_glu = s5_w_glu[0].astype(BF16)
    mp_ssm, ms_ssm = split_mod(mod_ssm[0])
    xp, st_p = _s5_layer(xp, mp_ssm, g_ssm, wb, wc, ar, ai, d_skip,
                         jnp.zeros((bp, 2, S5_GROUPS, S5_STATE), F32), w_glu)
    h_s = _modulate_time_major(xs, ms_ssm, g_ssm, bs)
    y_s, st_s = _s5_scan(h_s, wb, wc, ar, ai, d_skip,
                         state_s5[0].reshape(bs, 2, S5_GROUPS * S5_STATE).transpose(1, 0, 2), bs, 1)
    xs = _glu_residual(y_s, xs, ms_ssm, w_glu, bs)
    st_s = st_s.transpose(1, 0, 2).reshape(bs, 2, S5_GROUPS, S5_STATE)

    w1_1, w2_1 = mlp_w1[1].astype(BF16), mlp_w2[1].astype(BF16)
    mp_mlp, ms_mlp = split_mod(mod_mlp[1])
    g_mlp1 = mlp_norm_g[1][None, :]
    xp = _mlp(xp, mp_mlp, g_mlp1, w1_1, w2_1, MLP_TM, MLP_TF)
    xs = _mlp(xs, ms_mlp, g_mlp1, w1_1, w2_1, bs, MLP_TF)

    return (xp, xs.reshape(bs, 1, D), moba_p, moba_s, nsa_p, nsa_s, win_p, win_s, st_p[None], st_s[None])
```

```python
import functools
import math

import jax
import jax.numpy as jnp
import numpy as np
from jax import lax
from jax.experimental import pallas as pl
from jax.experimental.pallas import tpu as pltpu

F32 = jnp.float32
BF16 = jnp.bfloat16
HIGHEST = lax.Precision.HIGHEST

D = 1024
HEAD_DIM = 64
MOBA_HEADS = 8
NSA_HEADS = 8
NSA_GROUP = 4
MOBA_BLOCK = 256
MOBA_TOPK = 3
CMP_BLOCK = 32
CMP_HIDDEN = 128
SEL_BLOCK = 64
SEL_TOPK = 16
WINDOW = 512
NUM_BUCKETS = 32
MAX_DISTANCE = 128
PAGE = 128
PAST_LEN = 8192
D_FF = 4 * D
S5_GROUPS = 64
S5_STATE = 64
S5_GROUP_CH = 16
IN_COLS = 3 * 512 + 512 + 6 * 128 + 3 * NSA_HEADS
IN_COLS_PAD = 23 * 128
EPS = 1e-6
SCALE = HEAD_DIM ** -0.5
LOG2E = math.log2(math.e)
LANES = 128
TQ = 256
N_BIAS_TILES = 5
MLP_TM = 1024
MLP_TF = 1024
NEG = -1e30
M_INIT = -1e15
VMEM_LIMIT = 56 * 1024 * 1024

_NT = (((1,), (1,)), ((), ()))


def _cparams(sem):
    return pltpu.CompilerParams(dimension_semantics=sem, vmem_limit_bytes=VMEM_LIMIT)


def _dot(a, b, **kw):
    return jnp.dot(a, b, preferred_element_type=F32, **kw)


def _dot_nt(a, b, **kw):
    return lax.dot_general(a, b, _NT, preferred_element_type=F32, **kw)


def _modulate(x, g, shift, scale):
    ms = jnp.mean(x * x, axis=-1, keepdims=True)
    return x * lax.rsqrt(ms + EPS) * g * (1.0 + scale) + shift


def _group_mean_sq(z, avg):
    sq = z * z
    hi = sq.astype(BF16)
    lo = (sq - hi.astype(F32)).astype(BF16)
    return _dot(hi, avg) + _dot(lo, avg)


def _col(x, lane, idx):
    return jnp.sum(jnp.where(lane == idx, x, 0.0), axis=1, keepdims=True)


def _adaln_kernel(c_ref, w_ref, b_ref, o_ref):
    c = c_ref[...]
    s = c * jax.nn.sigmoid(c)
    o_ref[0] = _dot(s, w_ref[0], precision=HIGHEST) + b_ref[0]


def _adaln(c_all, w, b):
    nl, n = w.shape[0], c_all.shape[0]
    return pl.pallas_call(
        _adaln_kernel,
        grid=(nl, 3),
        in_specs=[pl.BlockSpec((n, D), lambda l, j: (0, 0)),
                  pl.BlockSpec((1, D, D), lambda l, j: (l, 0, j)),
                  pl.BlockSpec((1, 1, D), lambda l, j: (l, 0, j))],
        out_specs=pl.BlockSpec((1, n, D), lambda l, j: (l, 0, j)),
        out_shape=jax.ShapeDtypeStruct((nl, n, 3 * D), F32),
        compiler_params=_cparams(("arbitrary", "arbitrary")),
        name="adaln",
    )(c_all, w, b.reshape(nl, 1, 3 * D))


def _log_bucket_starts():
    max_exact = NUM_BUCKETS // 2
    n = np.arange(max_exact, 4 * MAX_DISTANCE, dtype=np.float32)
    large = max_exact + (np.log(n / np.float32(max_exact)) / np.float32(math.log(MAX_DISTANCE / max_exact))
                         * np.float32(NUM_BUCKETS - max_exact)).astype(np.int32)
    large = np.minimum(large, NUM_BUCKETS - 1)
    return [int(np.argmax(large >= b)) + max_exact for b in range(max_exact, NUM_BUCKETS)]


def _bias_kernel(rb_ref, d_ref, o_ref, *, scale, first_head):
    h = pl.program_id(0) + first_head
    dist = d_ref[...]
    n = jnp.maximum(dist, 0)
    max_exact = NUM_BUCKETS // 2
    acc = jnp.zeros(dist.shape, F32)
    for k in range(max_exact):
        acc = jnp.where(n == k, rb_ref[k, h], acc)
    for j, start in enumerate(_log_bucket_starts()):
        acc = jnp.where(n >= start, rb_ref[max_exact + j, h], acc)
    o_ref[0] = jnp.where(dist < 0, NEG, acc * scale)


def _bias_table(rel_bias, dist, scale=1.0, first_head=0, nh=None):
    r, c = dist.shape
    nh = rel_bias.shape[1] - first_head if nh is None else nh
    return pl.pallas_call(
        functools.partial(_bias_kernel, scale=scale, first_head=first_head),
        grid=(nh,),
        in_specs=[pl.BlockSpec(memory_space=pltpu.SMEM),
                  pl.BlockSpec((r, c), lambda h: (0, 0))],
        out_specs=pl.BlockSpec((1, r, c), lambda h: (h, 0, 0)),
        out_shape=jax.ShapeDtypeStruct((nh, r, c), F32),
        compiler_params=_cparams(("arbitrary",)),
        name="bias_table",
    )(rel_bias, dist)


def _proj_kernel(x_ref, mod_ref, g_ref, w_ref, qkg_ref, avg_ref,
                 mq_ref, mkv_ref, nq_ref, nkv_ref, wkv_ref, gt_ref, *page_major_refs, q_scale):
    x = x_ref[0]
    mod = mod_ref[0]
    h = _modulate(x, g_ref[...], mod[:, 0:D], mod[:, D:2 * D])
    z = _dot(h.astype(BF16), w_ref[...])
    avg = avg_ref[...]

    def normed(lo, gi):
        zs = z[:, lo:lo + LANES]
        return zs * lax.rsqrt(_group_mean_sq(zs, avg) + EPS) * qkg_ref[gi:gi + 1, :]

    for t in range(4):
        mq_ref[0, :, t * LANES:(t + 1) * LANES] = normed(t * LANES, 0) * q_scale
        mkv_ref[0, :, t * LANES:(t + 1) * LANES] = normed(512 + t * LANES, 1)
        nq_ref[0, :, t * LANES:(t + 1) * LANES] = normed(1536 + t * LANES, 2) * q_scale
    mkv_ref[0, :, 512:1024] = z[:, 1024:1536]
    nkv_ref[0, :, 0:256] = z[:, 2048:2304]
    nkv_ref[0, :, 256:384] = normed(2304, 4)
    nkv_ref[0, :, 384:512] = z[:, 2432:2560]
    wkv_ref[0, :, 0:128] = normed(2560, 5)
    wkv_ref[0, :, 128:256] = z[:, 2688:2816]
    gt_ref[0] = jax.nn.sigmoid(z[:, 2816:2944])
    if page_major_refs:
        mkv_t_ref, nkv_t_ref = page_major_refs
        for p in range(x.shape[0] // PAGE):
            mkv_t_ref[0, p] = mkv_ref[0, p * PAGE:(p + 1) * PAGE, :].T
            nkv_t_ref[0, p] = nkv_ref[0, p * PAGE:(p + 1) * PAGE, :].T


def _attn_proj(x, mod, g, w_pad, qkg_t, avg, tm, q_scale, page_major=False):
    b, t, _ = x.shape
    tmod = mod.shape[1]
    row = lambda width: pl.BlockSpec((1, tm, width), lambda i, j: (i, j, 0))
    shp = lambda width: jax.ShapeDtypeStruct((b, t, width), F32)
    out_specs = [row(512), row(1024), row(512), row(512), row(256), row(128)]
    out_shape = [shp(512), shp(1024), shp(512), shp(512), shp(256), shp(128)]
    if page_major:
        for width in (1024, 512):
            out_specs.append(pl.BlockSpec((1, tm // PAGE, width, PAGE), lambda i, j: (i, j, 0, 0)))
            out_shape.append(jax.ShapeDtypeStruct((b, t // PAGE, width, PAGE), F32))
    return pl.pallas_call(
        functools.partial(_proj_kernel, q_scale=q_scale),
        grid=(b, t // tm),
        in_specs=[row(D),
                  pl.BlockSpec((1, tmod, 3 * D), lambda i, j: (i, 0, 0)),
                  pl.BlockSpec((1, D), lambda i, j: (0, 0)),
                  pl.BlockSpec((D, IN_COLS_PAD), lambda i, j: (0, 0)),
                  pl.BlockSpec((8, LANES), lambda i, j: (0, 0)),
                  pl.BlockSpec((LANES, LANES), lambda i, j: (0, 0))],
        out_specs=out_specs,
        out_shape=out_shape,
        compiler_params=_cparams(("arbitrary", "arbitrary")),
        name="attn_proj",
    )(x, mod, g, w_pad, qkg_t, avg)


def _rank_rows(score, rowi, ncand):
    rank = jnp.zeros(score.shape, F32)
    for m in range(ncand):
        rm = score[m:m + 1, :]
        rank = rank + jnp.where(rm > score, 1.0, 0.0) + jnp.where((rm == score) & (m < rowi), 1.0, 0.0)
    return rank


def _columns_from_rows(x_t):
    pad = jnp.zeros((LANES - x_t.shape[0], x_t.shape[1]), F32)
    return jnp.concatenate([x_t, pad], axis=0).T


def _softmax_pv(pieces, v_all):
    m = pieces[0]
    for s in pieces[1:]:
        m = jnp.maximum(m, s)
    m = jnp.maximum(jnp.max(m, axis=1, keepdims=True), M_INIT)
    ps = [jnp.exp2(s - m) for s in pieces]
    tot = ps[0]
    for p in ps[1:]:
        tot = tot + p
    l = jnp.sum(tot, axis=1, keepdims=True)
    p_all = jnp.concatenate([p.astype(BF16) for p in ps], axis=1) if len(ps) > 1 else ps[0].astype(BF16)
    return _dot(p_all, v_all) / jnp.maximum(l, 1e-30)


def _moba_prompt_kernel(q_ref, k_ref, v_ref, t_ref, o_ref, km_scr, kb_scr, vb_scr):
    s_len = q_ref.shape[1]
    nblk = s_len // MOBA_BLOCK
    nq = s_len // TQ
    lane = lax.broadcasted_iota(jnp.int32, (TQ, LANES), 1)
    rowb = lax.broadcasted_iota(jnp.int32, (nblk, TQ), 0)
    km_scr[...] = jnp.zeros(km_scr.shape, F32)
    for n in range(nblk):
        km_scr[n:n + 1, :] = jnp.mean(k_ref[0, n * MOBA_BLOCK:(n + 1) * MOBA_BLOCK, :], axis=0, keepdims=True)
    kmean = km_scr[...]
    kb_scr[...] = k_ref[0].astype(BF16)
    vb_scr[...] = v_ref[0].astype(BF16)

    for i in range(nq):
        r0 = i * TQ
        q2 = q_ref[0, r0:r0 + TQ, :]
        kall = kb_scr[0:r0 + TQ, :]
        vall = vb_scr[0:r0 + TQ, :]
        outs = []
        for e in range(2):
            qe = jnp.where(lane // HEAD_DIM == e, q2, 0.0)
            gate_t = _dot_nt(kmean, qe, precision=HIGHEST)[0:nblk]
            gm = jnp.where(rowb < i, gate_t, -jnp.inf)
            sel = ((_rank_rows(gm, rowb, nblk) < MOBA_TOPK) & (rowb < i)) | (rowb == i)
            cb = _columns_from_rows(jnp.where(sel, 0.0, NEG))
            s = _dot_nt(qe.astype(BF16), kall)
            far_bias = t_ref[e, 2, 0:1, 0:1]
            pieces = []
            for n in range(i + 1):
                seg = s[:, n * TQ:(n + 1) * TQ]
                if i - n < 2:
                    pieces.append(seg + t_ref[e, i - n] + cb[:, n:n + 1])
                else:
                    pieces.append(seg + (cb[:, n:n + 1] + far_bias))
            outs.append(_softmax_pv(pieces, vall))
        o_ref[0, r0:r0 + TQ, :] = jnp.where(lane < HEAD_DIM, outs[0], outs[1])


def _moba_prompt(mq, mkv, tb):
    b, s, _ = mq.shape
    npair = MOBA_HEADS // 2
    return pl.pallas_call(
        _moba_prompt_kernel,
        grid=(b, npair),
        in_specs=[pl.BlockSpec((1, s, LANES), lambda i, p: (i, 0, p)),
                  pl.BlockSpec((1, s, LANES), lambda i, p: (i, 0, p)),
                  pl.BlockSpec((1, s, LANES), lambda i, p: (i, 0, npair + p)),
                  pl.BlockSpec((2, N_BIAS_TILES, TQ, TQ), lambda i, p: (p, 0, 0, 0))],
        out_specs=pl.BlockSpec((1, s, LANES), lambda i, p: (i, 0, p)),
        out_shape=jax.ShapeDtypeStruct((b, s, 512), F32),
        scratch_shapes=[pltpu.VMEM((LANES, LANES), F32), pltpu.VMEM((s, LANES), BF16), pltpu.VMEM((s, LANES), BF16)],
        compiler_params=_cparams(("arbitrary", "arbitrary")),
        name="moba_prompt",
    )(mq, mkv, mkv, tb)


def _compress_tokens(load_k, load_v, pos_ref, w1_ref, w2_ref):
    hk = hv = None
    for r in range(0, CMP_BLOCK, 2):
        xk = [(load_k(r + t) + pos_ref[r + t:r + t + 1, 0:LANES]).astype(BF16) for t in range(2)]
        xv = [(load_v(r + t) + pos_ref[r + t:r + t + 1, LANES:2 * LANES]).astype(BF16) for t in range(2)]
        dk = _dot(jnp.concatenate(xk, axis=1), w1_ref[0, r // 2])
        dv = _dot(jnp.concatenate(xv, axis=1), w1_ref[1, r // 2])
        hk = dk if hk is None else hk + dk
        hv = dv if hv is None else hv + dv
    ck = _dot(jax.nn.gelu(hk).astype(BF16), w2_ref[0])
    cv = _dot(jax.nn.gelu(hv).astype(BF16), w2_ref[1])
    return ck, cv


def _cmp_prompt_kernel(xk_ref, xv_ref, pos_ref, w1_ref, w2_ref, gkc_ref, avg_ref, kc_ref, vc_ref):
    nblk = xk_ref.shape[1] // CMP_BLOCK
    ck, cv = _compress_tokens(lambda r: xk_ref[0, pl.ds(r, nblk, stride=CMP_BLOCK), :],
                              lambda r: xv_ref[0, pl.ds(r, nblk, stride=CMP_BLOCK), :], pos_ref, w1_ref, w2_ref)
    ck = ck * lax.rsqrt(_group_mean_sq(ck, avg_ref[...]) + EPS) * gkc_ref[...]
    kc_ref[0] = jnp.zeros((LANES, LANES), F32)
    vc_ref[0] = jnp.zeros((LANES, LANES), F32)
    kc_ref[0, 0:nblk, :] = ck
    vc_ref[0, 0:nblk, :] = cv


def _cmp_prompt(nkv, pos, w1bd, w2bd, gkc, avg):
    b, s, _ = nkv.shape
    const = lambda shape: pl.BlockSpec(shape, lambda i: (0,) * len(shape))
    return pl.pallas_call(
        _cmp_prompt_kernel,
        grid=(b,),
        in_specs=[pl.BlockSpec((1, s, LANES), lambda i: (i, 0, 0)), pl.BlockSpec((1, s, LANES), lambda i: (i, 0, 1)),
                  const((CMP_BLOCK, 256)), const((2, CMP_BLOCK // 2, 256, 256)), const((2, 256, LANES)),
                  const((1, LANES)), const((LANES, LANES))],
        out_specs=[pl.BlockSpec((1, LANES, LANES), lambda i: (i, 0, 0))] * 2,
        out_shape=[jax.ShapeDtypeStruct((b, LANES, LANES), F32)] * 2,
        compiler_params=_cparams(("arbitrary",)),
        name="nsa_compress_prompt",
    )(nkv, nkv, pos, w1bd, w2bd, gkc, avg)


def _nsa_prompt_kernel(q_ref, ks_ref, vs_ref, kw_ref, vw_ref, kc_ref, vc_ref, g_ref, t_ref, tc_ref,
                       o_ref, ksb_scr, vsb_scr, kwb_scr, vwb_scr, ex_scr):
    s_len = q_ref.shape[1]
    k = pl.program_id(1)
    lane = lax.broadcasted_iota(jnp.int32, (TQ, LANES), 1)
    kvmask = (lane // HEAD_DIM) == k
    kc = kc_ref[0].astype(BF16)
    vc = vc_ref[0].astype(BF16)
    nsel = s_len // SEL_BLOCK
    nq = s_len // TQ
    ncmp = s_len // CMP_BLOCK
    rowb = lax.broadcasted_iota(jnp.int32, (nsel, TQ), 0)
    qpos = lax.broadcasted_iota(jnp.int32, (nsel, TQ), 1)
    pair_r = lax.broadcasted_iota(jnp.int32, (nsel, LANES), 0)
    pair_c = lax.broadcasted_iota(jnp.int32, (nsel, LANES), 1)
    pair_t = jnp.where((pair_c // (SEL_BLOCK // CMP_BLOCK) == pair_r) & (pair_c < ncmp), 1.0, 0.0)
    e_r = lax.broadcasted_iota(jnp.int32, (LANES, s_len), 0)
    e_c = lax.broadcasted_iota(jnp.int32, (LANES, s_len), 1)
    ex_scr[...] = jnp.where(e_r == e_c // SEL_BLOCK, 1.0, 0.0).astype(BF16)
    ksb_scr[...] = ks_ref[0].astype(BF16)
    vsb_scr[...] = vs_ref[0].astype(BF16)
    kwb_scr[...] = kw_ref[0].astype(BF16)
    vwb_scr[...] = vw_ref[0].astype(BF16)

    def qtile(i, _):
        r0 = pl.multiple_of(i * TQ, TQ)
        qs = []
        for h in range(NSA_GROUP):
            q2 = q_ref[0, pl.ds(r0, TQ), (h // 2) * LANES:(h // 2 + 1) * LANES]
            qa = jnp.where(k == (h % 2), q2, pltpu.roll(q2, HEAD_DIM, 1))
            qs.append(jnp.where(kvmask, qa, 0.0).astype(BF16))

        imp = jnp.zeros((TQ, LANES), F32)
        o_cmp = []
        for h in range(NSA_GROUP):
            s = _dot_nt(qs[h], kc) + tc_ref[h, pl.ds(r0, TQ), :]
            m = jnp.maximum(jnp.max(s, axis=1, keepdims=True), M_INIT)
            p = jnp.exp2(s - m)
            p = p / jnp.maximum(jnp.sum(p, axis=1, keepdims=True), 1e-30)
            imp = imp + p
            o_cmp.append(_dot(p.astype(BF16), vc))

        imp_t = _dot_nt(pair_t, imp, precision=HIGHEST)
        own = (r0 + qpos) // SEL_BLOCK
        sc = jnp.where(rowb < own, imp_t, -jnp.inf)
        sel = ((_rank_rows(sc, rowb, nsel) < SEL_TOPK) & (rowb < own)) | (rowb == own)
        selb = _columns_from_rows(jnp.where(sel, 1.0, 0.0)).astype(BF16)

        wk, wv, wt = [], [], []
        for j, tidx in enumerate((3, 1, 0)):
            n = i - 2 + j
            c0 = pl.multiple_of(jnp.maximum(n, 0) * TQ, TQ)
            wk.append(kwb_scr[pl.ds(c0, TQ), :])
            wv.append(vwb_scr[pl.ds(c0, TQ), :])
            wt.append(jnp.where(n < 0, N_BIAS_TILES - 1, tidx))
        kw_all = jnp.concatenate(wk, axis=0)
        vw_all = jnp.concatenate(wv, axis=0)
        g = g_ref[0, pl.ds(r0, TQ), :]
        o_win = []
        for h in range(NSA_GROUP):
            s = _dot_nt(qs[h], kw_all)
            o_win.append(_softmax_pv([s[:, j * TQ:(j + 1) * TQ] + t_ref[h, wt[j]] for j in range(3)], vw_all))

        for c in range(1, nq + 1):
            @pl.when(i + 1 == c)
            def _():
                nkeys = c * TQ
                addm = (_dot(selb, ex_scr[:, 0:nkeys]) - 1.0) * (-NEG)
                kall = ksb_scr[0:nkeys, :]
                vall = vsb_scr[0:nkeys, :]
                res = []
                for h in range(NSA_GROUP):
                    s = _dot_nt(qs[h], kall) + addm
                    pieces = [s[:, n * TQ:(n + 1) * TQ] + t_ref[h, min(c - 1 - n, 2)] for n in range(c)]
                    o_sel = _softmax_pv(pieces, vall)
                    hg = (k * NSA_GROUP + h) * 3
                    o = (_col(g, lane, hg) * o_cmp[h] + _col(g, lane, hg + 1) * o_sel
                         + _col(g, lane, hg + 2) * o_win[h])
                    res.append(jnp.where(k == (h % 2), o, pltpu.roll(o, HEAD_DIM, 1)))
                for t in range(2):
                    o_ref[0, pl.ds(r0, TQ), t * LANES:(t + 1) * LANES] = jnp.where(
                        lane < HEAD_DIM, res[2 * t], res[2 * t + 1])
        return 0

    lax.fori_loop(0, nq, qtile, 0)


def _nsa_prompt(nq, nkv, wkv, kcmp, vcmp, gates, tb, tc):
    b, s, _ = nq.shape
    col = lambda arr_cols, cb: pl.BlockSpec((1, s, LANES), lambda i, k: (i, 0, cb))
    return pl.pallas_call(
        _nsa_prompt_kernel,
        grid=(b, 2),
        in_specs=[pl.BlockSpec((1, s, 256), lambda i, k: (i, 0, k)),
                  col(512, 2), col(512, 3), col(256, 0), col(256, 1),
                  pl.BlockSpec((1, LANES, LANES), lambda i, k: (i, 0, 0)),
                  pl.BlockSpec((1, LANES, LANES), lambda i, k: (i, 0, 0)),
                  pl.BlockSpec((1, s, LANES), lambda i, k: (i, 0, 0)),
                  pl.BlockSpec((NSA_GROUP, N_BIAS_TILES, TQ, TQ), lambda i, k: (k, 0, 0, 0)),
                  pl.BlockSpec((NSA_GROUP, s, LANES), lambda i, k: (k, 0, 0))],
        out_specs=pl.BlockSpec((1, s, 256), lambda i, k: (i, 0, k)),
        out_shape=jax.ShapeDtypeStruct((b, s, 512), F32),
        scratch_shapes=[pltpu.VMEM((s, LANES), BF16)] * 4 + [pltpu.VMEM((LANES, s), BF16)],
        compiler_params=_cparams(("arbitrary", "arbitrary")),
        name="nsa_prompt",
    )(nq, nkv, nkv, wkv, wkv, kcmp, vcmp, gates, tb, tc)


PAGES_PER_STEP = 16
CMP_PITCH = 40


def _rank_lt(score, lane, ncand, topk):
    rank = jnp.zeros(score.shape, F32)
    for m in range(ncand):
        col = score[:, m:m + 1]
        beats = (col > score) | ((col == score) & (m < lane))
        rank = rank + jnp.where(beats, 1.0, 0.0)
    return rank < topk


def _merge_blocks(sel, m_all, l_all, acc_scr, nblk, s_self, v_self):
    mx = jnp.maximum(jnp.max(jnp.where(sel, m_all, NEG), axis=1, keepdims=True), s_self)
    w = jnp.exp(jnp.where(sel, m_all - mx, NEG))
    w_self = jnp.exp(s_self - mx)
    den = jnp.sum(w * l_all, axis=1, keepdims=True) + w_self
    num = w_self * v_self
    for j in range(nblk):
        num = num + w[:, j:j + 1] * acc_scr[j]
    return num / den


def _moba_sample_kernel(pt_ref, *refs):
    pages = refs[:PAGES_PER_STEP]
    qm_ref, kn_ref, vn_ref, tsb_ref, misc_ref, o_ref, g_scr, m_scr, l_scr, acc_scr = refs[PAGES_PER_STEP:]
    s = pl.program_id(1)
    nstep = pl.num_programs(1)
    nblk = PAST_LEN // MOBA_BLOCK
    width = MOBA_HEADS * HEAD_DIM
    qm = qm_ref[0]
    qb = qm.astype(BF16)
    lane = lax.broadcasted_iota(jnp.int32, (MOBA_HEADS, LANES), 1)

    @pl.when(s == 0)
    def _():
        g_scr[...] = jnp.zeros(g_scr.shape, F32)
        m_scr[...] = jnp.zeros(m_scr.shape, F32)
        l_scr[...] = jnp.zeros(l_scr.shape, F32)

    npb = PAGES_PER_STEP // 2
    blk0 = s * npb
    kt_all = jnp.concatenate([pages[t][0, 0].astype(BF16) for t in range(PAGES_PER_STEP)], axis=1)
    vt_all = jnp.concatenate([pages[t][0, 1].astype(BF16) for t in range(PAGES_PER_STEP)], axis=1)
    raw = _dot(qb, kt_all)
    far = jnp.broadcast_to(misc_ref[:, 1:2], (MOBA_HEADS, MOBA_BLOCK))
    g_new, m_new, l_new = g_scr[...], m_scr[...], l_scr[...]
    p_rows = []
    for j in range(npb):
        seg = raw[:, j * MOBA_BLOCK:(j + 1) * MOBA_BLOCK]
        gate = jnp.sum(seg, axis=1, keepdims=True)
        sc = seg + (jnp.where(s == nstep - 1, tsb_ref[...], far) if j == npb - 1 else far)
        mj = jnp.max(sc, axis=1, keepdims=True)
        p = jnp.exp(sc - mj)
        g_new = jnp.where(lane == blk0 + j, gate, g_new)
        m_new = jnp.where(lane == blk0 + j, mj, m_new)
        l_new = jnp.where(lane == blk0 + j, jnp.sum(p, axis=1, keepdims=True), l_new)
        zeros = jnp.zeros((MOBA_HEADS, MOBA_BLOCK), F32)
        p_rows.append(jnp.concatenate([p if t == j else zeros for t in range(npb)], axis=1))
    g_scr[...] = g_new
    m_scr[...] = m_new
    l_scr[...] = l_new
    acc = _dot_nt(jnp.concatenate(p_rows, axis=0).astype(BF16), vt_all)
    for j in range(npb):
        acc_scr[blk0 + j] = acc[j * MOBA_HEADS:(j + 1) * MOBA_HEADS]

    @pl.when(s == nstep - 1)
    def _():
        gm = jnp.where(lane < nblk, g_scr[...], -jnp.inf)
        sel = _rank_lt(gm, lane, nblk, MOBA_TOPK) & (lane < nblk)
        s_self = jnp.sum(qm * kn_ref[0], axis=1, keepdims=True) + misc_ref[:, 0:1]
        o = _merge_blocks(sel, m_scr[...], l_scr[...], acc_scr, nblk, s_self, vn_ref[0])
        hrow = lax.broadcasted_iota(jnp.int32, (MOBA_HEADS, width), 0)
        hlane = lax.broadcasted_iota(jnp.int32, (MOBA_HEADS, width), 1)
        o_ref[0] = jnp.sum(jnp.where(hlane // HEAD_DIM == hrow, o, 0.0), axis=0, keepdims=True)


def _moba_sample(page_table, cache_t, qmat, knew, vnew, tsb, misc):
    nb, npages = page_table.shape
    nstep = npages // PAGES_PER_STEP
    width = MOBA_HEADS * HEAD_DIM
    nblk = PAST_LEN // MOBA_BLOCK

    def page_spec(j):
        return pl.BlockSpec((1, 2, width, PAGE), lambda b, s, pt: (pt[b, s * PAGES_PER_STEP + j], 0, 0, 0))

    per_b = lambda shape: pl.BlockSpec((1,) + shape, lambda b, s, pt: (b, 0, 0))
    const = lambda shape: pl.BlockSpec(shape, lambda b, s, pt: (0,) * len(shape))
    grid_spec = pltpu.PrefetchScalarGridSpec(
        num_scalar_prefetch=1,
        grid=(nb, nstep),
        in_specs=[page_spec(j) for j in range(PAGES_PER_STEP)]
        + [per_b((MOBA_HEADS, width)), per_b((1, width)), per_b((1, width)),
           const((MOBA_HEADS, MOBA_BLOCK)), const((MOBA_HEADS, LANES))],
        out_specs=per_b((1, width)),
        scratch_shapes=[pltpu.VMEM((MOBA_HEADS, LANES), F32)] * 3 + [pltpu.VMEM((nblk, MOBA_HEADS, width), F32)],
    )
    return pl.pallas_call(
        _moba_sample_kernel,
        grid_spec=grid_spec,
        out_shape=jax.ShapeDtypeStruct((nb, 1, width), F32),
        compiler_params=_cparams(("arbitrary", "arbitrary")),
        name="moba_sample",
    )(page_table, *([cache_t] * PAGES_PER_STEP), qmat, knew, vnew, tsb, misc)


def _nsa_sample_kernel(pt_ref, *refs):
    pages = refs[:PAGES_PER_STEP]
    (qm_ref, ksn_ref, vsn_ref, kwn_ref, vwn_ref, win_ref, g_ref, tsn_ref, misc_ref, tcs_ref, tws_ref,
     pos_ref, w1_ref, w2_ref, gkc_ref, avg_ref, o_ref, xk_scr, xv_scr, m_scr, l_scr, acc_scr) = refs[PAGES_PER_STEP:]
    s = pl.program_id(1)
    nstep = pl.num_programs(1)
    nsel = PAST_LEN // SEL_BLOCK
    ncmp = PAST_LEN // CMP_BLOCK
    qm = qm_ref[0]
    qb = qm.astype(BF16)
    lane = lax.broadcasted_iota(jnp.int32, (NSA_HEADS, LANES), 1)
    row = lax.broadcasted_iota(jnp.int32, (NSA_HEADS, LANES), 0)
    lo = lane < HEAD_DIM

    @pl.when(s == 0)
    def _():
        m_scr[...] = jnp.zeros(m_scr.shape, F32)
        l_scr[...] = jnp.zeros(l_scr.shape, F32)

    for j in range(PAGES_PER_STEP):
        pg = s * PAGES_PER_STEP + j
        kc = pages[j][0, 0:LANES, :].T
        vc = pages[j][0, LANES:2 * LANES, :].T
        for b4 in range(PAGE // CMP_BLOCK):
            r0 = pl.multiple_of((pg * (PAGE // CMP_BLOCK) + b4) * CMP_PITCH, 8)
            xk_scr[pl.ds(r0, CMP_BLOCK), :] = kc[b4 * CMP_BLOCK:(b4 + 1) * CMP_BLOCK, :]
            xv_scr[pl.ds(r0, CMP_BLOCK), :] = vc[b4 * CMP_BLOCK:(b4 + 1) * CMP_BLOCK, :]

    ks_all = jnp.concatenate([pages[t][0, 256:384, :].astype(BF16) for t in range(PAGES_PER_STEP)], axis=1)
    vs_all = jnp.concatenate([pages[t][0, 384:512, :].astype(BF16) for t in range(PAGES_PER_STEP)], axis=1)
    raw = _dot(qb, ks_all)
    far = jnp.broadcast_to(misc_ref[:, 1:2], (NSA_HEADS, PAGE))
    m_new, l_new = m_scr[...], l_scr[...]
    b0 = 2 * s * PAGES_PER_STEP
    zeros = jnp.zeros((NSA_HEADS, PAGE), F32)
    p_rows = []
    for j in range(PAGES_PER_STEP):
        sc = raw[:, j * PAGE:(j + 1) * PAGE]
        sc = sc + (jnp.where(s == nstep - 1, tsn_ref[...], far) if j == PAGES_PER_STEP - 1 else far)
        m0 = jnp.max(jnp.where(lo, sc, NEG), axis=1, keepdims=True)
        m1 = jnp.max(jnp.where(lo, NEG, sc), axis=1, keepdims=True)
        p = jnp.exp(sc - jnp.where(lo, m0, m1))
        p0, p1 = jnp.where(lo, p, 0.0), jnp.where(lo, 0.0, p)
        l0 = jnp.sum(p0, axis=1, keepdims=True)
        l1 = jnp.sum(p1, axis=1, keepdims=True)
        bj = b0 + 2 * j
        m_new = jnp.where(lane == bj, m0, jnp.where(lane == bj + 1, m1, m_new))
        l_new = jnp.where(lane == bj, l0, jnp.where(lane == bj + 1, l1, l_new))
        for ph in (p0, p1):
            p_rows.append(jnp.concatenate([ph if t == j else zeros for t in range(PAGES_PER_STEP)], axis=1))
    m_scr[...] = m_new
    l_scr[...] = l_new
    acc = _dot_nt(jnp.concatenate(p_rows, axis=0).astype(BF16), vs_all)
    for b in range(2 * PAGES_PER_STEP):
        acc_scr[b0 + b] = acc[b * NSA_HEADS:(b + 1) * NSA_HEADS]

    @pl.when(s == nstep - 1)
    def _():
        ck, cv = _compress_tokens(lambda r: xk_scr[pl.ds(r, ncmp, stride=CMP_PITCH), :],
                                  lambda r: xv_scr[pl.ds(r, ncmp, stride=CMP_PITCH), :], pos_ref, w1_ref, w2_ref)
        ck = ck * lax.rsqrt(_group_mean_sq(ck, avg_ref[...]) + EPS) * gkc_ref[...]
        sc = _dot_nt(qb, ck.astype(BF16)) + tcs_ref[...]
        m = jnp.maximum(jnp.max(sc, axis=1, keepdims=True), M_INIT)
        pc = jnp.exp(sc - m)
        pc = pc / jnp.maximum(jnp.sum(pc, axis=1, keepdims=True), 1e-30)
        o_cmp = _dot(pc.astype(BF16), cv.astype(BF16))
        g0 = pc[0:1] + pc[1:2] + pc[2:3] + pc[3:4]
        g1 = pc[4:5] + pc[5:6] + pc[6:7] + pc[7:8]
        rowc = lax.broadcasted_iota(jnp.int32, (NSA_HEADS, ncmp), 0)
        imp = jnp.where(rowc < NSA_GROUP, g0, g1)
        pr = lax.broadcasted_iota(jnp.int32, (ncmp, LANES), 0)
        pc_ = lax.broadcasted_iota(jnp.int32, (ncmp, LANES), 1)
        pair = jnp.where(pr // (SEL_BLOCK // CMP_BLOCK) == pc_, 1.0, 0.0)
        impb = _dot(imp, pair, precision=HIGHEST)
        own = PAST_LEN // SEL_BLOCK
        sel = _rank_lt(jnp.where(lane < own, impb, -jnp.inf), lane, nsel, SEL_TOPK) & (lane < own)
        s_self = jnp.sum(qm * ksn_ref[0], axis=1, keepdims=True) + misc_ref[:, 0:1]
        o_sel = _merge_blocks(sel, m_scr[...], l_scr[...], acc_scr, nsel, s_self, vsn_ref[0])
        kw_t = win_ref[0, 0:LANES, :].astype(BF16)
        vw_t = win_ref[0, LANES:2 * LANES, :].astype(BF16)
        sw = _dot(qb, kw_t) + tws_ref[...]
        sw_self = jnp.sum(qm * kwn_ref[0], axis=1, keepdims=True) + misc_ref[:, 0:1]
        mw = jnp.maximum(jnp.max(sw, axis=1, keepdims=True), sw_self)
        pw = jnp.exp(sw - mw)
        pw_self = jnp.exp(sw_self - mw)
        o_win = ((_dot_nt(pw.astype(BF16), vw_t) + pw_self * vwn_ref[0])
                 / (jnp.sum(pw, axis=1, keepdims=True) + pw_self))
        gt = jnp.broadcast_to(g_ref[0], (NSA_HEADS, LANES))
        o8 = (_col(gt, lane, 3 * row) * o_cmp + _col(gt, lane, 3 * row + 1) * o_sel
              + _col(gt, lane, 3 * row + 2) * o_win)
        lane1 = lax.broadcasted_iota(jnp.int32, (1, LANES), 1)
        tiles = []
        for t in range(NSA_HEADS // 2):
            ha, hb = 2 * t, 2 * t + 1
            ra = o8[ha:ha + 1, :]
            rb = o8[hb:hb + 1, :]
            if ha // NSA_GROUP == 1:
                ra = pltpu.roll(ra, HEAD_DIM, 1)
            if hb // NSA_GROUP == 0:
                rb = pltpu.roll(rb, HEAD_DIM, 1)
            tiles.append(jnp.where(lane1 < HEAD_DIM, ra, rb))
        o_ref[0] = jnp.concatenate(tiles, axis=1)


def _nsa_sample(page_table, cache, qmat, ksn, vsn, kwn, vwn, win, gates, tsn, misc, tcs, tws,
                pos, w1bd, w2bd, gkc, avg):
    nb, npages = page_table.shape
    nstep = npages // PAGES_PER_STEP
    nsel = PAST_LEN // SEL_BLOCK

    def page_spec(j):
        return pl.BlockSpec((1, 512, PAGE), lambda b, s, pt: (pt[b, s * PAGES_PER_STEP + j], 0, 0))

    per_b = lambda shape: pl.BlockSpec((1,) + shape, lambda b, s, pt: (b, 0, 0))
    const = lambda shape: pl.BlockSpec(shape, lambda b, s, pt: (0,) * len(shape))
    cmp_rows = PAST_LEN // CMP_BLOCK * CMP_PITCH
    grid_spec = pltpu.PrefetchScalarGridSpec(
        num_scalar_prefetch=1,
        grid=(nb, nstep),
        in_specs=[page_spec(j) for j in range(PAGES_PER_STEP)]
        + [per_b((NSA_HEADS, LANES)), per_b((1, LANES)), per_b((1, LANES)), per_b((1, LANES)), per_b((1, LANES)),
           per_b((256, WINDOW)), per_b((1, LANES)),
           const((NSA_HEADS, PAGE)), const((NSA_HEADS, LANES)), const((NSA_HEADS, PAST_LEN // CMP_BLOCK)),
           const((NSA_HEADS, WINDOW)),
           const((CMP_BLOCK, 256)), const((2, CMP_BLOCK // 2, 256, 256)), const((2, 256, LANES)),
           const((1, LANES)), const((LANES, LANES))],
        out_specs=per_b((1, 512)),
        scratch_shapes=[pltpu.VMEM((cmp_rows, LANES), F32), pltpu.VMEM((cmp_rows, LANES), F32),
                        pltpu.VMEM((NSA_HEADS, LANES), F32),
                        pltpu.VMEM((NSA_HEADS, LANES), F32), pltpu.VMEM((nsel, NSA_HEADS, LANES), F32)],
    )
    return pl.pallas_call(
        _nsa_sample_kernel,
        grid_spec=grid_spec,
        out_shape=jax.ShapeDtypeStruct((nb, 1, 512), F32),
        compiler_params=_cparams(("arbitrary", "arbitrary")),
        name="nsa_sample",
    )(page_table, *([cache] * PAGES_PER_STEP), qmat, ksn, vsn, kwn, vwn, win, gates, tsn, misc, tcs, tws,
      pos, w1bd, w2bd, gkc, avg)


def _outproj_kernel(x_ref, mod_ref, om_ref, on_ref, w_ref, o_ref):
    y = _dot(om_ref[0].astype(BF16), w_ref[0:512, :]) + _dot(on_ref[0].astype(BF16), w_ref[512:1024, :])
    o_ref[0] = x_ref[0] + mod_ref[0][:, 2 * D:3 * D] * y


def _outproj(x, mod, o_m, o_n, w, tm):
    b, t, _ = x.shape
    tmod = mod.shape[1]
    row = lambda width: pl.BlockSpec((1, tm, width), lambda i, j: (i, j, 0))
    return pl.pallas_call(
        _outproj_kernel,
        grid=(b, t // tm),
        in_specs=[row(D), pl.BlockSpec((1, tmod, 3 * D), lambda i, j: (i, 0, 0)), row(512), row(512),
                  pl.BlockSpec((D, D), lambda i, j: (0, 0))],
        out_specs=row(D),
        out_shape=jax.ShapeDtypeStruct((b, t, D), F32),
        compiler_params=_cparams(("arbitrary", "arbitrary")),
        name="attn_outproj",
    )(x, mod, o_m, o_n, w)


def _mlp_kernel(x_ref, mod_ref, g_ref, w1_ref, w2_ref, o_ref, h_scr, acc_scr):
    kf = pl.program_id(2)

    @pl.when(kf == 0)
    def _():
        mod = mod_ref[0]
        h_scr[...] = _modulate(x_ref[0], g_ref[...], mod[:, 0:D], mod[:, D:2 * D]).astype(BF16)
        acc_scr[...] = jnp.zeros(acc_scr.shape, F32)

    a = jnp.square(jnp.maximum(_dot(h_scr[...], w1_ref[...]), 0.0))
    acc_scr[...] += _dot(a.astype(BF16), w2_ref[...])

    @pl.when(kf == pl.num_programs(2) - 1)
    def _():
        o_ref[0] = x_ref[0] + mod_ref[0][:, 2 * D:3 * D] * acc_scr[...]


def _mlp(x, mod, g, w1, w2, tm, tf):
    b, t, _ = x.shape
    tmod = mod.shape[1]
    return pl.pallas_call(
        _mlp_kernel,
        grid=(b, t // tm, D_FF // tf),
        in_specs=[pl.BlockSpec((1, tm, D), lambda i, j, kf: (i, j, 0)),
                  pl.BlockSpec((1, tmod, 3 * D), lambda i, j, kf: (i, 0, 0)),
                  pl.BlockSpec((1, D), lambda i, j, kf: (0, 0)),
                  pl.BlockSpec((D, tf), lambda i, j, kf: (0, kf)),
                  pl.BlockSpec((tf, D), lambda i, j, kf: (kf, 0))],
        out_specs=pl.BlockSpec((1, tm, D), lambda i, j, kf: (i, j, 0)),
        out_shape=jax.ShapeDtypeStruct((b, t, D), F32),
        scratch_shapes=[pltpu.VMEM((tm, D), BF16), pltpu.VMEM((tm, D), F32)],
        compiler_params=_cparams(("arbitrary", "arbitrary", "arbitrary")),
        name="mlp",
    )(x, mod, g, w1, w2)


def _s5_disc_kernel(are_ref, aim_ref, ldt_ref, bre_ref, bim_ref, abre_ref, abim_ref, bbre_ref, bbim_ref):
    a_re, a_im = are_ref[...], aim_ref[...]
    dt = jnp.exp(ldt_ref[...])
    decay = jnp.exp(dt * a_re)
    ab_re, ab_im = decay * jnp.cos(dt * a_im), decay * jnp.sin(dt * a_im)
    den = a_re * a_re + a_im * a_im
    f_re = ((ab_re - 1) * a_re + ab_im * a_im) / den
    f_im = (ab_im * a_re - (ab_re - 1) * a_im) / den
    br, bi = bre_ref[...], bim_ref[...]
    abre_ref[...] = ab_re
    abim_ref[...] = ab_im
    bbre_ref[...] = f_re * br - f_im * bi
    bbim_ref[...] = f_re * bi + f_im * br


def _s5_discretize(a_re, a_im, log_dt, b_re, b_im):
    rep = lambda a: jnp.repeat(a, S5_GROUP_CH, axis=1)
    shp = jax.ShapeDtypeStruct((S5_GROUPS, S5_STATE * S5_GROUP_CH), F32)
    ldt = jnp.broadcast_to(log_dt[:, None], (S5_GROUPS, S5_STATE * S5_GROUP_CH))
    flat = lambda a: a.reshape(S5_GROUPS, S5_STATE * S5_GROUP_CH)
    ab_re, ab_im, bb_re, bb_im = pl.pallas_call(
        _s5_disc_kernel, out_shape=[shp] * 4, name="s5_discretize",
    )(rep(a_re), rep(a_im), ldt, flat(b_re), flat(b_im))
    unrep = lambda a: a[:, ::S5_GROUP_CH]
    unflat = lambda a: a.reshape(S5_GROUPS, S5_STATE, S5_GROUP_CH)
    return unrep(ab_re), unrep(ab_im), unflat(bb_re), unflat(bb_im)


def _modulate_tm_kernel(x_ref, mod_ref, g_ref, o_ref):
    mod = mod_ref[0]
    o_ref[...] = _modulate(x_ref[0], g_ref[...], mod[:, 0:D], mod[:, D:2 * D])


def _modulate_time_major(x, mod, g, tl):
    b, t, _ = x.shape
    tmod = mod.shape[1]
    return pl.pallas_call(
        _modulate_tm_kernel,
        grid=(b, t // tl),
        in_specs=[pl.BlockSpec((1, tl, D), lambda i, j: (i, j, 0)),
                  pl.BlockSpec((1, tmod, 3 * D), lambda i, j: (i, 0, 0)),
                  pl.BlockSpec((1, D), lambda i, j: (0, 0))],
        out_specs=pl.BlockSpec((tl, D), lambda i, j: (j, i)),
        out_shape=jax.ShapeDtypeStruct((t, b * D), F32),
        compiler_params=_cparams(("arbitrary", "arbitrary")),
        name="s5_modulate",
    )(x, mod, g)


S5_CB = 256
S5_NS = S5_CB // S5_GROUP_CH * S5_STATE


def _s5_scan_kernel(h_ref, wb_ref, wc_ref, ar_ref, ai_ref, d_ref, h0_ref, y_ref, so_ref, xs_scr, st_scr, *, tl, r):
    i = pl.program_id(1)

    @pl.when(i == 0)
    def _():
        st_scr[...] = h0_ref[...]

    u = h_ref[...]
    xs_scr[...] = _dot(u.astype(BF16), wb_ref[0])
    ar = jnp.broadcast_to(ar_ref[0], (r, S5_NS))
    ai = jnp.broadcast_to(ai_ref[0], (r, S5_NS))

    def step(t, carry):
        xr, xi = carry
        r0 = pl.multiple_of(t * r, r)
        nr = ar * xr - ai * xi + xs_scr[pl.ds(r0, r), 0:S5_NS]
        ni = ar * xi + ai * xr + xs_scr[pl.ds(r0, r), S5_NS:2 * S5_NS]
        xs_scr[pl.ds(r0, r), 0:S5_NS] = nr
        xs_scr[pl.ds(r0, r), S5_NS:2 * S5_NS] = ni
        return nr, ni

    xr, xi = lax.fori_loop(0, tl, step, (st_scr[0], st_scr[1]))
    st_scr[0] = xr
    st_scr[1] = xi
    y_ref[...] = _dot(xs_scr[...].astype(BF16), wc_ref[0]) + d_ref[...] * u

    @pl.when(i == pl.num_programs(1) - 1)
    def _():
        so_ref[...] = st_scr[...]


def _s5_scan(h_tm, wb, wc, ar, ai, d_skip, h0, r, tl):
    rows = h_tm.shape[0]
    nj = D // S5_CB
    return pl.pallas_call(
        functools.partial(_s5_scan_kernel, tl=tl, r=r),
        grid=(nj, rows // (tl * r)),
        in_specs=[pl.BlockSpec((tl * r, S5_CB), lambda j, i: (i, j)),
                  pl.BlockSpec((1, S5_CB, 2 * S5_NS), lambda j, i: (j, 0, 0)),
                  pl.BlockSpec((1, 2 * S5_NS, S5_CB), lambda j, i: (j, 0, 0)),
                  pl.BlockSpec((1, 1, S5_NS), lambda j, i: (j, 0, 0)),
                  pl.BlockSpec((1, 1, S5_NS), lambda j, i: (j, 0, 0)),
                  pl.BlockSpec((1, S5_CB), lambda j, i: (0, j)),
                  pl.BlockSpec((2, r, S5_NS), lambda j, i: (0, 0, j))],
        out_specs=[pl.BlockSpec((tl * r, S5_CB), lambda j, i: (i, j)),
                   pl.BlockSpec((2, r, S5_NS), lambda j, i: (0, 0, j))],
        out_shape=[jax.ShapeDtypeStruct((rows, D), F32), jax.ShapeDtypeStruct((2, r, S5_GROUPS * S5_STATE), F32)],
        scratch_shapes=[pltpu.VMEM((tl * r, 2 * S5_NS), F32), pltpu.VMEM((2, r, S5_NS), F32)],
        compiler_params=_cparams(("arbitrary", "arbitrary")),
        name="s5_scan",
    )(h_tm, wb, wc, ar, ai, d_skip, h0)


def _glu_kernel(y_ref, x_ref, mod_ref, w_ref, o_ref):
    z = _dot(jax.nn.gelu(y_ref[...]).astype(BF16), w_ref[...])
    o_ref[0] = x_ref[0] + mod_ref[0][:, 2 * D:3 * D] * (z[:, 0:D] * jax.nn.sigmoid(z[:, D:2 * D]))


def _glu_residual(y_tm, x, mod, w, tl):
    b, t, _ = x.shape
    tmod = mod.shape[1]
    return pl.pallas_call(
        _glu_kernel,
        grid=(b, t // tl),
        in_specs=[pl.BlockSpec((tl, D), lambda i, j: (j, i)),
                  pl.BlockSpec((1, tl, D), lambda i, j: (i, j, 0)),
                  pl.BlockSpec((1, tmod, 3 * D), lambda i, j: (i, 0, 0)),
                  pl.BlockSpec((D, 2 * D), lambda i, j: (0, 0))],
        out_specs=pl.BlockSpec((1, tl, D), lambda i, j: (i, j, 0)),
        out_shape=jax.ShapeDtypeStruct((b, t, D), F32),
        compiler_params=_cparams(("arbitrary", "arbitrary")),
        name="s5_glu",
    )(y_tm, x, mod, w)


def _s5_block_weights(bb_re, bb_im, c_re, c_im):
    nj, ng = D // S5_CB, S5_CB // S5_GROUP_CH
    eye = jnp.eye(ng, dtype=F32)

    def wb_part(bb):
        t = bb.reshape(nj, ng, S5_STATE, S5_GROUP_CH).transpose(0, 1, 3, 2)
        return jnp.einsum("jgcn,gh->jgchn", t, eye).reshape(nj, S5_CB, S5_NS)

    def wc_part(c):
        t = c.reshape(nj, ng, S5_GROUP_CH, S5_STATE).transpose(0, 1, 3, 2)
        return jnp.einsum("jgnc,gh->jgnhc", t, eye).reshape(nj, S5_NS, S5_CB)

    wb = jnp.concatenate([wb_part(bb_re), wb_part(bb_im)], axis=2).astype(BF16)
    wc = jnp.concatenate([wc_part(c_re), -wc_part(c_im)], axis=1).astype(BF16)
    return wb, wc


S5_TL = 32
S5_PITCH = 40
S5_NSLAB = 2 * S5_GROUPS * S5_STATE // LANES


def _s5_fused_kernel(x_ref, mod_ref, g_ref, wb_ref, wc_ref, ar_ref, ai_ref, d_ref, h0_ref, wg_ref,
                     o_ref, so_ref, xs_scr, st_scr):
    i = pl.program_id(0)
    nb = x_ref.shape[0]
    nj = D // S5_CB
    tiles = S5_NS // LANES

    @pl.when(i == 0)
    def _():
        xs_scr[...] = jnp.zeros(xs_scr.shape, F32)
        for j in range(nj):
            for comp in range(2):
                for q in range(tiles):
                    st_scr[(2 * j + comp) * tiles + q] = h0_ref[comp, :, j * S5_NS + q * LANES:j * S5_NS + (q + 1) * LANES]

    h = jnp.concatenate([_modulate(x_ref[b], g_ref[...], mod_ref[b][:, 0:D], mod_ref[b][:, D:2 * D])
                         for b in range(nb)], axis=0)
    hb = h.astype(BF16)
    ys = []
    for j in range(nj):
        bu = _dot(hb[:, j * S5_CB:(j + 1) * S5_CB], wb_ref[j])
        base = 2 * j * tiles
        for lt in range(2 * tiles):
            for b in range(nb):
                xs_scr[base + lt, b * S5_PITCH:b * S5_PITCH + S5_TL, :] = bu[b * S5_TL:(b + 1) * S5_TL,
                                                                             lt * LANES:(lt + 1) * LANES]
        ar = [jnp.broadcast_to(ar_ref[j][:, q * LANES:(q + 1) * LANES], (nb, LANES)) for q in range(tiles)]
        ai = [jnp.broadcast_to(ai_ref[j][:, q * LANES:(q + 1) * LANES], (nb, LANES)) for q in range(tiles)]

        def step(t, carry):
            new = []
            for q in range(tiles):
                xr, xi = carry[2 * q], carry[2 * q + 1]
                rows = pl.ds(t, nb, stride=S5_PITCH)
                nr = ar[q] * xr - ai[q] * xi + xs_scr[base + q, rows, :]
                ni = ar[q] * xi + ai[q] * xr + xs_scr[base + tiles + q, rows, :]
                xs_scr[base + q, rows, :] = nr
                xs_scr[base + tiles + q, rows, :] = ni
                new += [nr, ni]
            return tuple(new)

        init = []
        for q in range(tiles):
            init += [st_scr[base + q], st_scr[base + tiles + q]]
        fin = lax.fori_loop(0, S5_TL, step, tuple(init), unroll=True)
        for q in range(tiles):
            st_scr[base + q] = fin[2 * q]
            st_scr[base + tiles + q] = fin[2 * q + 1]
        states = jnp.concatenate([xs_scr[base + lt] for lt in range(2 * tiles)], axis=1)
        ys.append(_dot(states.astype(BF16), wc_ref[j]))
    y_all = jnp.concatenate(ys, axis=1)
    y = jnp.concatenate([y_all[b * S5_PITCH:b * S5_PITCH + S5_TL] for b in range(nb)], axis=0) + d_ref[...] * h
    z = _dot(jax.nn.gelu(y).astype(BF16), wg_ref[...])
    out = z[:, 0:D] * jax.nn.sigmoid(z[:, D:2 * D])
    for b in range(nb):
        o_ref[b] = x_ref[b] + mod_ref[b][:, 2 * D:3 * D] * out[b * S5_TL:(b + 1) * S5_TL]

    @pl.when(i == pl.num_programs(0) - 1)
    def _():
        for j in range(nj):
            for comp in range(2):
                for q in range(tiles):
                    so_ref[comp, :, j * S5_NS + q * LANES:j * S5_NS + (q + 1) * LANES] = st_scr[(2 * j + comp) * tiles + q]


def _s5_layer(x, mod, g, wb, wc, ar, ai, d_skip, h0, w_glu):
    b, t, _ = x.shape
    ns = S5_GROUPS * S5_STATE
    const = lambda shape: pl.BlockSpec(shape, lambda i: (0,) * len(shape))
    x_new, st = pl.pallas_call(
        _s5_fused_kernel,
        grid=(t // S5_TL,),
        in_specs=[pl.BlockSpec((b, S5_TL, D), lambda i: (0, i, 0)),
                  const((b, 1, 3 * D)), const((1, D)),
                  const((D // S5_CB, S5_CB, 2 * S5_NS)), const((D // S5_CB, 2 * S5_NS, S5_CB)),
                  const((D // S5_CB, 1, S5_NS)), const((D // S5_CB, 1, S5_NS)),
                  const((1, D)), const((2, b, ns)), const((D, 2 * D))],
        out_specs=[pl.BlockSpec((b, S5_TL, D), lambda i: (0, i, 0)), const((2, b, ns))],
        out_shape=[jax.ShapeDtypeStruct((b, t, D), F32), jax.ShapeDtypeStruct((2, b, ns), F32)],
        scratch_shapes=[pltpu.VMEM((S5_NSLAB, b * S5_PITCH, LANES), F32), pltpu.VMEM((S5_NSLAB, b, LANES), F32)],
        compiler_params=_cparams(("arbitrary",)),
        name="s5_fused",
    )(x, mod, g, wb, wc, ar, ai, d_skip, h0.reshape(b, 2, ns).transpose(1, 0, 2), w_glu)
    return x_new, st.transpose(1, 0, 2).reshape(b, 2, S5_GROUPS, S5_STATE)


def _dist_tiles():
    r = jnp.arange(TQ, dtype=jnp.int32)[:, None]
    c = jnp.arange(TQ, dtype=jnp.int32)[None, :]
    d0 = r - c
    edge = 2 * TQ + r - c
    return jnp.concatenate([d0, TQ + d0, 2 * TQ + d0, jnp.where(edge <= WINDOW, edge, -1),
                            jnp.full((TQ, TQ), -1, jnp.int32)], axis=0)


def _dist_cmp(seq):
    q = jnp.arange(seq, dtype=jnp.int32)[:, None]
    n = jnp.arange(LANES, dtype=jnp.int32)[None, :]
    return jnp.where(n < seq // CMP_BLOCK, q - ((n + 1) * CMP_BLOCK - 1), -1)


_SAMPLE_TABLE_SIZES = (LANES, MOBA_BLOCK, PAGE, PAST_LEN // CMP_BLOCK, WINDOW)


def _dist_sample():
    ar = lambda n: jnp.arange(n, dtype=jnp.int32)
    misc = jnp.zeros((LANES,), jnp.int32).at[1].set(MAX_DISTANCE * 4)
    moba = MOBA_BLOCK - ar(MOBA_BLOCK)
    sel = PAGE - ar(PAGE)
    cmp_ = PAST_LEN - ((ar(PAST_LEN // CMP_BLOCK) + 1) * CMP_BLOCK - 1)
    win = WINDOW - ar(WINDOW)
    return jnp.concatenate([misc, moba, sel, cmp_, win])[None, :]


def _block_diag2(w):
    z = jnp.zeros_like(w)
    return jnp.concatenate([jnp.concatenate([w, z], axis=-1), jnp.concatenate([z, w], axis=-1)], axis=-2)


def kernel(x_prompt, x_sample, cache_moba_kv, cache_nsa_kv, state_nsa_win, state_s5, page_table, c_prompt, c_sample, rel_bias, attn_norm_g, attn_ada_w, attn_ada_b, attn_w_in, attn_qk_g, nsa_cmp_pos, nsa_cmp_w1, nsa_cmp_w2, attn_w_out, ssm_norm_g, ssm_ada_w, ssm_ada_b, s5_a_re, s5_a_im, s5_log_dt, s5_b_re, s5_b_im, s5_c_re, s5_c_im, s5_d, s5_w_glu, mlp_norm_g, mlp_ada_w, mlp_ada_b, mlp_w1, mlp_w2):
    bp, seq, _ = x_prompt.shape
    bs = x_sample.shape[0]
    assert seq % TQ == 0 and x_sample.shape[1] == 1
    n_pool = cache_moba_kv.shape[1]

    c_all = jnp.concatenate([c_prompt, c_sample], axis=0)
    split_mod = lambda m: (m[:bp, None, :], m[None, bp:, :])
    mod_attn = _adaln(c_all, attn_ada_w, attn_ada_b)
    mod_ssm = _adaln(c_all, ssm_ada_w, ssm_ada_b)
    mod_mlp = _adaln(c_all, mlp_ada_w, mlp_ada_b)

    xp = x_prompt
    xs = x_sample.reshape(1, bs, D)

    tb = _bias_table(rel_bias, _dist_tiles(), LOG2E).reshape(2 * MOBA_HEADS, N_BIAS_TILES, TQ, TQ)
    tc = _bias_table(rel_bias, _dist_cmp(seq), LOG2E, first_head=MOBA_HEADS)
    ts = _bias_table(rel_bias, _dist_sample())[:, 0, :]
    offs = [0]
    for size in _SAMPLE_TABLE_SIZES:
        offs.append(offs[-1] + size)
    part = lambda heads, t: ts[heads, offs[t]:offs[t + 1]]
    hm, hn = slice(0, MOBA_HEADS), slice(MOBA_HEADS, 2 * MOBA_HEADS)
    misc_m, misc_n, tsb, tsn, tcs, tws = part(hm, 0), part(hn, 0), part(hm, 1), part(hn, 2), part(hn, 3), part(hn, 4)

    w_in = jnp.pad(attn_w_in[0], ((0, 0), (0, IN_COLS_PAD - IN_COLS))).astype(BF16)
    qkg_t = jnp.pad(jnp.tile(attn_qk_g[0], (1, 2)), ((0, 2), (0, 0)))
    lr = jnp.arange(LANES)
    avg = jnp.where(lr[:, None] // HEAD_DIM == lr[None, :] // HEAD_DIM, 1.0 / HEAD_DIM, 0.0).astype(BF16)
    g_attn = attn_norm_g[0][None, :]
    w_out = attn_w_out[0].astype(BF16)
    pos = jnp.concatenate([nsa_cmp_pos[0, 0], nsa_cmp_pos[0, 0], nsa_cmp_pos[0, 1], nsa_cmp_pos[0, 1]], axis=1)
    w1bd = _block_diag2(nsa_cmp_w1[0].reshape(2, CMP_BLOCK, HEAD_DIM, CMP_HIDDEN)).astype(BF16)
    w1bd = w1bd.reshape(2, CMP_BLOCK // 2, 256, 256)
    w2bd = _block_diag2(nsa_cmp_w2[0]).astype(BF16)
    gkc = qkg_t[3:4]

    mp_attn, ms_attn = split_mod(mod_attn[0])
    mq, mkv, nq, nkv, wkv, gates, mkv_t, nkv_t = _attn_proj(xp, mp_attn, g_attn, w_in, qkg_t, avg, 512,
                                                            SCALE * LOG2E, page_major=True)
    o_moba = _moba_prompt(mq, mkv, tb[:MOBA_HEADS])
    kcmp, vcmp = _cmp_prompt(nkv, pos, w1bd, w2bd, gkc, avg)
    o_nsa = _nsa_prompt(nq, nkv, wkv, kcmp, vcmp, gates, tb[MOBA_HEADS:], tc)
    xp = _outproj(xp, mp_attn, o_moba, o_nsa, w_out, 512)
    npg = seq // PAGE
    moba_p = mkv_t.reshape(1, bp, npg, 2, MOBA_HEADS, HEAD_DIM, PAGE).transpose(0, 1, 2, 6, 3, 4, 5)
    nsa_p = nkv_t.reshape(1, bp, npg, 4, 2, HEAD_DIM, PAGE).transpose(0, 1, 2, 6, 3, 4, 5)
    win_p = wkv[:, seq - min(WINDOW, seq):].reshape(1, bp, min(WINDOW, seq), 2, 2, HEAD_DIM)
    mq_s, mkv_s, nq_s, nkv_s, wkv_s, gates_s = _attn_proj(xs, ms_attn, g_attn, w_in, qkg_t, avg, bs, SCALE)
    cache_m_t = cache_moba_kv.transpose(0, 1, 3, 4, 5, 2).reshape(n_pool, 2, 512, PAGE)
    cache_n_t = cache_nsa_kv.transpose(0, 1, 3, 4, 5, 2).reshape(n_pool, 512, PAGE)
    win_t = state_nsa_win[0].transpose(0, 2, 3, 4, 1).reshape(bs, 256, WINDOW)
    lw = jnp.arange(512)
    qmat_m = jnp.where(lw[None, None, :] // HEAD_DIM == jnp.arange(MOBA_HEADS)[None, :, None], mq_s[0][:, None, :], 0.0)
    col3 = lambda a, lo, width: a[0][:, None, lo:lo + width]
    o_moba_s = _moba_sample(page_table, cache_m_t, qmat_m, col3(mkv_s, 0, 512), col3(mkv_s, 512, 512), tsb, misc_m)
    nq4 = nq_s[0].reshape(bs, NSA_HEADS, HEAD_DIM)
    kvh = jnp.arange(NSA_HEADS) // NSA_GROUP
    qmat_n = jnp.concatenate([jnp.where(kvh[None, :, None] == 0, nq4, 0.0),
                              jnp.where(kvh[None, :, None] == 1, nq4, 0.0)], axis=2)
    o_nsa_s = _nsa_sample(page_table, cache_n_t, qmat_n,
                          col3(nkv_s, 256, LANES), col3(nkv_s, 384, LANES), col3(wkv_s, 0, LANES),
                          col3(wkv_s, 128, LANES), win_t, gates_s.reshape(bs, 1, LANES),
                          tsn, misc_n, tcs, tws, pos, w1bd, w2bd, gkc, avg)
    xs = _outproj(xs, ms_attn, o_moba_s.reshape(1, bs, 512), o_nsa_s.reshape(1, bs, 512), w_out, bs)
    moba_s = mkv_s.reshape(1, bs, 1, 2, MOBA_HEADS, HEAD_DIM)
    nsa_s = nkv_s.reshape(1, bs, 1, 4, 2, HEAD_DIM)
    win_s = jnp.concatenate([state_nsa_win[0][:, 1:], wkv_s[0].reshape(bs, 1, 2, 2, HEAD_DIM)], axis=1)[None]

    w1_0, w2_0 = mlp_w1[0].astype(BF16), mlp_w2[0].astype(BF16)
    mp_mlp, ms_mlp = split_mod(mod_mlp[0])
    g_mlp0 = mlp_norm_g[0][None, :]
    xp = _mlp(xp, mp_mlp, g_mlp0, w1_0, w2_0, MLP_TM, MLP_TF)
    xs = _mlp(xs, ms_mlp, g_mlp0, w1_0, w2_0, bs, MLP_TF)

    ab_re, ab_im, bb_re, bb_im = _s5_discretize(s5_a_re[0], s5_a_im[0], s5_log_dt[0], s5_b_re[0], s5_b_im[0])
    wb, wc = _s5_block_weights(bb_re, bb_im, s5_c_re[0], s5_c_im[0])
    nj = D // S5_CB
    ar = ab_re.reshape(nj, 1, S5_NS)
    ai = ab_im.reshape(nj, 1, S5_NS)
    g_ssm = ssm_norm_g[0][None, :]
    d_skip = s5_d[0][None, :]
    w_glu = s5_w_glu[0].astype(BF16)
    mp_ssm, ms_ssm = split_mod(mod_ssm[0])
    xp, st_p = _s5_layer(xp, mp_ssm, g_ssm, wb, wc, ar, ai, d_skip,
                         jnp.zeros((bp, 2, S5_GROUPS, S5_STATE), F32), w_glu)
    h_s = _modulate_time_major(xs, ms_ssm, g_ssm, bs)
    y_s, st_s = _s5_scan(h_s, wb, wc, ar, ai, d_skip,
                         state_s5[0].reshape(bs, 2, S5_GROUPS * S5_STATE).transpose(1, 0, 2), bs, 1)
    xs = _glu_residual(y_s, xs, ms_ssm, w_glu, bs)
    st_s = st_s.transpose(1, 0, 2).reshape(bs, 2, S5_GROUPS, S5_STATE)

    w1_1, w2_1 = mlp_w1[1].astype(BF16), mlp_w2[1].astype(BF16)
    mp_mlp, ms_mlp = split_mod(mod_mlp[1])
    g_mlp1 = mlp_norm_g[1][None, :]
    xp = _mlp(xp, mp_mlp, g_mlp1, w1_1, w2_1, MLP_TM, MLP_TF)
    xs = _mlp(xs, ms_mlp, g_mlp1, w1_1, w2_1, bs, MLP_TF)

    return (xp, xs.reshape(bs, 1, D), moba_p, moba_s, nsa_p, nsa_s, win_p, win_s, st_p[None], st_s[None])
```

```python
import functools
import math

import jax
import jax.numpy as jnp
import numpy as np
from jax import lax
from jax.experimental import pallas as pl
from jax.experimental.pallas import tpu as pltpu

F32 = jnp.float32
BF16 = jnp.bfloat16
HIGHEST = lax.Precision.HIGHEST

D = 1024
HEAD_DIM = 64
MOBA_HEADS = 8
NSA_HEADS = 8
NSA_GROUP = 4
MOBA_BLOCK = 256
MOBA_TOPK = 3
CMP_BLOCK = 32
CMP_HIDDEN = 128
SEL_BLOCK = 64
SEL_TOPK = 16
WINDOW = 512
NUM_BUCKETS = 32
MAX_DISTANCE = 128
PAGE = 128
PAST_LEN = 8192
D_FF = 4 * D
S5_GROUPS = 64
S5_STATE = 64
S5_GROUP_CH = 16
IN_COLS = 3 * 512 + 512 + 6 * 128 + 3 * NSA_HEADS
IN_COLS_PAD = 23 * 128
EPS = 1e-6
SCALE = HEAD_DIM ** -0.5
LOG2E = math.log2(math.e)
LANES = 128
TQ = 256
N_BIAS_TILES = 5
MLP_TM = 1024
MLP_TF = 1024
NEG = -1e30
M_INIT = -1e15
VMEM_LIMIT = 56 * 1024 * 1024

_NT = (((1,), (1,)), ((), ()))


def _cparams(sem):
    return pltpu.CompilerParams(dimension_semantics=sem, vmem_limit_bytes=VMEM_LIMIT)


def _dot(a, b, **kw):
    return jnp.dot(a, b, preferred_element_type=F32, **kw)


def _dot_nt(a, b, **kw):
    return lax.dot_general(a, b, _NT, preferred_element_type=F32, **kw)


def _modulate(x, g, shift, scale):
    ms = jnp.mean(x * x, axis=-1, keepdims=True)
    return x * lax.rsqrt(ms + EPS) * g * (1.0 + scale) + shift


def _group_mean_sq(z, avg):
    sq = z * z
    hi = sq.astype(BF16)
    lo = (sq - hi.astype(F32)).astype(BF16)
    return _dot(hi, avg) + _dot(lo, avg)


def _col(x, lane, idx):
    return jnp.sum(jnp.where(lane == idx, x, 0.0), axis=1, keepdims=True)


def _adaln_kernel(c_ref, w_ref, b_ref, o_ref):
    c = c_ref[...]
    s = c * jax.nn.sigmoid(c)
    o_ref[0] = _dot(s, w_ref[0], precision=HIGHEST) + b_ref[0]


def _adaln(c_all, w, b):
    nl, n = w.shape[0], c_all.shape[0]
    return pl.pallas_call(
        _adaln_kernel,
        grid=(nl, 3),
        in_specs=[pl.BlockSpec((n, D), lambda l, j: (0, 0)),
                  pl.BlockSpec((1, D, D), lambda l, j: (l, 0, j)),
                  pl.BlockSpec((1, 1, D), lambda l, j: (l, 0, j))],
        out_specs=pl.BlockSpec((1, n, D), lambda l, j: (l, 0, j)),
        out_shape=jax.ShapeDtypeStruct((nl, n, 3 * D), F32),
        compiler_params=_cparams(("arbitrary", "arbitrary")),
        name="adaln",
    )(c_all, w, b.reshape(nl, 1, 3 * D))


def _log_bucket_starts():
    max_exact = NUM_BUCKETS // 2
    n = np.arange(max_exact, 4 * MAX_DISTANCE, dtype=np.float32)
    large = max_exact + (np.log(n / np.float32(max_exact)) / np.float32(math.log(MAX_DISTANCE / max_exact))
                         * np.float32(NUM_BUCKETS - max_exact)).astype(np.int32)
    large = np.minimum(large, NUM_BUCKETS - 1)
    return [int(np.argmax(large >= b)) + max_exact for b in range(max_exact, NUM_BUCKETS)]


def _bias_kernel(rb_ref, d_ref, o_ref, *, scale, first_head):
    h = pl.program_id(0) + first_head
    dist = d_ref[...]
    n = jnp.maximum(dist, 0)
    max_exact = NUM_BUCKETS // 2
    acc = jnp.zeros(dist.shape, F32)
    for k in range(max_exact):
        acc = jnp.where(n == k, rb_ref[k, h], acc)
    for j, start in enumerate(_log_bucket_starts()):
        acc = jnp.where(n >= start, rb_ref[max_exact + j, h], acc)
    o_ref[0] = jnp.where(dist < 0, NEG, acc * scale)


def _bias_table(rel_bias, dist, scale=1.0, first_head=0, nh=None):
    r, c = dist.shape
    nh = rel_bias.shape[1] - first_head if nh is None else nh
    return pl.pallas_call(
        functools.partial(_bias_kernel, scale=scale, first_head=first_head),
        grid=(nh,),
        in_specs=[pl.BlockSpec(memory_space=pltpu.SMEM),
                  pl.BlockSpec((r, c), lambda h: (0, 0))],
        out_specs=pl.BlockSpec((1, r, c), lambda h: (h, 0, 0)),
        out_shape=jax.ShapeDtypeStruct((nh, r, c), F32),
        compiler_params=_cparams(("arbitrary",)),
        name="bias_table",
    )(rel_bias, dist)


def _proj_kernel(x_ref, mod_ref, g_ref, w_ref, qkg_ref, avg_ref,
                 mq_ref, mkv_ref, nq_ref, nkv_ref, wkv_ref, gt_ref, *page_major_refs, q_scale):
    x = x_ref[0]
    mod = mod_ref[0]
    h = _modulate(x, g_ref[...], mod[:, 0:D], mod[:, D:2 * D])
    z = _dot(h.astype(BF16), w_ref[...])
    avg = avg_ref[...]

    def normed(lo, gi):
        zs = z[:, lo:lo + LANES]
        return zs * lax.rsqrt(_group_mean_sq(zs, avg) + EPS) * qkg_ref[gi:gi + 1, :]

    for t in range(4):
        mq_ref[0, :, t * LANES:(t + 1) * LANES] = normed(t * LANES, 0) * q_scale
        mkv_ref[0, :, t * LANES:(t + 1) * LANES] = normed(512 + t * LANES, 1)
        nq_ref[0, :, t * LANES:(t + 1) * LANES] = normed(1536 + t * LANES, 2) * q_scale
    mkv_ref[0, :, 512:1024] = z[:, 1024:1536]
    nkv_ref[0, :, 0:256] = z[:, 2048:2304]
    nkv_ref[0, :, 256:384] = normed(2304, 4)
    nkv_ref[0, :, 384:512] = z[:, 2432:2560]
    wkv_ref[0, :, 0:128] = normed(2560, 5)
    wkv_ref[0, :, 128:256] = z[:, 2688:2816]
    gt_ref[0] = jax.nn.sigmoid(z[:, 2816:2944])
    if page_major_refs:
        mkv_t_ref, nkv_t_ref = page_major_refs
        for p in range(x.shape[0] // PAGE):
            mkv_t_ref[0, p] = mkv_ref[0, p * PAGE:(p + 1) * PAGE, :].T
            nkv_t_ref[0, p] = nkv_ref[0, p * PAGE:(p + 1) * PAGE, :].T


def _attn_proj(x, mod, g, w_pad, qkg_t, avg, tm, q_scale, page_major=False):
    b, t, _ = x.shape
    tmod = mod.shape[1]
    row = lambda width: pl.BlockSpec((1, tm, width), lambda i, j: (i, j, 0))
    shp = lambda width: jax.ShapeDtypeStruct((b, t, width), F32)
    out_specs = [row(512), row(1024), row(512), row(512), row(256), row(128)]
    out_shape = [shp(512), shp(1024), shp(512), shp(512), shp(256), shp(128)]
    if page_major:
        for width in (1024, 512):
            out_specs.append(pl.BlockSpec((1, tm // PAGE, width, PAGE), lambda i, j: (i, j, 0, 0)))
            out_shape.append(jax.ShapeDtypeStruct((b, t // PAGE, width, PAGE), F32))
    return pl.pallas_call(
        functools.partial(_proj_kernel, q_scale=q_scale),
        grid=(b, t // tm),
        in_specs=[row(D),
                  pl.BlockSpec((1, tmod, 3 * D), lambda i, j: (i, 0, 0)),
                  pl.BlockSpec((1, D), lambda i, j: (0, 0)),
                  pl.BlockSpec((D, IN_COLS_PAD), lambda i, j: (0, 0)),
                  pl.BlockSpec((8, LANES), lambda i, j: (0, 0)),
                  pl.BlockSpec((LANES, LANES), lambda i, j: (0, 0))],
        out_specs=out_specs,
        out_shape=out_shape,
        compiler_params=_cparams(("arbitrary", "arbitrary")),
        name="attn_proj",
    )(x, mod, g, w_pad, qkg_t, avg)


def _rank_rows(score, rowi, ncand):
    rank = jnp.zeros(score.shape, F32)
    for m in range(ncand):
        rm = score[m:m + 1, :]
        rank = rank + jnp.where(rm > score, 1.0, 0.0) + jnp.where((rm == score) & (m < rowi), 1.0, 0.0)
    return rank


def _columns_from_rows(x_t):
    pad = jnp.zeros((LANES - x_t.shape[0], x_t.shape[1]), F32)
    return jnp.concatenate([x_t, pad], axis=0).T


def _softmax_pv(pieces, v_all):
    m = pieces[0]
    for s in pieces[1:]:
        m = jnp.maximum(m, s)
    m = jnp.maximum(jnp.max(m, axis=1, keepdims=True), M_INIT)
    ps = [jnp.exp2(s - m) for s in pieces]
    tot = ps[0]
    for p in ps[1:]:
        tot = tot + p
    l = jnp.sum(tot, axis=1, keepdims=True)
    p_all = jnp.concatenate([p.astype(BF16) for p in ps], axis=1) if len(ps) > 1 else ps[0].astype(BF16)
    return _dot(p_all, v_all) / jnp.maximum(l, 1e-30)


def _moba_prompt_kernel(q_ref, k_ref, v_ref, t_ref, o_ref, km_scr, kb_scr, vb_scr):
    s_len = q_ref.shape[1]
    nblk = s_len // MOBA_BLOCK
    nq = s_len // TQ
    lane = lax.broadcasted_iota(jnp.int32, (TQ, LANES), 1)
    rowb = lax.broadcasted_iota(jnp.int32, (nblk, TQ), 0)
    km_scr[...] = jnp.zeros(km_scr.shape, F32)
    for n in range(nblk):
        km_scr[n:n + 1, :] = jnp.mean(k_ref[0, n * MOBA_BLOCK:(n + 1) * MOBA_BLOCK, :], axis=0, keepdims=True)
    kmean = km_scr[...]
    kb_scr[...] = k_ref[0].astype(BF16)
    vb_scr[...] = v_ref[0].astype(BF16)

    for i in range(nq):
        r0 = i * TQ
        q2 = q_ref[0, r0:r0 + TQ, :]
        kall = kb_scr[0:r0 + TQ, :]
        vall = vb_scr[0:r0 + TQ, :]
        outs = []
        for e in range(2):
            qe = jnp.where(lane // HEAD_DIM == e, q2, 0.0)
            gate_t = _dot_nt(kmean, qe, precision=HIGHEST)[0:nblk]
            gm = jnp.where(rowb < i, gate_t, -jnp.inf)
            sel = ((_rank_rows(gm, rowb, nblk) < MOBA_TOPK) & (rowb < i)) | (rowb == i)
            cb = _columns_from_rows(jnp.where(sel, 0.0, NEG))
            s = _dot_nt(qe.astype(BF16), kall)
            far_bias = t_ref[e, 2, 0:1, 0:1]
            pieces = []
            for n in range(i + 1):
                seg = s[:, n * TQ:(n + 1) * TQ]
                if i - n < 2:
                    pieces.append(seg + t_ref[e, i - n] + cb[:, n:n + 1])
                else:
                    pieces.append(seg + (cb[:, n:n + 1] + far_bias))
            outs.append(_softmax_pv(pieces, vall))
        o_ref[0, r0:r0 + TQ, :] = jnp.where(lane < HEAD_DIM, outs[0], outs[1])


def _moba_prompt(mq, mkv, tb):
    b, s, _ = mq.shape
    npair = MOBA_HEADS // 2
    return pl.pallas_call(
        _moba_prompt_kernel,
        grid=(b, npair),
        in_specs=[pl.BlockSpec((1, s, LANES), lambda i, p: (i, 0, p)),
                  pl.BlockSpec((1, s, LANES), lambda i, p: (i, 0, p)),
                  pl.BlockSpec((1, s, LANES), lambda i, p: (i, 0, npair + p)),
                  pl.BlockSpec((2, N_BIAS_TILES, TQ, TQ), lambda i, p: (p, 0, 0, 0))],
        out_specs=pl.BlockSpec((1, s, LANES), lambda i, p: (i, 0, p)),
        out_shape=jax.ShapeDtypeStruct((b, s, 512), F32),
        scratch_shapes=[pltpu.VMEM((LANES, LANES), F32), pltpu.VMEM((s, LANES), BF16), pltpu.VMEM((s, LANES), BF16)],
        compiler_params=_cparams(("arbitrary", "arbitrary")),
        name="moba_prompt",
    )(mq, mkv, mkv, tb)


def _compress_tokens(load_k, load_v, pos_ref, w1_ref, w2_ref):
    hk = hv = None
    for r in range(0, CMP_BLOCK, 2):
        xk = [(load_k(r + t) + pos_ref[r + t:r + t + 1, 0:LANES]).astype(BF16) for t in range(2)]
        xv = [(load_v(r + t) + pos_ref[r + t:r + t + 1, LANES:2 * LANES]).astype(BF16) for t in range(2)]
        dk = _dot(jnp.concatenate(xk, axis=1), w1_ref[0, r // 2])
        dv = _dot(jnp.concatenate(xv, axis=1), w1_ref[1, r // 2])
        hk = dk if hk is None else hk + dk
        hv = dv if hv is None else hv + dv
    ck = _dot(jax.nn.gelu(hk).astype(BF16), w2_ref[0])
    cv = _dot(jax.nn.gelu(hv).astype(BF16), w2_ref[1])
    return ck, cv


def _cmp_prompt_kernel(xk_ref, xv_ref, pos_ref, w1_ref, w2_ref, gkc_ref, avg_ref, kc_ref, vc_ref):
    nblk = xk_ref.shape[1] // CMP_BLOCK
    ck, cv = _compress_tokens(lambda r: xk_ref[0, pl.ds(r, nblk, stride=CMP_BLOCK), :],
                              lambda r: xv_ref[0, pl.ds(r, nblk, stride=CMP_BLOCK), :], pos_ref, w1_ref, w2_ref)
    ck = ck * lax.rsqrt(_group_mean_sq(ck, avg_ref[...]) + EPS) * gkc_ref[...]
    kc_ref[0] = jnp.zeros((LANES, LANES), F32)
    vc_ref[0] = jnp.zeros((LANES, LANES), F32)
    kc_ref[0, 0:nblk, :] = ck
    vc_ref[0, 0:nblk, :] = cv


def _cmp_prompt(nkv, pos, w1bd, w2bd, gkc, avg):
    b, s, _ = nkv.shape
    const = lambda shape: pl.BlockSpec(shape, lambda i: (0,) * len(shape))
    return pl.pallas_call(
        _cmp_prompt_kernel,
        grid=(b,),
        in_specs=[pl.BlockSpec((1, s, LANES), lambda i: (i, 0, 0)), pl.BlockSpec((1, s, LANES), lambda i: (i, 0, 1)),
                  const((CMP_BLOCK, 256)), const((2, CMP_BLOCK // 2, 256, 256)), const((2, 256, LANES)),
                  const((1, LANES)), const((LANES, LANES))],
        out_specs=[pl.BlockSpec((1, LANES, LANES), lambda i: (i, 0, 0))] * 2,
        out_shape=[jax.ShapeDtypeStruct((b, LANES, LANES), F32)] * 2,
        compiler_params=_cparams(("arbitrary",)),
        name="nsa_compress_prompt",
    )(nkv, nkv, pos, w1bd, w2bd, gkc, avg)


def _nsa_prompt_kernel(q_ref, ks_ref, vs_ref, kw_ref, vw_ref, kc_ref, vc_ref, g_ref, t_ref, tc_ref,
                       o_ref, ksb_scr, vsb_scr, kwb_scr, vwb_scr, ex_scr):
    s_len = q_ref.shape[1]
    k = pl.program_id(1)
    lane = lax.broadcasted_iota(jnp.int32, (TQ, LANES), 1)
    kvmask = (lane // HEAD_DIM) == k
    kc = kc_ref[0].astype(BF16)
    vc = vc_ref[0].astype(BF16)
    nsel = s_len // SEL_BLOCK
    nq = s_len // TQ
    ncmp = s_len // CMP_BLOCK
    rowb = lax.broadcasted_iota(jnp.int32, (nsel, TQ), 0)
    qpos = lax.broadcasted_iota(jnp.int32, (nsel, TQ), 1)
    pair_r = lax.broadcasted_iota(jnp.int32, (nsel, LANES), 0)
    pair_c = lax.broadcasted_iota(jnp.int32, (nsel, LANES), 1)
    pair_t = jnp.where((pair_c // (SEL_BLOCK // CMP_BLOCK) == pair_r) & (pair_c < ncmp), 1.0, 0.0)
    e_r = lax.broadcasted_iota(jnp.int32, (LANES, s_len), 0)
    e_c = lax.broadcasted_iota(jnp.int32, (LANES, s_len), 1)
    ex_scr[...] = jnp.where(e_r == e_c // SEL_BLOCK, 1.0, 0.0).astype(BF16)
    ksb_scr[...] = ks_ref[0].astype(BF16)
    vsb_scr[...] = vs_ref[0].astype(BF16)
    kwb_scr[...] = kw_ref[0].astype(BF16)
    vwb_scr[...] = vw_ref[0].astype(BF16)

    def qtile(i, _):
        r0 = pl.multiple_of(i * TQ, TQ)
        qs = []
        for h in range(NSA_GROUP):
            q2 = q_ref[0, pl.ds(r0, TQ), (h // 2) * LANES:(h // 2 + 1) * LANES]
            qa = jnp.where(k == (h % 2), q2, pltpu.roll(q2, HEAD_DIM, 1))
            qs.append(jnp.where(kvmask, qa, 0.0).astype(BF16))

        imp = jnp.zeros((TQ, LANES), F32)
        o_cmp = []
        for h in range(NSA_GROUP):
            s = _dot_nt(qs[h], kc) + tc_ref[h, pl.ds(r0, TQ), :]
            m = jnp.maximum(jnp.max(s, axis=1, keepdims=True), M_INIT)
            p = jnp.exp2(s - m)
            p = p / jnp.maximum(jnp.sum(p, axis=1, keepdims=True), 1e-30)
            imp = imp + p
            o_cmp.append(_dot(p.astype(BF16), vc))

        imp_t = _dot_nt(pair_t, imp, precision=HIGHEST)
        own = (r0 + qpos) // SEL_BLOCK
        sc = jnp.where(rowb < own, imp_t, -jnp.inf)
        sel = ((_rank_rows(sc, rowb, nsel) < SEL_TOPK) & (rowb < own)) | (rowb == own)
        selb = _columns_from_rows(jnp.where(sel, 1.0, 0.0)).astype(BF16)

        wk, wv, wt = [], [], []
        for j, tidx in enumerate((3, 1, 0)):
            n = i - 2 + j
            c0 = pl.multiple_of(jnp.maximum(n, 0) * TQ, TQ)
            wk.append(kwb_scr[pl.ds(c0, TQ), :])
            wv.append(vwb_scr[pl.ds(c0, TQ), :])
            wt.append(jnp.where(n < 0, N_BIAS_TILES - 1, tidx))
        kw_all = jnp.concatenate(wk, axis=0)
        vw_all = jnp.concatenate(wv, axis=0)
        g = g_ref[0, pl.ds(r0, TQ), :]
        o_win = []
        for h in range(NSA_GROUP):
            s = _dot_nt(qs[h], kw_all)
            o_win.append(_softmax_pv([s[:, j * TQ:(j + 1) * TQ] + t_ref[h, wt[j]] for j in range(3)], vw_all))

        for c in range(1, nq + 1):
            @pl.when(i + 1 == c)
            def _():
                nkeys = c * TQ
                addm = (_dot(selb, ex_scr[:, 0:nkeys]) - 1.0) * (-NEG)
                kall = ksb_scr[0:nkeys, :]
                vall = vsb_scr[0:nkeys, :]
                res = []
                for h in range(NSA_GROUP):
                    s = _dot_nt(qs[h], kall) + addm
                    pieces = [s[:, n * TQ:(n + 1) * TQ] + t_ref[h, min(c - 1 - n, 2)] for n in range(c)]
                    o_sel = _softmax_pv(pieces, vall)
                    hg = (k * NSA_GROUP + h) * 3
                    o = (_col(g, lane, hg) * o_cmp[h] + _col(g, lane, hg + 1) * o_sel
                         + _col(g, lane, hg + 2) * o_win[h])
                    res.append(jnp.where(k == (h % 2), o, pltpu.roll(o, HEAD_DIM, 1)))
                for t in range(2):
                    o_ref[0, pl.ds(r0, TQ), t * LANES:(t + 1) * LANES] = jnp.where(
                        lane < HEAD_DIM, res[2 * t], res[2 * t + 1])
        return 0

    lax.fori_loop(0, nq, qtile, 0)


def _nsa_prompt(nq, nkv, wkv, kcmp, vcmp, gates, tb, tc):
    b, s, _ = nq.shape
    col = lambda arr_cols, cb: pl.BlockSpec((1, s, LANES), lambda i, k: (i, 0, cb))
    return pl.pallas_call(
        _nsa_prompt_kernel,
        grid=(b, 2),
        in_specs=[pl.BlockSpec((1, s, 256), lambda i, k: (i, 0, k)),
                  col(512, 2), col(512, 3), col(256, 0), col(256, 1),
                  pl.BlockSpec((1, LANES, LANES), lambda i, k: (i, 0, 0)),
                  pl.BlockSpec((1, LANES, LANES), lambda i, k: (i, 0, 0)),
                  pl.BlockSpec((1, s, LANES), lambda i, k: (i, 0, 0)),
                  pl.BlockSpec((NSA_GROUP, N_BIAS_TILES, TQ, TQ), lambda i, k: (k, 0, 0, 0)),
                  pl.BlockSpec((NSA_GROUP, s, LANES), lambda i, k: (k, 0, 0))],
        out_specs=pl.BlockSpec((1, s, 256), lambda i, k: (i, 0, k)),
        out_shape=jax.ShapeDtypeStruct((b, s, 512), F32),
        scratch_shapes=[pltpu.VMEM((s, LANES), BF16)] * 4 + [pltpu.VMEM((LANES, s), BF16)],
        compiler_params=_cparams(("arbitrary", "arbitrary")),
        name="nsa_prompt",
    )(nq, nkv, nkv, wkv, wkv, kcmp, vcmp, gates, tb, tc)


PAGES_PER_STEP = 16
CMP_PITCH = 40


def _rank_lt(score, lane, ncand, topk):
    rank = jnp.zeros(score.shape, F32)
    for m in range(ncand):
        col = score[:, m:m + 1]
        beats = (col > score) | ((col == score) & (m < lane))
        rank = rank + jnp.where(beats, 1.0, 0.0)
    return rank < topk


def _merge_blocks(sel, m_all, l_all, acc_scr, nblk, s_self, v_self):
    mx = jnp.maximum(jnp.max(jnp.where(sel, m_all, NEG), axis=1, keepdims=True), s_self)
    w = jnp.exp(jnp.where(sel, m_all - mx, NEG))
    w_self = jnp.exp(s_self - mx)
    den = jnp.sum(w * l_all, axis=1, keepdims=True) + w_self
    num = w_self * v_self
    for j in range(nblk):
        num = num + w[:, j:j + 1] * acc_scr[j]
    return num / den


def _moba_sample_kernel(pt_ref, *refs):
    pages = refs[:PAGES_PER_STEP]
    qm_ref, kn_ref, vn_ref, tsb_ref, misc_ref, o_ref, g_scr, m_scr, l_scr, acc_scr = refs[PAGES_PER_STEP:]
    s = pl.program_id(1)
    nstep = pl.num_programs(1)
    nblk = PAST_LEN // MOBA_BLOCK
    width = MOBA_HEADS * HEAD_DIM
    qm = qm_ref[0]
    qb = qm.astype(BF16)
    lane = lax.broadcasted_iota(jnp.int32, (MOBA_HEADS, LANES), 1)

    @pl.when(s == 0)
    def _():
        g_scr[...] = jnp.zeros(g_scr.shape, F32)
        m_scr[...] = jnp.zeros(m_scr.shape, F32)
        l_scr[...] = jnp.zeros(l_scr.shape, F32)

    npb = PAGES_PER_STEP // 2
    blk0 = s * npb
    kt_all = jnp.concatenate([pages[t][0, 0].astype(BF16) for t in range(PAGES_PER_STEP)], axis=1)
    vt_all = jnp.concatenate([pages[t][0, 1].astype(BF16) for t in range(PAGES_PER_STEP)], axis=1)
    raw = _dot(qb, kt_all)
    far = jnp.broadcast_to(misc_ref[:, 1:2], (MOBA_HEADS, MOBA_BLOCK))
    g_new, m_new, l_new = g_scr[...], m_scr[...], l_scr[...]
    p_rows = []
    for j in range(npb):
        seg = raw[:, j * MOBA_BLOCK:(j + 1) * MOBA_BLOCK]
        gate = jnp.sum(seg, axis=1, keepdims=True)
        sc = seg + (jnp.where(s == nstep - 1, tsb_ref[...], far) if j == npb - 1 else far)
        mj = jnp.max(sc, axis=1, keepdims=True)
        p = jnp.exp(sc - mj)
        g_new = jnp.where(lane == blk0 + j, gate, g_new)
        m_new = jnp.where(lane == blk0 + j, mj, m_new)
        l_new = jnp.where(lane == blk0 + j, jnp.sum(p, axis=1, keepdims=True), l_new)
        zeros = jnp.zeros((MOBA_HEADS, MOBA_BLOCK), F32)
        p_rows.append(jnp.concatenate([p if t == j else zeros for t in range(npb)], axis=1))
    g_scr[...] = g_new
    m_scr[...] = m_new
    l_scr[...] = l_new
    acc = _dot_nt(jnp.concatenate(p_rows, axis=0).astype(BF16), vt_all)
    for j in range(npb):
        acc_scr[blk0 + j] = acc[j * MOBA_HEADS:(j + 1) * MOBA_HEADS]

    @pl.when(s == nstep - 1)
    def _():
        gm = jnp.where(lane < nblk, g_scr[...], -jnp.inf)
        sel = _rank_lt(gm, lane, nblk, MOBA_TOPK) & (lane < nblk)
        s_self = jnp.sum(qm * kn_ref[0], axis=1, keepdims=True) + misc_ref[:, 0:1]
        o = _merge_blocks(sel, m_scr[...], l_scr[...], acc_scr, nblk, s_self, vn_ref[0])
        hrow = lax.broadcasted_iota(jnp.int32, (MOBA_HEADS, width), 0)
        hlane = lax.broadcasted_iota(jnp.int32, (MOBA_HEADS, width), 1)
        o_ref[0] = jnp.sum(jnp.where(hlane // HEAD_DIM == hrow, o, 0.0), axis=0, keepdims=True)


def _moba_sample(page_table, cache_t, qmat, knew, vnew, tsb, misc):
    nb, npages = page_table.shape
    nstep = npages // PAGES_PER_STEP
    width = MOBA_HEADS * HEAD_DIM
    nblk = PAST_LEN // MOBA_BLOCK

    def page_spec(j):
        return pl.BlockSpec((1, 2, width, PAGE), lambda b, s, pt: (pt[b, s * PAGES_PER_STEP + j], 0, 0, 0))

    per_b = lambda shape: pl.BlockSpec((1,) + shape, lambda b, s, pt: (b, 0, 0))
    const = lambda shape: pl.BlockSpec(shape, lambda b, s, pt: (0,) * len(shape))
    grid_spec = pltpu.PrefetchScalarGridSpec(
        num_scalar_prefetch=1,
        grid=(nb, nstep),
        in_specs=[page_spec(j) for j in range(PAGES_PER_STEP)]
        + [per_b((MOBA_HEADS, width)), per_b((1, width)), per_b((1, width)),
           const((MOBA_HEADS, MOBA_BLOCK)), const((MOBA_HEADS, LANES))],
        out_specs=per_b((1, width)),
        scratch_shapes=[pltpu.VMEM((MOBA_HEADS, LANES), F32)] * 3 + [pltpu.VMEM((nblk, MOBA_HEADS, width), F32)],
    )
    return pl.pallas_call(
        _moba_sample_kernel,
        grid_spec=grid_spec,
        out_shape=jax.ShapeDtypeStruct((nb, 1, width), F32),
        compiler_params=_cparams(("arbitrary", "arbitrary")),
        name="moba_sample",
    )(page_table, *([cache_t] * PAGES_PER_STEP), qmat, knew, vnew, tsb, misc)


def _nsa_sample_kernel(pt_ref, *refs):
    pages = refs[:PAGES_PER_STEP]
    (qm_ref, ksn_ref, vsn_ref, kwn_ref, vwn_ref, win_ref, g_ref, tsn_ref, misc_ref, tcs_ref, tws_ref,
     pos_ref, w1_ref, w2_ref, gkc_ref, avg_ref, o_ref, xk_scr, xv_scr, m_scr, l_scr, acc_scr) = refs[PAGES_PER_STEP:]
    s = pl.program_id(1)
    nstep = pl.num_programs(1)
    nsel = PAST_LEN // SEL_BLOCK
    ncmp = PAST_LEN // CMP_BLOCK
    qm = qm_ref[0]
    qb = qm.astype(BF16)
    lane = lax.broadcasted_iota(jnp.int32, (NSA_HEADS, LANES), 1)
    row = lax.broadcasted_iota(jnp.int32, (NSA_HEADS, LANES), 0)
    lo = lane < HEAD_DIM

    @pl.when(s == 0)
    def _():
        m_scr[...] = jnp.zeros(m_scr.shape, F32)
        l_scr[...] = jnp.zeros(l_scr.shape, F32)

    for j in range(PAGES_PER_STEP):
        pg = s * PAGES_PER_STEP + j
        kc = pages[j][0, 0:LANES, :].T
        vc = pages[j][0, LANES:2 * LANES, :].T
        for b4 in range(PAGE // CMP_BLOCK):
            r0 = pl.multiple_of((pg * (PAGE // CMP_BLOCK) + b4) * CMP_PITCH, 8)
            xk_scr[pl.ds(r0, CMP_BLOCK), :] = kc[b4 * CMP_BLOCK:(b4 + 1) * CMP_BLOCK, :]
            xv_scr[pl.ds(r0, CMP_BLOCK), :] = vc[b4 * CMP_BLOCK:(b4 + 1) * CMP_BLOCK, :]

    ks_all = jnp.concatenate([pages[t][0, 256:384, :].astype(BF16) for t in range(PAGES_PER_STEP)], axis=1)
    vs_all = jnp.concatenate([pages[t][0, 384:512, :].astype(BF16) for t in range(PAGES_PER_STEP)], axis=1)
    raw = _dot(qb, ks_all)
    far = jnp.broadcast_to(misc_ref[:, 1:2], (NSA_HEADS, PAGE))
    m_new, l_new = m_scr[...], l_scr[...]
    b0 = 2 * s * PAGES_PER_STEP
    zeros = jnp.zeros((NSA_HEADS, PAGE), F32)
    p_rows = []
    for j in range(PAGES_PER_STEP):
        sc = raw[:, j * PAGE:(j + 1) * PAGE]
        sc = sc + (jnp.where(s == nstep - 1, tsn_ref[...], far) if j == PAGES_PER_STEP - 1 else far)
        m0 = jnp.max(jnp.where(lo, sc, NEG), axis=1, keepdims=True)
        m1 = jnp.max(jnp.where(lo, NEG, sc), axis=1, keepdims=True)
        p = jnp.exp(sc - jnp.where(lo, m0, m1))
        p0, p1 = jnp.where(lo, p, 0.0), jnp.where(lo, 0.0, p)
        l0 = jnp.sum(p0, axis=1, keepdims=True)
        l1 = jnp.sum(p1, axis=1, keepdims=True)
        bj = b0 + 2 * j
        m_new = jnp.where(lane == bj, m0, jnp.where(lane == bj + 1, m1, m_new))
        l_new = jnp.where(lane == bj, l0, jnp.where(lane == bj + 1, l1, l_new))
        for ph in (p0, p1):
            p_rows.append(jnp.concatenate([ph if t == j else zeros for t in range(PAGES_PER_STEP)], axis=1))
    m_scr[...] = m_new
    l_scr[...] = l_new
    acc = _dot_nt(jnp.concatenate(p_rows, axis=0).astype(BF16), vs_all)
    for b in range(2 * PAGES_PER_STEP):
        acc_scr[b0 + b] = acc[b * NSA_HEADS:(b + 1) * NSA_HEADS]

    @pl.when(s == nstep - 1)
    def _():
        ck, cv = _compress_tokens(lambda r: xk_scr[pl.ds(r, ncmp, stride=CMP_PITCH), :],
                                  lambda r: xv_scr[pl.ds(r, ncmp, stride=CMP_PITCH), :], pos_ref, w1_ref, w2_ref)
        ck = ck * lax.rsqrt(_group_mean_sq(ck, avg_ref[...]) + EPS) * gkc_ref[...]
        sc = _dot_nt(qb, ck.astype(BF16)) + tcs_ref[...]
        m = jnp.maximum(jnp.max(sc, axis=1, keepdims=True), M_INIT)
        pc = jnp.exp(sc - m)
        pc = pc / jnp.maximum(jnp.sum(pc, axis=1, keepdims=True), 1e-30)
        o_cmp = _dot(pc.astype(BF16), cv.astype(BF16))
        g0 = pc[0:1] + pc[1:2] + pc[2:3] + pc[3:4]
        g1 = pc[4:5] + pc[5:6] + pc[6:7] + pc[7:8]
        rowc = lax.broadcasted_iota(jnp.int32, (NSA_HEADS, ncmp), 0)
        imp = jnp.where(rowc < NSA_GROUP, g0, g1)
        pr = lax.broadcasted_iota(jnp.int32, (ncmp, LANES), 0)
        pc_ = lax.broadcasted_iota(jnp.int32, (ncmp, LANES), 1)
        pair = jnp.where(pr // (SEL_BLOCK // CMP_BLOCK) == pc_, 1.0, 0.0)
        impb = _dot(imp, pair, precision=HIGHEST)
        own = PAST_LEN // SEL_BLOCK
        sel = _rank_lt(jnp.where(lane < own, impb, -jnp.inf), lane, nsel, SEL_TOPK) & (lane < own)
        s_self = jnp.sum(qm * ksn_ref[0], axis=1, keepdims=True) + misc_ref[:, 0:1]
        o_sel = _merge_blocks(sel, m_scr[...], l_scr[...], acc_scr, nsel, s_self, vsn_ref[0])
        kw_t = win_ref[0, 0:LANES, :].astype(BF16)
        vw_t = win_ref[0, LANES:2 * LANES, :].astype(BF16)
        sw = _dot(qb, kw_t) + tws_ref[...]
        sw_self = jnp.sum(qm * kwn_ref[0], axis=1, keepdims=True) + misc_ref[:, 0:1]
        mw = jnp.maximum(jnp.max(sw, axis=1, keepdims=True), sw_self)
        pw = jnp.exp(sw - mw)
        pw_self = jnp.exp(sw_self - mw)
        o_win = ((_dot_nt(pw.astype(BF16), vw_t) + pw_self * vwn_ref[0])
                 / (jnp.sum(pw, axis=1, keepdims=True) + pw_self))
        gt = jnp.broadcast_to(g_ref[0], (NSA_HEADS, LANES))
        o8 = (_col(gt, lane, 3 * row) * o_cmp + _col(gt, lane, 3 * row + 1) * o_sel
              + _col(gt, lane, 3 * row + 2) * o_win)
        lane1 = lax.broadcasted_iota(jnp.int32, (1, LANES), 1)
        tiles = []
        for t in range(NSA_HEADS // 2):
            ha, hb = 2 * t, 2 * t + 1
            ra = o8[ha:ha + 1, :]
            rb = o8[hb:hb + 1, :]
            if ha // NSA_GROUP == 1:
                ra = pltpu.roll(ra, HEAD_DIM, 1)
            if hb // NSA_GROUP == 0:
                rb = pltpu.roll(rb, HEAD_DIM, 1)
            tiles.append(jnp.where(lane1 < HEAD_DIM, ra, rb))
        o_ref[0] = jnp.concatenate(tiles, axis=1)


def _nsa_sample(page_table, cache, qmat, ksn, vsn, kwn, vwn, win, gates, tsn, misc, tcs, tws,
                pos, w1bd, w2bd, gkc, avg):
    nb, npages = page_table.shape
    nstep = npages // PAGES_PER_STEP
    nsel = PAST_LEN // SEL_BLOCK

    def page_spec(j):
        return pl.BlockSpec((1, 512, PAGE), lambda b, s, pt: (pt[b, s * PAGES_PER_STEP + j], 0, 0))

    per_b = lambda shape: pl.BlockSpec((1,) + shape, lambda b, s, pt: (b, 0, 0))
    const = lambda shape: pl.BlockSpec(shape, lambda b, s, pt: (0,) * len(shape))
    cmp_rows = PAST_LEN // CMP_BLOCK * CMP_PITCH
    grid_spec = pltpu.PrefetchScalarGridSpec(
        num_scalar_prefetch=1,
        grid=(nb, nstep),
        in_specs=[page_spec(j) for j in range(PAGES_PER_STEP)]
        + [per_b((NSA_HEADS, LANES)), per_b((1, LANES)), per_b((1, LANES)), per_b((1, LANES)), per_b((1, LANES)),
           per_b((256, WINDOW)), per_b((1, LANES)),
           const((NSA_HEADS, PAGE)), const((NSA_HEADS, LANES)), const((NSA_HEADS, PAST_LEN // CMP_BLOCK)),
           const((NSA_HEADS, WINDOW)),
           const((CMP_BLOCK, 256)), const((2, CMP_BLOCK // 2, 256, 256)), const((2, 256, LANES)),
           const((1, LANES)), const((LANES, LANES))],
        out_specs=per_b((1, 512)),
        scratch_shapes=[pltpu.VMEM((cmp_rows, LANES), F32), pltpu.VMEM((cmp_rows, LANES), F32),
                        pltpu.VMEM((NSA_HEADS, LANES), F32),
                        pltpu.VMEM((NSA_HEADS, LANES), F32), pltpu.VMEM((nsel, NSA_HEADS, LANES), F32)],
    )
    return pl.pallas_call(
        _nsa_sample_kernel,
        grid_spec=grid_spec,
        out_shape=jax.ShapeDtypeStruct((nb, 1, 512), F32),
        compiler_params=_cparams(("arbitrary", "arbitrary")),
        name="nsa_sample",
    )(page_table, *([cache] * PAGES_PER_STEP), qmat, ksn, vsn, kwn, vwn, win, gates, tsn, misc, tcs, tws,
      pos, w1bd, w2bd, gkc, avg)


def _outproj_kernel(x_ref, mod_ref, om_ref, on_ref, w_ref, o_ref):
    y = _dot(om_ref[0].astype(BF16), w_ref[0:512, :]) + _dot(on_ref[0].astype(BF16), w_ref[512:1024, :])
    o_ref[0] = x_ref[0] + mod_ref[0][:, 2 * D:3 * D] * y


def _outproj(x, mod, o_m, o_n, w, tm):
    b, t, _ = x.shape
    tmod = mod.shape[1]
    row = lambda width: pl.BlockSpec((1, tm, width), lambda i, j: (i, j, 0))
    return pl.pallas_call(
        _outproj_kernel,
        grid=(b, t // tm),
        in_specs=[row(D), pl.BlockSpec((1, tmod, 3 * D), lambda i, j: (i, 0, 0)), row(512), row(512),
                  pl.BlockSpec((D, D), lambda i, j: (0, 0))],
        out_specs=row(D),
        out_shape=jax.ShapeDtypeStruct((b, t, D), F32),
        compiler_params=_cparams(("arbitrary", "arbitrary")),
        name="attn_outproj",
    )(x, mod, o_m, o_n, w)


def _mlp_kernel(x_ref, mod_ref, g_ref, w1_ref, w2_ref, o_ref, h_scr, acc_scr):
    kf = pl.program_id(2)

    @pl.when(kf == 0)
    def _():
        mod = mod_ref[0]
        h_scr[...] = _modulate(x_ref[0], g_ref[...], mod[:, 0:D], mod[:, D:2 * D]).astype(BF16)
        acc_scr[...] = jnp.zeros(acc_scr.shape, F32)

    a = jnp.square(jnp.maximum(_dot(h_scr[...], w1_ref[...]), 0.0))
    acc_scr[...] += _dot(a.astype(BF16), w2_ref[...])

    @pl.when(kf == pl.num_programs(2) - 1)
    def _():
        o_ref[0] = x_ref[0] + mod_ref[0][:, 2 * D:3 * D] * acc_scr[...]


def _mlp(x, mod, g, w1, w2, tm, tf):
    b, t, _ = x.shape
    tmod = mod.shape[1]
    return pl.pallas_call(
        _mlp_kernel,
        grid=(b, t // tm, D_FF // tf),
        in_specs=[pl.BlockSpec((1, tm, D), lambda i, j, kf: (i, j, 0)),
                  pl.BlockSpec((1, tmod, 3 * D), lambda i, j, kf: (i, 0, 0)),
                  pl.BlockSpec((1, D), lambda i, j, kf: (0, 0)),
                  pl.BlockSpec((D, tf), lambda i, j, kf: (0, kf)),
                  pl.BlockSpec((tf, D), lambda i, j, kf: (kf, 0))],
        out_specs=pl.BlockSpec((1, tm, D), lambda i, j, kf: (i, j, 0)),
        out_shape=jax.ShapeDtypeStruct((b, t, D), F32),
        scratch_shapes=[pltpu.VMEM((tm, D), BF16), pltpu.VMEM((tm, D), F32)],
        compiler_params=_cparams(("arbitrary", "arbitrary", "arbitrary")),
        name="mlp",
    )(x, mod, g, w1, w2)


def _s5_disc_kernel(are_ref, aim_ref, ldt_ref, bre_ref, bim_ref, abre_ref, abim_ref, bbre_ref, bbim_ref):
    a_re, a_im = are_ref[...], aim_ref[...]
    dt = jnp.exp(ldt_ref[...])
    decay = jnp.exp(dt * a_re)
    ab_re, ab_im = decay * jnp.cos(dt * a_im), decay * jnp.sin(dt * a_im)
    den = a_re * a_re + a_im * a_im
    f_re = ((ab_re - 1) * a_re + ab_im * a_im) / den
    f_im = (ab_im * a_re - (ab_re - 1) * a_im) / den
    br, bi = bre_ref[...], bim_ref[...]
    abre_ref[...] = ab_re
    abim_ref[...] = ab_im
    bbre_ref[...] = f_re * br - f_im * bi
    bbim_ref[...] = f_re * bi + f_im * br


def _s5_discretize(a_re, a_im, log_dt, b_re, b_im):
    rep = lambda a: jnp.repeat(a, S5_GROUP_CH, axis=1)
    shp = jax.ShapeDtypeStruct((S5_GROUPS, S5_STATE * S5_GROUP_CH), F32)
    ldt = jnp.broadcast_to(log_dt[:, None], (S5_GROUPS, S5_STATE * S5_GROUP_CH))
    flat = lambda a: a.reshape(S5_GROUPS, S5_STATE * S5_GROUP_CH)
    ab_re, ab_im, bb_re, bb_im = pl.pallas_call(
        _s5_disc_kernel, out_shape=[shp] * 4, name="s5_discretize",
    )(rep(a_re), rep(a_im), ldt, flat(b_re), flat(b_im))
    unrep = lambda a: a[:, ::S5_GROUP_CH]
    unflat = lambda a: a.reshape(S5_GROUPS, S5_STATE, S5_GROUP_CH)
    return unrep(ab_re), unrep(ab_im), unflat(bb_re), unflat(bb_im)


def _modulate_tm_kernel(x_ref, mod_ref, g_ref, o_ref):
    mod = mod_ref[0]
    o_ref[...] = _modulate(x_ref[0], g_ref[...], mod[:, 0:D], mod[:, D:2 * D])


def _modulate_time_major(x, mod, g, tl):
    b, t, _ = x.shape
    tmod = mod.shape[1]
    return pl.pallas_call(
        _modulate_tm_kernel,
        grid=(b, t // tl),
        in_specs=[pl.BlockSpec((1, tl, D), lambda i, j: (i, j, 0)),
                  pl.BlockSpec((1, tmod, 3 * D), lambda i, j: (i, 0, 0)),
                  pl.BlockSpec((1, D), lambda i, j: (0, 0))],
        out_specs=pl.BlockSpec((tl, D), lambda i, j: (j, i)),
        out_shape=jax.ShapeDtypeStruct((t, b * D), F32),
        compiler_params=_cparams(("arbitrary", "arbitrary")),
        name="s5_modulate",
    )(x, mod, g)


S5_CB = 256
S5_NS = S5_CB // S5_GROUP_CH * S5_STATE


def _s5_scan_kernel(h_ref, wb_ref, wc_ref, ar_ref, ai_ref, d_ref, h0_ref, y_ref, so_ref, xs_scr, st_scr, *, tl, r):
    i = pl.program_id(1)

    @pl.when(i == 0)
    def _():
        st_scr[...] = h0_ref[...]

    u = h_ref[...]
    xs_scr[...] = _dot(u.astype(BF16), wb_ref[0])
    ar = jnp.broadcast_to(ar_ref[0], (r, S5_NS))
    ai = jnp.broadcast_to(ai_ref[0], (r, S5_NS))

    def step(t, carry):
        xr, xi = carry
        r0 = pl.multiple_of(t * r, r)
        nr = ar * xr - ai * xi + xs_scr[pl.ds(r0, r), 0:S5_NS]
        ni = ar * xi + ai * xr + xs_scr[pl.ds(r0, r), S5_NS:2 * S5_NS]
        xs_scr[pl.ds(r0, r), 0:S5_NS] = nr
        xs_scr[pl.ds(r0, r), S5_NS:2 * S5_NS] = ni
        return nr, ni

    xr, xi = lax.fori_loop(0, tl, step, (st_scr[0], st_scr[1]))
    st_scr[0] = xr
    st_scr[1] = xi
    y_ref[...] = _dot(xs_scr[...].astype(BF16), wc_ref[0]) + d_ref[...] * u

    @pl.when(i == pl.num_programs(1) - 1)
    def _():
        so_ref[...] = st_scr[...]


def _s5_scan(h_tm, wb, wc, ar, ai, d_skip, h0, r, tl):
    rows = h_tm.shape[0]
    nj = D // S5_CB
    return pl.pallas_call(
        functools.partial(_s5_scan_kernel, tl=tl, r=r),
        grid=(nj, rows // (tl * r)),
        in_specs=[pl.BlockSpec((tl * r, S5_CB), lambda j, i: (i, j)),
                  pl.BlockSpec((1, S5_CB, 2 * S5_NS), lambda j, i: (j, 0, 0)),
                  pl.BlockSpec((1, 2 * S5_NS, S5_CB), lambda j, i: (j, 0, 0)),
                  pl.BlockSpec((1, 1, S5_NS), lambda j, i: (j, 0, 0)),
                  pl.BlockSpec((1, 1, S5_NS), lambda j, i: (j, 0, 0)),
                  pl.BlockSpec((1, S5_CB), lambda j, i: (0, j)),
                  pl.BlockSpec((2, r, S5_NS), lambda j, i: (0, 0, j))],
        out_specs=[pl.BlockSpec((tl * r, S5_CB), lambda j, i: (i, j)),
                   pl.BlockSpec((2, r, S5_NS), lambda j, i: (0, 0, j))],
        out_shape=[jax.ShapeDtypeStruct((rows, D), F32), jax.ShapeDtypeStruct((2, r, S5_GROUPS * S5_STATE), F32)],
        scratch_shapes=[pltpu.VMEM((tl * r, 2 * S5_NS), F32), pltpu.VMEM((2, r, S5_NS), F32)],
        compiler_params=_cparams(("arbitrary", "arbitrary")),
        name="s5_scan",
    )(h_tm, wb, wc, ar, ai, d_skip, h0)


def _glu_kernel(y_ref, x_ref, mod_ref, w_ref, o_ref):
    z = _dot(jax.nn.gelu(y_ref[...]).astype(BF16), w_ref[...])
    o_ref[0] = x_ref[0] + mod_ref[0][:, 2 * D:3 * D] * (z[:, 0:D] * jax.nn.sigmoid(z[:, D:2 * D]))


def _glu_residual(y_tm, x, mod, w, tl):
    b, t, _ = x.shape
    tmod = mod.shape[1]
    return pl.pallas_call(
        _glu_kernel,
        grid=(b, t // tl),
        in_specs=[pl.BlockSpec((tl, D), lambda i, j: (j, i)),
                  pl.BlockSpec((1, tl, D), lambda i, j: (i, j, 0)),
                  pl.BlockSpec((1, tmod, 3 * D), lambda i, j: (i, 0, 0)),
                  pl.BlockSpec((D, 2 * D), lambda i, j: (0, 0))],
        out_specs=pl.BlockSpec((1, tl, D), lambda i, j: (i, j, 0)),
        out_shape=jax.ShapeDtypeStruct((b, t, D), F32),
        compiler_params=_cparams(("arbitrary", "arbitrary")),
        name="s5_glu",
    )(y_tm, x, mod, w)


def _s5_block_weights(bb_re, bb_im, c_re, c_im):
    nj, ng = D // S5_CB, S5_CB // S5_GROUP_CH
    eye = jnp.eye(ng, dtype=F32)

    def wb_part(bb):
        t = bb.reshape(nj, ng, S5_STATE, S5_GROUP_CH).transpose(0, 1, 3, 2)
        return jnp.einsum("jgcn,gh->jgchn", t, eye).reshape(nj, S5_CB, S5_NS)

    def wc_part(c):
        t = c.reshape(nj, ng, S5_GROUP_CH, S5_STATE).transpose(0, 1, 3, 2)
        return jnp.einsum("jgnc,gh->jgnhc", t, eye).reshape(nj, S5_NS, S5_CB)

    wb = jnp.concatenate([wb_part(bb_re), wb_part(bb_im)], axis=2).astype(BF16)
    wc = jnp.concatenate([wc_part(c_re), -wc_part(c_im)], axis=1).astype(BF16)
    return wb, wc


S5_TL = 64
S5_PITCH = 72
S5_NSLAB = 2 * S5_GROUPS * S5_STATE // LANES


def _s5_fused_kernel(x_ref, mod_ref, g_ref, wb_ref, wc_ref, ar_ref, ai_ref, d_ref, h0_ref, wg_ref,
                     o_ref, so_ref, xs_scr, st_scr):
    i = pl.program_id(0)
    nb = x_ref.shape[0]
    nj = D // S5_CB
    tiles = S5_NS // LANES

    @pl.when(i == 0)
    def _():
        xs_scr[...] = jnp.zeros(xs_scr.shape, F32)
        for j in range(nj):
            for comp in range(2):
                for q in range(tiles):
                    st_scr[(2 * j + comp) * tiles + q] = h0_ref[comp, :, j * S5_NS + q * LANES:j * S5_NS + (q + 1) * LANES]

    h = jnp.concatenate([_modulate(x_ref[b], g_ref[...], mod_ref[b][:, 0:D], mod_ref[b][:, D:2 * D])
                         for b in range(nb)], axis=0)
    hb = h.astype(BF16)
    ys = []
    for j in range(nj):
        bu = _dot(hb[:, j * S5_CB:(j + 1) * S5_CB], wb_ref[j])
        base = 2 * j * tiles
        for lt in range(2 * tiles):
            for b in range(nb):
                xs_scr[base + lt, b * S5_PITCH:b * S5_PITCH + S5_TL, :] = bu[b * S5_TL:(b + 1) * S5_TL,
                                                                             lt * LANES:(lt + 1) * LANES]
        ar = [jnp.broadcast_to(ar_ref[j][:, q * LANES:(q + 1) * LANES], (nb, LANES)) for q in range(tiles)]
        ai = [jnp.broadcast_to(ai_ref[j][:, q * LANES:(q + 1) * LANES], (nb, LANES)) for q in range(tiles)]

        def step(t, carry):
            new = []
            for q in range(tiles):
                xr, xi = carry[2 * q], carry[2 * q + 1]
                rows = pl.ds(t, nb, stride=S5_PITCH)
                nr = ar[q] * xr - ai[q] * xi + xs_scr[base + q, rows, :]
                ni = ar[q] * xi + ai[q] * xr + xs_scr[base + tiles + q, rows, :]
                xs_scr[base + q, rows, :] = nr
                xs_scr[base + tiles + q, rows, :] = ni
                new += [nr, ni]
            return tuple(new)

        init = []
        for q in range(tiles):
            init += [st_scr[base + q], st_scr[base + tiles + q]]
        fin = lax.fori_loop(0, S5_TL, step, tuple(init), unroll=True)
        for q in range(tiles):
            st_scr[base + q] = fin[2 * q]
            st_scr[base + tiles + q] = fin[2 * q + 1]
        states = jnp.concatenate([xs_scr[base + lt] for lt in range(2 * tiles)], axis=1)
        ys.append(_dot(states.astype(BF16), wc_ref[j]))
    y_all = jnp.concatenate(ys, axis=1)
    y = jnp.concatenate([y_all[b * S5_PITCH:b * S5_PITCH + S5_TL] for b in range(nb)], axis=0) + d_ref[...] * h
    z = _dot(jax.nn.gelu(y).astype(BF16), wg_ref[...])
    out = z[:, 0:D] * jax.nn.sigmoid(z[:, D:2 * D])
    for b in range(nb):
        o_ref[b] = x_ref[b] + mod_ref[b][:, 2 * D:3 * D] * out[b * S5_TL:(b + 1) * S5_TL]

    @pl.when(i == pl.num_programs(0) - 1)
    def _():
        for j in range(nj):
            for comp in range(2):
                for q in range(tiles):
                    so_ref[comp, :, j * S5_NS + q * LANES:j * S5_NS + (q + 1) * LANES] = st_scr[(2 * j + comp) * tiles + q]


def _s5_layer(x, mod, g, wb, wc, ar, ai, d_skip, h0, w_glu):
    b, t, _ = x.shape
    ns = S5_GROUPS * S5_STATE
    const = lambda shape: pl.BlockSpec(shape, lambda i: (0,) * len(shape))
    once = lambda shape: pl.BlockSpec(shape, lambda i: (0,) * len(shape), pipeline_mode=pl.Buffered(1))
    x_new, st = pl.pallas_call(
        _s5_fused_kernel,
        grid=(t // S5_TL,),
        in_specs=[pl.BlockSpec((b, S5_TL, D), lambda i: (0, i, 0)),
                  const((b, 1, 3 * D)), const((1, D)),
                  once((D // S5_CB, S5_CB, 2 * S5_NS)), once((D // S5_CB, 2 * S5_NS, S5_CB)),
                  const((D // S5_CB, 1, S5_NS)), const((D // S5_CB, 1, S5_NS)),
                  const((1, D)), const((2, b, ns)), once((D, 2 * D))],
        out_specs=[pl.BlockSpec((b, S5_TL, D), lambda i: (0, i, 0)), const((2, b, ns))],
        out_shape=[jax.ShapeDtypeStruct((b, t, D), F32), jax.ShapeDtypeStruct((2, b, ns), F32)],
        scratch_shapes=[pltpu.VMEM((S5_NSLAB, b * S5_PITCH, LANES), F32), pltpu.VMEM((S5_NSLAB, b, LANES), F32)],
        compiler_params=_cparams(("arbitrary",)),
        name="s5_fused",
    )(x, mod, g, wb, wc, ar, ai, d_skip, h0.reshape(b, 2, ns).transpose(1, 0, 2), w_glu)
    return x_new, st.transpose(1, 0, 2).reshape(b, 2, S5_GROUPS, S5_STATE)


def _dist_tiles():
    r = jnp.arange(TQ, dtype=jnp.int32)[:, None]
    c = jnp.arange(TQ, dtype=jnp.int32)[None, :]
    d0 = r - c
    edge = 2 * TQ + r - c
    return jnp.concatenate([d0, TQ + d0, 2 * TQ + d0, jnp.where(edge <= WINDOW, edge, -1),
                            jnp.full((TQ, TQ), -1, jnp.int32)], axis=0)


def _dist_cmp(seq):
    q = jnp.arange(seq, dtype=jnp.int32)[:, None]
    n = jnp.arange(LANES, dtype=jnp.int32)[None, :]
    return jnp.where(n < seq // CMP_BLOCK, q - ((n + 1) * CMP_BLOCK - 1), -1)


_SAMPLE_TABLE_SIZES = (LANES, MOBA_BLOCK, PAGE, PAST_LEN // CMP_BLOCK, WINDOW)


def _dist_sample():
    ar = lambda n: jnp.arange(n, dtype=jnp.int32)
    misc = jnp.zeros((LANES,), jnp.int32).at[1].set(MAX_DISTANCE * 4)
    moba = MOBA_BLOCK - ar(MOBA_BLOCK)
    sel = PAGE - ar(PAGE)
    cmp_ = PAST_LEN - ((ar(PAST_LEN // CMP_BLOCK) + 1) * CMP_BLOCK - 1)
    win = WINDOW - ar(WINDOW)
    return jnp.concatenate([misc, moba, sel, cmp_, win])[None, :]


def _block_diag2(w):
    z = jnp.zeros_like(w)
    return jnp.concatenate([jnp.concatenate([w, z], axis=-1), jnp.concatenate([z, w], axis=-1)], axis=-2)


def kernel(x_prompt, x_sample, cache_moba_kv, cache_nsa_kv, state_nsa_win, state_s5, page_table, c_prompt, c_sample, rel_bias, attn_norm_g, attn_ada_w, attn_ada_b, attn_w_in, attn_qk_g, nsa_cmp_pos, nsa_cmp_w1, nsa_cmp_w2, attn_w_out, ssm_norm_g, ssm_ada_w, ssm_ada_b, s5_a_re, s5_a_im, s5_log_dt, s5_b_re, s5_b_im, s5_c_re, s5_c_im, s5_d, s5_w_glu, mlp_norm_g, mlp_ada_w, mlp_ada_b, mlp_w1, mlp_w2):
    bp, seq, _ = x_prompt.shape
    bs = x_sample.shape[0]
    assert seq % TQ == 0 and x_sample.shape[1] == 1
    n_pool = cache_moba_kv.shape[1]

    c_all = jnp.concatenate([c_prompt, c_sample], axis=0)
    split_mod = lambda m: (m[:bp, None, :], m[None, bp:, :])
    mod_attn = _adaln(c_all, attn_ada_w, attn_ada_b)
    mod_ssm = _adaln(c_all, ssm_ada_w, ssm_ada_b)
    mod_mlp = _adaln(c_all, mlp_ada_w, mlp_ada_b)

    xp = x_prompt
    xs = x_sample.reshape(1, bs, D)

    tb = _bias_table(rel_bias, _dist_tiles(), LOG2E).reshape(2 * MOBA_HEADS, N_BIAS_TILES, TQ, TQ)
    tc = _bias_table(rel_bias, _dist_cmp(seq), LOG2E, first_head=MOBA_HEADS)
    ts = _bias_table(rel_bias, _dist_sample())[:, 0, :]
    offs = [0]
    for size in _SAMPLE_TABLE_SIZES:
        offs.append(offs[-1] + size)
    part = lambda heads, t: ts[heads, offs[t]:offs[t + 1]]
    hm, hn = slice(0, MOBA_HEADS), slice(MOBA_HEADS, 2 * MOBA_HEADS)
    misc_m, misc_n, tsb, tsn, tcs, tws = part(hm, 0), part(hn, 0), part(hm, 1), part(hn, 2), part(hn, 3), part(hn, 4)

    w_in = jnp.pad(attn_w_in[0], ((0, 0), (0, IN_COLS_PAD - IN_COLS))).astype(BF16)
    qkg_t = jnp.pad(jnp.tile(attn_qk_g[0], (1, 2)), ((0, 2), (0, 0)))
    lr = jnp.arange(LANES)
    avg = jnp.where(lr[:, None] // HEAD_DIM == lr[None, :] // HEAD_DIM, 1.0 / HEAD_DIM, 0.0).astype(BF16)
    g_attn = attn_norm_g[0][None, :]
    w_out = attn_w_out[0].astype(BF16)
    pos = jnp.concatenate([nsa_cmp_pos[0, 0], nsa_cmp_pos[0, 0], nsa_cmp_pos[0, 1], nsa_cmp_pos[0, 1]], axis=1)
    w1bd = _block_diag2(nsa_cmp_w1[0].reshape(2, CMP_BLOCK, HEAD_DIM, CMP_HIDDEN)).astype(BF16)
    w1bd = w1bd.reshape(2, CMP_BLOCK // 2, 256, 256)
    w2bd = _block_diag2(nsa_cmp_w2[0]).astype(BF16)
    gkc = qkg_t[3:4]

    mp_attn, ms_attn = split_mod(mod_attn[0])
    mq, mkv, nq, nkv, wkv, gates, mkv_t, nkv_t = _attn_proj(xp, mp_attn, g_attn, w_in, qkg_t, avg, 512,
                                                            SCALE * LOG2E, page_major=True)
    o_moba = _moba_prompt(mq, mkv, tb[:MOBA_HEADS])
    kcmp, vcmp = _cmp_prompt(nkv, pos, w1bd, w2bd, gkc, avg)
    o_nsa = _nsa_prompt(nq, nkv, wkv, kcmp, vcmp, gates, tb[MOBA_HEADS:], tc)
    xp = _outproj(xp, mp_attn, o_moba, o_nsa, w_out, 512)
    npg = seq // PAGE
    moba_p = mkv_t.reshape(1, bp, npg, 2, MOBA_HEADS, HEAD_DIM, PAGE).transpose(0, 1, 2, 6, 3, 4, 5)
    nsa_p = nkv_t.reshape(1, bp, npg, 4, 2, HEAD_DIM, PAGE).transpose(0, 1, 2, 6, 3, 4, 5)
    win_p = wkv[:, seq - min(WINDOW, seq):].reshape(1, bp, min(WINDOW, seq), 2, 2, HEAD_DIM)
    mq_s, mkv_s, nq_s, nkv_s, wkv_s, gates_s = _attn_proj(xs, ms_attn, g_attn, w_in, qkg_t, avg, bs, SCALE)
    cache_m_t = cache_moba_kv.transpose(0, 1, 3, 4, 5, 2).reshape(n_pool, 2, 512, PAGE)
    cache_n_t = cache_nsa_kv.transpose(0, 1, 3, 4, 5, 2).reshape(n_pool, 512, PAGE)
    win_t = state_nsa_win[0].transpose(0, 2, 3, 4, 1).reshape(bs, 256, WINDOW)
    lw = jnp.arange(512)
    qmat_m = jnp.where(lw[None, None, :] // HEAD_DIM == jnp.arange(MOBA_HEADS)[None, :, None], mq_s[0][:, None, :], 0.0)
    col3 = lambda a, lo, width: a[0][:, None, lo:lo + width]
    o_moba_s = _moba_sample(page_table, cache_m_t, qmat_m, col3(mkv_s, 0, 512), col3(mkv_s, 512, 512), tsb, misc_m)
    nq4 = nq_s[0].reshape(bs, NSA_HEADS, HEAD_DIM)
    kvh = jnp.arange(NSA_HEADS) // NSA_GROUP
    qmat_n = jnp.concatenate([jnp.where(kvh[None, :, None] == 0, nq4, 0.0),
                              jnp.where(kvh[None, :, None] == 1, nq4, 0.0)], axis=2)
    o_nsa_s = _nsa_sample(page_table, cache_n_t, qmat_n,
                          col3(nkv_s, 256, LANES), col3(nkv_s, 384, LANES), col3(wkv_s, 0, LANES),
                          col3(wkv_s, 128, LANES), win_t, gates_s.reshape(bs, 1, LANES),
                          tsn, misc_n, tcs, tws, pos, w1bd, w2bd, gkc, avg)
    xs = _outproj(xs, ms_attn, o_moba_s.reshape(1, bs, 512), o_nsa_s.reshape(1, bs, 512), w_out, bs)
    moba_s = mkv_s.reshape(1, bs, 1, 2, MOBA_HEADS, HEAD_DIM)
    nsa_s = nkv_s.reshape(1, bs, 1, 4, 2, HEAD_DIM)
    win_s = jnp.concatenate([state_nsa_win[0][:, 1:], wkv_s[0].reshape(bs, 1, 2, 2, HEAD_DIM)], axis=1)[None]

    w1_0, w2_0 = mlp_w1[0].astype(BF16), mlp_w2[0].astype(BF16)
    mp_mlp, ms_mlp = split_mod(mod_mlp[0])
    g_mlp0 = mlp_norm_g[0][None, :]
    xp = _mlp(xp, mp_mlp, g_mlp0, w1_0, w2_0, MLP_TM, MLP_TF)
    xs = _mlp(xs, ms_mlp, g_mlp0, w1_0, w2_0, bs, MLP_TF)

    ab_re, ab_im, bb_re, bb_im = _s5_discretize(s5_a_re[0], s5_a_im[0], s5_log_dt[0], s5_b_re[0], s5_b_im[0])
    wb, wc = _s5_block_weights(bb_re, bb_im, s5_c_re[0], s5_c_im[0])
    nj = D // S5_CB
    ar = ab_re.reshape(nj, 1, S5_NS)
    ai = ab_im.reshape(nj, 1, S5_NS)
    g_ssm = ssm_norm_g[0][None, :]
    d_skip = s5_d[0][None, :]
    w_glu = s5_w_glu[0].astype(BF16)
    mp_ssm, ms_ssm = split_mod(mod_ssm[0])
    xp, st_p = _s5_layer(xp, mp_ssm, g_ssm, wb, wc, ar, ai, d_skip,
                         jnp.zeros((bp, 2, S5_GROUPS, S5_STATE), F32), w_glu)
    h_s = _modulate_time_major(xs, ms_ssm, g_ssm, bs)
    y_s, st_s = _s5_scan(h_s, wb, wc, ar, ai, d_skip,
                         state_s5[0].reshape(bs, 2, S5_GROUPS * S5_STATE).transpose(1, 0, 2), bs, 1)
    xs = _glu_residual(y_s, xs, ms_ssm, w_glu, bs)
    st_s = st_s.transpose(1, 0, 2).reshape(bs, 2, S5_GROUPS, S5_STATE)

    w1_1, w2_1 = mlp_w1[1].astype(BF16), mlp_w2[1].astype(BF16)
    mp_mlp, ms_mlp = split_mod(mod_mlp[1])
    g_mlp1 = mlp_norm_g[1][None, :]
    xp = _mlp(xp, mp_mlp, g_mlp1, w1_1, w2_1, MLP_TM, MLP_TF)
    xs = _mlp(xs, ms_mlp, g_mlp1, w1_1, w2_1, bs, MLP_TF)

    return (xp, xs.reshape(bs, 1, D), moba_p, moba_s, nsa_p, nsa_s, win_p, win_s, st_p[None], st_s[None])
```
